```python
import jax, jax.numpy as jnp
from jax import lax
import numpy as np

D_MODEL = 2048
BATCH = 4
SEQ = 4096
DEPTH = 2

N_META = 16
GLA_HEADS = 4
GLA_KDIM = D_MODEL // 2
GLA_VDIM = D_MODEL
GLA_DK = GLA_KDIM // GLA_HEADS
GLA_DV = GLA_VDIM // GLA_HEADS
GLA_GATE_RANK = 16
GLA_GATE_TAU = 16.0
GLA_CHUNK = 64
MLA_HEADS = D_MODEL // 128
MLA_Q_LORA = D_MODEL // 4
MLA_KV_LORA = D_MODEL // 4
MLA_NOPE = 128
MLA_ROPE = 64
MLA_QK = MLA_NOPE + MLA_ROPE
MLA_V = 128
ROPE_THETA = 10000.0
Q_BLOCK = 128
FF_HIDDEN = -(-(8 * D_MODEL) // (3 * 256)) * 256
EPS = 1e-6
IN_SPLITS = (GLA_KDIM, GLA_KDIM, GLA_VDIM, GLA_VDIM, GLA_GATE_RANK,
             MLA_Q_LORA, MLA_KV_LORA, MLA_ROPE, D_MODEL, D_MODEL)
N_IN = sum(IN_SPLITS)

kernel_name = "hybrid_gla_mla_gated_block"


def rmsnorm(x, g):
    xf = x.astype(jnp.float32)
    y = xf * lax.rsqrt(jnp.mean(xf * xf, axis=-1, keepdims=True) + EPS)
    return (y * g.astype(jnp.float32)).astype(x.dtype)


def rope_tables(length):
    inv = 1.0 / (ROPE_THETA ** (jnp.arange(0, MLA_ROPE, 2, dtype=jnp.float32) / MLA_ROPE))
    ang = jnp.arange(length, dtype=jnp.float32)[:, None] * inv[None, :]
    return jnp.cos(ang), jnp.sin(ang)


def apply_rope(x, cos, sin):
    xf = x.astype(jnp.float32)
    x1, x2 = jnp.split(xf, 2, axis=-1)
    return jnp.concatenate([x1 * cos - x2 * sin, x2 * cos + x1 * sin], axis=-1).astype(x.dtype)


def split_heads(t, n, d):
    b, l, _ = t.shape
    return t.reshape(b, l, n, d).transpose(0, 2, 1, 3)


def gla_chunk(state, xs):
    q, k, v, g = xs
    c = q.shape[2]
    b = jnp.cumsum(g, axis=2)
    qd = q * jnp.exp(b)
    causal = jnp.tril(jnp.ones((c, c), dtype=bool))
    a = jnp.einsum('bhtd,bhsd->bhts', qd, k * jnp.exp(-b))
    a = jnp.where(causal, a, 0.0)
    o = jnp.einsum('bhts,bhsv->bhtv', a, v) + jnp.einsum('bhtd,bhdv->bhtv', qd, state)
    b_last = b[:, :, -1, :]
    new_state = (jnp.exp(b_last)[..., None] * state
                 + jnp.einsum('bhsd,bhsv->bhdv', k * jnp.exp(b_last[:, :, None, :] - b), v))
    return new_state, o


def gla_scan(q, k, v, g):
    bsz, h, length, _ = q.shape
    n_real = length - N_META
    s0 = jnp.zeros((bsz, h, GLA_DK, GLA_DV), jnp.float32)
    s_meta, o_meta = gla_chunk(s0, (q[:, :, :N_META], k[:, :, :N_META],
                                    v[:, :, :N_META], g[:, :, :N_META]))
    n_chunks = n_real // GLA_CHUNK

    def to_chunks(t):
        return jnp.moveaxis(t[:, :, N_META:].reshape(bsz, h, n_chunks, GLA_CHUNK, t.shape[-1]), 2, 0)

    _, o_real = lax.scan(gla_chunk, s_meta, (to_chunks(q), to_chunks(k), to_chunks(v), to_chunks(g)))
    o_real = jnp.moveaxis(o_real, 0, 2).reshape(bsz, h, n_real, GLA_DV)
    return jnp.concatenate([o_meta, o_real], axis=2)


def gla_branch(zq, zk, zv, zr, zlr, gate_w2, gate_b, onorm_g):
    bsz, length, _ = zq.shape
    q = split_heads(zq, GLA_HEADS, GLA_DK).astype(jnp.float32) * (GLA_DK ** -0.5)
    k = split_heads(zk, GLA_HEADS, GLA_DK).astype(jnp.float32)
    v = split_heads(zv, GLA_HEADS, GLA_DV).astype(jnp.float32)
    logit = (zlr @ gate_w2 + gate_b).astype(jnp.float32)
    g = split_heads(jax.nn.log_sigmoid(logit) / GLA_GATE_TAU, GLA_HEADS, GLA_DK)
    o = gla_scan(q, k, v, g).transpose(0, 2, 1, 3)
    o = rmsnorm(o, onorm_g)
    r = zr.reshape(bsz, length, GLA_HEADS, GLA_DV).astype(jnp.float32)
    return (o * jax.nn.silu(r)).astype(zr.dtype).reshape(bsz, length, GLA_VDIM)


def causal_block_attention(q, k, v):
    bsz, h, length, dqk = q.shape
    scale = dqk ** -0.5
    s_m = jnp.einsum('bhqd,bhkd->bhqk', q[:, :, :N_META], k[:, :, :N_META]).astype(jnp.float32) * scale
    s_m = jnp.where(jnp.tril(jnp.ones((N_META, N_META), dtype=bool)), s_m, -1e30)
    o_meta = jnp.einsum('bhqk,bhkd->bhqd', jax.nn.softmax(s_m, axis=-1).astype(v.dtype), v[:, :, :N_META])
    n_real = length - N_META
    n_blocks = n_real // Q_BLOCK
    q_blocks = jnp.moveaxis(q[:, :, N_META:].reshape(bsz, h, n_blocks, Q_BLOCK, dqk), 2, 0)
    key_pos = jnp.arange(length)

    def one_block(args):
        qb, i = args
        q_pos = N_META + i * Q_BLOCK + jnp.arange(Q_BLOCK)
        s = jnp.einsum('bhqd,bhkd->bhqk', qb, k).astype(jnp.float32) * scale
        s = jnp.where(key_pos[None, :] <= q_pos[:, None], s, -1e30)
        p = jax.nn.softmax(s, axis=-1).astype(v.dtype)
        return jnp.einsum('bhqk,bhkd->bhqd', p, v)

    o_real = lax.map(one_block, (q_blocks, jnp.arange(n_blocks)))
    o_real = jnp.moveaxis(o_real, 0, 2).reshape(bsz, h, n_real, v.shape[-1])
    return jnp.concatenate([o_meta, o_real], axis=2)


def mla_branch(zcq, zckv, zkr, q_a_norm_g, w_uq, kv_a_norm_g, w_ukv, q_norm_g, k_norm_g, cos, sin):
    bsz, length, _ = zcq.shape
    q = (rmsnorm(zcq, q_a_norm_g) @ w_uq).reshape(bsz, length, MLA_HEADS, MLA_QK)
    kv = (rmsnorm(zckv, kv_a_norm_g) @ w_ukv).reshape(bsz, length, MLA_HEADS, MLA_NOPE + MLA_V)
    k_nope, v = kv[..., :MLA_NOPE], kv[..., MLA_NOPE:]
    k_rope = jnp.broadcast_to(zkr[:, :, None, :], (bsz, length, MLA_HEADS, MLA_ROPE))
    k = jnp.concatenate([k_nope, k_rope], axis=-1)
    q = rmsnorm(q, q_norm_g)
    k = rmsnorm(k, k_norm_g)
    cs, sn = cos[:, None, :], sin[:, None, :]
    q = jnp.concatenate([q[..., :MLA_NOPE], apply_rope(q[..., MLA_NOPE:], cs, sn)], axis=-1)
    k = jnp.concatenate([k[..., :MLA_NOPE], apply_rope(k[..., MLA_NOPE:], cs, sn)], axis=-1)
    o = causal_block_attention(q.transpose(0, 2, 1, 3), k.transpose(0, 2, 1, 3), v.transpose(0, 2, 1, 3))
    return o.transpose(0, 2, 1, 3).reshape(bsz, length, MLA_HEADS * MLA_V)


def setup_inputs(seed: int = 0) -> dict:
    key = jax.random.key(seed)
    ks = jax.random.split(key, 20)
    f32 = jnp.float32

    def nrm(k, shape, fan_in):
        return jax.random.normal(k, shape, f32) * (fan_in ** -0.5)

    def gain(k, shape):
        return 1.0 + 0.02 * jax.random.normal(k, shape, f32)

    return {
        "x": jax.random.normal(ks[0], (BATCH, SEQ, D_MODEL), f32),
        "meta_tokens": jax.random.normal(ks[1], (N_META, D_MODEL), f32),
        "norm1_g": gain(ks[2], (DEPTH, D_MODEL)),
        "w_in": nrm(ks[3], (DEPTH, D_MODEL, N_IN), D_MODEL),
        "gla_gate_w2": nrm(ks[4], (DEPTH, GLA_GATE_RANK, GLA_KDIM), GLA_GATE_RANK),
        "gla_gate_b": 0.1 * jax.random.normal(ks[5], (DEPTH, GLA_KDIM), f32),
        "gla_onorm_g": gain(ks[6], (DEPTH, GLA_DV)),
        "w_branch_a": nrm(ks[7], (DEPTH, GLA_VDIM, D_MODEL), GLA_VDIM),
        "q_a_norm_g": gain(ks[8], (DEPTH, MLA_Q_LORA)),
        "w_uq": nrm(ks[9], (DEPTH, MLA_Q_LORA, MLA_HEADS * MLA_QK), MLA_Q_LORA),
        "kv_a_norm_g": gain(ks[10], (DEPTH, MLA_KV_LORA)),
        "w_ukv": nrm(ks[11], (DEPTH, MLA_KV_LORA, MLA_HEADS * (MLA_NOPE + MLA_V)), MLA_KV_LORA),
        "q_norm_g": gain(ks[12], (DEPTH, MLA_QK)),
        "k_norm_g": gain(ks[13], (DEPTH, MLA_QK)),
        "w_branch_b": nrm(ks[14], (DEPTH, MLA_HEADS * MLA_V, D_MODEL), MLA_HEADS * MLA_V),
        "w_out": nrm(ks[15], (DEPTH, D_MODEL, D_MODEL), D_MODEL),
        "norm2_g": gain(ks[16], (DEPTH, D_MODEL)),
        "w_gate_up": nrm(ks[17], (DEPTH, D_MODEL, 2 * FF_HIDDEN), D_MODEL),
        "w_down": nrm(ks[18], (DEPTH, FF_HIDDEN, D_MODEL), FF_HIDDEN),
    }


def reference(x, meta_tokens, norm1_g, w_in, gla_gate_w2, gla_gate_b, gla_onorm_g, w_branch_a,
              q_a_norm_g, w_uq, kv_a_norm_g, w_ukv, q_norm_g, k_norm_g, w_branch_b, w_out,
              norm2_g, w_gate_up, w_down):
    bsz, seq, d = x.shape
    meta = jnp.broadcast_to(meta_tokens[None].astype(x.dtype), (bsz, N_META, d))
    h_res = jnp.concatenate([meta, x], axis=1)
    cos, sin = rope_tables(h_res.shape[1])
    split_idx = np.cumsum(IN_SPLITS)[:-1].tolist()
    for l in range(DEPTH):
        h = rmsnorm(h_res, norm1_g[l])
        z = h @ w_in[l]
        zq, zk, zv, zr, zlr, zcq, zckv, zkr, za, zb = jnp.split(z, split_idx, axis=-1)
        y_a = gla_branch(zq, zk, zv, zr, zlr, gla_gate_w2[l], gla_gate_b[l], gla_onorm_g[l]) @ w_branch_a[l]
        y_b = mla_branch(zcq, zckv, zkr, q_a_norm_g[l], w_uq[l], kv_a_norm_g[l], w_ukv[l],
                         q_norm_g[l], k_norm_g[l], cos, sin) @ w_branch_b[l]
        merged = jax.nn.sigmoid(za) * y_a + jax.nn.sigmoid(zb) * y_b
        h_res = h_res + merged @ w_out[l]
        h2 = rmsnorm(h_res, norm2_g[l])
        gu = h2 @ w_gate_up[l]
        g, u = gu[..., :FF_HIDDEN], gu[..., FF_HIDDEN:]
        h_res = h_res + (jax.nn.silu(g) * u) @ w_down[l]
    return h_res[:, N_META:]
```

```python
import functools

import jax
import jax.numpy as jnp
from jax import lax
from jax.experimental import pallas as pl
from jax.experimental.pallas import tpu as pltpu

D_MODEL = 2048
BATCH = 4
SEQ = 4096
DEPTH = 2
N_META = 16
ROWS = BATCH * SEQ

GLA_HEADS = 4
GLA_DK = 256
GLA_DV = 512
GLA_GATE_RANK = 16
GLA_GATE_TAU = 16.0
GLA_CHUNK = 64

MLA_HEADS = 16
MLA_LORA = 512
MLA_NOPE = 128
MLA_ROPE = 64
MLA_QK = 192
MLA_V = 128
MLA_HEAD_PAD = 256
ROPE_THETA = 10000.0
FF_HIDDEN = 5632
EPS = 1e-6

Z_Q, Z_K, Z_V, Z_R, Z_CQ, Z_CKV, Z_A, Z_B = 0, 1024, 2048, 4096, 6144, 6656, 7168, 9216
N_MAIN = 11264
N_SMALL = 256

VMEM_LIMIT = 56 * 1024 * 1024
BF16 = jnp.bfloat16
F32 = jnp.float32

_NT = (((1,), (1,)), ((), ()))
_TN = (((0,), (0,)), ((), ()))


def _params(*sem):
    return pltpu.CompilerParams(dimension_semantics=sem, vmem_limit_bytes=VMEM_LIMIT)


def _sigmoid(x):
    return 1.0 / (1.0 + jnp.exp(-x))


def _rms_kernel(x_ref, g_ref, o_ref):
    x = x_ref[...]
    ms = jnp.mean(x * x, axis=-1, keepdims=True)
    o_ref[...] = (x * lax.rsqrt(ms + EPS) * g_ref[...]).astype(o_ref.dtype)


def _rms(x, g, bm):
    m, d = x.shape
    return pl.pallas_call(
        _rms_kernel,
        grid=(m // bm,),
        in_specs=[pl.BlockSpec((bm, d), lambda i: (i, 0)),
                  pl.BlockSpec((1, d), lambda i: (0, 0))],
        out_specs=pl.BlockSpec((bm, d), lambda i: (i, 0)),
        out_shape=jax.ShapeDtypeStruct((m, d), BF16),
        compiler_params=_params("parallel"),
        name="rmsnorm",
    )(x, g.reshape(1, d))


def _mm_kernel(a_ref, w_ref, o_ref):
    o_ref[...] = jnp.dot(a_ref[...], w_ref[...], preferred_element_type=F32).astype(o_ref.dtype)


def _matmul(a, w, bm, bn, out_dtype=F32, name="matmul"):
    m, k = a.shape
    n = w.shape[1]
    return pl.pallas_call(
        _mm_kernel,
        grid=(m // bm, n // bn),
        in_specs=[pl.BlockSpec((bm, k), lambda i, j: (i, 0)),
                  pl.BlockSpec((k, bn), lambda i, j: (0, j))],
        out_specs=pl.BlockSpec((bm, bn), lambda i, j: (i, j)),
        out_shape=jax.ShapeDtypeStruct((m, n), out_dtype),
        compiler_params=_params("parallel", "parallel"),
        name=name,
    )(a, w)


def _gla_kernel(q_ref, k_ref, v_ref, r_ref, zlr_ref, w2_ref, gb_ref, og_ref, s0_ref,
                o_ref, *rest, chunk, n_chunks, emit_state):
    if emit_state:
        sfin_ref, st_ref = rest
    else:
        (st_ref,) = rest
    t = pl.program_id(2)

    @pl.when(t == 0)
    def _():
        st_ref[...] = s0_ref[0]

    row = lax.broadcasted_iota(jnp.int32, (chunk, chunk), 0)
    col = lax.broadcasted_iota(jnp.int32, (chunk, chunk), 1)
    causal = col <= row
    tri = jnp.where(causal, 1.0, 0.0).astype(BF16)
    w2 = w2_ref[...]
    gb = gb_ref[...]
    og = og_ref[...]

    def body(c, carry):
        sl = pl.ds(pl.multiple_of(c * chunk, chunk), chunk)
        q = q_ref[sl, :] * (GLA_DK ** -0.5)
        k = k_ref[sl, :]
        v = v_ref[sl, :].astype(BF16)
        logit = jnp.dot(zlr_ref[sl, :].astype(BF16), w2, preferred_element_type=F32) + gb
        g = (jnp.minimum(logit, 0.0) - jnp.log1p(jnp.exp(-jnp.abs(logit)))) * (1.0 / GLA_GATE_TAU)
        g_hi = g.astype(BF16)
        g_lo = (g - g_hi.astype(F32)).astype(BF16)
        cs = jnp.dot(tri, jnp.concatenate([g_hi, g_lo], axis=1), preferred_element_type=F32)
        b = cs[:, :GLA_DK] + cs[:, GLA_DK:]
        b_last = b[chunk - 1:chunk, :]
        qd = (q * jnp.exp(b)).astype(BF16)
        kd = (k * jnp.exp(-b)).astype(BF16)
        a = lax.dot_general(qd, kd, _NT, preferred_element_type=F32)
        a = jnp.where(causal, a, 0.0).astype(BF16)
        st = st_ref[...]
        o = (jnp.dot(a, v, preferred_element_type=F32)
             + lax.dot_general(qd, st.astype(BF16), _NT, preferred_element_type=F32))
        k2 = (k * jnp.exp(b_last - b)).astype(BF16)
        upd = lax.dot_general(v, k2, _TN, preferred_element_type=F32)
        st_ref[...] = st * jnp.exp(b_last) + upd
        ms = jnp.mean(o * o, axis=-1, keepdims=True)
        on = o * lax.rsqrt(ms + EPS) * og
        r = r_ref[sl, :]
        o_ref[sl, :] = (on * (r * _sigmoid(r))).astype(o_ref.dtype)
        return carry

    lax.fori_loop(0, n_chunks, body, 0)

    if emit_state:
        @pl.when(t == pl.num_programs(2) - 1)
        def _():
            sfin_ref[0] = st_ref[...]


def _gla(z, zs, w2p, gate_b, onorm_g, s0, *, batch, tokens, block, chunk, emit_state):
    nt = tokens // block
    rows = batch * tokens
    kern = functools.partial(_gla_kernel, chunk=chunk, n_chunks=block // chunk,
                             emit_state=emit_state)

    def rowmap(b, h, t):
        return b * nt + t

    in_specs = [
        pl.BlockSpec((block, GLA_DK), lambda b, h, t: (rowmap(b, h, t), Z_Q // GLA_DK + h)),
        pl.BlockSpec((block, GLA_DK), lambda b, h, t: (rowmap(b, h, t), Z_K // GLA_DK + h)),
        pl.BlockSpec((block, GLA_DV), lambda b, h, t: (rowmap(b, h, t), Z_V // GLA_DV + h)),
        pl.BlockSpec((block, GLA_DV), lambda b, h, t: (rowmap(b, h, t), Z_R // GLA_DV + h)),
        pl.BlockSpec((block, 128), lambda b, h, t: (rowmap(b, h, t), 0)),
        pl.BlockSpec((128, GLA_DK), lambda b, h, t: (0, h)),
        pl.BlockSpec((1, GLA_DK), lambda b, h, t: (0, h)),
        pl.BlockSpec((1, GLA_DV), lambda b, h, t: (0, 0)),
        pl.BlockSpec((1, GLA_DV, GLA_DK), lambda b, h, t: (h, 0, 0)),
    ]
    out_specs = [pl.BlockSpec((block, GLA_DV), lambda b, h, t: (rowmap(b, h, t), h))]
    out_shape = [jax.ShapeDtypeStruct((rows, GLA_HEADS * GLA_DV), BF16)]
    if emit_state:
        out_specs.append(pl.BlockSpec((1, GLA_DV, GLA_DK), lambda b, h, t: (b * GLA_HEADS + h, 0, 0)))
        out_shape.append(jax.ShapeDtypeStruct((batch * GLA_HEADS, GLA_DV, GLA_DK), F32))
    res = pl.pallas_call(
        kern,
        grid=(batch, GLA_HEADS, nt),
        in_specs=in_specs,
        out_specs=out_specs,
        out_shape=out_shape,
        scratch_shapes=[pltpu.VMEM((GLA_DV, GLA_DK), F32)],
        compiler_params=_params("parallel", "parallel", "arbitrary"),
        name="gla",
    )(z, z, z, z, zs, w2p, gate_b.reshape(1, -1), onorm_g.reshape(1, -1), s0)
    return res


def _rope(x, c, s):
    return x * c + pltpu.roll(x, 64, 1) * s


def _qproj_kernel(zc_ref, ng_ref, w_ref, hg_ref, c_ref, s_ref, o_ref):
    x = zc_ref[...]
    ms = jnp.mean(x * x, axis=-1, keepdims=True)
    xn = (x * lax.rsqrt(ms + EPS) * ng_ref[...]).astype(BF16)
    c = c_ref[...]
    s = s_ref[...]
    hg = hg_ref[...]
    scale = MLA_QK ** -0.5
    for h in range(MLA_HEADS):
        lo = h * MLA_HEAD_PAD
        y = jnp.dot(xn, w_ref[:, lo:lo + MLA_HEAD_PAD], preferred_element_type=F32)
        ms = jnp.sum(y * y, axis=-1, keepdims=True) * (1.0 / MLA_QK)
        yn = y * lax.rsqrt(ms + EPS) * hg
        o_ref[:, lo:lo + 128] = (yn[:, :128] * scale).astype(o_ref.dtype)
        o_ref[:, lo + 128:lo + 256] = (_rope(yn[:, 128:], c, s) * scale).astype(o_ref.dtype)


def _qproj(z, ng, w, hg, ctab, stab, bm):
    m = z.shape[0]
    nt = ctab.shape[0] // bm
    return pl.pallas_call(
        _qproj_kernel,
        grid=(m // bm,),
        in_specs=[pl.BlockSpec((bm, MLA_LORA), lambda i: (i, Z_CQ // MLA_LORA)),
                  pl.BlockSpec((1, MLA_LORA), lambda i: (0, 0)),
                  pl.BlockSpec((MLA_LORA, MLA_HEADS * MLA_HEAD_PAD), lambda i: (0, 0)),
                  pl.BlockSpec((1, MLA_HEAD_PAD), lambda i: (0, 0)),
                  pl.BlockSpec((bm, 128), lambda i: (i % nt, 0)),
                  pl.BlockSpec((bm, 128), lambda i: (i % nt, 0))],
        out_specs=pl.BlockSpec((bm, MLA_HEADS * MLA_HEAD_PAD), lambda i: (i, 0)),
        out_shape=jax.ShapeDtypeStruct((m, MLA_HEADS * MLA_HEAD_PAD), BF16),
        compiler_params=_params("parallel"),
        name="q_proj",
    )(z, ng.reshape(1, -1), w, hg, ctab, stab)


def _kvproj_kernel(zc_ref, kr_ref, ng_ref, w_ref, hg_ref, c_ref, s_ref, k_ref, v_ref):
    x = zc_ref[...]
    ms = jnp.mean(x * x, axis=-1, keepdims=True)
    xn = (x * lax.rsqrt(ms + EPS) * ng_ref[...]).astype(BF16)
    c = c_ref[...]
    s = s_ref[...]
    hg = hg_ref[...]
    kr = kr_ref[...]
    kr_ss = jnp.sum(kr * kr, axis=-1, keepdims=True)
    for h in range(MLA_HEADS):
        lo = h * 256
        y = jnp.dot(xn, w_ref[:, lo:lo + 256], preferred_element_type=F32)
        kn = y[:, :MLA_NOPE]
        ms = (jnp.sum(kn * kn, axis=-1, keepdims=True) + kr_ss) * (1.0 / MLA_QK)
        rs = lax.rsqrt(ms + EPS)
        k_ref[:, lo:lo + 128] = (kn * rs * hg[:, :128]).astype(k_ref.dtype)
        k_ref[:, lo + 128:lo + 256] = _rope(kr * rs * hg[:, 128:], c, s).astype(k_ref.dtype)
        v_ref[:, h * MLA_V:(h + 1) * MLA_V] = y[:, MLA_NOPE:].astype(v_ref.dtype)


def _kvproj(z, zs, ng, w, hg, ctab, stab, bm):
    m = z.shape[0]
    nt = ctab.shape[0] // bm
    return pl.pallas_call(
        _kvproj_kernel,
        grid=(m // bm,),
        in_specs=[pl.BlockSpec((bm, MLA_LORA), lambda i: (i, Z_CKV // MLA_LORA)),
                  pl.BlockSpec((bm, 128), lambda i: (i, 1)),
                  pl.BlockSpec((1, MLA_LORA), lambda i: (0, 0)),
                  pl.BlockSpec((MLA_LORA, MLA_HEADS * 256), lambda i: (0, 0)),
                  pl.BlockSpec((1, MLA_HEAD_PAD), lambda i: (0, 0)),
                  pl.BlockSpec((bm, 128), lambda i: (i % nt, 0)),
                  pl.BlockSpec((bm, 128), lambda i: (i % nt, 0))],
        out_specs=[pl.BlockSpec((bm, MLA_HEADS * MLA_HEAD_PAD), lambda i: (i, 0)),
                   pl.BlockSpec((bm, MLA_HEADS * MLA_V), lambda i: (i, 0))],
        out_shape=[jax.ShapeDtypeStruct((m, MLA_HEADS * MLA_HEAD_PAD), BF16),
                   jax.ShapeDtypeStruct((m, MLA_HEADS * MLA_V), BF16)],
        compiler_params=_params("parallel"),
        name="kv_proj",
    )(z, zs, ng.reshape(1, -1), w, hg, ctab, stab)


def _attn_kernel(q_ref, k_ref, v_ref, km_ref, vm_ref, o_ref, m_sc, l_sc, acc_sc, *, blk):
    i = pl.program_id(2)
    q = q_ref[...]
    s = lax.dot_general(q, km_ref[...], _NT, preferred_element_type=F32)
    colm = lax.broadcasted_iota(jnp.int32, s.shape, 1)
    s = jnp.where(colm < N_META, s, -1e30)
    m = jnp.max(s, axis=-1, keepdims=True)
    p = jnp.exp(s - m)
    m_sc[...] = m
    l_sc[...] = jnp.sum(p, axis=-1, keepdims=True)
    acc_sc[...] = jnp.dot(p.astype(BF16), vm_ref[...], preferred_element_type=F32)

    def step(kb, masked):
        sl = pl.ds(pl.multiple_of(kb * blk, blk), blk)
        s = lax.dot_general(q, k_ref[sl, :], _NT, preferred_element_type=F32)
        if masked:
            row = lax.broadcasted_iota(jnp.int32, s.shape, 0)
            col = lax.broadcasted_iota(jnp.int32, s.shape, 1)
            s = jnp.where(col <= row, s, -1e30)
        m_prev = m_sc[...]
        m_new = jnp.maximum(m_prev, jnp.max(s, axis=-1, keepdims=True))
        alpha = jnp.exp(m_prev - m_new)
        p = jnp.exp(s - m_new)
        l_sc[...] = alpha * l_sc[...] + jnp.sum(p, axis=-1, keepdims=True)
        acc_sc[...] = alpha * acc_sc[...] + jnp.dot(p.astype(BF16), v_ref[sl, :],
                                                    preferred_element_type=F32)
        m_sc[...] = m_new

    def body(kb, carry):
        step(kb, False)
        return carry

    lax.fori_loop(0, i, body, 0)
    step(i, True)
    o_ref[...] = (acc_sc[...] / l_sc[...]).astype(o_ref.dtype)


def _attention(q, k, v, kmp, vmp, blk):
    nq = SEQ // blk
    kern = functools.partial(_attn_kernel, blk=blk)
    return pl.pallas_call(
        kern,
        grid=(BATCH, MLA_HEADS, nq),
        in_specs=[pl.BlockSpec((blk, MLA_HEAD_PAD), lambda b, h, i: (b * nq + i, h)),
                  pl.BlockSpec((SEQ, MLA_HEAD_PAD), lambda b, h, i: (b, h)),
                  pl.BlockSpec((SEQ, MLA_V), lambda b, h, i: (b, h)),
                  pl.BlockSpec((128, MLA_HEAD_PAD), lambda b, h, i: (0, h)),
                  pl.BlockSpec((128, MLA_V), lambda b, h, i: (0, h))],
        out_specs=pl.BlockSpec((blk, MLA_V), lambda b, h, i: (b * nq + i, h)),
        out_shape=jax.ShapeDtypeStruct((ROWS, MLA_HEADS * MLA_V), BF16),
        scratch_shapes=[pltpu.VMEM((blk, 1), F32), pltpu.VMEM((blk, 1), F32),
                        pltpu.VMEM((blk, MLA_V), F32)],
        compiler_params=_params("parallel", "parallel", "arbitrary"),
        name="mla_attention",
    )(q, k, v, kmp, vmp)


def _meta_attn_kernel(q_ref, km_ref, vm_ref, o_ref):
    s = lax.dot_general(q_ref[...], km_ref[...], _NT, preferred_element_type=F32)
    row = lax.broadcasted_iota(jnp.int32, s.shape, 0)
    col = lax.broadcasted_iota(jnp.int32, s.shape, 1)
    s = jnp.where(col <= row, s, -1e30)
    m = jnp.max(s, axis=-1, keepdims=True)
    p = jnp.exp(s - m)
    l = jnp.sum(p, axis=-1, keepdims=True)
    o = jnp.dot(p.astype(BF16), vm_ref[...], preferred_element_type=F32)
    o_ref[...] = (o / l).astype(o_ref.dtype)


def _meta_attention(qm, kmp, vmp):
    return pl.pallas_call(
        _meta_attn_kernel,
        grid=(MLA_HEADS,),
        in_specs=[pl.BlockSpec((N_META, MLA_HEAD_PAD), lambda h: (0, h)),
                  pl.BlockSpec((128, MLA_HEAD_PAD), lambda h: (0, h)),
                  pl.BlockSpec((128, MLA_V), lambda h: (0, h))],
        out_specs=pl.BlockSpec((N_META, MLA_V), lambda h: (0, h)),
        out_shape=jax.ShapeDtypeStruct((N_META, MLA_HEADS * MLA_V), BF16),
        compiler_params=_params("parallel"),
        name="meta_attention",
    )(qm, kmp, vmp)


def _merge_kernel(a_ref, b_ref, wa_ref, wb_ref, za_ref, zb_ref, o_ref):
    ya = jnp.dot(a_ref[...], wa_ref[...], preferred_element_type=F32)
    yb = jnp.dot(b_ref[...], wb_ref[...], preferred_element_type=F32)
    o_ref[...] = (_sigmoid(za_ref[...]) * ya + _sigmoid(zb_ref[...]) * yb).astype(o_ref.dtype)


def _merge(a, b, wa, wb, z, bm, bn):
    m = a.shape[0]
    return pl.pallas_call(
        _merge_kernel,
        grid=(m // bm, D_MODEL // bn),
        in_specs=[pl.BlockSpec((bm, D_MODEL), lambda i, j: (i, 0)),
                  pl.BlockSpec((bm, D_MODEL), lambda i, j: (i, 0)),
                  pl.BlockSpec((D_MODEL, bn), lambda i, j: (0, j)),
                  pl.BlockSpec((D_MODEL, bn), lambda i, j: (0, j)),
                  pl.BlockSpec((bm, bn), lambda i, j: (i, Z_A // bn + j)),
                  pl.BlockSpec((bm, bn), lambda i, j: (i, Z_B // bn + j))],
        out_specs=pl.BlockSpec((bm, bn), lambda i, j: (i, j)),
        out_shape=jax.ShapeDtypeStruct((m, D_MODEL), BF16),
        compiler_params=_params("parallel", "parallel"),
        name="branch_merge",
    )(a, b, wa, wb, z, z)


def _outproj_kernel(m_ref, w_ref, h_ref, g_ref, ho_ref, hn_ref):
    hn = h_ref[...] + jnp.dot(m_ref[...], w_ref[...], preferred_element_type=F32)
    ho_ref[...] = hn
    ms = jnp.mean(hn * hn, axis=-1, keepdims=True)
    hn_ref[...] = (hn * lax.rsqrt(ms + EPS) * g_ref[...]).astype(hn_ref.dtype)


def _outproj(mg, w, h, g, bm):
    m = mg.shape[0]
    return pl.pallas_call(
        _outproj_kernel,
        grid=(m // bm,),
        in_specs=[pl.BlockSpec((bm, D_MODEL), lambda i: (i, 0)),
                  pl.BlockSpec((D_MODEL, D_MODEL), lambda i: (0, 0)),
                  pl.BlockSpec((bm, D_MODEL), lambda i: (i, 0)),
                  pl.BlockSpec((1, D_MODEL), lambda i: (0, 0))],
        out_specs=[pl.BlockSpec((bm, D_MODEL), lambda i: (i, 0)),
                   pl.BlockSpec((bm, D_MODEL), lambda i: (i, 0))],
        out_shape=[jax.ShapeDtypeStruct((m, D_MODEL), F32),
                   jax.ShapeDtypeStruct((m, D_MODEL), BF16)],
        compiler_params=_params("parallel"),
        name="out_proj",
    )(mg, w, h, g.reshape(1, -1))


def _gateup_kernel(h_ref, wg_ref, wu_ref, o_ref):
    h = h_ref[...]
    g = jnp.dot(h, wg_ref[...], preferred_element_type=F32)
    u = jnp.dot(h, wu_ref[...], preferred_element_type=F32)
    o_ref[...] = (g * _sigmoid(g) * u).astype(o_ref.dtype)


def _gateup(h, w, bm, bn):
    m = h.shape[0]
    nb = FF_HIDDEN // bn
    return pl.pallas_call(
        _gateup_kernel,
        grid=(m // bm, nb),
        in_specs=[pl.BlockSpec((bm, D_MODEL), lambda i, j: (i, 0)),
                  pl.BlockSpec((D_MODEL, bn), lambda i, j: (0, j)),
                  pl.BlockSpec((D_MODEL, bn), lambda i, j: (0, nb + j))],
        out_specs=pl.BlockSpec((bm, bn), lambda i, j: (i, j)),
        out_shape=jax.ShapeDtypeStruct((m, FF_HIDDEN), BF16),
        compiler_params=_params("parallel", "parallel"),
        name="gate_up",
    )(h, w, w)


def _down_kernel(a_ref, w_ref, h_ref, g_ref, ho_ref, hn_ref, acc_ref):
    k = pl.program_id(1)

    @pl.when(k == 0)
    def _():
        acc_ref[...] = jnp.zeros_like(acc_ref)

    acc_ref[...] += jnp.dot(a_ref[...], w_ref[...], preferred_element_type=F32)

    @pl.when(k == pl.num_programs(1) - 1)
    def _():
        hn = h_ref[...] + acc_ref[...]
        ho_ref[...] = hn
        ms = jnp.mean(hn * hn, axis=-1, keepdims=True)
        hn_ref[...] = (hn * lax.rsqrt(ms + EPS) * g_ref[...]).astype(hn_ref.dtype)


def _down(a, w, h, g, bm, bk):
    m = a.shape[0]
    return pl.pallas_call(
        _down_kernel,
        grid=(m // bm, FF_HIDDEN // bk),
        in_specs=[pl.BlockSpec((bm, bk), lambda i, k: (i, k)),
                  pl.BlockSpec((bk, D_MODEL), lambda i, k: (k, 0)),
                  pl.BlockSpec((bm, D_MODEL), lambda i, k: (i, 0)),
                  pl.BlockSpec((1, D_MODEL), lambda i, k: (0, 0))],
        out_specs=[pl.BlockSpec((bm, D_MODEL), lambda i, k: (i, 0)),
                   pl.BlockSpec((bm, D_MODEL), lambda i, k: (i, 0))],
        out_shape=[jax.ShapeDtypeStruct((m, D_MODEL), F32),
                   jax.ShapeDtypeStruct((m, D_MODEL), BF16)],
        scratch_shapes=[pltpu.VMEM((bm, D_MODEL), F32)],
        compiler_params=_params("parallel", "arbitrary"),
        name="down_proj",
    )(a, w, h, g.reshape(1, -1))


def _rope_layout(t):
    zeros = jnp.zeros(t.shape[:-1] + (32,), t.dtype)
    return jnp.concatenate([t[..., :32], zeros, t[..., 32:], zeros], axis=-1)


def _head_layout(t):
    return jnp.concatenate([t[..., :MLA_NOPE], _rope_layout(t[..., MLA_NOPE:])], axis=-1)


def _rope_tables():
    length = N_META + SEQ
    inv = 1.0 / (ROPE_THETA ** (jnp.arange(0, MLA_ROPE, 2, dtype=F32) / MLA_ROPE))
    ang = jnp.arange(length, dtype=F32)[:, None] * inv[None, :]
    cos, sin = jnp.cos(ang), jnp.sin(ang)
    zeros = jnp.zeros_like(cos)
    ctab = jnp.concatenate([cos, zeros, cos, zeros], axis=-1)
    stab = jnp.concatenate([-sin, zeros, sin, zeros], axis=-1)
    return ctab, stab


def kernel(x, meta_tokens, norm1_g, w_in, gla_gate_w2, gla_gate_b, gla_onorm_g, w_branch_a,
           q_a_norm_g, w_uq, kv_a_norm_g, w_ukv, q_norm_g, k_norm_g, w_branch_b, w_out,
           norm2_g, w_gate_up, w_down):
    ctab, stab = _rope_tables()
    ctab_m, stab_m = ctab[:N_META], stab[:N_META]
    ctab_r, stab_r = ctab[N_META:], stab[N_META:]

    h_res = x.reshape(ROWS, D_MODEL)
    hm_res = meta_tokens.astype(F32)
    h = _rms(h_res, norm1_g[0], 512)
    hm = _rms(hm_res, norm1_g[0], N_META)
    s_zero = jnp.zeros((GLA_HEADS, GLA_DV, GLA_DK), F32)

    for l in range(DEPTH):
        last = l == DEPTH - 1
        wl = w_in[l]
        w_main = jnp.concatenate([wl[:, :6144], wl[:, 6160:7184], wl[:, 7248:]], axis=1).astype(BF16)
        w_zlr = wl[:, 6144:6160]
        w_zkr = wl[:, 7184:7248]
        w_small = jnp.concatenate(
            [w_zlr, jnp.zeros((D_MODEL, 128 - GLA_GATE_RANK), F32), _rope_layout(w_zkr)],
            axis=1).astype(BF16)
        w2p = jnp.concatenate(
            [gla_gate_w2[l], jnp.zeros((128 - GLA_GATE_RANK, GLA_HEADS * GLA_DK), F32)],
            axis=0).astype(BF16)
        wq = _head_layout(w_uq[l].reshape(MLA_LORA, MLA_HEADS, MLA_QK)).reshape(
            MLA_LORA, MLA_HEADS * MLA_HEAD_PAD).astype(BF16)
        wkv = w_ukv[l].astype(BF16)
        qg = _head_layout(q_norm_g[l]).reshape(1, MLA_HEAD_PAD)
        kg = _head_layout(k_norm_g[l]).reshape(1, MLA_HEAD_PAD)
        wa = w_branch_a[l].astype(BF16)
        wb = w_branch_b[l].astype(BF16)
        wo = w_out[l].astype(BF16)
        wgu = w_gate_up[l].astype(BF16)
        wd = w_down[l].astype(BF16)
        g_next = norm1_g[l + 1] if not last else norm1_g[l]

        z = _matmul(h, w_main, 1024, 1024, name="in_proj")
        zs = _matmul(h, w_small, 1024, N_SMALL, name="in_proj_small")
        zm = _matmul(hm, w_main, N_META, 1024, name="in_proj_meta")
        zsm = _matmul(hm, w_small, N_META, N_SMALL, name="in_proj_small_meta")

        gla_m, s_meta = _gla(zm, zsm, w2p, gla_gate_b[l], gla_onorm_g[l], s_zero,
                             batch=1, tokens=N_META, block=N_META, chunk=N_META, emit_state=True)
        (gla_r,) = _gla(z, zs, w2p, gla_gate_b[l], gla_onorm_g[l], s_meta,
                        batch=BATCH, tokens=SEQ, block=512, chunk=GLA_CHUNK, emit_state=False)

        km, vm = _kvproj(zm, zsm, kv_a_norm_g[l], wkv, kg, ctab_m, stab_m, N_META)
        kr, vr = _kvproj(z, zs, kv_a_norm_g[l], wkv, kg, ctab_r, stab_r, 512)
        kmp = jnp.pad(km, ((0, 128 - N_META), (0, 0)))
        vmp = jnp.pad(vm, ((0, 128 - N_META), (0, 0)))
        qr = _qproj(z, q_a_norm_g[l], wq, qg, ctab_r, stab_r, 512)
        att_r = _attention(qr, kr, vr, kmp, vmp, 512)

        merged = _merge(gla_r, att_r, wa, wb, z, 1024, 512)
        h_res, h2 = _outproj(merged, wo, h_res, norm2_g[l], 512)
        act = _gateup(h2, wgu, 1024, 512)
        h_res, h = _down(act, wd, h_res, g_next, 512, 1408)

        if not last:
            qm = _qproj(zm, q_a_norm_g[l], wq, qg, ctab_m, stab_m, N_META)
            att_m = _meta_attention(qm, kmp, vmp)
            merged_m = _merge(gla_m, att_m, wa, wb, zm, N_META, 512)
            hm_res, hm2 = _outproj(merged_m, wo, hm_res, norm2_g[l], N_META)
            act_m = _gateup(hm2, wgu, N_META, 512)
            hm_res, hm = _down(act_m, wd, hm_res, g_next, N_META, 1408)

    return h_res.reshape(BATCH, SEQ, D_MODEL)
```

```python
import functools

import jax
import jax.numpy as jnp
from jax import lax
from jax.experimental import pallas as pl
from jax.experimental.pallas import tpu as pltpu

D_MODEL = 2048
BATCH = 4
SEQ = 4096
DEPTH = 2
N_META = 16
ROWS = BATCH * SEQ

GLA_HEADS = 4
GLA_DK = 256
GLA_DV = 512
GLA_GATE_RANK = 16
GLA_GATE_TAU = 16.0
GLA_CHUNK = 64

MLA_HEADS = 16
MLA_LORA = 512
MLA_NOPE = 128
MLA_ROPE = 64
MLA_QK = 192
MLA_V = 128
MLA_HEAD_PAD = 256
ROPE_THETA = 10000.0
FF_HIDDEN = 5632
EPS = 1e-6
LOG2E = 1.4426950408889634

Z_Q, Z_K, Z_V, Z_R, Z_CQ, Z_CKV, Z_A, Z_B = 0, 1024, 2048, 4096, 6144, 6656, 7168, 9216
N_MAIN = 11264
N_SMALL = 256

VMEM_LIMIT = 56 * 1024 * 1024
BF16 = jnp.bfloat16
F32 = jnp.float32

_NT = (((1,), (1,)), ((), ()))
_TN = (((0,), (0,)), ((), ()))


def _params(*sem):
    return pltpu.CompilerParams(dimension_semantics=sem, vmem_limit_bytes=VMEM_LIMIT)


def _sigmoid(x):
    return 1.0 / (1.0 + jnp.exp(-x))


def _rms_kernel(x_ref, g_ref, o_ref):
    x = x_ref[...]
    ms = jnp.mean(x * x, axis=-1, keepdims=True)
    o_ref[...] = (x * lax.rsqrt(ms + EPS) * g_ref[...]).astype(o_ref.dtype)


def _rms(x, g, bm):
    m, d = x.shape
    return pl.pallas_call(
        _rms_kernel,
        grid=(m // bm,),
        in_specs=[pl.BlockSpec((bm, d), lambda i: (i, 0)),
                  pl.BlockSpec((1, d), lambda i: (0, 0))],
        out_specs=pl.BlockSpec((bm, d), lambda i: (i, 0)),
        out_shape=jax.ShapeDtypeStruct((m, d), BF16),
        compiler_params=_params("parallel"),
        name="rmsnorm",
    )(x, g.reshape(1, d))


def _mm_kernel(a_ref, w_ref, o_ref):
    o_ref[...] = jnp.dot(a_ref[...], w_ref[...], preferred_element_type=F32).astype(o_ref.dtype)


def _matmul(a, w, bm, bn, out_dtype=F32, name="matmul"):
    m, k = a.shape
    n = w.shape[1]
    return pl.pallas_call(
        _mm_kernel,
        grid=(m // bm, n // bn),
        in_specs=[pl.BlockSpec((bm, k), lambda i, j: (i, 0)),
                  pl.BlockSpec((k, bn), lambda i, j: (0, j))],
        out_specs=pl.BlockSpec((bm, bn), lambda i, j: (i, j)),
        out_shape=jax.ShapeDtypeStruct((m, n), out_dtype),
        compiler_params=_params("parallel", "parallel"),
        name=name,
    )(a, w)


def _gla_kernel(q_ref, k_ref, v_ref, r_ref, zlr_ref, w2_ref, gb_ref, og_ref, s0_ref,
                o_ref, *rest, chunk, n_chunks, emit_state):
    if emit_state:
        sfin_ref, st_ref = rest
    else:
        (st_ref,) = rest
    t = pl.program_id(2)

    @pl.when(t == 0)
    def _():
        st_ref[...] = s0_ref[0]

    row = lax.broadcasted_iota(jnp.int32, (chunk, chunk), 0)
    col = lax.broadcasted_iota(jnp.int32, (chunk, chunk), 1)
    causal = col <= row
    tri = jnp.where(causal, 1.0, 0.0).astype(BF16)
    w2 = w2_ref[...]
    gb = gb_ref[...]
    og = og_ref[...]

    def body(c, carry):
        sl = pl.ds(pl.multiple_of(c * chunk, chunk), chunk)
        q = q_ref[sl, :] * (GLA_DK ** -0.5)
        k = k_ref[sl, :]
        v = v_ref[sl, :].astype(BF16)
        logit = jnp.dot(zlr_ref[sl, :].astype(BF16), w2, preferred_element_type=F32) + gb
        g = (jnp.minimum(logit, 0.0) - jnp.log1p(jnp.exp(-jnp.abs(logit)))) * (1.0 / GLA_GATE_TAU)
        g_hi = g.astype(BF16)
        g_lo = (g - g_hi.astype(F32)).astype(BF16)
        cs = jnp.dot(tri, jnp.concatenate([g_hi, g_lo], axis=1), preferred_element_type=F32)
        b = cs[:, :GLA_DK] + cs[:, GLA_DK:]
        b_last = b[chunk - 1:chunk, :]
        qd = (q * jnp.exp(b)).astype(BF16)
        kd = (k * jnp.exp(-b)).astype(BF16)
        a = lax.dot_general(qd, kd, _NT, preferred_element_type=F32)
        a = jnp.where(causal, a, 0.0).astype(BF16)
        st = st_ref[...]
        o = (jnp.dot(a, v, preferred_element_type=F32)
             + lax.dot_general(qd, st.astype(BF16), _NT, preferred_element_type=F32))
        k2 = (k * jnp.exp(b_last - b)).astype(BF16)
        upd = lax.dot_general(v, k2, _TN, preferred_element_type=F32)
        st_ref[...] = st * jnp.exp(b_last) + upd
        ms = jnp.mean(o * o, axis=-1, keepdims=True)
        on = o * lax.rsqrt(ms + EPS) * og
        r = r_ref[sl, :]
        o_ref[sl, :] = (on * (r * _sigmoid(r))).astype(o_ref.dtype)
        return carry

    lax.fori_loop(0, n_chunks, body, 0)

    if emit_state:
        @pl.when(t == pl.num_programs(2) - 1)
        def _():
            sfin_ref[0] = st_ref[...]


def _gla(z, zs, w2p, gate_b, onorm_g, s0, *, batch, tokens, block, chunk, emit_state):
    nt = tokens // block
    rows = batch * tokens
    kern = functools.partial(_gla_kernel, chunk=chunk, n_chunks=block // chunk,
                             emit_state=emit_state)

    def rowmap(b, h, t):
        return b * nt + t

    in_specs = [
        pl.BlockSpec((block, GLA_DK), lambda b, h, t: (rowmap(b, h, t), Z_Q // GLA_DK + h)),
        pl.BlockSpec((block, GLA_DK), lambda b, h, t: (rowmap(b, h, t), Z_K // GLA_DK + h)),
        pl.BlockSpec((block, GLA_DV), lambda b, h, t: (rowmap(b, h, t), Z_V // GLA_DV + h)),
        pl.BlockSpec((block, GLA_DV), lambda b, h, t: (rowmap(b, h, t), Z_R // GLA_DV + h)),
        pl.BlockSpec((block, 128), lambda b, h, t: (rowmap(b, h, t), 0)),
        pl.BlockSpec((128, GLA_DK), lambda b, h, t: (0, h)),
        pl.BlockSpec((1, GLA_DK), lambda b, h, t: (0, h)),
        pl.BlockSpec((1, GLA_DV), lambda b, h, t: (0, 0)),
        pl.BlockSpec((1, GLA_DV, GLA_DK), lambda b, h, t: (h, 0, 0)),
    ]
    out_specs = [pl.BlockSpec((block, GLA_DV), lambda b, h, t: (rowmap(b, h, t), h))]
    out_shape = [jax.ShapeDtypeStruct((rows, GLA_HEADS * GLA_DV), BF16)]
    if emit_state:
        out_specs.append(pl.BlockSpec((1, GLA_DV, GLA_DK), lambda b, h, t: (b * GLA_HEADS + h, 0, 0)))
        out_shape.append(jax.ShapeDtypeStruct((batch * GLA_HEADS, GLA_DV, GLA_DK), F32))
    res = pl.pallas_call(
        kern,
        grid=(batch, GLA_HEADS, nt),
        in_specs=in_specs,
        out_specs=out_specs,
        out_shape=out_shape,
        scratch_shapes=[pltpu.VMEM((GLA_DV, GLA_DK), F32)],
        compiler_params=_params("parallel", "parallel", "arbitrary"),
        name="gla",
    )(z, z, z, z, zs, w2p, gate_b.reshape(1, -1), onorm_g.reshape(1, -1), s0)
    return res


def _rope(x, c, s):
    return x * c + pltpu.roll(x, 64, 1) * s


def _qproj_kernel(zc_ref, ng_ref, w_ref, hg_ref, c_ref, s_ref, o_ref):
    x = zc_ref[...]
    ms = jnp.mean(x * x, axis=-1, keepdims=True)
    xn = (x * lax.rsqrt(ms + EPS) * ng_ref[...]).astype(BF16)
    c = c_ref[...]
    s = s_ref[...]
    hg = hg_ref[...]
    scale = MLA_QK ** -0.5 * LOG2E
    for h in range(MLA_HEADS):
        lo = h * MLA_HEAD_PAD
        y = jnp.dot(xn, w_ref[:, lo:lo + MLA_HEAD_PAD], preferred_element_type=F32)
        ms = jnp.sum(y * y, axis=-1, keepdims=True) * (1.0 / MLA_QK)
        yn = y * lax.rsqrt(ms + EPS) * hg
        o_ref[:, lo:lo + 128] = (yn[:, :128] * scale).astype(o_ref.dtype)
        o_ref[:, lo + 128:lo + 256] = (_rope(yn[:, 128:], c, s) * scale).astype(o_ref.dtype)


def _qproj(z, ng, w, hg, ctab, stab, bm):
    m = z.shape[0]
    nt = ctab.shape[0] // bm
    return pl.pallas_call(
        _qproj_kernel,
        grid=(m // bm,),
        in_specs=[pl.BlockSpec((bm, MLA_LORA), lambda i: (i, Z_CQ // MLA_LORA)),
                  pl.BlockSpec((1, MLA_LORA), lambda i: (0, 0)),
                  pl.BlockSpec((MLA_LORA, MLA_HEADS * MLA_HEAD_PAD), lambda i: (0, 0)),
                  pl.BlockSpec((1, MLA_HEAD_PAD), lambda i: (0, 0)),
                  pl.BlockSpec((bm, 128), lambda i: (i % nt, 0)),
                  pl.BlockSpec((bm, 128), lambda i: (i % nt, 0))],
        out_specs=pl.BlockSpec((bm, MLA_HEADS * MLA_HEAD_PAD), lambda i: (i, 0)),
        out_shape=jax.ShapeDtypeStruct((m, MLA_HEADS * MLA_HEAD_PAD), BF16),
        compiler_params=_params("parallel"),
        name="q_proj",
    )(z, ng.reshape(1, -1), w, hg, ctab, stab)


def _kvproj_kernel(zc_ref, kr_ref, ng_ref, wk_ref, wvt_ref, hg_ref, c_ref, s_ref, k_ref, vt_ref):
    x = zc_ref[...]
    ms = jnp.mean(x * x, axis=-1, keepdims=True)
    xn = (x * lax.rsqrt(ms + EPS) * ng_ref[...]).astype(BF16)
    c = c_ref[...]
    s = s_ref[...]
    hg = hg_ref[...]
    kr = kr_ref[...]
    kr_ss = jnp.sum(kr * kr, axis=-1, keepdims=True)
    for hp in range(MLA_HEADS // 2):
        y = jnp.dot(xn, wk_ref[:, hp * 256:(hp + 1) * 256], preferred_element_type=F32)
        for sub in range(2):
            lo = (2 * hp + sub) * MLA_HEAD_PAD
            kn = y[:, sub * MLA_NOPE:(sub + 1) * MLA_NOPE]
            ms = (jnp.sum(kn * kn, axis=-1, keepdims=True) + kr_ss) * (1.0 / MLA_QK)
            rs = lax.rsqrt(ms + EPS)
            k_ref[:, lo:lo + 128] = (kn * rs * hg[:, :128]).astype(k_ref.dtype)
            k_ref[:, lo + 128:lo + 256] = _rope(kr * rs * hg[:, 128:], c, s).astype(k_ref.dtype)
    rows_per_dot = 512
    for j in range(MLA_HEADS * MLA_V // rows_per_dot):
        sl = slice(j * rows_per_dot, (j + 1) * rows_per_dot)
        vt_ref[sl, :] = lax.dot_general(wvt_ref[sl, :], xn, _NT,
                                        preferred_element_type=F32).astype(vt_ref.dtype)


def _kvproj(z, zs, ng, wk, wvt, hg, ctab, stab, bm):
    m = z.shape[0]
    nt = ctab.shape[0] // bm
    return pl.pallas_call(
        _kvproj_kernel,
        grid=(m // bm,),
        in_specs=[pl.BlockSpec((bm, MLA_LORA), lambda i: (i, Z_CKV // MLA_LORA)),
                  pl.BlockSpec((bm, 128), lambda i: (i, 1)),
                  pl.BlockSpec((1, MLA_LORA), lambda i: (0, 0)),
                  pl.BlockSpec((MLA_LORA, MLA_HEADS * MLA_NOPE), lambda i: (0, 0)),
                  pl.BlockSpec((MLA_HEADS * MLA_V, MLA_LORA), lambda i: (0, 0)),
                  pl.BlockSpec((1, MLA_HEAD_PAD), lambda i: (0, 0)),
                  pl.BlockSpec((bm, 128), lambda i: (i % nt, 0)),
                  pl.BlockSpec((bm, 128), lambda i: (i % nt, 0))],
        out_specs=[pl.BlockSpec((bm, MLA_HEADS * MLA_HEAD_PAD), lambda i: (i, 0)),
                   pl.BlockSpec((MLA_HEADS * MLA_V, bm), lambda i: (0, i))],
        out_shape=[jax.ShapeDtypeStruct((m, MLA_HEADS * MLA_HEAD_PAD), BF16),
                   jax.ShapeDtypeStruct((MLA_HEADS * MLA_V, m), BF16)],
        compiler_params=_params("parallel"),
        name="kv_proj",
    )(z, zs, ng.reshape(1, -1), wk, wvt, hg, ctab, stab)


def _attn_kernel(q_ref, k_ref, vt_ref, km_ref, vmt_ref, o_ref, m_sc, l_sc, acc_sc, *, blk):
    i = pl.program_id(2)
    q = q_ref[...]
    s = lax.dot_general(km_ref[...], q, _NT, preferred_element_type=F32)
    key = lax.broadcasted_iota(jnp.int32, s.shape, 0)
    s = jnp.where(key < N_META, s, -1e30)
    m = jnp.max(s, axis=0, keepdims=True)
    p = jnp.exp2(s - m)
    m_sc[...] = m
    l_sc[...] = jnp.sum(p, axis=0, keepdims=True)
    acc_sc[...] = jnp.dot(vmt_ref[...], p.astype(BF16), preferred_element_type=F32)

    def step(kb, masked):
        sl = pl.ds(pl.multiple_of(kb * blk, blk), blk)
        s = lax.dot_general(k_ref[sl, :], q, _NT, preferred_element_type=F32)
        if masked:
            key = lax.broadcasted_iota(jnp.int32, s.shape, 0)
            qry = lax.broadcasted_iota(jnp.int32, s.shape, 1)
            s = jnp.where(key <= qry, s, -1e30)
        m_prev = m_sc[...]
        m_new = jnp.maximum(m_prev, jnp.max(s, axis=0, keepdims=True))
        alpha = jnp.exp2(m_prev - m_new)
        p = jnp.exp2(s - m_new)
        l_sc[...] = alpha * l_sc[...] + jnp.sum(p, axis=0, keepdims=True)
        acc_sc[...] = alpha * acc_sc[...] + jnp.dot(vt_ref[:, sl], p.astype(BF16),
                                                    preferred_element_type=F32)
        m_sc[...] = m_new

    def body(kb, carry):
        step(kb, False)
        return carry

    lax.fori_loop(0, i, body, 0)
    step(i, True)
    o_ref[...] = (acc_sc[...] * (1.0 / l_sc[...])).T.astype(o_ref.dtype)


def _attention(q, k, vt, kmp, vmtp, blk):
    nq = SEQ // blk
    kern = functools.partial(_attn_kernel, blk=blk)
    return pl.pallas_call(
        kern,
        grid=(BATCH, MLA_HEADS, nq),
        in_specs=[pl.BlockSpec((blk, MLA_HEAD_PAD), lambda b, h, i: (b * nq + i, h)),
                  pl.BlockSpec((SEQ, MLA_HEAD_PAD), lambda b, h, i: (b, h)),
                  pl.BlockSpec((MLA_V, SEQ), lambda b, h, i: (h, b)),
                  pl.BlockSpec((128, MLA_HEAD_PAD), lambda b, h, i: (0, h)),
                  pl.BlockSpec((MLA_V, 128), lambda b, h, i: (h, 0))],
        out_specs=pl.BlockSpec((blk, MLA_V), lambda b, h, i: (b * nq + i, h)),
        out_shape=jax.ShapeDtypeStruct((ROWS, MLA_HEADS * MLA_V), BF16),
        scratch_shapes=[pltpu.VMEM((1, blk), F32), pltpu.VMEM((1, blk), F32),
                        pltpu.VMEM((MLA_V, blk), F32)],
        compiler_params=_params("parallel", "parallel", "arbitrary"),
        name="mla_attention",
    )(q, k, vt, kmp, vmtp)


def _meta_attn_kernel(q_ref, km_ref, vmt_ref, o_ref):
    s = lax.dot_general(q_ref[...], km_ref[...], _NT, preferred_element_type=F32)
    row = lax.broadcasted_iota(jnp.int32, s.shape, 0)
    col = lax.broadcasted_iota(jnp.int32, s.shape, 1)
    s = jnp.where(col <= row, s, -1e30)
    m = jnp.max(s, axis=-1, keepdims=True)
    p = jnp.exp2(s - m)
    l = jnp.sum(p, axis=-1, keepdims=True)
    o = lax.dot_general(p.astype(BF16), vmt_ref[...], _NT, preferred_element_type=F32)
    o_ref[...] = (o / l).astype(o_ref.dtype)


def _meta_attention(qm, kmp, vmtp):
    return pl.pallas_call(
        _meta_attn_kernel,
        grid=(MLA_HEADS,),
        in_specs=[pl.BlockSpec((N_META, MLA_HEAD_PAD), lambda h: (0, h)),
                  pl.BlockSpec((128, MLA_HEAD_PAD), lambda h: (0, h)),
                  pl.BlockSpec((MLA_V, 128), lambda h: (h, 0))],
        out_specs=pl.BlockSpec((N_META, MLA_V), lambda h: (0, h)),
        out_shape=jax.ShapeDtypeStruct((N_META, MLA_HEADS * MLA_V), BF16),
        compiler_params=_params("parallel"),
        name="meta_attention",
    )(qm, kmp, vmtp)


def _merge_kernel(a_ref, b_ref, wa_ref, wb_ref, za_ref, zb_ref, o_ref):
    ya = jnp.dot(a_ref[...], wa_ref[...], preferred_element_type=F32)
    yb = jnp.dot(b_ref[...], wb_ref[...], preferred_element_type=F32)
    o_ref[...] = (_sigmoid(za_ref[...]) * ya + _sigmoid(zb_ref[...]) * yb).astype(o_ref.dtype)


def _merge(a, b, wa, wb, z, bm, bn):
    m = a.shape[0]
    return pl.pallas_call(
        _merge_kernel,
        grid=(m // bm, D_MODEL // bn),
        in_specs=[pl.BlockSpec((bm, D_MODEL), lambda i, j: (i, 0)),
                  pl.BlockSpec((bm, D_MODEL), lambda i, j: (i, 0)),
                  pl.BlockSpec((D_MODEL, bn), lambda i, j: (0, j)),
                  pl.BlockSpec((D_MODEL, bn), lambda i, j: (0, j)),
                  pl.BlockSpec((bm, bn), lambda i, j: (i, Z_A // bn + j)),
                  pl.BlockSpec((bm, bn), lambda i, j: (i, Z_B // bn + j))],
        out_specs=pl.BlockSpec((bm, bn), lambda i, j: (i, j)),
        out_shape=jax.ShapeDtypeStruct((m, D_MODEL), BF16),
        compiler_params=_params("parallel", "parallel"),
        name="branch_merge",
    )(a, b, wa, wb, z, z)


def _outproj_kernel(m_ref, w_ref, h_ref, g_ref, ho_ref, hn_ref):
    hn = h_ref[...] + jnp.dot(m_ref[...], w_ref[...], preferred_element_type=F32)
    ho_ref[...] = hn
    ms = jnp.mean(hn * hn, axis=-1, keepdims=True)
    hn_ref[...] = (hn * lax.rsqrt(ms + EPS) * g_ref[...]).astype(hn_ref.dtype)


def _outproj(mg, w, h, g, bm):
    m = mg.shape[0]
    return pl.pallas_call(
        _outproj_kernel,
        grid=(m // bm,),
        in_specs=[pl.BlockSpec((bm, D_MODEL), lambda i: (i, 0)),
                  pl.BlockSpec((D_MODEL, D_MODEL), lambda i: (0, 0)),
                  pl.BlockSpec((bm, D_MODEL), lambda i: (i, 0)),
                  pl.BlockSpec((1, D_MODEL), lambda i: (0, 0))],
        out_specs=[pl.BlockSpec((bm, D_MODEL), lambda i: (i, 0)),
                   pl.BlockSpec((bm, D_MODEL), lambda i: (i, 0))],
        out_shape=[jax.ShapeDtypeStruct((m, D_MODEL), F32),
                   jax.ShapeDtypeStruct((m, D_MODEL), BF16)],
        compiler_params=_params("parallel"),
        name="out_proj",
    )(mg, w, h, g.reshape(1, -1))


def _gateup_kernel(h_ref, wg_ref, wu_ref, o_ref):
    h = h_ref[...]
    g = jnp.dot(h, wg_ref[...], preferred_element_type=F32)
    u = jnp.dot(h, wu_ref[...], preferred_element_type=F32)
    o_ref[...] = (g * _sigmoid(g) * u).astype(o_ref.dtype)


def _gateup(h, w, bm, bn):
    m = h.shape[0]
    nb = FF_HIDDEN // bn
    return pl.pallas_call(
        _gateup_kernel,
        grid=(m // bm, nb),
        in_specs=[pl.BlockSpec((bm, D_MODEL), lambda i, j: (i, 0)),
                  pl.BlockSpec((D_MODEL, bn), lambda i, j: (0, j)),
                  pl.BlockSpec((D_MODEL, bn), lambda i, j: (0, nb + j))],
        out_specs=pl.BlockSpec((bm, bn), lambda i, j: (i, j)),
        out_shape=jax.ShapeDtypeStruct((m, FF_HIDDEN), BF16),
        compiler_params=_params("parallel", "parallel"),
        name="gate_up",
    )(h, w, w)


def _down_kernel(a_ref, w_ref, h_ref, g_ref, ho_ref, hn_ref, acc_ref):
    k = pl.program_id(1)

    @pl.when(k == 0)
    def _():
        acc_ref[...] = jnp.zeros_like(acc_ref)

    acc_ref[...] += jnp.dot(a_ref[...], w_ref[...], preferred_element_type=F32)

    @pl.when(k == pl.num_programs(1) - 1)
    def _():
        hn = h_ref[...] + acc_ref[...]
        ho_ref[...] = hn
        ms = jnp.mean(hn * hn, axis=-1, keepdims=True)
        hn_ref[...] = (hn * lax.rsqrt(ms + EPS) * g_ref[...]).astype(hn_ref.dtype)


def _down(a, w, h, g, bm, bk):
    m = a.shape[0]
    return pl.pallas_call(
        _down_kernel,
        grid=(m // bm, FF_HIDDEN // bk),
        in_specs=[pl.BlockSpec((bm, bk), lambda i, k: (i, k)),
                  pl.BlockSpec((bk, D_MODEL), lambda i, k: (k, 0)),
                  pl.BlockSpec((bm, D_MODEL), lambda i, k: (i, 0)),
                  pl.BlockSpec((1, D_MODEL), lambda i, k: (0, 0))],
        out_specs=[pl.BlockSpec((bm, D_MODEL), lambda i, k: (i, 0)),
                   pl.BlockSpec((bm, D_MODEL), lambda i, k: (i, 0))],
        out_shape=[jax.ShapeDtypeStruct((m, D_MODEL), F32),
                   jax.ShapeDtypeStruct((m, D_MODEL), BF16)],
        scratch_shapes=[pltpu.VMEM((bm, D_MODEL), F32)],
        compiler_params=_params("parallel", "arbitrary"),
        name="down_proj",
    )(a, w, h, g.reshape(1, -1))


def _rope_layout(t):
    zeros = jnp.zeros(t.shape[:-1] + (32,), t.dtype)
    return jnp.concatenate([t[..., :32], zeros, t[..., 32:], zeros], axis=-1)


def _head_layout(t):
    return jnp.concatenate([t[..., :MLA_NOPE], _rope_layout(t[..., MLA_NOPE:])], axis=-1)


def _rope_tables():
    length = N_META + SEQ
    inv = 1.0 / (ROPE_THETA ** (jnp.arange(0, MLA_ROPE, 2, dtype=F32) / MLA_ROPE))
    ang = jnp.arange(length, dtype=F32)[:, None] * inv[None, :]
    cos, sin = jnp.cos(ang), jnp.sin(ang)
    zeros = jnp.zeros_like(cos)
    ctab = jnp.concatenate([cos, zeros, cos, zeros], axis=-1)
    stab = jnp.concatenate([-sin, zeros, sin, zeros], axis=-1)
    return ctab, stab


def kernel(x, meta_tokens, norm1_g, w_in, gla_gate_w2, gla_gate_b, gla_onorm_g, w_branch_a,
           q_a_norm_g, w_uq, kv_a_norm_g, w_ukv, q_norm_g, k_norm_g, w_branch_b, w_out,
           norm2_g, w_gate_up, w_down):
    ctab, stab = _rope_tables()
    ctab_m, stab_m = ctab[:N_META], stab[:N_META]
    ctab_r, stab_r = ctab[N_META:], stab[N_META:]

    h_res = x.reshape(ROWS, D_MODEL)
    hm_res = meta_tokens.astype(F32)
    h = _rms(h_res, norm1_g[0], 512)
    hm = _rms(hm_res, norm1_g[0], N_META)
    s_zero = jnp.zeros((GLA_HEADS, GLA_DV, GLA_DK), F32)

    for l in range(DEPTH):
        last = l == DEPTH - 1
        wl = w_in[l]
        w_main = jnp.concatenate([wl[:, :6144], wl[:, 6160:7184], wl[:, 7248:]], axis=1).astype(BF16)
        w_zlr = wl[:, 6144:6160]
        w_zkr = wl[:, 7184:7248]
        w_small = jnp.concatenate(
            [w_zlr, jnp.zeros((D_MODEL, 128 - GLA_GATE_RANK), F32), _rope_layout(w_zkr)],
            axis=1).astype(BF16)
        w2p = jnp.concatenate(
            [gla_gate_w2[l], jnp.zeros((128 - GLA_GATE_RANK, GLA_HEADS * GLA_DK), F32)],
            axis=0).astype(BF16)
        wq = _head_layout(w_uq[l].reshape(MLA_LORA, MLA_HEADS, MLA_QK)).reshape(
            MLA_LORA, MLA_HEADS * MLA_HEAD_PAD).astype(BF16)
        wkv3 = w_ukv[l].reshape(MLA_LORA, MLA_HEADS, MLA_NOPE + MLA_V)
        wk = wkv3[:, :, :MLA_NOPE].reshape(MLA_LORA, MLA_HEADS * MLA_NOPE).astype(BF16)
        wvt = wkv3[:, :, MLA_NOPE:].reshape(MLA_LORA, MLA_HEADS * MLA_V).T.astype(BF16)
        qg = _head_layout(q_norm_g[l]).reshape(1, MLA_HEAD_PAD)
        kg = _head_layout(k_norm_g[l]).reshape(1, MLA_HEAD_PAD)
        wa = w_branch_a[l].astype(BF16)
        wb = w_branch_b[l].astype(BF16)
        wo = w_out[l].astype(BF16)
        wgu = w_gate_up[l].astype(BF16)
        wd = w_down[l].astype(BF16)
        g_next = norm1_g[l + 1] if not last else norm1_g[l]

        z = _matmul(h, w_main, 1024, 1024, name="in_proj")
        zs = _matmul(h, w_small, 1024, N_SMALL, name="in_proj_small")
        zm = _matmul(hm, w_main, N_META, 1024, name="in_proj_meta")
        zsm = _matmul(hm, w_small, N_META, N_SMALL, name="in_proj_small_meta")

        gla_m, s_meta = _gla(zm, zsm, w2p, gla_gate_b[l], gla_onorm_g[l], s_zero,
                             batch=1, tokens=N_META, block=N_META, chunk=N_META, emit_state=True)
        (gla_r,) = _gla(z, zs, w2p, gla_gate_b[l], gla_onorm_g[l], s_meta,
                        batch=BATCH, tokens=SEQ, block=512, chunk=GLA_CHUNK, emit_state=False)

        km, vmt = _kvproj(zm, zsm, kv_a_norm_g[l], wk, wvt, kg, ctab_m, stab_m, N_META)
        kr, vrt = _kvproj(z, zs, kv_a_norm_g[l], wk, wvt, kg, ctab_r, stab_r, 512)
        kmp = jnp.pad(km, ((0, 128 - N_META), (0, 0)))
        vmtp = jnp.pad(vmt, ((0, 0), (0, 128 - N_META)))
        qr = _qproj(z, q_a_norm_g[l], wq, qg, ctab_r, stab_r, 512)
        att_r = _attention(qr, kr, vrt, kmp, vmtp, 512)

        merged = _merge(gla_r, att_r, wa, wb, z, 1024, 512)
        h_res, h2 = _outproj(merged, wo, h_res, norm2_g[l], 512)
        act = _gateup(h2, wgu, 1024, 512)
        h_res, h = _down(act, wd, h_res, g_next, 512, 1408)

        if not last:
            qm = _qproj(zm, q_a_norm_g[l], wq, qg, ctab_m, stab_m, N_META)
            att_m = _meta_attention(qm, kmp, vmtp)
            merged_m = _merge(gla_m, att_m, wa, wb, zm, N_META, 512)
            hm_res, hm2 = _outproj(merged_m, wo, hm_res, norm2_g[l], N_META)
            act_m = _gateup(hm2, wgu, N_META, 512)
            hm_res, hm = _down(act_m, wd, hm_res, g_next, N_META, 1408)

    return h_res.reshape(BATCH, SEQ, D_MODEL)
```

```python
import functools

import jax
import jax.numpy as jnp
from jax import lax
from jax.experimental import pallas as pl
from jax.experimental.pallas import tpu as pltpu

D_MODEL = 2048
BATCH = 4
SEQ = 4096
DEPTH = 2
N_META = 16
ROWS = BATCH * SEQ

GLA_HEADS = 4
GLA_DK = 256
GLA_DV = 512
GLA_GATE_RANK = 16
GLA_GATE_TAU = 16.0
GLA_CHUNK = 64

MLA_HEADS = 16
MLA_LORA = 512
MLA_NOPE = 128
MLA_ROPE = 64
MLA_QK = 192
MLA_V = 128
MLA_HEAD_PAD = 256
ROPE_THETA = 10000.0
FF_HIDDEN = 5632
EPS = 1e-6
LOG2E = 1.4426950408889634

Z_Q, Z_K, Z_V, Z_R, Z_CQ, Z_CKV, Z_A, Z_B = 0, 1024, 2048, 4096, 6144, 6656, 7168, 9216
N_MAIN = 11264
N_SMALL = 256

VMEM_LIMIT = 56 * 1024 * 1024
BF16 = jnp.bfloat16
F32 = jnp.float32

_NT = (((1,), (1,)), ((), ()))
_TN = (((0,), (0,)), ((), ()))


def _params(*sem):
    return pltpu.CompilerParams(dimension_semantics=sem, vmem_limit_bytes=VMEM_LIMIT)


def _sigmoid(x):
    return 1.0 / (1.0 + jnp.exp(-x))


def _rms_kernel(x_ref, g_ref, o_ref):
    x = x_ref[...]
    ms = jnp.mean(x * x, axis=-1, keepdims=True)
    o_ref[...] = (x * lax.rsqrt(ms + EPS) * g_ref[...]).astype(o_ref.dtype)


def _rms(x, g, bm):
    m, d = x.shape
    return pl.pallas_call(
        _rms_kernel,
        grid=(m // bm,),
        in_specs=[pl.BlockSpec((bm, d), lambda i: (i, 0)),
                  pl.BlockSpec((1, d), lambda i: (0, 0))],
        out_specs=pl.BlockSpec((bm, d), lambda i: (i, 0)),
        out_shape=jax.ShapeDtypeStruct((m, d), BF16),
        compiler_params=_params("parallel"),
        name="rmsnorm",
    )(x, g.reshape(1, d))


def _mm_kernel(a_ref, w_ref, o_ref):
    o_ref[...] = jnp.dot(a_ref[...], w_ref[...], preferred_element_type=F32).astype(o_ref.dtype)


def _matmul(a, w, bm, bn, out_dtype=F32, name="matmul"):
    m, k = a.shape
    n = w.shape[1]
    return pl.pallas_call(
        _mm_kernel,
        grid=(m // bm, n // bn),
        in_specs=[pl.BlockSpec((bm, k), lambda i, j: (i, 0)),
                  pl.BlockSpec((k, bn), lambda i, j: (0, j))],
        out_specs=pl.BlockSpec((bm, bn), lambda i, j: (i, j)),
        out_shape=jax.ShapeDtypeStruct((m, n), out_dtype),
        compiler_params=_params("parallel", "parallel"),
        name=name,
    )(a, w)


def _gla_kernel(q_ref, k_ref, v_ref, r_ref, zlr_ref, w2_ref, gb_ref, og_ref, s0_ref,
                o_ref, *rest, chunk, n_chunks, emit_state):
    if emit_state:
        sfin_ref, st_ref = rest
    else:
        (st_ref,) = rest
    t = pl.program_id(1)
    heads = range(GLA_HEADS)

    @pl.when(t == 0)
    def _():
        st_ref[...] = s0_ref[...]

    row = lax.broadcasted_iota(jnp.int32, (chunk, chunk), 0)
    col = lax.broadcasted_iota(jnp.int32, (chunk, chunk), 1)
    causal = col <= row
    tri = jnp.where(causal, 1.0, 0.0).astype(BF16)
    w2 = w2_ref[...]
    gb = gb_ref[...]
    og = og_ref[...]
    kdim = GLA_HEADS * GLA_DK

    def body(c, carry):
        sl = pl.ds(pl.multiple_of(c * chunk, chunk), chunk)
        logit = jnp.dot(zlr_ref[sl, :].astype(BF16), w2, preferred_element_type=F32) + gb
        g = (jnp.minimum(logit, 0.0) - jnp.log1p(jnp.exp(-jnp.abs(logit)))) * (1.0 / GLA_GATE_TAU)
        g_hi = g.astype(BF16)
        g_lo = (g - g_hi.astype(F32)).astype(BF16)
        cs = jnp.dot(tri, jnp.concatenate([g_hi, g_lo], axis=1), preferred_element_type=F32)
        b = cs[:, :kdim] + cs[:, kdim:]
        b_last = b[chunk - 1:chunk, :]
        eb = jnp.exp(b)
        enb = jnp.exp(-b)
        erel = jnp.exp(b_last - b)
        elast = jnp.exp(b_last)
        k = k_ref[sl, :]
        qd = (q_ref[sl, :] * (GLA_DK ** -0.5) * eb).astype(BF16)
        kd = (k * enb).astype(BF16)
        k2 = (k * erel).astype(BF16)
        v = v_ref[sl, :].astype(BF16)

        def hk(x, h):
            return x[:, h * GLA_DK:(h + 1) * GLA_DK]

        def hv(x, h):
            return x[:, h * GLA_DV:(h + 1) * GLA_DV]

        a = [lax.dot_general(hk(qd, h), hk(kd, h), _NT, preferred_element_type=F32) for h in heads]
        st = [st_ref[h] for h in heads]
        o_state = [lax.dot_general(hk(qd, h), st[h].astype(BF16), _NT, preferred_element_type=F32)
                   for h in heads]
        o_local = [jnp.dot(jnp.where(causal, a[h], 0.0).astype(BF16), hv(v, h),
                           preferred_element_type=F32) for h in heads]
        upd = [lax.dot_general(hv(v, h), hk(k2, h), _TN, preferred_element_type=F32)
               for h in heads]
        for h in heads:
            st_ref[h] = st[h] * hk(elast, h) + upd[h]
        for h in heads:
            o = o_local[h] + o_state[h]
            ms = jnp.mean(o * o, axis=-1, keepdims=True)
            on = o * lax.rsqrt(ms + EPS) * og
            r = r_ref[sl, h * GLA_DV:(h + 1) * GLA_DV]
            o_ref[sl, h * GLA_DV:(h + 1) * GLA_DV] = (on * (r * _sigmoid(r))).astype(o_ref.dtype)
        return carry

    lax.fori_loop(0, n_chunks, body, 0)

    if emit_state:
        @pl.when(t == pl.num_programs(1) - 1)
        def _():
            sfin_ref[...] = st_ref[...]


def _gla(z, zs, w2p, gate_b, onorm_g, s0, *, batch, tokens, block, chunk, emit_state):
    nt = tokens // block
    rows = batch * tokens
    kern = functools.partial(_gla_kernel, chunk=chunk, n_chunks=block // chunk,
                             emit_state=emit_state)
    kdim = GLA_HEADS * GLA_DK
    vdim = GLA_HEADS * GLA_DV
    in_specs = [
        pl.BlockSpec((block, kdim), lambda b, t: (b * nt + t, Z_Q // kdim)),
        pl.BlockSpec((block, kdim), lambda b, t: (b * nt + t, Z_K // kdim)),
        pl.BlockSpec((block, vdim), lambda b, t: (b * nt + t, Z_V // vdim)),
        pl.BlockSpec((block, vdim), lambda b, t: (b * nt + t, Z_R // vdim)),
        pl.BlockSpec((block, 128), lambda b, t: (b * nt + t, 0)),
        pl.BlockSpec((128, kdim), lambda b, t: (0, 0)),
        pl.BlockSpec((1, kdim), lambda b, t: (0, 0)),
        pl.BlockSpec((1, GLA_DV), lambda b, t: (0, 0)),
        pl.BlockSpec((GLA_HEADS, GLA_DV, GLA_DK), lambda b, t: (0, 0, 0)),
    ]
    out_specs = [pl.BlockSpec((block, vdim), lambda b, t: (b * nt + t, 0))]
    out_shape = [jax.ShapeDtypeStruct((rows, vdim), BF16)]
    if emit_state:
        out_specs.append(pl.BlockSpec((GLA_HEADS, GLA_DV, GLA_DK), lambda b, t: (b, 0, 0)))
        out_shape.append(jax.ShapeDtypeStruct((batch * GLA_HEADS, GLA_DV, GLA_DK), F32))
    res = pl.pallas_call(
        kern,
        grid=(batch, nt),
        in_specs=in_specs,
        out_specs=out_specs,
        out_shape=out_shape,
        scratch_shapes=[pltpu.VMEM((GLA_HEADS, GLA_DV, GLA_DK), F32)],
        compiler_params=_params("parallel", "arbitrary"),
        name="gla",
    )(z, z, z, z, zs, w2p, gate_b.reshape(1, -1), onorm_g.reshape(1, -1), s0)
    return res


def _rope(x, c, s):
    return x * c + pltpu.roll(x, 64, 1) * s


def _qproj_kernel(zc_ref, ng_ref, w_ref, hg_ref, c_ref, s_ref, o_ref):
    x = zc_ref[...]
    ms = jnp.mean(x * x, axis=-1, keepdims=True)
    xn = (x * lax.rsqrt(ms + EPS) * ng_ref[...]).astype(BF16)
    c = c_ref[...]
    s = s_ref[...]
    hg = hg_ref[...]
    scale = MLA_QK ** -0.5 * LOG2E
    for h in range(MLA_HEADS):
        lo = h * MLA_HEAD_PAD
        y = jnp.dot(xn, w_ref[:, lo:lo + MLA_HEAD_PAD], preferred_element_type=F32)
        ms = jnp.sum(y * y, axis=-1, keepdims=True) * (1.0 / MLA_QK)
        yn = y * lax.rsqrt(ms + EPS) * hg
        o_ref[:, lo:lo + 128] = (yn[:, :128] * scale).astype(o_ref.dtype)
        o_ref[:, lo + 128:lo + 256] = (_rope(yn[:, 128:], c, s) * scale).astype(o_ref.dtype)


def _qproj(z, ng, w, hg, ctab, stab, bm):
    m = z.shape[0]
    nt = ctab.shape[0] // bm
    return pl.pallas_call(
        _qproj_kernel,
        grid=(m // bm,),
        in_specs=[pl.BlockSpec((bm, MLA_LORA), lambda i: (i, Z_CQ // MLA_LORA)),
                  pl.BlockSpec((1, MLA_LORA), lambda i: (0, 0)),
                  pl.BlockSpec((MLA_LORA, MLA_HEADS * MLA_HEAD_PAD), lambda i: (0, 0)),
                  pl.BlockSpec((1, MLA_HEAD_PAD), lambda i: (0, 0)),
                  pl.BlockSpec((bm, 128), lambda i: (i % nt, 0)),
                  pl.BlockSpec((bm, 128), lambda i: (i % nt, 0))],
        out_specs=pl.BlockSpec((bm, MLA_HEADS * MLA_HEAD_PAD), lambda i: (i, 0)),
        out_shape=jax.ShapeDtypeStruct((m, MLA_HEADS * MLA_HEAD_PAD), BF16),
        compiler_params=_params("parallel"),
        name="q_proj",
    )(z, ng.reshape(1, -1), w, hg, ctab, stab)


def _kvproj_kernel(zc_ref, kr_ref, ng_ref, wk_ref, wvt_ref, hg_ref, c_ref, s_ref, k_ref, vt_ref):
    x = zc_ref[...]
    ms = jnp.mean(x * x, axis=-1, keepdims=True)
    xn = (x * lax.rsqrt(ms + EPS) * ng_ref[...]).astype(BF16)
    c = c_ref[...]
    s = s_ref[...]
    hg = hg_ref[...]
    kr = kr_ref[...]
    kr_ss = jnp.sum(kr * kr, axis=-1, keepdims=True)
    for hp in range(MLA_HEADS // 2):
        y = jnp.dot(xn, wk_ref[:, hp * 256:(hp + 1) * 256], preferred_element_type=F32)
        for sub in range(2):
            lo = (2 * hp + sub) * MLA_HEAD_PAD
            kn = y[:, sub * MLA_NOPE:(sub + 1) * MLA_NOPE]
            ms = (jnp.sum(kn * kn, axis=-1, keepdims=True) + kr_ss) * (1.0 / MLA_QK)
            rs = lax.rsqrt(ms + EPS)
            k_ref[:, lo:lo + 128] = (kn * rs * hg[:, :128]).astype(k_ref.dtype)
            k_ref[:, lo + 128:lo + 256] = _rope(kr * rs * hg[:, 128:], c, s).astype(k_ref.dtype)
    rows_per_dot = 512
    for j in range(MLA_HEADS * MLA_V // rows_per_dot):
        sl = slice(j * rows_per_dot, (j + 1) * rows_per_dot)
        vt_ref[sl, :] = lax.dot_general(wvt_ref[sl, :], xn, _NT,
                                        preferred_element_type=F32).astype(vt_ref.dtype)


def _kvproj(z, zs, ng, wk, wvt, hg, ctab, stab, bm):
    m = z.shape[0]
    nt = ctab.shape[0] // bm
    return pl.pallas_call(
        _kvproj_kernel,
        grid=(m // bm,),
        in_specs=[pl.BlockSpec((bm, MLA_LORA), lambda i: (i, Z_CKV // MLA_LORA)),
                  pl.BlockSpec((bm, 128), lambda i: (i, 1)),
                  pl.BlockSpec((1, MLA_LORA), lambda i: (0, 0)),
                  pl.BlockSpec((MLA_LORA, MLA_HEADS * MLA_NOPE), lambda i: (0, 0)),
                  pl.BlockSpec((MLA_HEADS * MLA_V, MLA_LORA), lambda i: (0, 0)),
                  pl.BlockSpec((1, MLA_HEAD_PAD), lambda i: (0, 0)),
                  pl.BlockSpec((bm, 128), lambda i: (i % nt, 0)),
                  pl.BlockSpec((bm, 128), lambda i: (i % nt, 0))],
        out_specs=[pl.BlockSpec((bm, MLA_HEADS * MLA_HEAD_PAD), lambda i: (i, 0)),
                   pl.BlockSpec((MLA_HEADS * MLA_V, bm), lambda i: (0, i))],
        out_shape=[jax.ShapeDtypeStruct((m, MLA_HEADS * MLA_HEAD_PAD), BF16),
                   jax.ShapeDtypeStruct((MLA_HEADS * MLA_V, m), BF16)],
        compiler_params=_params("parallel"),
        name="kv_proj",
    )(z, zs, ng.reshape(1, -1), wk, wvt, hg, ctab, stab)


def _attn_kernel(q_ref, k_ref, vt_ref, km_ref, vmt_ref, o_ref, m_sc, l_sc, acc_sc, *, blk, heads,
                 sub, ahead):
    i = pl.program_id(2)
    qs = [q_ref[:, h * MLA_HEAD_PAD:(h + 1) * MLA_HEAD_PAD] for h in range(heads)]

    for h in range(heads):
        s = lax.dot_general(km_ref[:, h * MLA_HEAD_PAD:(h + 1) * MLA_HEAD_PAD], qs[h], _NT,
                            preferred_element_type=F32)
        key = lax.broadcasted_iota(jnp.int32, s.shape, 0)
        s = jnp.where(key < N_META, s, -1e30)
        m = jnp.max(s, axis=0, keepdims=True)
        p = jnp.exp2(s - m)
        m_sc[h] = m
        l_sc[h] = jnp.sum(p, axis=0, keepdims=True)
        acc_sc[h] = jnp.dot(vmt_ref[h * MLA_V:(h + 1) * MLA_V, :], p.astype(BF16),
                            preferred_element_type=F32)

    n_sub = blk // sub
    tasks = [(j, h) for j in range(n_sub) for h in range(heads)]

    def step(kb, masked):
        base = pl.multiple_of(kb * blk, blk)

        def scores(j, h):
            return lax.dot_general(k_ref[pl.ds(base + j * sub, sub),
                                         h * MLA_HEAD_PAD:(h + 1) * MLA_HEAD_PAD], qs[h], _NT,
                                   preferred_element_type=F32)

        m = [m_sc[h] for h in range(heads)]
        l = [l_sc[h] for h in range(heads)]
        acc = [acc_sc[h] for h in range(heads)]
        pending = [scores(*t) for t in tasks[:ahead]]
        for n, (j, h) in enumerate(tasks):
            s = pending.pop(0)
            if n + ahead < len(tasks):
                pending.append(scores(*tasks[n + ahead]))
            if masked:
                key = lax.broadcasted_iota(jnp.int32, s.shape, 0) + j * sub
                qry = lax.broadcasted_iota(jnp.int32, s.shape, 1)
                s = jnp.where(key <= qry, s, -1e30)
            m_new = jnp.maximum(m[h], jnp.max(s, axis=0, keepdims=True))
            alpha = jnp.exp2(m[h] - m_new)
            p = jnp.exp2(s - m_new)
            l[h] = alpha * l[h] + jnp.sum(p, axis=0, keepdims=True)
            acc[h] = alpha * acc[h] + jnp.dot(
                vt_ref[h * MLA_V:(h + 1) * MLA_V, pl.ds(base + j * sub, sub)],
                p.astype(BF16), preferred_element_type=F32)
            m[h] = m_new
        for h in range(heads):
            m_sc[h] = m[h]
            l_sc[h] = l[h]
            acc_sc[h] = acc[h]

    def body(kb, carry):
        step(kb, False)
        return carry

    lax.fori_loop(0, i, body, 0)
    step(i, True)
    for h in range(heads):
        o_ref[:, h * MLA_V:(h + 1) * MLA_V] = (acc_sc[h] * (1.0 / l_sc[h])).T.astype(o_ref.dtype)


def _attention(q, k, vt, kmp, vmtp, blk, heads, sub, ahead):
    nq = SEQ // blk
    kern = functools.partial(_attn_kernel, blk=blk, heads=heads, sub=sub, ahead=ahead)
    return pl.pallas_call(
        kern,
        grid=(BATCH, MLA_HEADS // heads, nq),
        in_specs=[pl.BlockSpec((blk, heads * MLA_HEAD_PAD), lambda b, h, i: (b * nq + i, h)),
                  pl.BlockSpec((SEQ, heads * MLA_HEAD_PAD), lambda b, h, i: (b, h)),
                  pl.BlockSpec((heads * MLA_V, SEQ), lambda b, h, i: (h, b)),
                  pl.BlockSpec((128, heads * MLA_HEAD_PAD), lambda b, h, i: (0, h)),
                  pl.BlockSpec((heads * MLA_V, 128), lambda b, h, i: (h, 0))],
        out_specs=pl.BlockSpec((blk, heads * MLA_V), lambda b, h, i: (b * nq + i, h)),
        out_shape=jax.ShapeDtypeStruct((ROWS, MLA_HEADS * MLA_V), BF16),
        scratch_shapes=[pltpu.VMEM((heads, 1, blk), F32), pltpu.VMEM((heads, 1, blk), F32),
                        pltpu.VMEM((heads, MLA_V, blk), F32)],
        compiler_params=_params("parallel", "parallel", "arbitrary"),
        name="mla_attention",
    )(q, k, vt, kmp, vmtp)


def _meta_attn_kernel(q_ref, km_ref, vmt_ref, o_ref):
    s = lax.dot_general(q_ref[...], km_ref[...], _NT, preferred_element_type=F32)
    row = lax.broadcasted_iota(jnp.int32, s.shape, 0)
    col = lax.broadcasted_iota(jnp.int32, s.shape, 1)
    s = jnp.where(col <= row, s, -1e30)
    m = jnp.max(s, axis=-1, keepdims=True)
    p = jnp.exp2(s - m)
    l = jnp.sum(p, axis=-1, keepdims=True)
    o = lax.dot_general(p.astype(BF16), vmt_ref[...], _NT, preferred_element_type=F32)
    o_ref[...] = (o / l).astype(o_ref.dtype)


def _meta_attention(qm, kmp, vmtp):
    return pl.pallas_call(
        _meta_attn_kernel,
        grid=(MLA_HEADS,),
        in_specs=[pl.BlockSpec((N_META, MLA_HEAD_PAD), lambda h: (0, h)),
                  pl.BlockSpec((128, MLA_HEAD_PAD), lambda h: (0, h)),
                  pl.BlockSpec((MLA_V, 128), lambda h: (h, 0))],
        out_specs=pl.BlockSpec((N_META, MLA_V), lambda h: (0, h)),
        out_shape=jax.ShapeDtypeStruct((N_META, MLA_HEADS * MLA_V), BF16),
        compiler_params=_params("parallel"),
        name="meta_attention",
    )(qm, kmp, vmtp)


def _merge_kernel(a_ref, b_ref, wa_ref, wb_ref, za_ref, zb_ref, o_ref):
    ya = jnp.dot(a_ref[...], wa_ref[...], preferred_element_type=F32)
    yb = jnp.dot(b_ref[...], wb_ref[...], preferred_element_type=F32)
    o_ref[...] = (_sigmoid(za_ref[...]) * ya + _sigmoid(zb_ref[...]) * yb).astype(o_ref.dtype)


def _merge(a, b, wa, wb, z, bm, bn):
    m = a.shape[0]
    return pl.pallas_call(
        _merge_kernel,
        grid=(m // bm, D_MODEL // bn),
        in_specs=[pl.BlockSpec((bm, D_MODEL), lambda i, j: (i, 0)),
                  pl.BlockSpec((bm, D_MODEL), lambda i, j: (i, 0)),
                  pl.BlockSpec((D_MODEL, bn), lambda i, j: (0, j)),
                  pl.BlockSpec((D_MODEL, bn), lambda i, j: (0, j)),
                  pl.BlockSpec((bm, bn), lambda i, j: (i, Z_A // bn + j)),
                  pl.BlockSpec((bm, bn), lambda i, j: (i, Z_B // bn + j))],
        out_specs=pl.BlockSpec((bm, bn), lambda i, j: (i, j)),
        out_shape=jax.ShapeDtypeStruct((m, D_MODEL), BF16),
        compiler_params=_params("parallel", "parallel"),
        name="branch_merge",
    )(a, b, wa, wb, z, z)


def _outproj_kernel(m_ref, w_ref, h_ref, g_ref, ho_ref, hn_ref):
    hn = h_ref[...] + jnp.dot(m_ref[...], w_ref[...], preferred_element_type=F32)
    ho_ref[...] = hn
    ms = jnp.mean(hn * hn, axis=-1, keepdims=True)
    hn_ref[...] = (hn * lax.rsqrt(ms + EPS) * g_ref[...]).astype(hn_ref.dtype)


def _outproj(mg, w, h, g, bm):
    m = mg.shape[0]
    return pl.pallas_call(
        _outproj_kernel,
        grid=(m // bm,),
        in_specs=[pl.BlockSpec((bm, D_MODEL), lambda i: (i, 0)),
                  pl.BlockSpec((D_MODEL, D_MODEL), lambda i: (0, 0)),
                  pl.BlockSpec((bm, D_MODEL), lambda i: (i, 0)),
                  pl.BlockSpec((1, D_MODEL), lambda i: (0, 0))],
        out_specs=[pl.BlockSpec((bm, D_MODEL), lambda i: (i, 0)),
                   pl.BlockSpec((bm, D_MODEL), lambda i: (i, 0))],
        out_shape=[jax.ShapeDtypeStruct((m, D_MODEL), F32),
                   jax.ShapeDtypeStruct((m, D_MODEL), BF16)],
        compiler_params=_params("parallel"),
        name="out_proj",
    )(mg, w, h, g.reshape(1, -1))


def _gateup_kernel(h_ref, wg_ref, wu_ref, o_ref):
    h = h_ref[...]
    g = jnp.dot(h, wg_ref[...], preferred_element_type=F32)
    u = jnp.dot(h, wu_ref[...], preferred_element_type=F32)
    o_ref[...] = (g * _sigmoid(g) * u).astype(o_ref.dtype)


def _gateup(h, w, bm, bn):
    m = h.shape[0]
    nb = FF_HIDDEN // bn
    return pl.pallas_call(
        _gateup_kernel,
        grid=(m // bm, nb),
        in_specs=[pl.BlockSpec((bm, D_MODEL), lambda i, j: (i, 0)),
                  pl.BlockSpec((D_MODEL, bn), lambda i, j: (0, j)),
                  pl.BlockSpec((D_MODEL, bn), lambda i, j: (0, nb + j))],
        out_specs=pl.BlockSpec((bm, bn), lambda i, j: (i, j)),
        out_shape=jax.ShapeDtypeStruct((m, FF_HIDDEN), BF16),
        compiler_params=_params("parallel", "parallel"),
        name="gate_up",
    )(h, w, w)


def _down_kernel(a_ref, w_ref, h_ref, g_ref, ho_ref, hn_ref, acc_ref):
    k = pl.program_id(1)

    @pl.when(k == 0)
    def _():
        acc_ref[...] = jnp.zeros_like(acc_ref)

    acc_ref[...] += jnp.dot(a_ref[...], w_ref[...], preferred_element_type=F32)

    @pl.when(k == pl.num_programs(1) - 1)
    def _():
        hn = h_ref[...] + acc_ref[...]
        ho_ref[...] = hn
        ms = jnp.mean(hn * hn, axis=-1, keepdims=True)
        hn_ref[...] = (hn * lax.rsqrt(ms + EPS) * g_ref[...]).astype(hn_ref.dtype)


def _down(a, w, h, g, bm, bk):
    m = a.shape[0]
    return pl.pallas_call(
        _down_kernel,
        grid=(m // bm, FF_HIDDEN // bk),
        in_specs=[pl.BlockSpec((bm, bk), lambda i, k: (i, k)),
                  pl.BlockSpec((bk, D_MODEL), lambda i, k: (k, 0)),
                  pl.BlockSpec((bm, D_MODEL), lambda i, k: (i, 0)),
                  pl.BlockSpec((1, D_MODEL), lambda i, k: (0, 0))],
        out_specs=[pl.BlockSpec((bm, D_MODEL), lambda i, k: (i, 0)),
                   pl.BlockSpec((bm, D_MODEL), lambda i, k: (i, 0))],
        out_shape=[jax.ShapeDtypeStruct((m, D_MODEL), F32),
                   jax.ShapeDtypeStruct((m, D_MODEL), BF16)],
        scratch_shapes=[pltpu.VMEM((bm, D_MODEL), F32)],
        compiler_params=_params("parallel", "arbitrary"),
        name="down_proj",
    )(a, w, h, g.reshape(1, -1))


def _rope_layout(t):
    zeros = jnp.zeros(t.shape[:-1] + (32,), t.dtype)
    return jnp.concatenate([t[..., :32], zeros, t[..., 32:], zeros], axis=-1)


def _head_layout(t):
    return jnp.concatenate([t[..., :MLA_NOPE], _rope_layout(t[..., MLA_NOPE:])], axis=-1)


def _rope_tables():
    length = N_META + SEQ
    inv = 1.0 / (ROPE_THETA ** (jnp.arange(0, MLA_ROPE, 2, dtype=F32) / MLA_ROPE))
    ang = jnp.arange(length, dtype=F32)[:, None] * inv[None, :]
    cos, sin = jnp.cos(ang), jnp.sin(ang)
    zeros = jnp.zeros_like(cos)
    ctab = jnp.concatenate([cos, zeros, cos, zeros], axis=-1)
    stab = jnp.concatenate([-sin, zeros, sin, zeros], axis=-1)
    return ctab, stab


def kernel(x, meta_tokens, norm1_g, w_in, gla_gate_w2, gla_gate_b, gla_onorm_g, w_branch_a,
           q_a_norm_g, w_uq, kv_a_norm_g, w_ukv, q_norm_g, k_norm_g, w_branch_b, w_out,
           norm2_g, w_gate_up, w_down):
    ctab, stab = _rope_tables()
    ctab_m, stab_m = ctab[:N_META], stab[:N_META]
    ctab_r, stab_r = ctab[N_META:], stab[N_META:]

    h_res = x.reshape(ROWS, D_MODEL)
    hm_res = meta_tokens.astype(F32)
    h = _rms(h_res, norm1_g[0], 512)
    hm = _rms(hm_res, norm1_g[0], N_META)
    s_zero = jnp.zeros((GLA_HEADS, GLA_DV, GLA_DK), F32)

    for l in range(DEPTH):
        last = l == DEPTH - 1
        wl = w_in[l]
        w_main = jnp.concatenate([wl[:, :6144], wl[:, 6160:7184], wl[:, 7248:]], axis=1).astype(BF16)
        w_zlr = wl[:, 6144:6160]
        w_zkr = wl[:, 7184:7248]
        w_small = jnp.concatenate(
            [w_zlr, jnp.zeros((D_MODEL, 128 - GLA_GATE_RANK), F32), _rope_layout(w_zkr)],
            axis=1).astype(BF16)
        w2p = jnp.concatenate(
            [gla_gate_w2[l], jnp.zeros((128 - GLA_GATE_RANK, GLA_HEADS * GLA_DK), F32)],
            axis=0).astype(BF16)
        wq = _head_layout(w_uq[l].reshape(MLA_LORA, MLA_HEADS, MLA_QK)).reshape(
            MLA_LORA, MLA_HEADS * MLA_HEAD_PAD).astype(BF16)
        wkv3 = w_ukv[l].reshape(MLA_LORA, MLA_HEADS, MLA_NOPE + MLA_V)
        wk = wkv3[:, :, :MLA_NOPE].reshape(MLA_LORA, MLA_HEADS * MLA_NOPE).astype(BF16)
        wvt = wkv3[:, :, MLA_NOPE:].reshape(MLA_LORA, MLA_HEADS * MLA_V).T.astype(BF16)
        qg = _head_layout(q_norm_g[l]).reshape(1, MLA_HEAD_PAD)
        kg = _head_layout(k_norm_g[l]).reshape(1, MLA_HEAD_PAD)
        wa = w_branch_a[l].astype(BF16)
        wb = w_branch_b[l].astype(BF16)
        wo = w_out[l].astype(BF16)
        wgu = w_gate_up[l].astype(BF16)
        wd = w_down[l].astype(BF16)
        g_next = norm1_g[l + 1] if not last else norm1_g[l]

        z = _matmul(h, w_main, 1024, 1024, name="in_proj")
        zs = _matmul(h, w_small, 1024, N_SMALL, name="in_proj_small")
        zm = _matmul(hm, w_main, N_META, 1024, name="in_proj_meta")
        zsm = _matmul(hm, w_small, N_META, N_SMALL, name="in_proj_small_meta")

        gla_m, s_meta = _gla(zm, zsm, w2p, gla_gate_b[l], gla_onorm_g[l], s_zero,
                             batch=1, tokens=N_META, block=N_META, chunk=N_META, emit_state=True)
        (gla_r,) = _gla(z, zs, w2p, gla_gate_b[l], gla_onorm_g[l], s_meta,
                        batch=BATCH, tokens=SEQ, block=512, chunk=GLA_CHUNK, emit_state=False)

        km, vmt = _kvproj(zm, zsm, kv_a_norm_g[l], wk, wvt, kg, ctab_m, stab_m, N_META)
        kr, vrt = _kvproj(z, zs, kv_a_norm_g[l], wk, wvt, kg, ctab_r, stab_r, 512)
        kmp = jnp.pad(km, ((0, 128 - N_META), (0, 0)))
        vmtp = jnp.pad(vmt, ((0, 0), (0, 128 - N_META)))
        qr = _qproj(z, q_a_norm_g[l], wq, qg, ctab_r, stab_r, 512)
        att_r = _attention(qr, kr, vrt, kmp, vmtp, 512, 4, 512, 2)

        merged = _merge(gla_r, att_r, wa, wb, z, 1024, 512)
        h_res, h2 = _outproj(merged, wo, h_res, norm2_g[l], 512)
        act = _gateup(h2, wgu, 1024, 512)
        h_res, h = _down(act, wd, h_res, g_next, 512, 1408)

        if not last:
            qm = _qproj(zm, q_a_norm_g[l], wq, qg, ctab_m, stab_m, N_META)
            att_m = _meta_attention(qm, kmp, vmtp)
            merged_m = _merge(gla_m, att_m, wa, wb, zm, N_META, 512)
            hm_res, hm2 = _outproj(merged_m, wo, hm_res, norm2_g[l], N_META)
            act_m = _gateup(hm2, wgu, N_META, 512)
            hm_res, hm = _down(act_m, wd, hm_res, g_next, N_META, 1408)

    return h_res.reshape(BATCH, SEQ, D_MODEL)
```

```python
import functools

import jax
import jax.numpy as jnp
from jax import lax
from jax.experimental import pallas as pl
from jax.experimental.pallas import tpu as pltpu

D_MODEL = 2048
BATCH = 4
SEQ = 4096
DEPTH = 2
N_META = 16
ROWS = BATCH * SEQ

GLA_HEADS = 4
GLA_DK = 256
GLA_DV = 512
GLA_GATE_RANK = 16
GLA_GATE_TAU = 16.0
GLA_CHUNK = 64

MLA_HEADS = 16
MLA_LORA = 512
MLA_NOPE = 128
MLA_ROPE = 64
MLA_QK = 192
MLA_V = 128
MLA_HEAD_PAD = 256
ROPE_THETA = 10000.0
FF_HIDDEN = 5632
EPS = 1e-6
LOG2E = 1.4426950408889634

Z_Q, Z_K, Z_V, Z_R = 0, 1024, 2048, 4096
N_ZA = 6144
ZB_CQ, ZB_CKV, ZB_A, ZB_B = 0, 512, 1024, 3072
N_ZB = 5120
N_SMALL = 256

VMEM_LIMIT = 56 * 1024 * 1024
BF16 = jnp.bfloat16
F32 = jnp.float32

_NT = (((1,), (1,)), ((), ()))
_TN = (((0,), (0,)), ((), ()))


def _params(*sem):
    return pltpu.CompilerParams(dimension_semantics=sem, vmem_limit_bytes=VMEM_LIMIT)


def _sigmoid(x):
    return 1.0 / (1.0 + jnp.exp(-x))


def _rms_kernel(x_ref, g_ref, o_ref):
    x = x_ref[...]
    ms = jnp.mean(x * x, axis=-1, keepdims=True)
    o_ref[...] = (x * lax.rsqrt(ms + EPS) * g_ref[...]).astype(o_ref.dtype)


def _rms(x, g, bm):
    m, d = x.shape
    return pl.pallas_call(
        _rms_kernel,
        grid=(m // bm,),
        in_specs=[pl.BlockSpec((bm, d), lambda i: (i, 0)),
                  pl.BlockSpec((1, d), lambda i: (0, 0))],
        out_specs=pl.BlockSpec((bm, d), lambda i: (i, 0)),
        out_shape=jax.ShapeDtypeStruct((m, d), BF16),
        compiler_params=_params("parallel"),
        name="rmsnorm",
    )(x, g.reshape(1, d))


def _mm_kernel(a_ref, w_ref, o_ref):
    o_ref[...] = jnp.dot(a_ref[...], w_ref[...], preferred_element_type=F32).astype(o_ref.dtype)


def _matmul(a, w3, layer, bm, bn, name):
    m, k = a.shape
    n = w3.shape[2]
    return pl.pallas_call(
        _mm_kernel,
        grid=(m // bm, n // bn),
        in_specs=[pl.BlockSpec((bm, k), lambda i, j: (i, 0)),
                  pl.BlockSpec((None, k, bn), lambda i, j: (layer, 0, j))],
        out_specs=pl.BlockSpec((bm, bn), lambda i, j: (i, j)),
        out_shape=jax.ShapeDtypeStruct((m, n), F32),
        compiler_params=_params("parallel", "parallel"),
        name=name,
    )(a, w3)


def _mm_wstat_kernel(a_ref, w_ref, o_ref, wb_ref):
    @pl.when(pl.program_id(1) == 0)
    def _():
        wb_ref[...] = w_ref[...].astype(BF16)

    o_ref[...] = jnp.dot(a_ref[...], wb_ref[...], preferred_element_type=F32)


def _matmul_wstat(a, w3, layer, n, bm, bn, name):
    m, k = a.shape
    return pl.pallas_call(
        _mm_wstat_kernel,
        grid=(n // bn, m // bm),
        in_specs=[pl.BlockSpec((bm, k), lambda j, i: (i, 0)),
                  pl.BlockSpec((None, k, bn), lambda j, i: (layer, 0, j))],
        out_specs=pl.BlockSpec((bm, bn), lambda j, i: (i, j)),
        out_shape=jax.ShapeDtypeStruct((m, n), F32),
        scratch_shapes=[pltpu.VMEM((k, bn), BF16)],
        compiler_params=_params("arbitrary", "arbitrary"),
        name=name,
    )(a, w3)


def _gla_kernel(q_ref, k_ref, v_ref, r_ref, zlr_ref, w2_ref, gb_ref, og_ref, s0_ref,
                o_ref, *rest, chunk, n_chunks, emit_state):
    if emit_state:
        sfin_ref, st_ref = rest
    else:
        (st_ref,) = rest
    t = pl.program_id(1)
    heads = range(GLA_HEADS)

    @pl.when(t == 0)
    def _():
        st_ref[...] = s0_ref[...]

    row = lax.broadcasted_iota(jnp.int32, (chunk, chunk), 0)
    col = lax.broadcasted_iota(jnp.int32, (chunk, chunk), 1)
    causal = col <= row
    tri = jnp.where(causal, 1.0, 0.0).astype(BF16)
    w2 = w2_ref[...]
    gb = gb_ref[...]
    og = og_ref[...]
    kdim = GLA_HEADS * GLA_DK

    def body(c, carry):
        sl = pl.ds(pl.multiple_of(c * chunk, chunk), chunk)
        logit = jnp.dot(zlr_ref[sl, :].astype(BF16), w2, preferred_element_type=F32) + gb
        g = (jnp.minimum(logit, 0.0) - jnp.log1p(jnp.exp(-jnp.abs(logit)))) * (1.0 / GLA_GATE_TAU)
        g_hi = g.astype(BF16)
        g_lo = (g - g_hi.astype(F32)).astype(BF16)
        cs = jnp.dot(tri, jnp.concatenate([g_hi, g_lo], axis=1), preferred_element_type=F32)
        b = cs[:, :kdim] + cs[:, kdim:]
        b_last = b[chunk - 1:chunk, :]
        eb = jnp.exp(b)
        enb = jnp.exp(-b)
        erel = jnp.exp(b_last - b)
        elast = jnp.exp(b_last)
        k = k_ref[sl, :]
        qd = (q_ref[sl, :] * (GLA_DK ** -0.5) * eb).astype(BF16)
        kd = (k * enb).astype(BF16)
        k2 = (k * erel).astype(BF16)
        v = v_ref[sl, :].astype(BF16)

        def hk(x, h):
            return x[:, h * GLA_DK:(h + 1) * GLA_DK]

        def hv(x, h):
            return x[:, h * GLA_DV:(h + 1) * GLA_DV]

        a = [lax.dot_general(hk(qd, h), hk(kd, h), _NT, preferred_element_type=F32) for h in heads]
        st = [st_ref[h] for h in heads]
        o_state = [lax.dot_general(hk(qd, h), st[h].astype(BF16), _NT, preferred_element_type=F32)
                   for h in heads]
        o_local = [jnp.dot(jnp.where(causal, a[h], 0.0).astype(BF16), hv(v, h),
                           preferred_element_type=F32) for h in heads]
        upd = [lax.dot_general(hv(v, h), hk(k2, h), _TN, preferred_element_type=F32)
               for h in heads]
        for h in heads:
            st_ref[h] = st[h] * hk(elast, h) + upd[h]
        for h in heads:
            o = o_local[h] + o_state[h]
            ms = jnp.mean(o * o, axis=-1, keepdims=True)
            on = o * lax.rsqrt(ms + EPS) * og
            r = r_ref[sl, h * GLA_DV:(h + 1) * GLA_DV]
            o_ref[sl, h * GLA_DV:(h + 1) * GLA_DV] = (on * (r * _sigmoid(r))).astype(o_ref.dtype)
        return carry

    lax.fori_loop(0, n_chunks, body, 0)

    if emit_state:
        @pl.when(t == pl.num_programs(1) - 1)
        def _():
            sfin_ref[...] = st_ref[...]


def _gla(z, zs, w2p, gate_b, onorm_g, s0, *, batch, tokens, block, chunk, emit_state):
    nt = tokens // block
    rows = batch * tokens
    kern = functools.partial(_gla_kernel, chunk=chunk, n_chunks=block // chunk,
                             emit_state=emit_state)
    kdim = GLA_HEADS * GLA_DK
    vdim = GLA_HEADS * GLA_DV
    in_specs = [
        pl.BlockSpec((block, kdim), lambda b, t: (b * nt + t, Z_Q // kdim)),
        pl.BlockSpec((block, kdim), lambda b, t: (b * nt + t, Z_K // kdim)),
        pl.BlockSpec((block, vdim), lambda b, t: (b * nt + t, Z_V // vdim)),
        pl.BlockSpec((block, vdim), lambda b, t: (b * nt + t, Z_R // vdim)),
        pl.BlockSpec((block, 128), lambda b, t: (b * nt + t, 0)),
        pl.BlockSpec((128, kdim), lambda b, t: (0, 0)),
        pl.BlockSpec((1, kdim), lambda b, t: (0, 0)),
        pl.BlockSpec((1, GLA_DV), lambda b, t: (0, 0)),
        pl.BlockSpec((GLA_HEADS, GLA_DV, GLA_DK), lambda b, t: (0, 0, 0)),
    ]
    out_specs = [pl.BlockSpec((block, vdim), lambda b, t: (b * nt + t, 0))]
    out_shape = [jax.ShapeDtypeStruct((rows, vdim), BF16)]
    if emit_state:
        out_specs.append(pl.BlockSpec((GLA_HEADS, GLA_DV, GLA_DK), lambda b, t: (b, 0, 0)))
        out_shape.append(jax.ShapeDtypeStruct((batch * GLA_HEADS, GLA_DV, GLA_DK), F32))
    res = pl.pallas_call(
        kern,
        grid=(batch, nt),
        in_specs=in_specs,
        out_specs=out_specs,
        out_shape=out_shape,
        scratch_shapes=[pltpu.VMEM((GLA_HEADS, GLA_DV, GLA_DK), F32)],
        compiler_params=_params("parallel", "arbitrary"),
        name="gla",
    )(z, z, z, z, zs, w2p, gate_b.reshape(1, -1), onorm_g.reshape(1, -1), s0)
    return res


def _rope(x, c, s):
    return x * c + pltpu.roll(x, 64, 1) * s


def _qproj_kernel(zc_ref, ng_ref, w_ref, hg_ref, c_ref, s_ref, o_ref):
    x = zc_ref[...]
    ms = jnp.mean(x * x, axis=-1, keepdims=True)
    xn = (x * lax.rsqrt(ms + EPS) * ng_ref[...]).astype(BF16)
    c = c_ref[...]
    s = s_ref[...]
    hg = hg_ref[...]
    scale = MLA_QK ** -0.5 * LOG2E
    for h in range(MLA_HEADS):
        lo = h * MLA_HEAD_PAD
        y = jnp.dot(xn, w_ref[:, lo:lo + MLA_HEAD_PAD], preferred_element_type=F32)
        ms = jnp.sum(y * y, axis=-1, keepdims=True) * (1.0 / MLA_QK)
        yn = y * lax.rsqrt(ms + EPS) * hg
        o_ref[:, lo:lo + 128] = (yn[:, :128] * scale).astype(o_ref.dtype)
        o_ref[:, lo + 128:lo + 256] = (_rope(yn[:, 128:], c, s) * scale).astype(o_ref.dtype)


def _qproj(z, ng, w, hg, ctab, stab, bm):
    m = z.shape[0]
    nt = ctab.shape[0] // bm
    return pl.pallas_call(
        _qproj_kernel,
        grid=(m // bm,),
        in_specs=[pl.BlockSpec((bm, MLA_LORA), lambda i: (i, ZB_CQ // MLA_LORA)),
                  pl.BlockSpec((1, MLA_LORA), lambda i: (0, 0)),
                  pl.BlockSpec((MLA_LORA, MLA_HEADS * MLA_HEAD_PAD), lambda i: (0, 0)),
                  pl.BlockSpec((1, MLA_HEAD_PAD), lambda i: (0, 0)),
                  pl.BlockSpec((bm, 128), lambda i: (i % nt, 0)),
                  pl.BlockSpec((bm, 128), lambda i: (i % nt, 0))],
        out_specs=pl.BlockSpec((bm, MLA_HEADS * MLA_HEAD_PAD), lambda i: (i, 0)),
        out_shape=jax.ShapeDtypeStruct((m, MLA_HEADS * MLA_HEAD_PAD), BF16),
        compiler_params=_params("parallel"),
        name="q_proj",
    )(z, ng.reshape(1, -1), w, hg, ctab, stab)


def _kvproj_kernel(zc_ref, kr_ref, ng_ref, wk_ref, wvt_ref, hg_ref, c_ref, s_ref, k_ref, vt_ref):
    x = zc_ref[...]
    ms = jnp.mean(x * x, axis=-1, keepdims=True)
    xn = (x * lax.rsqrt(ms + EPS) * ng_ref[...]).astype(BF16)
    c = c_ref[...]
    s = s_ref[...]
    hg = hg_ref[...]
    kr = kr_ref[...]
    kr_ss = jnp.sum(kr * kr, axis=-1, keepdims=True)
    for hp in range(MLA_HEADS // 2):
        y = jnp.dot(xn, wk_ref[:, hp * 256:(hp + 1) * 256], preferred_element_type=F32)
        for sub in range(2):
            lo = (2 * hp + sub) * MLA_HEAD_PAD
            kn = y[:, sub * MLA_NOPE:(sub + 1) * MLA_NOPE]
            ms = (jnp.sum(kn * kn, axis=-1, keepdims=True) + kr_ss) * (1.0 / MLA_QK)
            rs = lax.rsqrt(ms + EPS)
            k_ref[:, lo:lo + 128] = (kn * rs * hg[:, :128]).astype(k_ref.dtype)
            k_ref[:, lo + 128:lo + 256] = _rope(kr * rs * hg[:, 128:], c, s).astype(k_ref.dtype)
    rows_per_dot = 512
    for j in range(MLA_HEADS * MLA_V // rows_per_dot):
        sl = slice(j * rows_per_dot, (j + 1) * rows_per_dot)
        vt_ref[sl, :] = lax.dot_general(wvt_ref[sl, :], xn, _NT,
                                        preferred_element_type=F32).astype(vt_ref.dtype)


def _kvproj(z, zs, ng, wk, wvt, hg, ctab, stab, bm):
    m = z.shape[0]
    nt = ctab.shape[0] // bm
    return pl.pallas_call(
        _kvproj_kernel,
        grid=(m // bm,),
        in_specs=[pl.BlockSpec((bm, MLA_LORA), lambda i: (i, ZB_CKV // MLA_LORA)),
                  pl.BlockSpec((bm, 128), lambda i: (i, 1)),
                  pl.BlockSpec((1, MLA_LORA), lambda i: (0, 0)),
                  pl.BlockSpec((MLA_LORA, MLA_HEADS * MLA_NOPE), lambda i: (0, 0)),
                  pl.BlockSpec((MLA_HEADS * MLA_V, MLA_LORA), lambda i: (0, 0)),
                  pl.BlockSpec((1, MLA_HEAD_PAD), lambda i: (0, 0)),
                  pl.BlockSpec((bm, 128), lambda i: (i % nt, 0)),
                  pl.BlockSpec((bm, 128), lambda i: (i % nt, 0))],
        out_specs=[pl.BlockSpec((bm, MLA_HEADS * MLA_HEAD_PAD), lambda i: (i, 0)),
                   pl.BlockSpec((MLA_HEADS * MLA_V, bm), lambda i: (0, i))],
        out_shape=[jax.ShapeDtypeStruct((m, MLA_HEADS * MLA_HEAD_PAD), BF16),
                   jax.ShapeDtypeStruct((MLA_HEADS * MLA_V, m), BF16)],
        compiler_params=_params("parallel"),
        name="kv_proj",
    )(z, zs, ng.reshape(1, -1), wk, wvt, hg, ctab, stab)


def _attn_kernel(q_ref, k_ref, vt_ref, km_ref, vmt_ref, o_ref, m_sc, l_sc, acc_sc, *, blk, heads,
                 sub, ahead):
    i = pl.program_id(2)
    qs = [q_ref[:, h * MLA_HEAD_PAD:(h + 1) * MLA_HEAD_PAD] for h in range(heads)]

    for h in range(heads):
        s = lax.dot_general(km_ref[:, h * MLA_HEAD_PAD:(h + 1) * MLA_HEAD_PAD], qs[h], _NT,
                            preferred_element_type=F32)
        key = lax.broadcasted_iota(jnp.int32, s.shape, 0)
        s = jnp.where(key < N_META, s, -1e30)
        m = jnp.max(s, axis=0, keepdims=True)
        p = jnp.exp2(s - m)
        m_sc[h] = m
        l_sc[h] = jnp.sum(p, axis=0, keepdims=True)
        acc_sc[h] = jnp.dot(vmt_ref[h * MLA_V:(h + 1) * MLA_V, :], p.astype(BF16),
                            preferred_element_type=F32)

    n_sub = blk // sub
    tasks = [(j, h) for j in range(n_sub) for h in range(heads)]

    def step(kb, masked):
        base = pl.multiple_of(kb * blk, blk)

        def scores(j, h):
            return lax.dot_general(k_ref[pl.ds(base + j * sub, sub),
                                         h * MLA_HEAD_PAD:(h + 1) * MLA_HEAD_PAD], qs[h], _NT,
                                   preferred_element_type=F32)

        m = [m_sc[h] for h in range(heads)]
        l = [l_sc[h] for h in range(heads)]
        acc = [acc_sc[h] for h in range(heads)]
        pending = [scores(*t) for t in tasks[:ahead]]
        for n, (j, h) in enumerate(tasks):
            s = pending.pop(0)
            if n + ahead < len(tasks):
                pending.append(scores(*tasks[n + ahead]))
            if masked:
                key = lax.broadcasted_iota(jnp.int32, s.shape, 0) + j * sub
                qry = lax.broadcasted_iota(jnp.int32, s.shape, 1)
                s = jnp.where(key <= qry, s, -1e30)
            m_new = jnp.maximum(m[h], jnp.max(s, axis=0, keepdims=True))
            alpha = jnp.exp2(m[h] - m_new)
            p = jnp.exp2(s - m_new)
            l[h] = alpha * l[h] + jnp.sum(p, axis=0, keepdims=True)
            acc[h] = alpha * acc[h] + jnp.dot(
                vt_ref[h * MLA_V:(h + 1) * MLA_V, pl.ds(base + j * sub, sub)],
                p.astype(BF16), preferred_element_type=F32)
            m[h] = m_new
        for h in range(heads):
            m_sc[h] = m[h]
            l_sc[h] = l[h]
            acc_sc[h] = acc[h]

    def body(kb, carry):
        step(kb, False)
        return carry

    lax.fori_loop(0, i, body, 0)
    step(i, True)
    for h in range(heads):
        o_ref[:, h * MLA_V:(h + 1) * MLA_V] = (acc_sc[h] * (1.0 / l_sc[h])).T.astype(o_ref.dtype)


def _attention(q, k, vt, kmp, vmtp, blk, heads, sub, ahead):
    nq = SEQ // blk
    kern = functools.partial(_attn_kernel, blk=blk, heads=heads, sub=sub, ahead=ahead)
    return pl.pallas_call(
        kern,
        grid=(BATCH, MLA_HEADS // heads, nq),
        in_specs=[pl.BlockSpec((blk, heads * MLA_HEAD_PAD), lambda b, h, i: (b * nq + i, h)),
                  pl.BlockSpec((SEQ, heads * MLA_HEAD_PAD), lambda b, h, i: (b, h)),
                  pl.BlockSpec((heads * MLA_V, SEQ), lambda b, h, i: (h, b)),
                  pl.BlockSpec((128, heads * MLA_HEAD_PAD), lambda b, h, i: (0, h)),
                  pl.BlockSpec((heads * MLA_V, 128), lambda b, h, i: (h, 0))],
        out_specs=pl.BlockSpec((blk, heads * MLA_V), lambda b, h, i: (b * nq + i, h)),
        out_shape=jax.ShapeDtypeStruct((ROWS, MLA_HEADS * MLA_V), BF16),
        scratch_shapes=[pltpu.VMEM((heads, 1, blk), F32), pltpu.VMEM((heads, 1, blk), F32),
                        pltpu.VMEM((heads, MLA_V, blk), F32)],
        compiler_params=_params("parallel", "parallel", "arbitrary"),
        name="mla_attention",
    )(q, k, vt, kmp, vmtp)


def _meta_attn_kernel(q_ref, km_ref, vmt_ref, o_ref):
    s = lax.dot_general(q_ref[...], km_ref[...], _NT, preferred_element_type=F32)
    row = lax.broadcasted_iota(jnp.int32, s.shape, 0)
    col = lax.broadcasted_iota(jnp.int32, s.shape, 1)
    s = jnp.where(col <= row, s, -1e30)
    m = jnp.max(s, axis=-1, keepdims=True)
    p = jnp.exp2(s - m)
    l = jnp.sum(p, axis=-1, keepdims=True)
    o = lax.dot_general(p.astype(BF16), vmt_ref[...], _NT, preferred_element_type=F32)
    o_ref[...] = (o / l).astype(o_ref.dtype)


def _meta_attention(qm, kmp, vmtp):
    return pl.pallas_call(
        _meta_attn_kernel,
        grid=(MLA_HEADS,),
        in_specs=[pl.BlockSpec((N_META, MLA_HEAD_PAD), lambda h: (0, h)),
                  pl.BlockSpec((128, MLA_HEAD_PAD), lambda h: (0, h)),
                  pl.BlockSpec((MLA_V, 128), lambda h: (h, 0))],
        out_specs=pl.BlockSpec((N_META, MLA_V), lambda h: (0, h)),
        out_shape=jax.ShapeDtypeStruct((N_META, MLA_HEADS * MLA_V), BF16),
        compiler_params=_params("parallel"),
        name="meta_attention",
    )(qm, kmp, vmtp)


def _merge_kernel(a_ref, b_ref, wa_ref, wb_ref, za_ref, zb_ref, o_ref):
    ya = jnp.dot(a_ref[...], wa_ref[...], preferred_element_type=F32)
    yb = jnp.dot(b_ref[...], wb_ref[...], preferred_element_type=F32)
    o_ref[...] = (_sigmoid(za_ref[...]) * ya + _sigmoid(zb_ref[...]) * yb).astype(o_ref.dtype)


def _merge(a, b, wa3, wb3, layer, z, bm, bn):
    m = a.shape[0]
    return pl.pallas_call(
        _merge_kernel,
        grid=(m // bm, D_MODEL // bn),
        in_specs=[pl.BlockSpec((bm, D_MODEL), lambda i, j: (i, 0)),
                  pl.BlockSpec((bm, D_MODEL), lambda i, j: (i, 0)),
                  pl.BlockSpec((None, D_MODEL, bn), lambda i, j: (layer, 0, j)),
                  pl.BlockSpec((None, D_MODEL, bn), lambda i, j: (layer, 0, j)),
                  pl.BlockSpec((bm, bn), lambda i, j: (i, ZB_A // bn + j)),
                  pl.BlockSpec((bm, bn), lambda i, j: (i, ZB_B // bn + j))],
        out_specs=pl.BlockSpec((bm, bn), lambda i, j: (i, j)),
        out_shape=jax.ShapeDtypeStruct((m, D_MODEL), BF16),
        compiler_params=_params("parallel", "parallel"),
        name="branch_merge",
    )(a, b, wa3, wb3, z, z)


def _outproj_kernel(m_ref, w_ref, h_ref, g_ref, ho_ref, hn_ref):
    hn = h_ref[...] + jnp.dot(m_ref[...], w_ref[...], preferred_element_type=F32)
    ho_ref[...] = hn
    ms = jnp.mean(hn * hn, axis=-1, keepdims=True)
    hn_ref[...] = (hn * lax.rsqrt(ms + EPS) * g_ref[...]).astype(hn_ref.dtype)


def _outproj(mg, w3, layer, h, g, bm):
    m = mg.shape[0]
    return pl.pallas_call(
        _outproj_kernel,
        grid=(m // bm,),
        in_specs=[pl.BlockSpec((bm, D_MODEL), lambda i: (i, 0)),
                  pl.BlockSpec((None, D_MODEL, D_MODEL), lambda i: (layer, 0, 0)),
                  pl.BlockSpec((bm, D_MODEL), lambda i: (i, 0)),
                  pl.BlockSpec((1, D_MODEL), lambda i: (0, 0))],
        out_specs=[pl.BlockSpec((bm, D_MODEL), lambda i: (i, 0)),
                   pl.BlockSpec((bm, D_MODEL), lambda i: (i, 0))],
        out_shape=[jax.ShapeDtypeStruct((m, D_MODEL), F32),
                   jax.ShapeDtypeStruct((m, D_MODEL), BF16)],
        compiler_params=_params("parallel"),
        name="out_proj",
    )(mg, w3, h, g.reshape(1, -1))


def _gateup_kernel(h_ref, wg_ref, wu_ref, o_ref, wgb_ref, wub_ref):
    @pl.when(pl.program_id(1) == 0)
    def _():
        wgb_ref[...] = wg_ref[...].astype(BF16)
        wub_ref[...] = wu_ref[...].astype(BF16)

    h = h_ref[...]
    g = jnp.dot(h, wgb_ref[...], preferred_element_type=F32)
    u = jnp.dot(h, wub_ref[...], preferred_element_type=F32)
    o_ref[...] = (g * _sigmoid(g) * u).astype(o_ref.dtype)


def _gateup(h, w3, layer, bm, bn):
    m = h.shape[0]
    nb = FF_HIDDEN // bn
    return pl.pallas_call(
        _gateup_kernel,
        grid=(nb, m // bm),
        in_specs=[pl.BlockSpec((bm, D_MODEL), lambda j, i: (i, 0)),
                  pl.BlockSpec((None, D_MODEL, bn), lambda j, i: (layer, 0, j)),
                  pl.BlockSpec((None, D_MODEL, bn), lambda j, i: (layer, 0, nb + j))],
        out_specs=pl.BlockSpec((bm, bn), lambda j, i: (i, j)),
        out_shape=jax.ShapeDtypeStruct((m, FF_HIDDEN), BF16),
        scratch_shapes=[pltpu.VMEM((D_MODEL, bn), BF16), pltpu.VMEM((D_MODEL, bn), BF16)],
        compiler_params=_params("arbitrary", "arbitrary"),
        name="gate_up",
    )(h, w3, w3)


def _down_kernel(a_ref, w_ref, h_ref, g_ref, ho_ref, hn_ref, acc_ref):
    k = pl.program_id(1)

    @pl.when(k == 0)
    def _():
        acc_ref[...] = jnp.zeros_like(acc_ref)

    acc_ref[...] += jnp.dot(a_ref[...], w_ref[...], preferred_element_type=F32)

    @pl.when(k == pl.num_programs(1) - 1)
    def _():
        hn = h_ref[...] + acc_ref[...]
        ho_ref[...] = hn
        ms = jnp.mean(hn * hn, axis=-1, keepdims=True)
        hn_ref[...] = (hn * lax.rsqrt(ms + EPS) * g_ref[...]).astype(hn_ref.dtype)


def _down(a, w3, layer, h, g, bm, bk):
    m = a.shape[0]
    return pl.pallas_call(
        _down_kernel,
        grid=(m // bm, FF_HIDDEN // bk),
        in_specs=[pl.BlockSpec((bm, bk), lambda i, k: (i, k)),
                  pl.BlockSpec((None, bk, D_MODEL), lambda i, k: (layer, k, 0)),
                  pl.BlockSpec((bm, D_MODEL), lambda i, k: (i, 0)),
                  pl.BlockSpec((1, D_MODEL), lambda i, k: (0, 0))],
        out_specs=[pl.BlockSpec((bm, D_MODEL), lambda i, k: (i, 0)),
                   pl.BlockSpec((bm, D_MODEL), lambda i, k: (i, 0))],
        out_shape=[jax.ShapeDtypeStruct((m, D_MODEL), F32),
                   jax.ShapeDtypeStruct((m, D_MODEL), BF16)],
        scratch_shapes=[pltpu.VMEM((bm, D_MODEL), F32)],
        compiler_params=_params("parallel", "arbitrary"),
        name="down_proj",
    )(a, w3, h, g.reshape(1, -1))


def _rope_layout(t):
    zeros = jnp.zeros(t.shape[:-1] + (32,), t.dtype)
    return jnp.concatenate([t[..., :32], zeros, t[..., 32:], zeros], axis=-1)


def _head_layout(t):
    return jnp.concatenate([t[..., :MLA_NOPE], _rope_layout(t[..., MLA_NOPE:])], axis=-1)


def _rope_tables():
    length = N_META + SEQ
    inv = 1.0 / (ROPE_THETA ** (jnp.arange(0, MLA_ROPE, 2, dtype=F32) / MLA_ROPE))
    ang = jnp.arange(length, dtype=F32)[:, None] * inv[None, :]
    cos, sin = jnp.cos(ang), jnp.sin(ang)
    zeros = jnp.zeros_like(cos)
    ctab = jnp.concatenate([cos, zeros, cos, zeros], axis=-1)
    stab = jnp.concatenate([-sin, zeros, sin, zeros], axis=-1)
    return ctab, stab


def kernel(x, meta_tokens, norm1_g, w_in, gla_gate_w2, gla_gate_b, gla_onorm_g, w_branch_a,
           q_a_norm_g, w_uq, kv_a_norm_g, w_ukv, q_norm_g, k_norm_g, w_branch_b, w_out,
           norm2_g, w_gate_up, w_down):
    ctab, stab = _rope_tables()
    ctab_m, stab_m = ctab[:N_META], stab[:N_META]
    ctab_r, stab_r = ctab[N_META:], stab[N_META:]

    h_res = x.reshape(ROWS, D_MODEL)
    hm_res = meta_tokens.astype(F32)
    h = _rms(h_res, norm1_g[0], 512)
    hm = _rms(hm_res, norm1_g[0], N_META)
    s_zero = jnp.zeros((GLA_HEADS, GLA_DV, GLA_DK), F32)

    w_zb = jnp.concatenate([w_in[:, :, 6160:7184], w_in[:, :, 7248:]], axis=2).astype(BF16)
    w_small = jnp.concatenate(
        [w_in[:, :, 6144:6160], jnp.zeros((DEPTH, D_MODEL, 128 - GLA_GATE_RANK), F32),
         _rope_layout(w_in[:, :, 7184:7248])], axis=2).astype(BF16)
    wa = w_branch_a.astype(BF16)
    wb = w_branch_b.astype(BF16)
    wo = w_out.astype(BF16)
    wd = w_down.astype(BF16)

    for l in range(DEPTH):
        last = l == DEPTH - 1
        w2p = jnp.concatenate(
            [gla_gate_w2[l], jnp.zeros((128 - GLA_GATE_RANK, GLA_HEADS * GLA_DK), F32)],
            axis=0).astype(BF16)
        wq = _head_layout(w_uq[l].reshape(MLA_LORA, MLA_HEADS, MLA_QK)).reshape(
            MLA_LORA, MLA_HEADS * MLA_HEAD_PAD).astype(BF16)
        wkv3 = w_ukv[l].reshape(MLA_LORA, MLA_HEADS, MLA_NOPE + MLA_V)
        wk = wkv3[:, :, :MLA_NOPE].reshape(MLA_LORA, MLA_HEADS * MLA_NOPE).astype(BF16)
        wvt = wkv3[:, :, MLA_NOPE:].reshape(MLA_LORA, MLA_HEADS * MLA_V).T.astype(BF16)
        qg = _head_layout(q_norm_g[l]).reshape(1, MLA_HEAD_PAD)
        kg = _head_layout(k_norm_g[l]).reshape(1, MLA_HEAD_PAD)
        g_next = norm1_g[l + 1] if not last else norm1_g[l]

        za = _matmul_wstat(h, w_in, l, N_ZA, 1024, 1024, name="in_proj_a")
        zb = _matmul(h, w_zb, l, 1024, 1024, name="in_proj_b")
        zs = _matmul(h, w_small, l, 1024, N_SMALL, name="in_proj_small")
        zam = _matmul_wstat(hm, w_in, l, N_ZA, N_META, 1024, name="in_proj_a_meta")
        zbm = _matmul(hm, w_zb, l, N_META, 1024, name="in_proj_b_meta")
        zsm = _matmul(hm, w_small, l, N_META, N_SMALL, name="in_proj_small_meta")

        gla_m, s_meta = _gla(zam, zsm, w2p, gla_gate_b[l], gla_onorm_g[l], s_zero,
                             batch=1, tokens=N_META, block=N_META, chunk=N_META, emit_state=True)
        (gla_r,) = _gla(za, zs, w2p, gla_gate_b[l], gla_onorm_g[l], s_meta,
                        batch=BATCH, tokens=SEQ, block=512, chunk=GLA_CHUNK, emit_state=False)

        km, vmt = _kvproj(zbm, zsm, kv_a_norm_g[l], wk, wvt, kg, ctab_m, stab_m, N_META)
        kr, vrt = _kvproj(zb, zs, kv_a_norm_g[l], wk, wvt, kg, ctab_r, stab_r, 512)
        kmp = jnp.pad(km, ((0, 128 - N_META), (0, 0)))
        vmtp = jnp.pad(vmt, ((0, 0), (0, 128 - N_META)))
        qr = _qproj(zb, q_a_norm_g[l], wq, qg, ctab_r, stab_r, 512)
        att_r = _attention(qr, kr, vrt, kmp, vmtp, 512, 4, 512, 2)

        merged = _merge(gla_r, att_r, wa, wb, l, zb, 1024, 512)
        h_res, h2 = _outproj(merged, wo, l, h_res, norm2_g[l], 512)
        act = _gateup(h2, w_gate_up, l, 1024, 512)
        h_res, h = _down(act, wd, l, h_res, g_next, 512, 1408)

        if not last:
            qm = _qproj(zbm, q_a_norm_g[l], wq, qg, ctab_m, stab_m, N_META)
            att_m = _meta_attention(qm, kmp, vmtp)
            merged_m = _merge(gla_m, att_m, wa, wb, l, zbm, N_META, 512)
            hm_res, hm2 = _outproj(merged_m, wo, l, hm_res, norm2_g[l], N_META)
            act_m = _gateup(hm2, w_gate_up, l, N_META, 512)
            hm_res, hm = _down(act_m, wd, l, hm_res, g_next, N_META, 1408)

    return h_res.reshape(BATCH, SEQ, D_MODEL)
```

```python
import functools

import jax
import jax.numpy as jnp
from jax import lax
from jax.experimental import pallas as pl
from jax.experimental.pallas import tpu as pltpu

D_MODEL = 2048
BATCH = 4
SEQ = 4096
DEPTH = 2
N_META = 16
ROWS = BATCH * SEQ

GLA_HEADS = 4
GLA_DK = 256
GLA_DV = 512
GLA_GATE_RANK = 16
GLA_GATE_TAU = 16.0
GLA_CHUNK = 64

MLA_HEADS = 16
MLA_LORA = 512
MLA_NOPE = 128
MLA_ROPE = 64
MLA_QK = 192
MLA_V = 128
MLA_HEAD_PAD = 256
ROPE_THETA = 10000.0
FF_HIDDEN = 5632
EPS = 1e-6
LOG2E = 1.4426950408889634

Z_Q, Z_K, Z_V, Z_R = 0, 1024, 2048, 4096
N_ZA = 6144
ZB_CQ, ZB_CKV, ZB_A, ZB_B = 0, 512, 1024, 3072
N_ZB = 5120
N_SMALL = 256

VMEM_LIMIT = 56 * 1024 * 1024
BF16 = jnp.bfloat16
F32 = jnp.float32

_NT = (((1,), (1,)), ((), ()))
_TN = (((0,), (0,)), ((), ()))


def _params(*sem):
    return pltpu.CompilerParams(dimension_semantics=sem, vmem_limit_bytes=VMEM_LIMIT)


def _sigmoid(x):
    return 1.0 / (1.0 + jnp.exp(-x))


def _rms_kernel(x_ref, g_ref, o_ref):
    x = x_ref[...]
    ms = jnp.mean(x * x, axis=-1, keepdims=True)
    o_ref[...] = (x * lax.rsqrt(ms + EPS) * g_ref[...]).astype(o_ref.dtype)


def _rms(x, g, bm):
    m, d = x.shape
    return pl.pallas_call(
        _rms_kernel,
        grid=(m // bm,),
        in_specs=[pl.BlockSpec((bm, d), lambda i: (i, 0)),
                  pl.BlockSpec((1, d), lambda i: (0, 0))],
        out_specs=pl.BlockSpec((bm, d), lambda i: (i, 0)),
        out_shape=jax.ShapeDtypeStruct((m, d), BF16),
        compiler_params=_params("parallel"),
        name="rmsnorm",
    )(x, g.reshape(1, d))


def _mm_nt_kernel(a_ref, wt_ref, o_ref):
    o_ref[...] = lax.dot_general(a_ref[...], wt_ref[...], _NT, preferred_element_type=F32)


def _matmul_nt(a, wt3, layer, bm, bn, name):
    m, k = a.shape
    n = wt3.shape[1]
    return pl.pallas_call(
        _mm_nt_kernel,
        grid=(m // bm, n // bn),
        in_specs=[pl.BlockSpec((bm, k), lambda i, j: (i, 0)),
                  pl.BlockSpec((None, bn, k), lambda i, j: (layer, j, 0))],
        out_specs=pl.BlockSpec((bm, bn), lambda i, j: (i, j)),
        out_shape=jax.ShapeDtypeStruct((m, n), F32),
        compiler_params=_params("parallel", "parallel"),
        name=name,
    )(a, wt3)


def _mm_nt_wstat_kernel(a_ref, wt_ref, o_ref, wb_ref):
    @pl.when(pl.program_id(1) == 0)
    def _():
        wb_ref[...] = wt_ref[...].astype(BF16)

    o_ref[...] = lax.dot_general(a_ref[...], wb_ref[...], _NT, preferred_element_type=F32)


def _matmul_nt_wstat(a, wt3, layer, n, bm, bn, name):
    m, k = a.shape
    return pl.pallas_call(
        _mm_nt_wstat_kernel,
        grid=(n // bn, m // bm),
        in_specs=[pl.BlockSpec((bm, k), lambda j, i: (i, 0)),
                  pl.BlockSpec((None, bn, k), lambda j, i: (layer, j, 0))],
        out_specs=pl.BlockSpec((bm, bn), lambda j, i: (i, j)),
        out_shape=jax.ShapeDtypeStruct((m, n), F32),
        scratch_shapes=[pltpu.VMEM((bn, k), BF16)],
        compiler_params=_params("arbitrary", "arbitrary"),
        name=name,
    )(a, wt3)


def _gla_kernel(q_ref, k_ref, v_ref, r_ref, zlr_ref, w2_ref, gb_ref, og_ref, s0_ref,
                o_ref, *rest, chunk, n_chunks, emit_state):
    if emit_state:
        sfin_ref, st_ref = rest
    else:
        (st_ref,) = rest
    t = pl.program_id(1)
    heads = range(GLA_HEADS)

    @pl.when(t == 0)
    def _():
        st_ref[...] = s0_ref[...]

    row = lax.broadcasted_iota(jnp.int32, (chunk, chunk), 0)
    col = lax.broadcasted_iota(jnp.int32, (chunk, chunk), 1)
    causal = col <= row
    tri = jnp.where(causal, 1.0, 0.0).astype(BF16)
    w2 = w2_ref[...]
    gb = gb_ref[...]
    og = og_ref[...]
    kdim = GLA_HEADS * GLA_DK

    def body(c, carry):
        sl = pl.ds(pl.multiple_of(c * chunk, chunk), chunk)
        logit = jnp.dot(zlr_ref[sl, :].astype(BF16), w2, preferred_element_type=F32) + gb
        g = (jnp.minimum(logit, 0.0) - jnp.log1p(jnp.exp(-jnp.abs(logit)))) * (1.0 / GLA_GATE_TAU)
        g_hi = g.astype(BF16)
        g_lo = (g - g_hi.astype(F32)).astype(BF16)
        cs = jnp.dot(tri, jnp.concatenate([g_hi, g_lo], axis=1), preferred_element_type=F32)
        b = cs[:, :kdim] + cs[:, kdim:]
        b_last = b[chunk - 1:chunk, :]
        eb = jnp.exp(b)
        enb = jnp.exp(-b)
        erel = jnp.exp(b_last - b)
        elast = jnp.exp(b_last)
        k = k_ref[sl, :]
        qd = (q_ref[sl, :] * (GLA_DK ** -0.5) * eb).astype(BF16)
        kd = (k * enb).astype(BF16)
        k2 = (k * erel).astype(BF16)
        v = v_ref[sl, :].astype(BF16)

        def hk(x, h):
            return x[:, h * GLA_DK:(h + 1) * GLA_DK]

        def hv(x, h):
            return x[:, h * GLA_DV:(h + 1) * GLA_DV]

        a = [lax.dot_general(hk(qd, h), hk(kd, h), _NT, preferred_element_type=F32) for h in heads]
        st = [st_ref[h] for h in heads]
        o_state = [lax.dot_general(hk(qd, h), st[h].astype(BF16), _NT, preferred_element_type=F32)
                   for h in heads]
        o_local = [jnp.dot(jnp.where(causal, a[h], 0.0).astype(BF16), hv(v, h),
                           preferred_element_type=F32) for h in heads]
        upd = [lax.dot_general(hv(v, h), hk(k2, h), _TN, preferred_element_type=F32)
               for h in heads]
        for h in heads:
            st_ref[h] = st[h] * hk(elast, h) + upd[h]
        for h in heads:
            o = o_local[h] + o_state[h]
            ms = jnp.mean(o * o, axis=-1, keepdims=True)
            on = o * lax.rsqrt(ms + EPS) * og
            r = r_ref[sl, h * GLA_DV:(h + 1) * GLA_DV]
            o_ref[sl, h * GLA_DV:(h + 1) * GLA_DV] = (on * (r * _sigmoid(r))).astype(o_ref.dtype)
        return carry

    lax.fori_loop(0, n_chunks, body, 0)

    if emit_state:
        @pl.when(t == pl.num_programs(1) - 1)
        def _():
            sfin_ref[...] = st_ref[...]


def _gla(z, zs, w2p, gate_b, onorm_g, s0, *, batch, tokens, block, chunk, emit_state):
    nt = tokens // block
    rows = batch * tokens
    kern = functools.partial(_gla_kernel, chunk=chunk, n_chunks=block // chunk,
                             emit_state=emit_state)
    kdim = GLA_HEADS * GLA_DK
    vdim = GLA_HEADS * GLA_DV
    in_specs = [
        pl.BlockSpec((block, kdim), lambda b, t: (b * nt + t, Z_Q // kdim)),
        pl.BlockSpec((block, kdim), lambda b, t: (b * nt + t, Z_K // kdim)),
        pl.BlockSpec((block, vdim), lambda b, t: (b * nt + t, Z_V // vdim)),
        pl.BlockSpec((block, vdim), lambda b, t: (b * nt + t, Z_R // vdim)),
        pl.BlockSpec((block, 128), lambda b, t: (b * nt + t, 0)),
        pl.BlockSpec((128, kdim), lambda b, t: (0, 0)),
        pl.BlockSpec((1, kdim), lambda b, t: (0, 0)),
        pl.BlockSpec((1, GLA_DV), lambda b, t: (0, 0)),
        pl.BlockSpec((GLA_HEADS, GLA_DV, GLA_DK), lambda b, t: (0, 0, 0)),
    ]
    out_specs = [pl.BlockSpec((block, vdim), lambda b, t: (b * nt + t, 0))]
    out_shape = [jax.ShapeDtypeStruct((rows, vdim), BF16)]
    if emit_state:
        out_specs.append(pl.BlockSpec((GLA_HEADS, GLA_DV, GLA_DK), lambda b, t: (b, 0, 0)))
        out_shape.append(jax.ShapeDtypeStruct((batch * GLA_HEADS, GLA_DV, GLA_DK), F32))
    res = pl.pallas_call(
        kern,
        grid=(batch, nt),
        in_specs=in_specs,
        out_specs=out_specs,
        out_shape=out_shape,
        scratch_shapes=[pltpu.VMEM((GLA_HEADS, GLA_DV, GLA_DK), F32)],
        compiler_params=_params("parallel", "arbitrary"),
        name="gla",
    )(z, z, z, z, zs, w2p, gate_b.reshape(1, -1), onorm_g.reshape(1, -1), s0)
    return res


def _rope(x, c, s):
    return x * c + pltpu.roll(x, 64, 1) * s


def _qproj_kernel(zc_ref, ng_ref, wt_ref, hg_ref, c_ref, s_ref, o_ref, *, heads_per_dot):
    x = zc_ref[...]
    ms = jnp.mean(x * x, axis=-1, keepdims=True)
    xn = (x * lax.rsqrt(ms + EPS) * ng_ref[...]).astype(BF16)
    bm = x.shape[0]
    c = c_ref[...]
    s = s_ref[...]
    hg = jnp.broadcast_to(hg_ref[...], (MLA_HEAD_PAD, bm))
    for hc in range(MLA_HEADS // heads_per_dot):
        base = hc * heads_per_dot * MLA_HEAD_PAD
        yt = lax.dot_general(wt_ref[base:base + heads_per_dot * MLA_HEAD_PAD, :], xn, _NT,
                             preferred_element_type=F32)
        for j in range(heads_per_dot):
            lo = base + j * MLA_HEAD_PAD
            y = yt[j * MLA_HEAD_PAD:(j + 1) * MLA_HEAD_PAD]
            ms = jnp.sum(y * y, axis=0, keepdims=True) * (1.0 / MLA_QK)
            yn = y * lax.rsqrt(ms + EPS) * hg
            x1 = yn[128:160]
            x2 = yn[192:224]
            o_ref[lo:lo + 128, :] = yn[:128].astype(o_ref.dtype)
            o_ref[lo + 128:lo + 160, :] = (x1 * c - x2 * s).astype(o_ref.dtype)
            o_ref[lo + 160:lo + 192, :] = yn[160:192].astype(o_ref.dtype)
            o_ref[lo + 192:lo + 224, :] = (x2 * c + x1 * s).astype(o_ref.dtype)
            o_ref[lo + 224:lo + 256, :] = yn[224:256].astype(o_ref.dtype)


def _qproj(z, ng, wt, hg_col, cos_t, sin_t, bm):
    m = z.shape[0]
    nt = cos_t.shape[1] // bm
    kern = functools.partial(_qproj_kernel, heads_per_dot=4)
    return pl.pallas_call(
        kern,
        grid=(m // bm,),
        in_specs=[pl.BlockSpec((bm, MLA_LORA), lambda i: (i, ZB_CQ // MLA_LORA)),
                  pl.BlockSpec((1, MLA_LORA), lambda i: (0, 0)),
                  pl.BlockSpec((MLA_HEADS * MLA_HEAD_PAD, MLA_LORA), lambda i: (0, 0)),
                  pl.BlockSpec((MLA_HEAD_PAD, 1), lambda i: (0, 0)),
                  pl.BlockSpec((MLA_ROPE // 2, bm), lambda i: (0, i % nt)),
                  pl.BlockSpec((MLA_ROPE // 2, bm), lambda i: (0, i % nt))],
        out_specs=pl.BlockSpec((MLA_HEADS * MLA_HEAD_PAD, bm), lambda i: (0, i)),
        out_shape=jax.ShapeDtypeStruct((MLA_HEADS * MLA_HEAD_PAD, m), BF16),
        compiler_params=_params("parallel"),
        name="q_proj",
    )(z, ng.reshape(1, -1), wt, hg_col, cos_t, sin_t)


def _kvproj_kernel(zc_ref, kr_ref, ng_ref, wk_ref, wvt_ref, hg_ref, c_ref, s_ref, k_ref, vt_ref):
    x = zc_ref[...]
    ms = jnp.mean(x * x, axis=-1, keepdims=True)
    xn = (x * lax.rsqrt(ms + EPS) * ng_ref[...]).astype(BF16)
    c = c_ref[...]
    s = s_ref[...]
    hg = hg_ref[...]
    kr = kr_ref[...]
    kr_ss = jnp.sum(kr * kr, axis=-1, keepdims=True)
    kr_rot = _rope(kr * hg[:, 128:], c, s)
    for hp in range(MLA_HEADS // 2):
        y = jnp.dot(xn, wk_ref[:, hp * 256:(hp + 1) * 256], preferred_element_type=F32)
        for sub in range(2):
            lo = (2 * hp + sub) * MLA_HEAD_PAD
            kn = y[:, sub * MLA_NOPE:(sub + 1) * MLA_NOPE]
            ms = (jnp.sum(kn * kn, axis=-1, keepdims=True) + kr_ss) * (1.0 / MLA_QK)
            rs = lax.rsqrt(ms + EPS)
            k_ref[:, lo:lo + 128] = (kn * rs * hg[:, :128]).astype(k_ref.dtype)
            k_ref[:, lo + 128:lo + 256] = (kr_rot * rs).astype(k_ref.dtype)
    rows_per_dot = 512
    for j in range(MLA_HEADS * MLA_V // rows_per_dot):
        sl = slice(j * rows_per_dot, (j + 1) * rows_per_dot)
        vt_ref[sl, :] = lax.dot_general(wvt_ref[sl, :], xn, _NT,
                                        preferred_element_type=F32).astype(vt_ref.dtype)


def _kvproj(z, zs, ng, wk, wvt, hg, ctab, stab, bm):
    m = z.shape[0]
    nt = ctab.shape[0] // bm
    return pl.pallas_call(
        _kvproj_kernel,
        grid=(m // bm,),
        in_specs=[pl.BlockSpec((bm, MLA_LORA), lambda i: (i, ZB_CKV // MLA_LORA)),
                  pl.BlockSpec((bm, 128), lambda i: (i, 1)),
                  pl.BlockSpec((1, MLA_LORA), lambda i: (0, 0)),
                  pl.BlockSpec((MLA_LORA, MLA_HEADS * MLA_NOPE), lambda i: (0, 0)),
                  pl.BlockSpec((MLA_HEADS * MLA_V, MLA_LORA), lambda i: (0, 0)),
                  pl.BlockSpec((1, MLA_HEAD_PAD), lambda i: (0, 0)),
                  pl.BlockSpec((bm, 128), lambda i: (i % nt, 0)),
                  pl.BlockSpec((bm, 128), lambda i: (i % nt, 0))],
        out_specs=[pl.BlockSpec((bm, MLA_HEADS * MLA_HEAD_PAD), lambda i: (i, 0)),
                   pl.BlockSpec((MLA_HEADS * MLA_V, bm), lambda i: (0, i))],
        out_shape=[jax.ShapeDtypeStruct((m, MLA_HEADS * MLA_HEAD_PAD), BF16),
                   jax.ShapeDtypeStruct((MLA_HEADS * MLA_V, m), BF16)],
        compiler_params=_params("parallel"),
        name="kv_proj",
    )(z, zs, ng.reshape(1, -1), wk, wvt, hg, ctab, stab)


def _attn_kernel(qt_ref, k_ref, vt_ref, km_ref, vmt_ref, o_ref, m_sc, l_sc, acc_sc, *, blk, heads,
                 sub, ahead):
    i = pl.program_id(2)
    qs = [qt_ref[h * MLA_HEAD_PAD:(h + 1) * MLA_HEAD_PAD, :] for h in range(heads)]

    for h in range(heads):
        s = jnp.dot(km_ref[:, h * MLA_HEAD_PAD:(h + 1) * MLA_HEAD_PAD], qs[h],
                    preferred_element_type=F32)
        key = lax.broadcasted_iota(jnp.int32, s.shape, 0)
        s = jnp.where(key < N_META, s, -1e30)
        m = jnp.max(s, axis=0, keepdims=True)
        p = jnp.exp2(s - m)
        m_sc[h] = m
        l_sc[h] = jnp.sum(p, axis=0, keepdims=True)
        acc_sc[h] = jnp.dot(vmt_ref[h * MLA_V:(h + 1) * MLA_V, :], p.astype(BF16),
                            preferred_element_type=F32)

    n_sub = blk // sub
    tasks = [(j, h) for j in range(n_sub) for h in range(heads)]

    def step(kb, masked):
        base = pl.multiple_of(kb * blk, blk)

        def scores(j, h):
            return jnp.dot(k_ref[pl.ds(base + j * sub, sub),
                                 h * MLA_HEAD_PAD:(h + 1) * MLA_HEAD_PAD], qs[h],
                           preferred_element_type=F32)

        m = [m_sc[h] for h in range(heads)]
        l = [l_sc[h] for h in range(heads)]
        acc = [acc_sc[h] for h in range(heads)]
        pending = [scores(*t) for t in tasks[:ahead]]
        for n, (j, h) in enumerate(tasks):
            s = pending.pop(0)
            if n + ahead < len(tasks):
                pending.append(scores(*tasks[n + ahead]))
            if masked:
                key = lax.broadcasted_iota(jnp.int32, s.shape, 0) + j * sub
                qry = lax.broadcasted_iota(jnp.int32, s.shape, 1)
                s = jnp.where(key <= qry, s, -1e30)
            m_new = jnp.maximum(m[h], jnp.max(s, axis=0, keepdims=True))
            alpha = jnp.exp2(m[h] - m_new)
            p = jnp.exp2(s - m_new)
            l[h] = alpha * l[h] + jnp.sum(p, axis=0, keepdims=True)
            acc[h] = alpha * acc[h] + jnp.dot(
                vt_ref[h * MLA_V:(h + 1) * MLA_V, pl.ds(base + j * sub, sub)],
                p.astype(BF16), preferred_element_type=F32)
            m[h] = m_new
        for h in range(heads):
            m_sc[h] = m[h]
            l_sc[h] = l[h]
            acc_sc[h] = acc[h]

    def body(kb, carry):
        step(kb, False)
        return carry

    lax.fori_loop(0, i, body, 0)
    step(i, True)
    for h in range(heads):
        o_ref[:, h * MLA_V:(h + 1) * MLA_V] = (acc_sc[h] * (1.0 / l_sc[h])).T.astype(o_ref.dtype)


def _attention(q, k, vt, kmp, vmtp, blk, heads, sub, ahead):
    nq = SEQ // blk
    kern = functools.partial(_attn_kernel, blk=blk, heads=heads, sub=sub, ahead=ahead)
    return pl.pallas_call(
        kern,
        grid=(BATCH, MLA_HEADS // heads, nq),
        in_specs=[pl.BlockSpec((heads * MLA_HEAD_PAD, blk), lambda b, h, i: (h, b * nq + i)),
                  pl.BlockSpec((SEQ, heads * MLA_HEAD_PAD), lambda b, h, i: (b, h)),
                  pl.BlockSpec((heads * MLA_V, SEQ), lambda b, h, i: (h, b)),
                  pl.BlockSpec((128, heads * MLA_HEAD_PAD), lambda b, h, i: (0, h)),
                  pl.BlockSpec((heads * MLA_V, 128), lambda b, h, i: (h, 0))],
        out_specs=pl.BlockSpec((blk, heads * MLA_V), lambda b, h, i: (b * nq + i, h)),
        out_shape=jax.ShapeDtypeStruct((ROWS, MLA_HEADS * MLA_V), BF16),
        scratch_shapes=[pltpu.VMEM((heads, 1, blk), F32), pltpu.VMEM((heads, 1, blk), F32),
                        pltpu.VMEM((heads, MLA_V, blk), F32)],
        compiler_params=_params("parallel", "parallel", "arbitrary"),
        name="mla_attention",
    )(q, k, vt, kmp, vmtp)


def _meta_attn_kernel(q_ref, km_ref, vmt_ref, o_ref):
    s = lax.dot_general(q_ref[...], km_ref[...], _NT, preferred_element_type=F32)
    row = lax.broadcasted_iota(jnp.int32, s.shape, 0)
    col = lax.broadcasted_iota(jnp.int32, s.shape, 1)
    s = jnp.where(col <= row, s, -1e30)
    m = jnp.max(s, axis=-1, keepdims=True)
    p = jnp.exp2(s - m)
    l = jnp.sum(p, axis=-1, keepdims=True)
    o = lax.dot_general(p.astype(BF16), vmt_ref[...], _NT, preferred_element_type=F32)
    o_ref[...] = (o / l).astype(o_ref.dtype)


def _meta_attention(qm, kmp, vmtp):
    return pl.pallas_call(
        _meta_attn_kernel,
        grid=(MLA_HEADS,),
        in_specs=[pl.BlockSpec((N_META, MLA_HEAD_PAD), lambda h: (0, h)),
                  pl.BlockSpec((128, MLA_HEAD_PAD), lambda h: (0, h)),
                  pl.BlockSpec((MLA_V, 128), lambda h: (h, 0))],
        out_specs=pl.BlockSpec((N_META, MLA_V), lambda h: (0, h)),
        out_shape=jax.ShapeDtypeStruct((N_META, MLA_HEADS * MLA_V), BF16),
        compiler_params=_params("parallel"),
        name="meta_attention",
    )(qm, kmp, vmtp)


def _merge_kernel(a_ref, b_ref, wa_ref, wb_ref, za_ref, zb_ref, o_ref):
    ya = jnp.dot(a_ref[...], wa_ref[...], preferred_element_type=F32)
    yb = jnp.dot(b_ref[...], wb_ref[...], preferred_element_type=F32)
    o_ref[...] = (_sigmoid(za_ref[...]) * ya + _sigmoid(zb_ref[...]) * yb).astype(o_ref.dtype)


def _merge(a, b, wa3, wb3, layer, z, bm, bn):
    m = a.shape[0]
    return pl.pallas_call(
        _merge_kernel,
        grid=(m // bm, D_MODEL // bn),
        in_specs=[pl.BlockSpec((bm, D_MODEL), lambda i, j: (i, 0)),
                  pl.BlockSpec((bm, D_MODEL), lambda i, j: (i, 0)),
                  pl.BlockSpec((None, D_MODEL, bn), lambda i, j: (layer, 0, j)),
                  pl.BlockSpec((None, D_MODEL, bn), lambda i, j: (layer, 0, j)),
                  pl.BlockSpec((bm, bn), lambda i, j: (i, ZB_A // bn + j)),
                  pl.BlockSpec((bm, bn), lambda i, j: (i, ZB_B // bn + j))],
        out_specs=pl.BlockSpec((bm, bn), lambda i, j: (i, j)),
        out_shape=jax.ShapeDtypeStruct((m, D_MODEL), BF16),
        compiler_params=_params("parallel", "parallel"),
        name="branch_merge",
    )(a, b, wa3, wb3, z, z)


def _outproj_kernel(m_ref, w_ref, h_ref, g_ref, ho_ref, hn_ref):
    hn = h_ref[...] + jnp.dot(m_ref[...], w_ref[...], preferred_element_type=F32)
    ho_ref[...] = hn
    ms = jnp.mean(hn * hn, axis=-1, keepdims=True)
    hn_ref[...] = (hn * lax.rsqrt(ms + EPS) * g_ref[...]).astype(hn_ref.dtype)


def _outproj(mg, w3, layer, h, g, bm):
    m = mg.shape[0]
    return pl.pallas_call(
        _outproj_kernel,
        grid=(m // bm,),
        in_specs=[pl.BlockSpec((bm, D_MODEL), lambda i: (i, 0)),
                  pl.BlockSpec((None, D_MODEL, D_MODEL), lambda i: (layer, 0, 0)),
                  pl.BlockSpec((bm, D_MODEL), lambda i: (i, 0)),
                  pl.BlockSpec((1, D_MODEL), lambda i: (0, 0))],
        out_specs=[pl.BlockSpec((bm, D_MODEL), lambda i: (i, 0)),
                   pl.BlockSpec((bm, D_MODEL), lambda i: (i, 0))],
        out_shape=[jax.ShapeDtypeStruct((m, D_MODEL), F32),
                   jax.ShapeDtypeStruct((m, D_MODEL), BF16)],
        compiler_params=_params("parallel"),
        name="out_proj",
    )(mg, w3, h, g.reshape(1, -1))


def _gateup_kernel(h_ref, wg_ref, wu_ref, o_ref, wgb_ref, wub_ref):
    @pl.when(pl.program_id(1) == 0)
    def _():
        wgb_ref[...] = wg_ref[...].astype(BF16)
        wub_ref[...] = wu_ref[...].astype(BF16)

    h = h_ref[...]
    g = jnp.dot(h, wgb_ref[...], preferred_element_type=F32)
    u = jnp.dot(h, wub_ref[...], preferred_element_type=F32)
    o_ref[...] = (g * _sigmoid(g) * u).astype(o_ref.dtype)


def _gateup(h, w3, layer, bm, bn):
    m = h.shape[0]
    nb = FF_HIDDEN // bn
    return pl.pallas_call(
        _gateup_kernel,
        grid=(nb, m // bm),
        in_specs=[pl.BlockSpec((bm, D_MODEL), lambda j, i: (i, 0)),
                  pl.BlockSpec((None, D_MODEL, bn), lambda j, i: (layer, 0, j)),
                  pl.BlockSpec((None, D_MODEL, bn), lambda j, i: (layer, 0, nb + j))],
        out_specs=pl.BlockSpec((bm, bn), lambda j, i: (i, j)),
        out_shape=jax.ShapeDtypeStruct((m, FF_HIDDEN), BF16),
        scratch_shapes=[pltpu.VMEM((D_MODEL, bn), BF16), pltpu.VMEM((D_MODEL, bn), BF16)],
        compiler_params=_params("arbitrary", "arbitrary"),
        name="gate_up",
    )(h, w3, w3)


def _down_kernel(a_ref, w_ref, h_ref, g_ref, ho_ref, hn_ref, acc_ref):
    k = pl.program_id(1)

    @pl.when(k == 0)
    def _():
        acc_ref[...] = jnp.zeros_like(acc_ref)

    acc_ref[...] += jnp.dot(a_ref[...], w_ref[...], preferred_element_type=F32)

    @pl.when(k == pl.num_programs(1) - 1)
    def _():
        hn = h_ref[...] + acc_ref[...]
        ho_ref[...] = hn
        ms = jnp.mean(hn * hn, axis=-1, keepdims=True)
        hn_ref[...] = (hn * lax.rsqrt(ms + EPS) * g_ref[...]).astype(hn_ref.dtype)


def _down(a, w3, layer, h, g, bm, bk):
    m = a.shape[0]
    return pl.pallas_call(
        _down_kernel,
        grid=(m // bm, FF_HIDDEN // bk),
        in_specs=[pl.BlockSpec((bm, bk), lambda i, k: (i, k)),
                  pl.BlockSpec((None, bk, D_MODEL), lambda i, k: (layer, k, 0)),
                  pl.BlockSpec((bm, D_MODEL), lambda i, k: (i, 0)),
                  pl.BlockSpec((1, D_MODEL), lambda i, k: (0, 0))],
        out_specs=[pl.BlockSpec((bm, D_MODEL), lambda i, k: (i, 0)),
                   pl.BlockSpec((bm, D_MODEL), lambda i, k: (i, 0))],
        out_shape=[jax.ShapeDtypeStruct((m, D_MODEL), F32),
                   jax.ShapeDtypeStruct((m, D_MODEL), BF16)],
        scratch_shapes=[pltpu.VMEM((bm, D_MODEL), F32)],
        compiler_params=_params("parallel", "arbitrary"),
        name="down_proj",
    )(a, w3, h, g.reshape(1, -1))


def _rope_layout(t):
    zeros = jnp.zeros(t.shape[:-1] + (32,), t.dtype)
    return jnp.concatenate([t[..., :32], zeros, t[..., 32:], zeros], axis=-1)


def _head_layout(t):
    return jnp.concatenate([t[..., :MLA_NOPE], _rope_layout(t[..., MLA_NOPE:])], axis=-1)


def _rope_tables():
    length = N_META + SEQ
    inv = 1.0 / (ROPE_THETA ** (jnp.arange(0, MLA_ROPE, 2, dtype=F32) / MLA_ROPE))
    ang = jnp.arange(length, dtype=F32)[:, None] * inv[None, :]
    cos, sin = jnp.cos(ang), jnp.sin(ang)
    zeros = jnp.zeros_like(cos)
    ctab = jnp.concatenate([cos, zeros, cos, zeros], axis=-1)
    stab = jnp.concatenate([-sin, zeros, sin, zeros], axis=-1)
    return ctab, stab, cos.T, sin.T


def kernel(x, meta_tokens, norm1_g, w_in, gla_gate_w2, gla_gate_b, gla_onorm_g, w_branch_a,
           q_a_norm_g, w_uq, kv_a_norm_g, w_ukv, q_norm_g, k_norm_g, w_branch_b, w_out,
           norm2_g, w_gate_up, w_down):
    ctab, stab, cos_t, sin_t = _rope_tables()
    ctab_m, stab_m = ctab[:N_META], stab[:N_META]
    ctab_r, stab_r = ctab[N_META:], stab[N_META:]
    cos_tm, sin_tm = cos_t[:, :N_META], sin_t[:, :N_META]
    cos_tr, sin_tr = cos_t[:, N_META:], sin_t[:, N_META:]

    h_res = x.reshape(ROWS, D_MODEL)
    hm_res = meta_tokens.astype(F32)
    h = _rms(h_res, norm1_g[0], 512)
    hm = _rms(hm_res, norm1_g[0], N_META)
    s_zero = jnp.zeros((GLA_HEADS, GLA_DV, GLA_DK), F32)

    w_in_t = jnp.swapaxes(w_in, 1, 2)
    w_zb = jnp.concatenate([w_in_t[:, 6160:7184], w_in_t[:, 7248:]], axis=1).astype(BF16)
    w_kr = w_in_t[:, 7184:7248]
    zrow = jnp.zeros((DEPTH, 32, D_MODEL), F32)
    w_small = jnp.concatenate(
        [w_in_t[:, 6144:6160], jnp.zeros((DEPTH, 128 - GLA_GATE_RANK, D_MODEL), F32),
         w_kr[:, :32], zrow, w_kr[:, 32:], zrow], axis=1).astype(BF16)
    wa = w_branch_a.astype(BF16)
    wb = w_branch_b.astype(BF16)
    wo = w_out.astype(BF16)
    wd = w_down.astype(BF16)

    for l in range(DEPTH):
        last = l == DEPTH - 1
        w2p = jnp.concatenate(
            [gla_gate_w2[l], jnp.zeros((128 - GLA_GATE_RANK, GLA_HEADS * GLA_DK), F32)],
            axis=0).astype(BF16)
        wqt = _head_layout(w_uq[l].reshape(MLA_LORA, MLA_HEADS, MLA_QK)).reshape(
            MLA_LORA, MLA_HEADS * MLA_HEAD_PAD).T.astype(BF16)
        wkv3 = w_ukv[l].reshape(MLA_LORA, MLA_HEADS, MLA_NOPE + MLA_V)
        wk = wkv3[:, :, :MLA_NOPE].reshape(MLA_LORA, MLA_HEADS * MLA_NOPE).astype(BF16)
        wvt = wkv3[:, :, MLA_NOPE:].reshape(MLA_LORA, MLA_HEADS * MLA_V).T.astype(BF16)
        qg = (_head_layout(q_norm_g[l]) * (MLA_QK ** -0.5 * LOG2E)).reshape(MLA_HEAD_PAD, 1)
        kg = _head_layout(k_norm_g[l]).reshape(1, MLA_HEAD_PAD)
        g_next = norm1_g[l + 1] if not last else norm1_g[l]

        za = _matmul_nt_wstat(h, w_in_t, l, N_ZA, 1024, 1024, name="in_proj_a")
        zb = _matmul_nt(h, w_zb, l, 1024, 1024, name="in_proj_b")
        zs = _matmul_nt(h, w_small, l, 1024, N_SMALL, name="in_proj_small")
        zam = _matmul_nt_wstat(hm, w_in_t, l, N_ZA, N_META, 1024, name="in_proj_a_meta")
        zbm = _matmul_nt(hm, w_zb, l, N_META, 1024, name="in_proj_b_meta")
        zsm = _matmul_nt(hm, w_small, l, N_META, N_SMALL, name="in_proj_small_meta")

        gla_m, s_meta = _gla(zam, zsm, w2p, gla_gate_b[l], gla_onorm_g[l], s_zero,
                             batch=1, tokens=N_META, block=N_META, chunk=N_META, emit_state=True)
        (gla_r,) = _gla(za, zs, w2p, gla_gate_b[l], gla_onorm_g[l], s_meta,
                        batch=BATCH, tokens=SEQ, block=512, chunk=GLA_CHUNK, emit_state=False)

        km, vmt = _kvproj(zbm, zsm, kv_a_norm_g[l], wk, wvt, kg, ctab_m, stab_m, N_META)
        kr, vrt = _kvproj(zb, zs, kv_a_norm_g[l], wk, wvt, kg, ctab_r, stab_r, 512)
        kmp = jnp.pad(km, ((0, 128 - N_META), (0, 0)))
        vmtp = jnp.pad(vmt, ((0, 0), (0, 128 - N_META)))
        qr = _qproj(zb, q_a_norm_g[l], wqt, qg, cos_tr, sin_tr, 512)
        att_r = _attention(qr, kr, vrt, kmp, vmtp, 512, 4, 512, 2)

        merged = _merge(gla_r, att_r, wa, wb, l, zb, 1024, 512)
        h_res, h2 = _outproj(merged, wo, l, h_res, norm2_g[l], 512)
        act = _gateup(h2, w_gate_up, l, 1024, 512)
        h_res, h = _down(act, wd, l, h_res, g_next, 512, 1408)

        if not last:
            qm = _qproj(zbm, q_a_norm_g[l], wqt, qg, cos_tm, sin_tm, N_META)
            att_m = _meta_attention(qm.T, kmp, vmtp)
            merged_m = _merge(gla_m, att_m, wa, wb, l, zbm, N_META, 512)
            hm_res, hm2 = _outproj(merged_m, wo, l, hm_res, norm2_g[l], N_META)
            act_m = _gateup(hm2, w_gate_up, l, N_META, 512)
            hm_res, hm = _down(act_m, wd, l, hm_res, g_next, N_META, 1408)

    return h_res.reshape(BATCH, SEQ, D_MODEL)
```

```python
import functools

import jax
import jax.numpy as jnp
from jax import lax
from jax.experimental import pallas as pl
from jax.experimental.pallas import tpu as pltpu

D_MODEL = 2048
BATCH = 4
SEQ = 4096
DEPTH = 2
N_META = 16
ROWS = BATCH * SEQ

GLA_HEADS = 4
GLA_DK = 256
GLA_DV = 512
GLA_GATE_RANK = 16
GLA_GATE_TAU = 16.0
GLA_CHUNK = 64

MLA_HEADS = 16
MLA_LORA = 512
MLA_NOPE = 128
MLA_ROPE = 64
MLA_QK = 192
MLA_V = 128
MLA_HEAD_PAD = 256
ROPE_THETA = 10000.0
FF_HIDDEN = 5632
EPS = 1e-6
LOG2E = 1.4426950408889634

W_IN_ZA = 6144
W_IN_C, W_IN_C_END = 6160, 7184
W_IN_G = 7248
Z_Q, Z_K, Z_V, Z_R, Z_CQ, Z_CKV, Z_A, Z_B = 0, 1024, 2048, 4096, 6144, 6656, 7168, 9216
N_Z = 11264
N_SMALL = 256

VMEM_LIMIT = 56 * 1024 * 1024
BF16 = jnp.bfloat16
F32 = jnp.float32

_NT = (((1,), (1,)), ((), ()))
_TN = (((0,), (0,)), ((), ()))


def _params(*sem):
    return pltpu.CompilerParams(dimension_semantics=sem, vmem_limit_bytes=VMEM_LIMIT)


def _sigmoid(x):
    return 1.0 / (1.0 + jnp.exp(-x))


def _rms_kernel(x_ref, g_ref, o_ref):
    x = x_ref[...]
    ms = jnp.mean(x * x, axis=-1, keepdims=True)
    o_ref[...] = (x * lax.rsqrt(ms + EPS) * g_ref[...]).astype(o_ref.dtype)


def _rms(x, g, bm):
    m, d = x.shape
    return pl.pallas_call(
        _rms_kernel,
        grid=(m // bm,),
        in_specs=[pl.BlockSpec((bm, d), lambda i: (i, 0)),
                  pl.BlockSpec((1, d), lambda i: (0, 0))],
        out_specs=pl.BlockSpec((bm, d), lambda i: (i, 0)),
        out_shape=jax.ShapeDtypeStruct((m, d), BF16),
        compiler_params=_params("parallel"),
        name="rmsnorm",
    )(x, g.reshape(1, d))


def _mm_nt_kernel(a_ref, wt_ref, o_ref):
    o_ref[...] = lax.dot_general(a_ref[...], wt_ref[...], _NT, preferred_element_type=F32)


def _matmul_nt(a, wt3, layer, bm, bn, name):
    m, k = a.shape
    n = wt3.shape[1]
    return pl.pallas_call(
        _mm_nt_kernel,
        grid=(m // bm, n // bn),
        in_specs=[pl.BlockSpec((bm, k), lambda i, j: (i, 0)),
                  pl.BlockSpec((None, bn, k), lambda i, j: (layer, j, 0))],
        out_specs=pl.BlockSpec((bm, bn), lambda i, j: (i, j)),
        out_shape=jax.ShapeDtypeStruct((m, n), F32),
        compiler_params=_params("parallel", "parallel"),
        name=name,
    )(a, wt3)


def _mm_nt_wstat_kernel(a_ref, wt_ref, o_ref, wb_ref):
    @pl.when(pl.program_id(1) == 0)
    def _():
        wb_ref[...] = wt_ref[0].astype(BF16)

    o_ref[...] = lax.dot_general(a_ref[...], wb_ref[...], _NT, preferred_element_type=F32)


def _in_proj_row0(j, bn):
    n_a = W_IN_ZA // bn
    n_c = (W_IN_C_END - W_IN_C) // bn
    return jnp.where(j < n_a, j * bn,
                     jnp.where(j < n_a + n_c, W_IN_C + (j - n_a) * bn,
                               W_IN_G + (j - n_a - n_c) * bn))


def _in_proj(a, wt3, layer, bm, bn, name):
    m, k = a.shape
    return pl.pallas_call(
        _mm_nt_wstat_kernel,
        grid=(N_Z // bn, m // bm),
        in_specs=[pl.BlockSpec((bm, k), lambda j, i: (i, 0)),
                  pl.BlockSpec((pl.Element(1), pl.Element(bn), pl.Element(k)),
                               lambda j, i: (layer, pl.multiple_of(_in_proj_row0(j, bn), 16), 0))],
        out_specs=pl.BlockSpec((bm, bn), lambda j, i: (i, j)),
        out_shape=jax.ShapeDtypeStruct((m, N_Z), F32),
        scratch_shapes=[pltpu.VMEM((bn, k), BF16)],
        compiler_params=_params("arbitrary", "arbitrary"),
        name=name,
    )(a, wt3)


def _gla_kernel(q_ref, k_ref, v_ref, r_ref, zlr_ref, w2_ref, gb_ref, og_ref, s0_ref,
                o_ref, *rest, chunk, n_chunks, emit_state):
    if emit_state:
        sfin_ref, st_ref = rest
    else:
        (st_ref,) = rest
    t = pl.program_id(1)
    heads = range(GLA_HEADS)

    @pl.when(t == 0)
    def _():
        st_ref[...] = s0_ref[...]

    row = lax.broadcasted_iota(jnp.int32, (chunk, chunk), 0)
    col = lax.broadcasted_iota(jnp.int32, (chunk, chunk), 1)
    causal = col <= row
    tri = jnp.where(causal, 1.0, 0.0).astype(BF16)
    w2 = w2_ref[...]
    gb = gb_ref[...]
    og = og_ref[...]
    kdim = GLA_HEADS * GLA_DK

    def body(c, carry):
        sl = pl.ds(pl.multiple_of(c * chunk, chunk), chunk)
        logit = jnp.dot(zlr_ref[sl, :].astype(BF16), w2, preferred_element_type=F32) + gb
        g = (jnp.minimum(logit, 0.0) - jnp.log1p(jnp.exp(-jnp.abs(logit)))) * (1.0 / GLA_GATE_TAU)
        g_hi = g.astype(BF16)
        g_lo = (g - g_hi.astype(F32)).astype(BF16)
        cs = jnp.dot(tri, jnp.concatenate([g_hi, g_lo], axis=1), preferred_element_type=F32)
        b = cs[:, :kdim] + cs[:, kdim:]
        b_last = b[chunk - 1:chunk, :]
        eb = jnp.exp(b)
        enb = jnp.exp(-b)
        erel = jnp.exp(b_last - b)
        elast = jnp.exp(b_last)
        k = k_ref[sl, :]
        qd = (q_ref[sl, :] * (GLA_DK ** -0.5) * eb).astype(BF16)
        kd = (k * enb).astype(BF16)
        k2 = (k * erel).astype(BF16)
        v = v_ref[sl, :].astype(BF16)

        def hk(x, h):
            return x[:, h * GLA_DK:(h + 1) * GLA_DK]

        def hv(x, h):
            return x[:, h * GLA_DV:(h + 1) * GLA_DV]

        a = [lax.dot_general(hk(qd, h), hk(kd, h), _NT, preferred_element_type=F32) for h in heads]
        st = [st_ref[h] for h in heads]
        o_state = [lax.dot_general(hk(qd, h), st[h].astype(BF16), _NT, preferred_element_type=F32)
                   for h in heads]
        o_local = [jnp.dot(jnp.where(causal, a[h], 0.0).astype(BF16), hv(v, h),
                           preferred_element_type=F32) for h in heads]
        upd = [lax.dot_general(hv(v, h), hk(k2, h), _TN, preferred_element_type=F32)
               for h in heads]
        for h in heads:
            st_ref[h] = st[h] * hk(elast, h) + upd[h]
        for h in heads:
            o = o_local[h] + o_state[h]
            ms = jnp.mean(o * o, axis=-1, keepdims=True)
            on = o * lax.rsqrt(ms + EPS) * og
            r = r_ref[sl, h * GLA_DV:(h + 1) * GLA_DV]
            o_ref[sl, h * GLA_DV:(h + 1) * GLA_DV] = (on * (r * _sigmoid(r))).astype(o_ref.dtype)
        return carry

    lax.fori_loop(0, n_chunks, body, 0)

    if emit_state:
        @pl.when(t == pl.num_programs(1) - 1)
        def _():
            sfin_ref[...] = st_ref[...]


def _gla(z, zs, w2p, gate_b, onorm_g, s0, *, batch, tokens, block, chunk, emit_state):
    nt = tokens // block
    rows = batch * tokens
    kern = functools.partial(_gla_kernel, chunk=chunk, n_chunks=block // chunk,
                             emit_state=emit_state)
    kdim = GLA_HEADS * GLA_DK
    vdim = GLA_HEADS * GLA_DV
    in_specs = [
        pl.BlockSpec((block, kdim), lambda b, t: (b * nt + t, Z_Q // kdim)),
        pl.BlockSpec((block, kdim), lambda b, t: (b * nt + t, Z_K // kdim)),
        pl.BlockSpec((block, vdim), lambda b, t: (b * nt + t, Z_V // vdim)),
        pl.BlockSpec((block, vdim), lambda b, t: (b * nt + t, Z_R // vdim)),
        pl.BlockSpec((block, 128), lambda b, t: (b * nt + t, 0)),
        pl.BlockSpec((128, kdim), lambda b, t: (0, 0)),
        pl.BlockSpec((1, kdim), lambda b, t: (0, 0)),
        pl.BlockSpec((1, GLA_DV), lambda b, t: (0, 0)),
        pl.BlockSpec((GLA_HEADS, GLA_DV, GLA_DK), lambda b, t: (0, 0, 0)),
    ]
    out_specs = [pl.BlockSpec((block, vdim), lambda b, t: (b * nt + t, 0))]
    out_shape = [jax.ShapeDtypeStruct((rows, vdim), BF16)]
    if emit_state:
        out_specs.append(pl.BlockSpec((GLA_HEADS, GLA_DV, GLA_DK), lambda b, t: (b, 0, 0)))
        out_shape.append(jax.ShapeDtypeStruct((batch * GLA_HEADS, GLA_DV, GLA_DK), F32))
    res = pl.pallas_call(
        kern,
        grid=(batch, nt),
        in_specs=in_specs,
        out_specs=out_specs,
        out_shape=out_shape,
        scratch_shapes=[pltpu.VMEM((GLA_HEADS, GLA_DV, GLA_DK), F32)],
        compiler_params=_params("parallel", "arbitrary"),
        name="gla",
    )(z, z, z, z, zs, w2p, gate_b.reshape(1, -1), onorm_g.reshape(1, -1), s0)
    return res


def _rope(x, c, s):
    return x * c + pltpu.roll(x, 64, 1) * s


def _qproj_kernel(zc_ref, ng_ref, wt_ref, hg_ref, c_ref, s_ref, o_ref, *, heads_per_dot):
    x = zc_ref[...]
    ms = jnp.mean(x * x, axis=-1, keepdims=True)
    xn = (x * lax.rsqrt(ms + EPS) * ng_ref[...]).astype(BF16)
    bm = x.shape[0]
    c = c_ref[...]
    s = s_ref[...]
    hg = jnp.broadcast_to(hg_ref[...], (MLA_HEAD_PAD, bm))
    for hc in range(MLA_HEADS // heads_per_dot):
        base = hc * heads_per_dot * MLA_HEAD_PAD
        yt = lax.dot_general(wt_ref[base:base + heads_per_dot * MLA_HEAD_PAD, :], xn, _NT,
                             preferred_element_type=F32)
        for j in range(heads_per_dot):
            lo = base + j * MLA_HEAD_PAD
            y = yt[j * MLA_HEAD_PAD:(j + 1) * MLA_HEAD_PAD]
            ms = jnp.sum(y * y, axis=0, keepdims=True) * (1.0 / MLA_QK)
            yn = y * lax.rsqrt(ms + EPS) * hg
            x1 = yn[128:160]
            x2 = yn[192:224]
            o_ref[lo:lo + 128, :] = yn[:128].astype(o_ref.dtype)
            o_ref[lo + 128:lo + 160, :] = (x1 * c - x2 * s).astype(o_ref.dtype)
            o_ref[lo + 160:lo + 192, :] = yn[160:192].astype(o_ref.dtype)
            o_ref[lo + 192:lo + 224, :] = (x2 * c + x1 * s).astype(o_ref.dtype)
            o_ref[lo + 224:lo + 256, :] = yn[224:256].astype(o_ref.dtype)


def _qproj(z, ng, wt, hg_col, cos_t, sin_t, bm):
    m = z.shape[0]
    nt = cos_t.shape[1] // bm
    kern = functools.partial(_qproj_kernel, heads_per_dot=4)
    return pl.pallas_call(
        kern,
        grid=(m // bm,),
        in_specs=[pl.BlockSpec((bm, MLA_LORA), lambda i: (i, Z_CQ // MLA_LORA)),
                  pl.BlockSpec((1, MLA_LORA), lambda i: (0, 0)),
                  pl.BlockSpec((MLA_HEADS * MLA_HEAD_PAD, MLA_LORA), lambda i: (0, 0)),
                  pl.BlockSpec((MLA_HEAD_PAD, 1), lambda i: (0, 0)),
                  pl.BlockSpec((MLA_ROPE // 2, bm), lambda i: (0, i % nt)),
                  pl.BlockSpec((MLA_ROPE // 2, bm), lambda i: (0, i % nt))],
        out_specs=pl.BlockSpec((MLA_HEADS * MLA_HEAD_PAD, bm), lambda i: (0, i)),
        out_shape=jax.ShapeDtypeStruct((MLA_HEADS * MLA_HEAD_PAD, m), BF16),
        compiler_params=_params("parallel"),
        name="q_proj",
    )(z, ng.reshape(1, -1), wt, hg_col, cos_t, sin_t)


def _kvproj_kernel(zc_ref, kr_ref, ng_ref, wk_ref, wvt_ref, hg_ref, c_ref, s_ref, k_ref, vt_ref):
    x = zc_ref[...]
    ms = jnp.mean(x * x, axis=-1, keepdims=True)
    xn = (x * lax.rsqrt(ms + EPS) * ng_ref[...]).astype(BF16)
    c = c_ref[...]
    s = s_ref[...]
    hg = hg_ref[...]
    kr = kr_ref[...]
    kr_ss = jnp.sum(kr * kr, axis=-1, keepdims=True)
    kr_rot = _rope(kr * hg[:, 128:], c, s)
    for hp in range(MLA_HEADS // 2):
        y = jnp.dot(xn, wk_ref[:, hp * 256:(hp + 1) * 256], preferred_element_type=F32)
        for sub in range(2):
            lo = (2 * hp + sub) * MLA_HEAD_PAD
            kn = y[:, sub * MLA_NOPE:(sub + 1) * MLA_NOPE]
            ms = (jnp.sum(kn * kn, axis=-1, keepdims=True) + kr_ss) * (1.0 / MLA_QK)
            rs = lax.rsqrt(ms + EPS)
            k_ref[:, lo:lo + 128] = (kn * rs * hg[:, :128]).astype(k_ref.dtype)
            k_ref[:, lo + 128:lo + 256] = (kr_rot * rs).astype(k_ref.dtype)
    rows_per_dot = 512
    for j in range(MLA_HEADS * MLA_V // rows_per_dot):
        sl = slice(j * rows_per_dot, (j + 1) * rows_per_dot)
        vt_ref[sl, :] = lax.dot_general(wvt_ref[sl, :], xn, _NT,
                                        preferred_element_type=F32).astype(vt_ref.dtype)


def _kvproj(z, zs, ng, wk, wvt, hg, ctab, stab, bm):
    m = z.shape[0]
    nt = ctab.shape[0] // bm
    return pl.pallas_call(
        _kvproj_kernel,
        grid=(m // bm,),
        in_specs=[pl.BlockSpec((bm, MLA_LORA), lambda i: (i, Z_CKV // MLA_LORA)),
                  pl.BlockSpec((bm, 128), lambda i: (i, 1)),
                  pl.BlockSpec((1, MLA_LORA), lambda i: (0, 0)),
                  pl.BlockSpec((MLA_LORA, MLA_HEADS * MLA_NOPE), lambda i: (0, 0)),
                  pl.BlockSpec((MLA_HEADS * MLA_V, MLA_LORA), lambda i: (0, 0)),
                  pl.BlockSpec((1, MLA_HEAD_PAD), lambda i: (0, 0)),
                  pl.BlockSpec((bm, 128), lambda i: (i % nt, 0)),
                  pl.BlockSpec((bm, 128), lambda i: (i % nt, 0))],
        out_specs=[pl.BlockSpec((bm, MLA_HEADS * MLA_HEAD_PAD), lambda i: (i, 0)),
                   pl.BlockSpec((MLA_HEADS * MLA_V, bm), lambda i: (0, i))],
        out_shape=[jax.ShapeDtypeStruct((m, MLA_HEADS * MLA_HEAD_PAD), BF16),
                   jax.ShapeDtypeStruct((MLA_HEADS * MLA_V, m), BF16)],
        compiler_params=_params("parallel"),
        name="kv_proj",
    )(z, zs, ng.reshape(1, -1), wk, wvt, hg, ctab, stab)


def _attn_kernel(qt_ref, k_ref, vt_ref, km_ref, vmt_ref, o_ref, m_sc, l_sc, acc_sc, *, blk, heads,
                 sub, ahead):
    i = pl.program_id(2)
    qs = [qt_ref[h * MLA_HEAD_PAD:(h + 1) * MLA_HEAD_PAD, :] for h in range(heads)]

    for h in range(heads):
        s = jnp.dot(km_ref[:, h * MLA_HEAD_PAD:(h + 1) * MLA_HEAD_PAD], qs[h],
                    preferred_element_type=F32)
        key = lax.broadcasted_iota(jnp.int32, s.shape, 0)
        s = jnp.where(key < N_META, s, -1e30)
        m = jnp.max(s, axis=0, keepdims=True)
        p = jnp.exp2(s - m)
        m_sc[h] = m
        l_sc[h] = jnp.sum(p, axis=0, keepdims=True)
        acc_sc[h] = jnp.dot(vmt_ref[h * MLA_V:(h + 1) * MLA_V, :], p.astype(BF16),
                            preferred_element_type=F32)

    n_sub = blk // sub
    tasks = [(j, h) for j in range(n_sub) for h in range(heads)]

    def step(kb, masked):
        base = pl.multiple_of(kb * blk, blk)

        def scores(j, h):
            return jnp.dot(k_ref[pl.ds(base + j * sub, sub),
                                 h * MLA_HEAD_PAD:(h + 1) * MLA_HEAD_PAD], qs[h],
                           preferred_element_type=F32)

        m = [m_sc[h] for h in range(heads)]
        l = [l_sc[h] for h in range(heads)]
        acc = [acc_sc[h] for h in range(heads)]
        pending = [scores(*t) for t in tasks[:ahead]]
        for n, (j, h) in enumerate(tasks):
            s = pending.pop(0)
            if n + ahead < len(tasks):
                pending.append(scores(*tasks[n + ahead]))
            if masked:
                key = lax.broadcasted_iota(jnp.int32, s.shape, 0) + j * sub
                qry = lax.broadcasted_iota(jnp.int32, s.shape, 1)
                s = jnp.where(key <= qry, s, -1e30)
            m_new = jnp.maximum(m[h], jnp.max(s, axis=0, keepdims=True))
            alpha = jnp.exp2(m[h] - m_new)
            p = jnp.exp2(s - m_new)
            l[h] = alpha * l[h] + jnp.sum(p, axis=0, keepdims=True)
            acc[h] = alpha * acc[h] + jnp.dot(
                vt_ref[h * MLA_V:(h + 1) * MLA_V, pl.ds(base + j * sub, sub)],
                p.astype(BF16), preferred_element_type=F32)
            m[h] = m_new
        for h in range(heads):
            m_sc[h] = m[h]
            l_sc[h] = l[h]
            acc_sc[h] = acc[h]

    def body(kb, carry):
        step(kb, False)
        return carry

    lax.fori_loop(0, i, body, 0)
    step(i, True)
    for h in range(heads):
        o_ref[:, h * MLA_V:(h + 1) * MLA_V] = (acc_sc[h] * (1.0 / l_sc[h])).T.astype(o_ref.dtype)


def _attention(q, k, vt, kmp, vmtp, blk, heads, sub, ahead):
    nq = SEQ // blk
    kern = functools.partial(_attn_kernel, blk=blk, heads=heads, sub=sub, ahead=ahead)
    return pl.pallas_call(
        kern,
        grid=(BATCH, MLA_HEADS // heads, nq),
        in_specs=[pl.BlockSpec((heads * MLA_HEAD_PAD, blk), lambda b, h, i: (h, b * nq + i)),
                  pl.BlockSpec((SEQ, heads * MLA_HEAD_PAD), lambda b, h, i: (b, h)),
                  pl.BlockSpec((heads * MLA_V, SEQ), lambda b, h, i: (h, b)),
                  pl.BlockSpec((128, heads * MLA_HEAD_PAD), lambda b, h, i: (0, h)),
                  pl.BlockSpec((heads * MLA_V, 128), lambda b, h, i: (h, 0))],
        out_specs=pl.BlockSpec((blk, heads * MLA_V), lambda b, h, i: (b * nq + i, h)),
        out_shape=jax.ShapeDtypeStruct((ROWS, MLA_HEADS * MLA_V), BF16),
        scratch_shapes=[pltpu.VMEM((heads, 1, blk), F32), pltpu.VMEM((heads, 1, blk), F32),
                        pltpu.VMEM((heads, MLA_V, blk), F32)],
        compiler_params=_params("parallel", "parallel", "arbitrary"),
        name="mla_attention",
    )(q, k, vt, kmp, vmtp)


def _meta_attn_kernel(q_ref, km_ref, vmt_ref, o_ref):
    s = lax.dot_general(q_ref[...], km_ref[...], _NT, preferred_element_type=F32)
    row = lax.broadcasted_iota(jnp.int32, s.shape, 0)
    col = lax.broadcasted_iota(jnp.int32, s.shape, 1)
    s = jnp.where(col <= row, s, -1e30)
    m = jnp.max(s, axis=-1, keepdims=True)
    p = jnp.exp2(s - m)
    l = jnp.sum(p, axis=-1, keepdims=True)
    o = lax.dot_general(p.astype(BF16), vmt_ref[...], _NT, preferred_element_type=F32)
    o_ref[...] = (o / l).astype(o_ref.dtype)


def _meta_attention(qm, kmp, vmtp):
    return pl.pallas_call(
        _meta_attn_kernel,
        grid=(MLA_HEADS,),
        in_specs=[pl.BlockSpec((N_META, MLA_HEAD_PAD), lambda h: (0, h)),
                  pl.BlockSpec((128, MLA_HEAD_PAD), lambda h: (0, h)),
                  pl.BlockSpec((MLA_V, 128), lambda h: (h, 0))],
        out_specs=pl.BlockSpec((N_META, MLA_V), lambda h: (0, h)),
        out_shape=jax.ShapeDtypeStruct((N_META, MLA_HEADS * MLA_V), BF16),
        compiler_params=_params("parallel"),
        name="meta_attention",
    )(qm, kmp, vmtp)


def _merge_kernel(a_ref, b_ref, wa_ref, wb_ref, za_ref, zb_ref, o_ref):
    ya = jnp.dot(a_ref[...], wa_ref[...], preferred_element_type=F32)
    yb = jnp.dot(b_ref[...], wb_ref[...], preferred_element_type=F32)
    o_ref[...] = (_sigmoid(za_ref[...]) * ya + _sigmoid(zb_ref[...]) * yb).astype(o_ref.dtype)


def _merge(a, b, wa3, wb3, layer, z, bm, bn):
    m = a.shape[0]
    return pl.pallas_call(
        _merge_kernel,
        grid=(m // bm, D_MODEL // bn),
        in_specs=[pl.BlockSpec((bm, D_MODEL), lambda i, j: (i, 0)),
                  pl.BlockSpec((bm, D_MODEL), lambda i, j: (i, 0)),
                  pl.BlockSpec((None, D_MODEL, bn), lambda i, j: (layer, 0, j)),
                  pl.BlockSpec((None, D_MODEL, bn), lambda i, j: (layer, 0, j)),
                  pl.BlockSpec((bm, bn), lambda i, j: (i, Z_A // bn + j)),
                  pl.BlockSpec((bm, bn), lambda i, j: (i, Z_B // bn + j))],
        out_specs=pl.BlockSpec((bm, bn), lambda i, j: (i, j)),
        out_shape=jax.ShapeDtypeStruct((m, D_MODEL), BF16),
        compiler_params=_params("parallel", "parallel"),
        name="branch_merge",
    )(a, b, wa3, wb3, z, z)


def _outproj_kernel(m_ref, w_ref, h_ref, g_ref, ho_ref, hn_ref):
    hn = h_ref[...] + jnp.dot(m_ref[...], w_ref[...], preferred_element_type=F32)
    ho_ref[...] = hn
    ms = jnp.mean(hn * hn, axis=-1, keepdims=True)
    hn_ref[...] = (hn * lax.rsqrt(ms + EPS) * g_ref[...]).astype(hn_ref.dtype)


def _outproj(mg, w3, layer, h, g, bm):
    m = mg.shape[0]
    return pl.pallas_call(
        _outproj_kernel,
        grid=(m // bm,),
        in_specs=[pl.BlockSpec((bm, D_MODEL), lambda i: (i, 0)),
                  pl.BlockSpec((None, D_MODEL, D_MODEL), lambda i: (layer, 0, 0)),
                  pl.BlockSpec((bm, D_MODEL), lambda i: (i, 0)),
                  pl.BlockSpec((1, D_MODEL), lambda i: (0, 0))],
        out_specs=[pl.BlockSpec((bm, D_MODEL), lambda i: (i, 0)),
                   pl.BlockSpec((bm, D_MODEL), lambda i: (i, 0))],
        out_shape=[jax.ShapeDtypeStruct((m, D_MODEL), F32),
                   jax.ShapeDtypeStruct((m, D_MODEL), BF16)],
        compiler_params=_params("parallel"),
        name="out_proj",
    )(mg, w3, h, g.reshape(1, -1))


def _gateup_kernel(h_ref, wg_ref, wu_ref, o_ref, wgb_ref, wub_ref):
    @pl.when(pl.program_id(1) == 0)
    def _():
        wgb_ref[...] = wg_ref[...].astype(BF16)
        wub_ref[...] = wu_ref[...].astype(BF16)

    h = h_ref[...]
    g = jnp.dot(h, wgb_ref[...], preferred_element_type=F32)
    u = jnp.dot(h, wub_ref[...], preferred_element_type=F32)
    o_ref[...] = (g * _sigmoid(g) * u).astype(o_ref.dtype)


def _gateup(h, w3, layer, bm, bn):
    m = h.shape[0]
    nb = FF_HIDDEN // bn
    return pl.pallas_call(
        _gateup_kernel,
        grid=(nb, m // bm),
        in_specs=[pl.BlockSpec((bm, D_MODEL), lambda j, i: (i, 0)),
                  pl.BlockSpec((None, D_MODEL, bn), lambda j, i: (layer, 0, j)),
                  pl.BlockSpec((None, D_MODEL, bn), lambda j, i: (layer, 0, nb + j))],
        out_specs=pl.BlockSpec((bm, bn), lambda j, i: (i, j)),
        out_shape=jax.ShapeDtypeStruct((m, FF_HIDDEN), BF16),
        scratch_shapes=[pltpu.VMEM((D_MODEL, bn), BF16), pltpu.VMEM((D_MODEL, bn), BF16)],
        compiler_params=_params("arbitrary", "arbitrary"),
        name="gate_up",
    )(h, w3, w3)


def _down_kernel(a_ref, w_ref, h_ref, g_ref, ho_ref, hn_ref, acc_ref):
    k = pl.program_id(1)

    @pl.when(k == 0)
    def _():
        acc_ref[...] = jnp.zeros_like(acc_ref)

    acc_ref[...] += jnp.dot(a_ref[...], w_ref[...], preferred_element_type=F32)

    @pl.when(k == pl.num_programs(1) - 1)
    def _():
        hn = h_ref[...] + acc_ref[...]
        ho_ref[...] = hn
        ms = jnp.mean(hn * hn, axis=-1, keepdims=True)
        hn_ref[...] = (hn * lax.rsqrt(ms + EPS) * g_ref[...]).astype(hn_ref.dtype)


def _down(a, w3, layer, h, g, bm, bk):
    m = a.shape[0]
    return pl.pallas_call(
        _down_kernel,
        grid=(m // bm, FF_HIDDEN // bk),
        in_specs=[pl.BlockSpec((bm, bk), lambda i, k: (i, k)),
                  pl.BlockSpec((None, bk, D_MODEL), lambda i, k: (layer, k, 0)),
                  pl.BlockSpec((bm, D_MODEL), lambda i, k: (i, 0)),
                  pl.BlockSpec((1, D_MODEL), lambda i, k: (0, 0))],
        out_specs=[pl.BlockSpec((bm, D_MODEL), lambda i, k: (i, 0)),
                   pl.BlockSpec((bm, D_MODEL), lambda i, k: (i, 0))],
        out_shape=[jax.ShapeDtypeStruct((m, D_MODEL), F32),
                   jax.ShapeDtypeStruct((m, D_MODEL), BF16)],
        scratch_shapes=[pltpu.VMEM((bm, D_MODEL), F32)],
        compiler_params=_params("parallel", "arbitrary"),
        name="down_proj",
    )(a, w3, h, g.reshape(1, -1))


def _rope_layout(t):
    zeros = jnp.zeros(t.shape[:-1] + (32,), t.dtype)
    return jnp.concatenate([t[..., :32], zeros, t[..., 32:], zeros], axis=-1)


def _head_layout(t):
    return jnp.concatenate([t[..., :MLA_NOPE], _rope_layout(t[..., MLA_NOPE:])], axis=-1)


def _rope_tables():
    length = N_META + SEQ
    inv = 1.0 / (ROPE_THETA ** (jnp.arange(0, MLA_ROPE, 2, dtype=F32) / MLA_ROPE))
    ang = jnp.arange(length, dtype=F32)[:, None] * inv[None, :]
    cos, sin = jnp.cos(ang), jnp.sin(ang)
    zeros = jnp.zeros_like(cos)
    ctab = jnp.concatenate([cos, zeros, cos, zeros], axis=-1)
    stab = jnp.concatenate([-sin, zeros, sin, zeros], axis=-1)
    return ctab, stab, cos.T, sin.T


def kernel(x, meta_tokens, norm1_g, w_in, gla_gate_w2, gla_gate_b, gla_onorm_g, w_branch_a,
           q_a_norm_g, w_uq, kv_a_norm_g, w_ukv, q_norm_g, k_norm_g, w_branch_b, w_out,
           norm2_g, w_gate_up, w_down):
    ctab, stab, cos_t, sin_t = _rope_tables()
    ctab_m, stab_m = ctab[:N_META], stab[:N_META]
    ctab_r, stab_r = ctab[N_META:], stab[N_META:]
    cos_tm, sin_tm = cos_t[:, :N_META], sin_t[:, :N_META]
    cos_tr, sin_tr = cos_t[:, N_META:], sin_t[:, N_META:]

    h_res = x.reshape(ROWS, D_MODEL)
    hm_res = meta_tokens.astype(F32)
    h = _rms(h_res, norm1_g[0], 512)
    hm = _rms(hm_res, norm1_g[0], N_META)
    s_zero = jnp.zeros((GLA_HEADS, GLA_DV, GLA_DK), F32)

    w_in_t = jnp.swapaxes(w_in, 1, 2)
    w_kr = w_in_t[:, 7184:7248]
    zrow = jnp.zeros((DEPTH, 32, D_MODEL), F32)
    w_small = jnp.concatenate(
        [w_in_t[:, 6144:6160], jnp.zeros((DEPTH, 128 - GLA_GATE_RANK, D_MODEL), F32),
         w_kr[:, :32], zrow, w_kr[:, 32:], zrow], axis=1).astype(BF16)
    wa = w_branch_a.astype(BF16)
    wb = w_branch_b.astype(BF16)
    wo = w_out.astype(BF16)
    wd = w_down.astype(BF16)

    for l in range(DEPTH):
        last = l == DEPTH - 1
        w2p = jnp.concatenate(
            [gla_gate_w2[l], jnp.zeros((128 - GLA_GATE_RANK, GLA_HEADS * GLA_DK), F32)],
            axis=0).astype(BF16)
        wqt = _head_layout(w_uq[l].reshape(MLA_LORA, MLA_HEADS, MLA_QK)).reshape(
            MLA_LORA, MLA_HEADS * MLA_HEAD_PAD).T.astype(BF16)
        wkv3 = w_ukv[l].reshape(MLA_LORA, MLA_HEADS, MLA_NOPE + MLA_V)
        wk = wkv3[:, :, :MLA_NOPE].reshape(MLA_LORA, MLA_HEADS * MLA_NOPE).astype(BF16)
        wvt = wkv3[:, :, MLA_NOPE:].reshape(MLA_LORA, MLA_HEADS * MLA_V).T.astype(BF16)
        qg = (_head_layout(q_norm_g[l]) * (MLA_QK ** -0.5 * LOG2E)).reshape(MLA_HEAD_PAD, 1)
        kg = _head_layout(k_norm_g[l]).reshape(1, MLA_HEAD_PAD)
        g_next = norm1_g[l + 1] if not last else norm1_g[l]

        z = _in_proj(h, w_in_t, l, 1024, 1024, name="in_proj")
        zs = _matmul_nt(h, w_small, l, 1024, N_SMALL, name="in_proj_small")
        zm = _in_proj(hm, w_in_t, l, N_META, 1024, name="in_proj_meta")
        zsm = _matmul_nt(hm, w_small, l, N_META, N_SMALL, name="in_proj_small_meta")

        gla_m, s_meta = _gla(zm, zsm, w2p, gla_gate_b[l], gla_onorm_g[l], s_zero,
                             batch=1, tokens=N_META, block=N_META, chunk=N_META, emit_state=True)
        (gla_r,) = _gla(z, zs, w2p, gla_gate_b[l], gla_onorm_g[l], s_meta,
                        batch=BATCH, tokens=SEQ, block=512, chunk=GLA_CHUNK, emit_state=False)

        km, vmt = _kvproj(zm, zsm, kv_a_norm_g[l], wk, wvt, kg, ctab_m, stab_m, N_META)
        kr, vrt = _kvproj(z, zs, kv_a_norm_g[l], wk, wvt, kg, ctab_r, stab_r, 512)
        kmp = jnp.pad(km, ((0, 128 - N_META), (0, 0)))
        vmtp = jnp.pad(vmt, ((0, 0), (0, 128 - N_META)))
        qr = _qproj(z, q_a_norm_g[l], wqt, qg, cos_tr, sin_tr, 512)
        att_r = _attention(qr, kr, vrt, kmp, vmtp, 512, 4, 512, 2)

        merged = _merge(gla_r, att_r, wa, wb, l, z, 1024, 512)
        h_res, h2 = _outproj(merged, wo, l, h_res, norm2_g[l], 512)
        act = _gateup(h2, w_gate_up, l, 1024, 512)
        h_res, h = _down(act, wd, l, h_res, g_next, 512, 2816)

        if not last:
            qm = _qproj(zm, q_a_norm_g[l], wqt, qg, cos_tm, sin_tm, N_META)
            att_m = _meta_attention(qm.T, kmp, vmtp)
            merged_m = _merge(gla_m, att_m, wa, wb, l, zm, N_META, 512)
            hm_res, hm2 = _outproj(merged_m, wo, l, hm_res, norm2_g[l], N_META)
            act_m = _gateup(hm2, w_gate_up, l, N_META, 512)
            hm_res, hm = _down(act_m, wd, l, hm_res, g_next, N_META, 1408)

    return h_res.reshape(BATCH, SEQ, D_MODEL)
```

```python
import functools

import jax
import jax.numpy as jnp
from jax import lax
from jax.experimental import pallas as pl
from jax.experimental.pallas import tpu as pltpu

D_MODEL = 2048
BATCH = 4
SEQ = 4096
DEPTH = 2
N_META = 16
ROWS = BATCH * SEQ

GLA_HEADS = 4
GLA_DK = 256
GLA_DV = 512
GLA_GATE_RANK = 16
GLA_GATE_TAU = 16.0
GLA_CHUNK = 64

MLA_HEADS = 16
MLA_LORA = 512
MLA_NOPE = 128
MLA_ROPE = 64
MLA_QK = 192
MLA_V = 128
MLA_HEAD_PAD = 256
ROPE_THETA = 10000.0
FF_HIDDEN = 5632
EPS = 1e-6
LOG2E = 1.4426950408889634

W_IN_ZA = 6144
W_IN_C, W_IN_C_END = 6160, 7184
W_IN_G = 7248
Z_Q, Z_K, Z_V, Z_R, Z_CQ, Z_CKV, Z_A, Z_B = 0, 1024, 2048, 4096, 6144, 6656, 7168, 9216
N_Z = 11264
N_SMALL = 256

VMEM_LIMIT = 56 * 1024 * 1024
BF16 = jnp.bfloat16
F32 = jnp.float32

_NT = (((1,), (1,)), ((), ()))
_TN = (((0,), (0,)), ((), ()))


def _params(*sem):
    return pltpu.CompilerParams(dimension_semantics=sem, vmem_limit_bytes=VMEM_LIMIT)


def _sigmoid(x):
    return 1.0 / (1.0 + jnp.exp(-x))


def _rms_kernel(x_ref, g_ref, o_ref):
    x = x_ref[...]
    ms = jnp.mean(x * x, axis=-1, keepdims=True)
    o_ref[...] = (x * lax.rsqrt(ms + EPS) * g_ref[...]).astype(o_ref.dtype)


def _rms(x, g, bm):
    m, d = x.shape
    return pl.pallas_call(
        _rms_kernel,
        grid=(m // bm,),
        in_specs=[pl.BlockSpec((bm, d), lambda i: (i, 0)),
                  pl.BlockSpec((1, d), lambda i: (0, 0))],
        out_specs=pl.BlockSpec((bm, d), lambda i: (i, 0)),
        out_shape=jax.ShapeDtypeStruct((m, d), BF16),
        compiler_params=_params("parallel"),
        name="rmsnorm",
    )(x, g.reshape(1, d))


def _mm_nt_kernel(a_ref, wt_ref, o_ref):
    o_ref[...] = lax.dot_general(a_ref[...], wt_ref[...].astype(BF16), _NT,
                                 preferred_element_type=F32)


def _matmul_nt(a, wt3, layer, bm, bn, name):
    m, k = a.shape
    n = wt3.shape[1]
    return pl.pallas_call(
        _mm_nt_kernel,
        grid=(m // bm, n // bn),
        in_specs=[pl.BlockSpec((bm, k), lambda i, j: (i, 0)),
                  pl.BlockSpec((None, bn, k), lambda i, j: (layer, j, 0))],
        out_specs=pl.BlockSpec((bm, bn), lambda i, j: (i, j)),
        out_shape=jax.ShapeDtypeStruct((m, n), F32),
        compiler_params=_params("parallel", "parallel"),
        name=name,
    )(a, wt3)


def _mm_nt_wstat_kernel(a_ref, wt_ref, o_ref, wb_ref):
    @pl.when(pl.program_id(1) == 0)
    def _():
        wb_ref[...] = wt_ref[0].astype(BF16)

    o_ref[...] = lax.dot_general(a_ref[...], wb_ref[...], _NT, preferred_element_type=F32)


def _in_proj_row0(j, bn):
    n_a = W_IN_ZA // bn
    n_c = (W_IN_C_END - W_IN_C) // bn
    return jnp.where(j < n_a, j * bn,
                     jnp.where(j < n_a + n_c, W_IN_C + (j - n_a) * bn,
                               W_IN_G + (j - n_a - n_c) * bn))


def _in_proj(a, wt3, layer, bm, bn, name):
    m, k = a.shape
    return pl.pallas_call(
        _mm_nt_wstat_kernel,
        grid=(N_Z // bn, m // bm),
        in_specs=[pl.BlockSpec((bm, k), lambda j, i: (i, 0)),
                  pl.BlockSpec((pl.Element(1), pl.Element(bn), pl.Element(k)),
                               lambda j, i: (layer, pl.multiple_of(_in_proj_row0(j, bn), 16), 0))],
        out_specs=pl.BlockSpec((bm, bn), lambda j, i: (i, j)),
        out_shape=jax.ShapeDtypeStruct((m, N_Z), F32),
        scratch_shapes=[pltpu.VMEM((bn, k), BF16)],
        compiler_params=_params("arbitrary", "arbitrary"),
        name=name,
    )(a, wt3)


def _gla_kernel(q_ref, k_ref, v_ref, r_ref, zlr_ref, w2_ref, gb_ref, og_ref, s0_ref,
                o_ref, *rest, chunk, n_chunks, emit_state):
    if emit_state:
        sfin_ref, st_ref = rest
    else:
        (st_ref,) = rest
    t = pl.program_id(1)
    heads = range(GLA_HEADS)

    @pl.when(t == 0)
    def _():
        st_ref[...] = s0_ref[...]

    row = lax.broadcasted_iota(jnp.int32, (chunk, chunk), 0)
    col = lax.broadcasted_iota(jnp.int32, (chunk, chunk), 1)
    causal = col <= row
    tri = jnp.where(causal, 1.0, 0.0).astype(BF16)
    w2 = w2_ref[...]
    gb = gb_ref[...]
    og = og_ref[...]
    kdim = GLA_HEADS * GLA_DK

    def body(c, carry):
        sl = pl.ds(pl.multiple_of(c * chunk, chunk), chunk)
        logit = jnp.dot(zlr_ref[sl, :].astype(BF16), w2, preferred_element_type=F32) + gb
        g = (jnp.minimum(logit, 0.0) - jnp.log1p(jnp.exp(-jnp.abs(logit)))) * (1.0 / GLA_GATE_TAU)
        g_hi = g.astype(BF16)
        g_lo = (g - g_hi.astype(F32)).astype(BF16)
        cs = jnp.dot(tri, jnp.concatenate([g_hi, g_lo], axis=1), preferred_element_type=F32)
        b = cs[:, :kdim] + cs[:, kdim:]
        b_last = b[chunk - 1:chunk, :]
        eb = jnp.exp(b)
        enb = jnp.exp(-b)
        erel = jnp.exp(b_last - b)
        elast = jnp.exp(b_last)
        k = k_ref[sl, :]
        qd = (q_ref[sl, :] * (GLA_DK ** -0.5) * eb).astype(BF16)
        kd = (k * enb).astype(BF16)
        k2 = (k * erel).astype(BF16)
        v = v_ref[sl, :].astype(BF16)

        def hk(x, h):
            return x[:, h * GLA_DK:(h + 1) * GLA_DK]

        def hv(x, h):
            return x[:, h * GLA_DV:(h + 1) * GLA_DV]

        a = [lax.dot_general(hk(qd, h), hk(kd, h), _NT, preferred_element_type=F32) for h in heads]
        st = [st_ref[h] for h in heads]
        o_state = [lax.dot_general(hk(qd, h), st[h].astype(BF16), _NT, preferred_element_type=F32)
                   for h in heads]
        o_local = [jnp.dot(jnp.where(causal, a[h], 0.0).astype(BF16), hv(v, h),
                           preferred_element_type=F32) for h in heads]
        upd = [lax.dot_general(hv(v, h), hk(k2, h), _TN, preferred_element_type=F32)
               for h in heads]
        for h in heads:
            st_ref[h] = st[h] * hk(elast, h) + upd[h]
        for h in heads:
            o = o_local[h] + o_state[h]
            ms = jnp.mean(o * o, axis=-1, keepdims=True)
            on = o * lax.rsqrt(ms + EPS) * og
            r = r_ref[sl, h * GLA_DV:(h + 1) * GLA_DV]
            o_ref[sl, h * GLA_DV:(h + 1) * GLA_DV] = (on * (r * _sigmoid(r))).astype(o_ref.dtype)
        return carry

    lax.fori_loop(0, n_chunks, body, 0)

    if emit_state:
        @pl.when(t == pl.num_programs(1) - 1)
        def _():
            sfin_ref[...] = st_ref[...]


def _gla(z, zs, w2p, gate_b, onorm_g, s0, *, batch, tokens, block, chunk, emit_state):
    nt = tokens // block
    rows = batch * tokens
    kern = functools.partial(_gla_kernel, chunk=chunk, n_chunks=block // chunk,
                             emit_state=emit_state)
    kdim = GLA_HEADS * GLA_DK
    vdim = GLA_HEADS * GLA_DV
    in_specs = [
        pl.BlockSpec((block, kdim), lambda b, t: (b * nt + t, Z_Q // kdim)),
        pl.BlockSpec((block, kdim), lambda b, t: (b * nt + t, Z_K // kdim)),
        pl.BlockSpec((block, vdim), lambda b, t: (b * nt + t, Z_V // vdim)),
        pl.BlockSpec((block, vdim), lambda b, t: (b * nt + t, Z_R // vdim)),
        pl.BlockSpec((block, 128), lambda b, t: (b * nt + t, 0)),
        pl.BlockSpec((128, kdim), lambda b, t: (0, 0)),
        pl.BlockSpec((1, kdim), lambda b, t: (0, 0)),
        pl.BlockSpec((1, GLA_DV), lambda b, t: (0, 0)),
        pl.BlockSpec((GLA_HEADS, GLA_DV, GLA_DK), lambda b, t: (0, 0, 0)),
    ]
    out_specs = [pl.BlockSpec((block, vdim), lambda b, t: (b * nt + t, 0))]
    out_shape = [jax.ShapeDtypeStruct((rows, vdim), BF16)]
    if emit_state:
        out_specs.append(pl.BlockSpec((GLA_HEADS, GLA_DV, GLA_DK), lambda b, t: (b, 0, 0)))
        out_shape.append(jax.ShapeDtypeStruct((batch * GLA_HEADS, GLA_DV, GLA_DK), F32))
    res = pl.pallas_call(
        kern,
        grid=(batch, nt),
        in_specs=in_specs,
        out_specs=out_specs,
        out_shape=out_shape,
        scratch_shapes=[pltpu.VMEM((GLA_HEADS, GLA_DV, GLA_DK), F32)],
        compiler_params=_params("parallel", "arbitrary"),
        name="gla",
    )(z, z, z, z, zs, w2p, gate_b.reshape(1, -1), onorm_g.reshape(1, -1), s0)
    return res


def _rope(x, c, s):
    return x * c + pltpu.roll(x, 64, 1) * s


def _qproj_kernel(zc_ref, ng_ref, wt_ref, hg_ref, c_ref, s_ref, o_ref, *, heads_per_dot):
    x = zc_ref[...]
    ms = jnp.mean(x * x, axis=-1, keepdims=True)
    xn = (x * lax.rsqrt(ms + EPS) * ng_ref[...]).astype(BF16)
    bm = x.shape[0]
    c = c_ref[...]
    s = s_ref[...]
    hg = jnp.broadcast_to(hg_ref[...], (MLA_HEAD_PAD, bm))
    for hc in range(MLA_HEADS // heads_per_dot):
        base = hc * heads_per_dot * MLA_HEAD_PAD
        yt = lax.dot_general(wt_ref[base:base + heads_per_dot * MLA_HEAD_PAD, :], xn, _NT,
                             preferred_element_type=F32)
        for j in range(heads_per_dot):
            lo = base + j * MLA_HEAD_PAD
            y = yt[j * MLA_HEAD_PAD:(j + 1) * MLA_HEAD_PAD]
            ms = jnp.sum(y * y, axis=0, keepdims=True) * (1.0 / MLA_QK)
            yn = y * lax.rsqrt(ms + EPS) * hg
            x1 = yn[128:160]
            x2 = yn[192:224]
            o_ref[lo:lo + 128, :] = yn[:128].astype(o_ref.dtype)
            o_ref[lo + 128:lo + 160, :] = (x1 * c - x2 * s).astype(o_ref.dtype)
            o_ref[lo + 160:lo + 192, :] = yn[160:192].astype(o_ref.dtype)
            o_ref[lo + 192:lo + 224, :] = (x2 * c + x1 * s).astype(o_ref.dtype)
            o_ref[lo + 224:lo + 256, :] = yn[224:256].astype(o_ref.dtype)


def _qproj(z, ng, wt, hg_col, cos_t, sin_t, bm):
    m = z.shape[0]
    nt = cos_t.shape[1] // bm
    kern = functools.partial(_qproj_kernel, heads_per_dot=4)
    return pl.pallas_call(
        kern,
        grid=(m // bm,),
        in_specs=[pl.BlockSpec((bm, MLA_LORA), lambda i: (i, Z_CQ // MLA_LORA)),
                  pl.BlockSpec((1, MLA_LORA), lambda i: (0, 0)),
                  pl.BlockSpec((MLA_HEADS * MLA_HEAD_PAD, MLA_LORA), lambda i: (0, 0)),
                  pl.BlockSpec((MLA_HEAD_PAD, 1), lambda i: (0, 0)),
                  pl.BlockSpec((MLA_ROPE // 2, bm), lambda i: (0, i % nt)),
                  pl.BlockSpec((MLA_ROPE // 2, bm), lambda i: (0, i % nt))],
        out_specs=pl.BlockSpec((MLA_HEADS * MLA_HEAD_PAD, bm), lambda i: (0, i)),
        out_shape=jax.ShapeDtypeStruct((MLA_HEADS * MLA_HEAD_PAD, m), BF16),
        compiler_params=_params("parallel"),
        name="q_proj",
    )(z, ng.reshape(1, -1), wt, hg_col, cos_t, sin_t)


def _kvproj_kernel(zc_ref, kr_ref, ng_ref, wk_ref, wvt_ref, hg_ref, c_ref, s_ref, k_ref, vt_ref):
    x = zc_ref[...]
    ms = jnp.mean(x * x, axis=-1, keepdims=True)
    xn = (x * lax.rsqrt(ms + EPS) * ng_ref[...]).astype(BF16)
    c = c_ref[...]
    s = s_ref[...]
    hg = hg_ref[...]
    kr = kr_ref[...]
    kr_ss = jnp.sum(kr * kr, axis=-1, keepdims=True)
    kr_rot = _rope(kr * hg[:, 128:], c, s)
    for hp in range(MLA_HEADS // 2):
        y = jnp.dot(xn, wk_ref[:, hp * 256:(hp + 1) * 256], preferred_element_type=F32)
        for sub in range(2):
            lo = (2 * hp + sub) * MLA_HEAD_PAD
            kn = y[:, sub * MLA_NOPE:(sub + 1) * MLA_NOPE]
            ms = (jnp.sum(kn * kn, axis=-1, keepdims=True) + kr_ss) * (1.0 / MLA_QK)
            rs = lax.rsqrt(ms + EPS)
            k_ref[:, lo:lo + 128] = (kn * rs * hg[:, :128]).astype(k_ref.dtype)
            k_ref[:, lo + 128:lo + 256] = (kr_rot * rs).astype(k_ref.dtype)
    rows_per_dot = 512
    for j in range(MLA_HEADS * MLA_V // rows_per_dot):
        sl = slice(j * rows_per_dot, (j + 1) * rows_per_dot)
        vt_ref[sl, :] = lax.dot_general(wvt_ref[sl, :], xn, _NT,
                                        preferred_element_type=F32).astype(vt_ref.dtype)


def _kvproj(z, zs, ng, wk, wvt, hg, ctab, stab, bm):
    m = z.shape[0]
    nt = ctab.shape[0] // bm
    return pl.pallas_call(
        _kvproj_kernel,
        grid=(m // bm,),
        in_specs=[pl.BlockSpec((bm, MLA_LORA), lambda i: (i, Z_CKV // MLA_LORA)),
                  pl.BlockSpec((bm, 128), lambda i: (i, 1)),
                  pl.BlockSpec((1, MLA_LORA), lambda i: (0, 0)),
                  pl.BlockSpec((MLA_LORA, MLA_HEADS * MLA_NOPE), lambda i: (0, 0)),
                  pl.BlockSpec((MLA_HEADS * MLA_V, MLA_LORA), lambda i: (0, 0)),
                  pl.BlockSpec((1, MLA_HEAD_PAD), lambda i: (0, 0)),
                  pl.BlockSpec((bm, 128), lambda i: (i % nt, 0)),
                  pl.BlockSpec((bm, 128), lambda i: (i % nt, 0))],
        out_specs=[pl.BlockSpec((bm, MLA_HEADS * MLA_HEAD_PAD), lambda i: (i, 0)),
                   pl.BlockSpec((MLA_HEADS * MLA_V, bm), lambda i: (0, i))],
        out_shape=[jax.ShapeDtypeStruct((m, MLA_HEADS * MLA_HEAD_PAD), BF16),
                   jax.ShapeDtypeStruct((MLA_HEADS * MLA_V, m), BF16)],
        compiler_params=_params("parallel"),
        name="kv_proj",
    )(z, zs, ng.reshape(1, -1), wk, wvt, hg, ctab, stab)


def _attn_kernel(qt_ref, k_ref, vt_ref, km_ref, vmt_ref, o_ref, m_sc, l_sc, acc_sc, s_sc, *, blk, heads,
                 qw, ahead):
    i = pl.program_id(2)
    qs = [qt_ref[h * MLA_HEAD_PAD:(h + 1) * MLA_HEAD_PAD, :] for h in range(heads)]

    for h in range(heads):
        s = jnp.dot(km_ref[:, h * MLA_HEAD_PAD:(h + 1) * MLA_HEAD_PAD], qs[h],
                    preferred_element_type=F32)
        m = jnp.max(s, axis=0, keepdims=True)
        p = jnp.exp2(s - m)
        m_sc[h] = m
        l_sc[h] = jnp.sum(p, axis=0, keepdims=True)
        acc_sc[h] = jnp.dot(vmt_ref[h * MLA_V:(h + 1) * MLA_V, :], p.astype(BF16),
                            preferred_element_type=F32)

    tasks = [(h, c) for c in range(blk // qw) for h in range(heads)]

    def scores(kb, h, c):
        ksl = pl.ds(pl.multiple_of(kb * blk, blk), blk)
        return jnp.dot(k_ref[ksl, h * MLA_HEAD_PAD:(h + 1) * MLA_HEAD_PAD],
                       qs[h][:, c * qw:(c + 1) * qw],
                       preferred_element_type=F32)

    for n in range(ahead):
        s_sc[n] = scores(0, *tasks[n])

    def step(kb, masked):
        ksl = pl.ds(pl.multiple_of(kb * blk, blk), blk)
        pending = [s_sc[n] for n in range(ahead)]
        handover = []
        for n, (h, c) in enumerate(tasks):
            csl = slice(c * qw, (c + 1) * qw)
            s = pending.pop(0)
            if n + ahead < len(tasks):
                pending.append(scores(kb, *tasks[n + ahead]))
            elif not masked:
                handover.append(scores(kb + 1, *tasks[n + ahead - len(tasks)]))
            if masked:
                key = lax.broadcasted_iota(jnp.int32, s.shape, 0)
                qry = lax.broadcasted_iota(jnp.int32, s.shape, 1) + c * qw
                s = jnp.where(key <= qry, s, -1e30)
            m_prev = m_sc[h, :, csl]
            m_new = jnp.maximum(m_prev, jnp.max(s, axis=0, keepdims=True))
            alpha = jnp.exp2(m_prev - m_new)
            p = jnp.exp2(s - m_new)
            l_sc[h, :, csl] = alpha * l_sc[h, :, csl] + jnp.sum(p, axis=0, keepdims=True)
            acc_sc[h, :, csl] = alpha * acc_sc[h, :, csl] + jnp.dot(
                vt_ref[h * MLA_V:(h + 1) * MLA_V, ksl], p.astype(BF16),
                preferred_element_type=F32)
            m_sc[h, :, csl] = m_new
        for n, s_next in enumerate(handover):
            s_sc[n] = s_next

    def body(kb, carry):
        step(kb, False)
        return carry

    lax.fori_loop(0, i, body, 0)
    step(i, True)
    for h in range(heads):
        o_ref[:, h * MLA_V:(h + 1) * MLA_V] = (acc_sc[h] * (1.0 / l_sc[h])).T.astype(o_ref.dtype)


def _attention(q, k, vt, km, vmt, blk, heads, qw, ahead):
    nq = SEQ // blk
    kern = functools.partial(_attn_kernel, blk=blk, heads=heads, qw=qw, ahead=ahead)
    return pl.pallas_call(
        kern,
        grid=(BATCH, MLA_HEADS // heads, nq),
        in_specs=[pl.BlockSpec((heads * MLA_HEAD_PAD, blk), lambda b, h, i: (h, b * nq + i)),
                  pl.BlockSpec((SEQ, heads * MLA_HEAD_PAD), lambda b, h, i: (b, h)),
                  pl.BlockSpec((heads * MLA_V, SEQ), lambda b, h, i: (h, b)),
                  pl.BlockSpec((N_META, heads * MLA_HEAD_PAD), lambda b, h, i: (0, h)),
                  pl.BlockSpec((heads * MLA_V, N_META), lambda b, h, i: (h, 0))],
        out_specs=pl.BlockSpec((blk, heads * MLA_V), lambda b, h, i: (b * nq + i, h)),
        out_shape=jax.ShapeDtypeStruct((ROWS, MLA_HEADS * MLA_V), BF16),
        scratch_shapes=[pltpu.VMEM((heads, 1, blk), F32), pltpu.VMEM((heads, 1, blk), F32),
                        pltpu.VMEM((heads, MLA_V, blk), F32), pltpu.VMEM((ahead, blk, qw), F32)],
        compiler_params=_params("parallel", "parallel", "arbitrary"),
        name="mla_attention",
    )(q, k, vt, km, vmt)


def _meta_attn_kernel(q_ref, km_ref, vmt_ref, o_ref):
    s = lax.dot_general(q_ref[...], km_ref[...], _NT, preferred_element_type=F32)
    row = lax.broadcasted_iota(jnp.int32, s.shape, 0)
    col = lax.broadcasted_iota(jnp.int32, s.shape, 1)
    s = jnp.where(col <= row, s, -1e30)
    m = jnp.max(s, axis=-1, keepdims=True)
    p = jnp.exp2(s - m)
    l = jnp.sum(p, axis=-1, keepdims=True)
    o = lax.dot_general(p.astype(BF16), vmt_ref[...], _NT, preferred_element_type=F32)
    o_ref[...] = (o / l).astype(o_ref.dtype)


def _meta_attention(qm, kmp, vmtp):
    return pl.pallas_call(
        _meta_attn_kernel,
        grid=(MLA_HEADS,),
        in_specs=[pl.BlockSpec((N_META, MLA_HEAD_PAD), lambda h: (0, h)),
                  pl.BlockSpec((128, MLA_HEAD_PAD), lambda h: (0, h)),
                  pl.BlockSpec((MLA_V, 128), lambda h: (h, 0))],
        out_specs=pl.BlockSpec((N_META, MLA_V), lambda h: (0, h)),
        out_shape=jax.ShapeDtypeStruct((N_META, MLA_HEADS * MLA_V), BF16),
        compiler_params=_params("parallel"),
        name="meta_attention",
    )(qm, kmp, vmtp)


def _merge_kernel(a_ref, b_ref, wa_ref, wb_ref, za_ref, zb_ref, o_ref):
    ya = jnp.dot(a_ref[...], wa_ref[...], preferred_element_type=F32)
    yb = jnp.dot(b_ref[...], wb_ref[...], preferred_element_type=F32)
    o_ref[...] = (_sigmoid(za_ref[...]) * ya + _sigmoid(zb_ref[...]) * yb).astype(o_ref.dtype)


def _merge(a, b, wa3, wb3, layer, z, bm, bn):
    m = a.shape[0]
    return pl.pallas_call(
        _merge_kernel,
        grid=(m // bm, D_MODEL // bn),
        in_specs=[pl.BlockSpec((bm, D_MODEL), lambda i, j: (i, 0)),
                  pl.BlockSpec((bm, D_MODEL), lambda i, j: (i, 0)),
                  pl.BlockSpec((None, D_MODEL, bn), lambda i, j: (layer, 0, j)),
                  pl.BlockSpec((None, D_MODEL, bn), lambda i, j: (layer, 0, j)),
                  pl.BlockSpec((bm, bn), lambda i, j: (i, Z_A // bn + j)),
                  pl.BlockSpec((bm, bn), lambda i, j: (i, Z_B // bn + j))],
        out_specs=pl.BlockSpec((bm, bn), lambda i, j: (i, j)),
        out_shape=jax.ShapeDtypeStruct((m, D_MODEL), BF16),
        compiler_params=_params("parallel", "parallel"),
        name="branch_merge",
    )(a, b, wa3, wb3, z, z)


def _outproj_kernel(m_ref, w_ref, h_ref, g_ref, ho_ref, hn_ref):
    hn = h_ref[...] + jnp.dot(m_ref[...], w_ref[...], preferred_element_type=F32)
    ho_ref[...] = hn
    ms = jnp.mean(hn * hn, axis=-1, keepdims=True)
    hn_ref[...] = (hn * lax.rsqrt(ms + EPS) * g_ref[...]).astype(hn_ref.dtype)


def _outproj(mg, w3, layer, h, g, bm):
    m = mg.shape[0]
    return pl.pallas_call(
        _outproj_kernel,
        grid=(m // bm,),
        in_specs=[pl.BlockSpec((bm, D_MODEL), lambda i: (i, 0)),
                  pl.BlockSpec((None, D_MODEL, D_MODEL), lambda i: (layer, 0, 0)),
                  pl.BlockSpec((bm, D_MODEL), lambda i: (i, 0)),
                  pl.BlockSpec((1, D_MODEL), lambda i: (0, 0))],
        out_specs=[pl.BlockSpec((bm, D_MODEL), lambda i: (i, 0)),
                   pl.BlockSpec((bm, D_MODEL), lambda i: (i, 0))],
        out_shape=[jax.ShapeDtypeStruct((m, D_MODEL), F32),
                   jax.ShapeDtypeStruct((m, D_MODEL), BF16)],
        compiler_params=_params("parallel"),
        name="out_proj",
    )(mg, w3, h, g.reshape(1, -1))


def _gateup_kernel(h_ref, wg_ref, wu_ref, o_ref, wgb_ref, wub_ref):
    @pl.when(pl.program_id(1) == 0)
    def _():
        wgb_ref[...] = wg_ref[...].astype(BF16)
        wub_ref[...] = wu_ref[...].astype(BF16)

    h = h_ref[...]
    g = jnp.dot(h, wgb_ref[...], preferred_element_type=F32)
    u = jnp.dot(h, wub_ref[...], preferred_element_type=F32)
    o_ref[...] = (g * _sigmoid(g) * u).astype(o_ref.dtype)


def _gateup(h, w3, layer, bm, bn):
    m = h.shape[0]
    nb = FF_HIDDEN // bn
    return pl.pallas_call(
        _gateup_kernel,
        grid=(nb, m // bm),
        in_specs=[pl.BlockSpec((bm, D_MODEL), lambda j, i: (i, 0)),
                  pl.BlockSpec((None, D_MODEL, bn), lambda j, i: (layer, 0, j)),
                  pl.BlockSpec((None, D_MODEL, bn), lambda j, i: (layer, 0, nb + j))],
        out_specs=pl.BlockSpec((bm, bn), lambda j, i: (i, j)),
        out_shape=jax.ShapeDtypeStruct((m, FF_HIDDEN), BF16),
        scratch_shapes=[pltpu.VMEM((D_MODEL, bn), BF16), pltpu.VMEM((D_MODEL, bn), BF16)],
        compiler_params=_params("arbitrary", "arbitrary"),
        name="gate_up",
    )(h, w3, w3)


def _down_kernel(a_ref, w_ref, h_ref, g_ref, ho_ref, hn_ref, acc_ref):
    k = pl.program_id(1)

    @pl.when(k == 0)
    def _():
        acc_ref[...] = jnp.zeros_like(acc_ref)

    acc_ref[...] += jnp.dot(a_ref[...], w_ref[...], preferred_element_type=F32)

    @pl.when(k == pl.num_programs(1) - 1)
    def _():
        hn = h_ref[...] + acc_ref[...]
        ho_ref[...] = hn
        ms = jnp.mean(hn * hn, axis=-1, keepdims=True)
        hn_ref[...] = (hn * lax.rsqrt(ms + EPS) * g_ref[...]).astype(hn_ref.dtype)


def _down(a, w3, layer, h, g, bm, bk):
    m = a.shape[0]
    return pl.pallas_call(
        _down_kernel,
        grid=(m // bm, FF_HIDDEN // bk),
        in_specs=[pl.BlockSpec((bm, bk), lambda i, k: (i, k)),
                  pl.BlockSpec((None, bk, D_MODEL), lambda i, k: (layer, k, 0)),
                  pl.BlockSpec((bm, D_MODEL), lambda i, k: (i, 0)),
                  pl.BlockSpec((1, D_MODEL), lambda i, k: (0, 0))],
        out_specs=[pl.BlockSpec((bm, D_MODEL), lambda i, k: (i, 0)),
                   pl.BlockSpec((bm, D_MODEL), lambda i, k: (i, 0))],
        out_shape=[jax.ShapeDtypeStruct((m, D_MODEL), F32),
                   jax.ShapeDtypeStruct((m, D_MODEL), BF16)],
        scratch_shapes=[pltpu.VMEM((bm, D_MODEL), F32)],
        compiler_params=_params("parallel", "arbitrary"),
        name="down_proj",
    )(a, w3, h, g.reshape(1, -1))


def _rope_layout(t):
    zeros = jnp.zeros(t.shape[:-1] + (32,), t.dtype)
    return jnp.concatenate([t[..., :32], zeros, t[..., 32:], zeros], axis=-1)


def _head_layout(t):
    return jnp.concatenate([t[..., :MLA_NOPE], _rope_layout(t[..., MLA_NOPE:])], axis=-1)


def _rope_tables():
    length = N_META + SEQ
    inv = 1.0 / (ROPE_THETA ** (jnp.arange(0, MLA_ROPE, 2, dtype=F32) / MLA_ROPE))
    ang = jnp.arange(length, dtype=F32)[:, None] * inv[None, :]
    cos, sin = jnp.cos(ang), jnp.sin(ang)
    zeros = jnp.zeros_like(cos)
    ctab = jnp.concatenate([cos, zeros, cos, zeros], axis=-1)
    stab = jnp.concatenate([-sin, zeros, sin, zeros], axis=-1)
    return ctab, stab, cos.T, sin.T


def kernel(x, meta_tokens, norm1_g, w_in, gla_gate_w2, gla_gate_b, gla_onorm_g, w_branch_a,
           q_a_norm_g, w_uq, kv_a_norm_g, w_ukv, q_norm_g, k_norm_g, w_branch_b, w_out,
           norm2_g, w_gate_up, w_down):
    ctab, stab, cos_t, sin_t = _rope_tables()
    ctab_m, stab_m = ctab[:N_META], stab[:N_META]
    ctab_r, stab_r = ctab[N_META:], stab[N_META:]
    cos_tm, sin_tm = cos_t[:, :N_META], sin_t[:, :N_META]
    cos_tr, sin_tr = cos_t[:, N_META:], sin_t[:, N_META:]

    h_res = x.reshape(ROWS, D_MODEL)
    hm_res = meta_tokens.astype(F32)
    h = _rms(h_res, norm1_g[0], 512)
    hm = _rms(hm_res, norm1_g[0], N_META)
    s_zero = jnp.zeros((GLA_HEADS, GLA_DV, GLA_DK), F32)

    w_in_t = jnp.swapaxes(w_in, 1, 2)
    w_kr = w_in_t[:, 7184:7248]
    zrow = jnp.zeros((DEPTH, 32, D_MODEL), F32)
    w_small = jnp.concatenate(
        [w_in_t[:, 6144:6160], jnp.zeros((DEPTH, 128 - GLA_GATE_RANK, D_MODEL), F32),
         w_kr[:, :32], zrow, w_kr[:, 32:], zrow], axis=1)
    wa = w_branch_a.astype(BF16)
    wb = w_branch_b.astype(BF16)
    wo = w_out.astype(BF16)
    wd = w_down.astype(BF16)

    for l in range(DEPTH):
        last = l == DEPTH - 1
        w2p = jnp.concatenate(
            [gla_gate_w2[l], jnp.zeros((128 - GLA_GATE_RANK, GLA_HEADS * GLA_DK), F32)],
            axis=0).astype(BF16)
        wqt = _head_layout(w_uq[l].reshape(MLA_LORA, MLA_HEADS, MLA_QK)).reshape(
            MLA_LORA, MLA_HEADS * MLA_HEAD_PAD).T.astype(BF16)
        wkv3 = w_ukv[l].reshape(MLA_LORA, MLA_HEADS, MLA_NOPE + MLA_V)
        wk = wkv3[:, :, :MLA_NOPE].reshape(MLA_LORA, MLA_HEADS * MLA_NOPE).astype(BF16)
        wvt = wkv3[:, :, MLA_NOPE:].reshape(MLA_LORA, MLA_HEADS * MLA_V).T.astype(BF16)
        qg = (_head_layout(q_norm_g[l]) * (MLA_QK ** -0.5 * LOG2E)).reshape(MLA_HEAD_PAD, 1)
        kg = _head_layout(k_norm_g[l]).reshape(1, MLA_HEAD_PAD)
        g_next = norm1_g[l + 1] if not last else norm1_g[l]

        z = _in_proj(h, w_in_t, l, 1024, 1024, name="in_proj")
        zs = _matmul_nt(h, w_small, l, 1024, N_SMALL, name="in_proj_small")
        zm = _in_proj(hm, w_in_t, l, N_META, 1024, name="in_proj_meta")
        zsm = _matmul_nt(hm, w_small, l, N_META, N_SMALL, name="in_proj_small_meta")

        gla_m, s_meta = _gla(zm, zsm, w2p, gla_gate_b[l], gla_onorm_g[l], s_zero,
                             batch=1, tokens=N_META, block=N_META, chunk=N_META, emit_state=True)
        (gla_r,) = _gla(z, zs, w2p, gla_gate_b[l], gla_onorm_g[l], s_meta,
                        batch=BATCH, tokens=SEQ, block=512, chunk=GLA_CHUNK, emit_state=False)

        km, vmt = _kvproj(zm, zsm, kv_a_norm_g[l], wk, wvt, kg, ctab_m, stab_m, N_META)
        kr, vrt = _kvproj(z, zs, kv_a_norm_g[l], wk, wvt, kg, ctab_r, stab_r, 512)
        kmp = jnp.pad(km, ((0, 128 - N_META), (0, 0)))
        vmtp = jnp.pad(vmt, ((0, 0), (0, 128 - N_META)))
        qr = _qproj(z, q_a_norm_g[l], wqt, qg, cos_tr, sin_tr, 512)
        att_r = _attention(qr, kr, vrt, km, vmt, 512, 4, 512, 4)

        merged = _merge(gla_r, att_r, wa, wb, l, z, 1024, 512)
        h_res, h2 = _outproj(merged, wo, l, h_res, norm2_g[l], 512)
        act = _gateup(h2, w_gate_up, l, 1024, 512)
        h_res, h = _down(act, wd, l, h_res, g_next, 512, 2816)

        if not last:
            qm = _qproj(zm, q_a_norm_g[l], wqt, qg, cos_tm, sin_tm, N_META)
            att_m = _meta_attention(qm.T, kmp, vmtp)
            merged_m = _merge(gla_m, att_m, wa, wb, l, zm, N_META, 512)
            hm_res, hm2 = _outproj(merged_m, wo, l, hm_res, norm2_g[l], N_META)
            act_m = _gateup(hm2, w_gate_up, l, N_META, 512)
            hm_res, hm = _down(act_m, wd, l, hm_res, g_next, N_META, 1408)

    return h_res.reshape(BATCH, SEQ, D_MODEL)
```

```python
import functools

import jax
import jax.numpy as jnp
from jax import lax
from jax.experimental import pallas as pl
from jax.experimental.pallas import tpu as pltpu

D_MODEL = 2048
BATCH = 4
SEQ = 4096
DEPTH = 2
N_META = 16
ROWS = BATCH * SEQ

GLA_HEADS = 4
GLA_DK = 256
GLA_DV = 512
GLA_GATE_RANK = 16
GLA_GATE_TAU = 16.0
GLA_CHUNK = 64

MLA_HEADS = 16
MLA_LORA = 512
MLA_NOPE = 128
MLA_ROPE = 64
MLA_QK = 192
MLA_V = 128
MLA_HEAD_PAD = 256
ROPE_THETA = 10000.0
FF_HIDDEN = 5632
EPS = 1e-6
LOG2E = 1.4426950408889634

W_IN_ZA = 6144
W_IN_C, W_IN_C_END = 6160, 7184
W_IN_G = 7248
Z_Q, Z_K, Z_V, Z_R, Z_CQ, Z_CKV, Z_A, Z_B = 0, 1024, 2048, 4096, 6144, 6656, 7168, 9216
N_Z = 11264
N_SMALL = 256

VMEM_LIMIT = 56 * 1024 * 1024
BF16 = jnp.bfloat16
F32 = jnp.float32

_NT = (((1,), (1,)), ((), ()))
_TN = (((0,), (0,)), ((), ()))


def _params(*sem):
    return pltpu.CompilerParams(dimension_semantics=sem, vmem_limit_bytes=VMEM_LIMIT)


def _sigmoid(x):
    return 1.0 / (1.0 + jnp.exp(-x))


def _rms_kernel(x_ref, g_ref, o_ref):
    x = x_ref[...]
    ms = jnp.mean(x * x, axis=-1, keepdims=True)
    o_ref[...] = (x * lax.rsqrt(ms + EPS) * g_ref[...]).astype(o_ref.dtype)


def _rms(x, g, bm):
    m, d = x.shape
    return pl.pallas_call(
        _rms_kernel,
        grid=(m // bm,),
        in_specs=[pl.BlockSpec((bm, d), lambda i: (i, 0)),
                  pl.BlockSpec((1, d), lambda i: (0, 0))],
        out_specs=pl.BlockSpec((bm, d), lambda i: (i, 0)),
        out_shape=jax.ShapeDtypeStruct((m, d), BF16),
        compiler_params=_params("parallel"),
        name="rmsnorm",
    )(x, g.reshape(1, d))


def _mm_nt_kernel(a_ref, wt_ref, o_ref):
    o_ref[...] = lax.dot_general(a_ref[...], wt_ref[...].astype(BF16), _NT,
                                 preferred_element_type=F32)


def _matmul_nt(a, wt3, layer, bm, bn, name):
    m, k = a.shape
    n = wt3.shape[1]
    return pl.pallas_call(
        _mm_nt_kernel,
        grid=(m // bm, n // bn),
        in_specs=[pl.BlockSpec((bm, k), lambda i, j: (i, 0)),
                  pl.BlockSpec((None, bn, k), lambda i, j: (layer, j, 0))],
        out_specs=pl.BlockSpec((bm, bn), lambda i, j: (i, j)),
        out_shape=jax.ShapeDtypeStruct((m, n), F32),
        compiler_params=_params("parallel", "parallel"),
        name=name,
    )(a, wt3)


def _mm_nt_wstat_kernel(a_ref, wt_ref, o_ref, wb_ref):
    @pl.when(pl.program_id(1) == 0)
    def _():
        wb_ref[...] = wt_ref[0].astype(BF16)

    o_ref[...] = lax.dot_general(a_ref[...], wb_ref[...], _NT, preferred_element_type=F32)


def _in_proj_row0(j, bn):
    n_a = W_IN_ZA // bn
    n_c = (W_IN_C_END - W_IN_C) // bn
    return jnp.where(j < n_a, j * bn,
                     jnp.where(j < n_a + n_c, W_IN_C + (j - n_a) * bn,
                               W_IN_G + (j - n_a - n_c) * bn))


def _in_proj(a, wt3, layer, bm, bn, name):
    m, k = a.shape
    return pl.pallas_call(
        _mm_nt_wstat_kernel,
        grid=(N_Z // bn, m // bm),
        in_specs=[pl.BlockSpec((bm, k), lambda j, i: (i, 0)),
                  pl.BlockSpec((pl.Element(1), pl.Element(bn), pl.Element(k)),
                               lambda j, i: (layer, pl.multiple_of(_in_proj_row0(j, bn), 16), 0))],
        out_specs=pl.BlockSpec((bm, bn), lambda j, i: (i, j)),
        out_shape=jax.ShapeDtypeStruct((m, N_Z), F32),
        scratch_shapes=[pltpu.VMEM((bn, k), BF16)],
        compiler_params=_params("arbitrary", "arbitrary"),
        name=name,
    )(a, wt3)


def _gla_kernel(q_ref, k_ref, v_ref, r_ref, zlr_ref, w2_ref, gb_ref, og_ref, s0_ref,
                o_ref, *rest, chunk, n_chunks, emit_state):
    if emit_state:
        sfin_ref, st_ref = rest
    else:
        (st_ref,) = rest
    t = pl.program_id(1)
    heads = range(GLA_HEADS)

    @pl.when(t == 0)
    def _():
        st_ref[...] = s0_ref[...]

    row = lax.broadcasted_iota(jnp.int32, (chunk, chunk), 0)
    col = lax.broadcasted_iota(jnp.int32, (chunk, chunk), 1)
    causal = col <= row
    tri = jnp.where(causal, 1.0, 0.0).astype(BF16)
    w2 = w2_ref[...]
    gb = gb_ref[...]
    og = og_ref[...]
    kdim = GLA_HEADS * GLA_DK

    def body(c, carry):
        sl = pl.ds(pl.multiple_of(c * chunk, chunk), chunk)
        logit = jnp.dot(zlr_ref[sl, :].astype(BF16), w2, preferred_element_type=F32) + gb
        g = (jnp.minimum(logit, 0.0) - jnp.log1p(jnp.exp(-jnp.abs(logit)))) * (1.0 / GLA_GATE_TAU)
        g_hi = g.astype(BF16)
        g_lo = (g - g_hi.astype(F32)).astype(BF16)
        cs = jnp.dot(tri, jnp.concatenate([g_hi, g_lo], axis=1), preferred_element_type=F32)
        b = cs[:, :kdim] + cs[:, kdim:]
        b_last = b[chunk - 1:chunk, :]
        eb = jnp.exp(b)
        enb = jnp.exp(-b)
        erel = jnp.exp(b_last - b)
        elast = jnp.exp(b_last)
        k = k_ref[sl, :]
        qd = (q_ref[sl, :] * (GLA_DK ** -0.5) * eb).astype(BF16)
        kd = (k * enb).astype(BF16)
        k2 = (k * erel).astype(BF16)
        v = v_ref[sl, :].astype(BF16)

        def hk(x, h):
            return x[:, h * GLA_DK:(h + 1) * GLA_DK]

        def hv(x, h):
            return x[:, h * GLA_DV:(h + 1) * GLA_DV]

        a = [lax.dot_general(hk(qd, h), hk(kd, h), _NT, preferred_element_type=F32) for h in heads]
        st = [st_ref[h] for h in heads]
        o_state = [lax.dot_general(hk(qd, h), st[h].astype(BF16), _NT, preferred_element_type=F32)
                   for h in heads]
        o_local = [jnp.dot(jnp.where(causal, a[h], 0.0).astype(BF16), hv(v, h),
                           preferred_element_type=F32) for h in heads]
        upd = [lax.dot_general(hv(v, h), hk(k2, h), _TN, preferred_element_type=F32)
               for h in heads]
        for h in heads:
            st_ref[h] = st[h] * hk(elast, h) + upd[h]
        for h in heads:
            o = o_local[h] + o_state[h]
            ms = jnp.mean(o * o, axis=-1, keepdims=True)
            on = o * lax.rsqrt(ms + EPS) * og
            r = r_ref[sl, h * GLA_DV:(h + 1) * GLA_DV]
            o_ref[sl, h * GLA_DV:(h + 1) * GLA_DV] = (on * (r * _sigmoid(r))).astype(o_ref.dtype)
        return carry

    lax.fori_loop(0, n_chunks, body, 0)

    if emit_state:
        @pl.when(t == pl.num_programs(1) - 1)
        def _():
            sfin_ref[...] = st_ref[...]


def _gla(z, zs, w2p, gate_b, onorm_g, s0, *, batch, tokens, block, chunk, emit_state):
    nt = tokens // block
    rows = batch * tokens
    kern = functools.partial(_gla_kernel, chunk=chunk, n_chunks=block // chunk,
                             emit_state=emit_state)
    kdim = GLA_HEADS * GLA_DK
    vdim = GLA_HEADS * GLA_DV
    in_specs = [
        pl.BlockSpec((block, kdim), lambda b, t: (b * nt + t, Z_Q // kdim)),
        pl.BlockSpec((block, kdim), lambda b, t: (b * nt + t, Z_K // kdim)),
        pl.BlockSpec((block, vdim), lambda b, t: (b * nt + t, Z_V // vdim)),
        pl.BlockSpec((block, vdim), lambda b, t: (b * nt + t, Z_R // vdim)),
        pl.BlockSpec((block, 128), lambda b, t: (b * nt + t, 0)),
        pl.BlockSpec((128, kdim), lambda b, t: (0, 0)),
        pl.BlockSpec((1, kdim), lambda b, t: (0, 0)),
        pl.BlockSpec((1, GLA_DV), lambda b, t: (0, 0)),
        pl.BlockSpec((GLA_HEADS, GLA_DV, GLA_DK), lambda b, t: (0, 0, 0)),
    ]
    out_specs = [pl.BlockSpec((block, vdim), lambda b, t: (b * nt + t, 0))]
    out_shape = [jax.ShapeDtypeStruct((rows, vdim), BF16)]
    if emit_state:
        out_specs.append(pl.BlockSpec((GLA_HEADS, GLA_DV, GLA_DK), lambda b, t: (b, 0, 0)))
        out_shape.append(jax.ShapeDtypeStruct((batch * GLA_HEADS, GLA_DV, GLA_DK), F32))
    res = pl.pallas_call(
        kern,
        grid=(batch, nt),
        in_specs=in_specs,
        out_specs=out_specs,
        out_shape=out_shape,
        scratch_shapes=[pltpu.VMEM((GLA_HEADS, GLA_DV, GLA_DK), F32)],
        compiler_params=_params("parallel", "arbitrary"),
        name="gla",
    )(z, z, z, z, zs, w2p, gate_b.reshape(1, -1), onorm_g.reshape(1, -1), s0)
    return res


def _rope(x, c, s):
    return x * c + pltpu.roll(x, 64, 1) * s


def _qproj_kernel(zc_ref, ng_ref, wt_ref, hg_ref, c_ref, s_ref, o_ref, *, heads_per_dot):
    x = zc_ref[...]
    ms = jnp.mean(x * x, axis=-1, keepdims=True)
    xn = (x * lax.rsqrt(ms + EPS) * ng_ref[...]).astype(BF16)
    bm = x.shape[0]
    c = c_ref[...]
    s = s_ref[...]
    hg = jnp.broadcast_to(hg_ref[...], (MLA_HEAD_PAD, bm))
    for hc in range(MLA_HEADS // heads_per_dot):
        base = hc * heads_per_dot * MLA_HEAD_PAD
        yt = lax.dot_general(wt_ref[base:base + heads_per_dot * MLA_HEAD_PAD, :], xn, _NT,
                             preferred_element_type=F32)
        for j in range(heads_per_dot):
            lo = base + j * MLA_HEAD_PAD
            y = yt[j * MLA_HEAD_PAD:(j + 1) * MLA_HEAD_PAD]
            ms = jnp.sum(y * y, axis=0, keepdims=True) * (1.0 / MLA_QK)
            yn = y * lax.rsqrt(ms + EPS) * hg
            x1 = yn[128:160]
            x2 = yn[192:224]
            o_ref[lo:lo + 128, :] = yn[:128].astype(o_ref.dtype)
            o_ref[lo + 128:lo + 160, :] = (x1 * c - x2 * s).astype(o_ref.dtype)
            o_ref[lo + 160:lo + 192, :] = yn[160:192].astype(o_ref.dtype)
            o_ref[lo + 192:lo + 224, :] = (x2 * c + x1 * s).astype(o_ref.dtype)
            o_ref[lo + 224:lo + 256, :] = yn[224:256].astype(o_ref.dtype)


def _qproj(z, ng, wt, hg_col, cos_t, sin_t, bm):
    m = z.shape[0]
    nt = cos_t.shape[1] // bm
    kern = functools.partial(_qproj_kernel, heads_per_dot=4)
    return pl.pallas_call(
        kern,
        grid=(m // bm,),
        in_specs=[pl.BlockSpec((bm, MLA_LORA), lambda i: (i, Z_CQ // MLA_LORA)),
                  pl.BlockSpec((1, MLA_LORA), lambda i: (0, 0)),
                  pl.BlockSpec((MLA_HEADS * MLA_HEAD_PAD, MLA_LORA), lambda i: (0, 0)),
                  pl.BlockSpec((MLA_HEAD_PAD, 1), lambda i: (0, 0)),
                  pl.BlockSpec((MLA_ROPE // 2, bm), lambda i: (0, i % nt)),
                  pl.BlockSpec((MLA_ROPE // 2, bm), lambda i: (0, i % nt))],
        out_specs=pl.BlockSpec((MLA_HEADS * MLA_HEAD_PAD, bm), lambda i: (0, i)),
        out_shape=jax.ShapeDtypeStruct((MLA_HEADS * MLA_HEAD_PAD, m), BF16),
        compiler_params=_params("parallel"),
        name="q_proj",
    )(z, ng.reshape(1, -1), wt, hg_col, cos_t, sin_t)


def _kvproj_kernel(zc_ref, kr_ref, ng_ref, wk_ref, wvt_ref, hg_ref, c_ref, s_ref, k_ref, vt_ref):
    x = zc_ref[...]
    ms = jnp.mean(x * x, axis=-1, keepdims=True)
    xn = (x * lax.rsqrt(ms + EPS) * ng_ref[...]).astype(BF16)
    c = c_ref[...]
    s = s_ref[...]
    hg = hg_ref[...]
    kr = kr_ref[...]
    kr_ss = jnp.sum(kr * kr, axis=-1, keepdims=True)
    kr_rot = _rope(kr * hg[:, 128:], c, s)
    for hp in range(MLA_HEADS // 2):
        y = jnp.dot(xn, wk_ref[:, hp * 256:(hp + 1) * 256], preferred_element_type=F32)
        for sub in range(2):
            lo = (2 * hp + sub) * MLA_HEAD_PAD
            kn = y[:, sub * MLA_NOPE:(sub + 1) * MLA_NOPE]
            ms = (jnp.sum(kn * kn, axis=-1, keepdims=True) + kr_ss) * (1.0 / MLA_QK)
            rs = lax.rsqrt(ms + EPS)
            k_ref[:, lo:lo + 128] = (kn * rs * hg[:, :128]).astype(k_ref.dtype)
            k_ref[:, lo + 128:lo + 256] = (kr_rot * rs).astype(k_ref.dtype)
    rows_per_dot = 512
    for j in range(MLA_HEADS * MLA_V // rows_per_dot):
        sl = slice(j * rows_per_dot, (j + 1) * rows_per_dot)
        vt_ref[sl, :] = lax.dot_general(wvt_ref[sl, :], xn, _NT,
                                        preferred_element_type=F32).astype(vt_ref.dtype)


def _kvproj(z, zs, ng, wk, wvt, hg, ctab, stab, bm):
    m = z.shape[0]
    nt = ctab.shape[0] // bm
    return pl.pallas_call(
        _kvproj_kernel,
        grid=(m // bm,),
        in_specs=[pl.BlockSpec((bm, MLA_LORA), lambda i: (i, Z_CKV // MLA_LORA)),
                  pl.BlockSpec((bm, 128), lambda i: (i, 1)),
                  pl.BlockSpec((1, MLA_LORA), lambda i: (0, 0)),
                  pl.BlockSpec((MLA_LORA, MLA_HEADS * MLA_NOPE), lambda i: (0, 0)),
                  pl.BlockSpec((MLA_HEADS * MLA_V, MLA_LORA), lambda i: (0, 0)),
                  pl.BlockSpec((1, MLA_HEAD_PAD), lambda i: (0, 0)),
                  pl.BlockSpec((bm, 128), lambda i: (i % nt, 0)),
                  pl.BlockSpec((bm, 128), lambda i: (i % nt, 0))],
        out_specs=[pl.BlockSpec((bm, MLA_HEADS * MLA_HEAD_PAD), lambda i: (i, 0)),
                   pl.BlockSpec((MLA_HEADS * MLA_V, bm), lambda i: (0, i))],
        out_shape=[jax.ShapeDtypeStruct((m, MLA_HEADS * MLA_HEAD_PAD), BF16),
                   jax.ShapeDtypeStruct((MLA_HEADS * MLA_V, m), BF16)],
        compiler_params=_params("parallel"),
        name="kv_proj",
    )(z, zs, ng.reshape(1, -1), wk, wvt, hg, ctab, stab)


def _attn_kernel(qt_ref, k_ref, vt_ref, km_ref, vmt_ref, o_ref, m_sc, l_sc, acc_sc, s_sc, *, blk, heads,
                 qw, ahead, nq):
    i = pl.program_id(2)

    def queries(qi, h, lo, width):
        return qt_ref[h * MLA_HEAD_PAD:(h + 1) * MLA_HEAD_PAD,
                      pl.ds(pl.multiple_of(qi * blk + lo, width), width)]

    for h in range(heads):
        s = jnp.dot(km_ref[:, h * MLA_HEAD_PAD:(h + 1) * MLA_HEAD_PAD], queries(i, h, 0, blk),
                    preferred_element_type=F32)
        m = jnp.max(s, axis=0, keepdims=True)
        p = jnp.exp2(s - m)
        m_sc[h] = m
        l_sc[h] = jnp.sum(p, axis=0, keepdims=True)
        acc_sc[h] = jnp.dot(vmt_ref[h * MLA_V:(h + 1) * MLA_V, :], p.astype(BF16),
                            preferred_element_type=F32)

    tasks = [(h, c) for c in range(blk // qw) for h in range(heads)]

    def scores(qi, kb, h, c):
        ksl = pl.ds(pl.multiple_of(kb * blk, blk), blk)
        return jnp.dot(k_ref[ksl, h * MLA_HEAD_PAD:(h + 1) * MLA_HEAD_PAD],
                       queries(qi, h, c * qw, qw),
                       preferred_element_type=F32)

    @pl.when(i == 0)
    def _():
        for n in range(ahead):
            s_sc[n] = scores(0, 0, *tasks[n])

    def step(kb, masked):
        ksl = pl.ds(pl.multiple_of(kb * blk, blk), blk)
        pending = [s_sc[n] for n in range(ahead)]
        handover = []
        for n, (h, c) in enumerate(tasks):
            csl = slice(c * qw, (c + 1) * qw)
            s = pending.pop(0)
            if n + ahead < len(tasks):
                pending.append(scores(i, kb, *tasks[n + ahead]))
            elif not masked:
                handover.append(scores(i, kb + 1, *tasks[n + ahead - len(tasks)]))
            else:
                handover.append(scores(jnp.minimum(i + 1, nq - 1), 0,
                                       *tasks[n + ahead - len(tasks)]))
            if masked:
                key = lax.broadcasted_iota(jnp.int32, s.shape, 0)
                qry = lax.broadcasted_iota(jnp.int32, s.shape, 1) + c * qw
                s = jnp.where(key <= qry, s, -1e30)
            m_prev = m_sc[h, :, csl]
            m_new = jnp.maximum(m_prev, jnp.max(s, axis=0, keepdims=True))
            alpha = jnp.exp2(m_prev - m_new)
            p = jnp.exp2(s - m_new)
            l_sc[h, :, csl] = alpha * l_sc[h, :, csl] + jnp.sum(p, axis=0, keepdims=True)
            acc_sc[h, :, csl] = alpha * acc_sc[h, :, csl] + jnp.dot(
                vt_ref[h * MLA_V:(h + 1) * MLA_V, ksl], p.astype(BF16),
                preferred_element_type=F32)
            m_sc[h, :, csl] = m_new
        for n, s_next in enumerate(handover):
            s_sc[n] = s_next

    def body(kb, carry):
        step(kb, False)
        return carry

    lax.fori_loop(0, i, body, 0)
    step(i, True)
    for h in range(heads):
        o_ref[:, h * MLA_V:(h + 1) * MLA_V] = (acc_sc[h] * (1.0 / l_sc[h])).T.astype(o_ref.dtype)


def _attention(q, k, vt, km, vmt, blk, heads, qw, ahead):
    nq = SEQ // blk
    kern = functools.partial(_attn_kernel, blk=blk, heads=heads, qw=qw, ahead=ahead, nq=nq)
    return pl.pallas_call(
        kern,
        grid=(BATCH, MLA_HEADS // heads, nq),
        in_specs=[pl.BlockSpec((heads * MLA_HEAD_PAD, SEQ), lambda b, h, i: (h, b)),
                  pl.BlockSpec((SEQ, heads * MLA_HEAD_PAD), lambda b, h, i: (b, h)),
                  pl.BlockSpec((heads * MLA_V, SEQ), lambda b, h, i: (h, b)),
                  pl.BlockSpec((N_META, heads * MLA_HEAD_PAD), lambda b, h, i: (0, h)),
                  pl.BlockSpec((heads * MLA_V, N_META), lambda b, h, i: (h, 0))],
        out_specs=pl.BlockSpec((blk, heads * MLA_V), lambda b, h, i: (b * nq + i, h)),
        out_shape=jax.ShapeDtypeStruct((ROWS, MLA_HEADS * MLA_V), BF16),
        scratch_shapes=[pltpu.VMEM((heads, 1, blk), F32), pltpu.VMEM((heads, 1, blk), F32),
                        pltpu.VMEM((heads, MLA_V, blk), F32), pltpu.VMEM((ahead, blk, qw), F32)],
        compiler_params=_params("parallel", "parallel", "arbitrary"),
        name="mla_attention",
    )(q, k, vt, km, vmt)


def _meta_attn_kernel(q_ref, km_ref, vmt_ref, o_ref):
    s = lax.dot_general(q_ref[...], km_ref[...], _NT, preferred_element_type=F32)
    row = lax.broadcasted_iota(jnp.int32, s.shape, 0)
    col = lax.broadcasted_iota(jnp.int32, s.shape, 1)
    s = jnp.where(col <= row, s, -1e30)
    m = jnp.max(s, axis=-1, keepdims=True)
    p = jnp.exp2(s - m)
    l = jnp.sum(p, axis=-1, keepdims=True)
    o = lax.dot_general(p.astype(BF16), vmt_ref[...], _NT, preferred_element_type=F32)
    o_ref[...] = (o / l).astype(o_ref.dtype)


def _meta_attention(qm, kmp, vmtp):
    return pl.pallas_call(
        _meta_attn_kernel,
        grid=(MLA_HEADS,),
        in_specs=[pl.BlockSpec((N_META, MLA_HEAD_PAD), lambda h: (0, h)),
                  pl.BlockSpec((128, MLA_HEAD_PAD), lambda h: (0, h)),
                  pl.BlockSpec((MLA_V, 128), lambda h: (h, 0))],
        out_specs=pl.BlockSpec((N_META, MLA_V), lambda h: (0, h)),
        out_shape=jax.ShapeDtypeStruct((N_META, MLA_HEADS * MLA_V), BF16),
        compiler_params=_params("parallel"),
        name="meta_attention",
    )(qm, kmp, vmtp)


def _merge_kernel(a_ref, b_ref, wa_ref, wb_ref, za_ref, zb_ref, o_ref):
    ga = _sigmoid(za_ref[...])
    gb = _sigmoid(zb_ref[...])
    ya = jnp.dot(a_ref[...], wa_ref[...], preferred_element_type=F32)
    yb = jnp.dot(b_ref[...], wb_ref[...], preferred_element_type=F32)
    o_ref[...] = (ga * ya + gb * yb).astype(o_ref.dtype)


def _merge(a, b, wa3, wb3, layer, z, bm, bn):
    m = a.shape[0]
    return pl.pallas_call(
        _merge_kernel,
        grid=(m // bm, D_MODEL // bn),
        in_specs=[pl.BlockSpec((bm, D_MODEL), lambda i, j: (i, 0)),
                  pl.BlockSpec((bm, D_MODEL), lambda i, j: (i, 0)),
                  pl.BlockSpec((None, D_MODEL, bn), lambda i, j: (layer, 0, j)),
                  pl.BlockSpec((None, D_MODEL, bn), lambda i, j: (layer, 0, j)),
                  pl.BlockSpec((bm, bn), lambda i, j: (i, Z_A // bn + j)),
                  pl.BlockSpec((bm, bn), lambda i, j: (i, Z_B // bn + j))],
        out_specs=pl.BlockSpec((bm, bn), lambda i, j: (i, j)),
        out_shape=jax.ShapeDtypeStruct((m, D_MODEL), BF16),
        compiler_params=_params("parallel", "parallel"),
        name="branch_merge",
    )(a, b, wa3, wb3, z, z)


def _outproj_kernel(m_ref, w_ref, h_ref, g_ref, ho_ref, hn_ref):
    hn = h_ref[...] + jnp.dot(m_ref[...], w_ref[...], preferred_element_type=F32)
    ho_ref[...] = hn
    ms = jnp.mean(hn * hn, axis=-1, keepdims=True)
    hn_ref[...] = (hn * lax.rsqrt(ms + EPS) * g_ref[...]).astype(hn_ref.dtype)


def _outproj(mg, w3, layer, h, g, bm):
    m = mg.shape[0]
    return pl.pallas_call(
        _outproj_kernel,
        grid=(m // bm,),
        in_specs=[pl.BlockSpec((bm, D_MODEL), lambda i: (i, 0)),
                  pl.BlockSpec((None, D_MODEL, D_MODEL), lambda i: (layer, 0, 0)),
                  pl.BlockSpec((bm, D_MODEL), lambda i: (i, 0)),
                  pl.BlockSpec((1, D_MODEL), lambda i: (0, 0))],
        out_specs=[pl.BlockSpec((bm, D_MODEL), lambda i: (i, 0)),
                   pl.BlockSpec((bm, D_MODEL), lambda i: (i, 0))],
        out_shape=[jax.ShapeDtypeStruct((m, D_MODEL), F32),
                   jax.ShapeDtypeStruct((m, D_MODEL), BF16)],
        compiler_params=_params("parallel"),
        name="out_proj",
    )(mg, w3, h, g.reshape(1, -1))


def _gateup_kernel(h_ref, wg_ref, wu_ref, o_ref, wgb_ref, wub_ref):
    @pl.when(pl.program_id(1) == 0)
    def _():
        wgb_ref[...] = wg_ref[...].astype(BF16)
        wub_ref[...] = wu_ref[...].astype(BF16)

    h = h_ref[...]
    g = jnp.dot(h, wgb_ref[...], preferred_element_type=F32)
    u = jnp.dot(h, wub_ref[...], preferred_element_type=F32)
    o_ref[...] = (g * _sigmoid(g) * u).astype(o_ref.dtype)


def _gateup(h, w3, layer, bm, bn):
    m = h.shape[0]
    nb = FF_HIDDEN // bn
    return pl.pallas_call(
        _gateup_kernel,
        grid=(nb, m // bm),
        in_specs=[pl.BlockSpec((bm, D_MODEL), lambda j, i: (i, 0)),
                  pl.BlockSpec((None, D_MODEL, bn), lambda j, i: (layer, 0, j)),
                  pl.BlockSpec((None, D_MODEL, bn), lambda j, i: (layer, 0, nb + j))],
        out_specs=pl.BlockSpec((bm, bn), lambda j, i: (i, j)),
        out_shape=jax.ShapeDtypeStruct((m, FF_HIDDEN), BF16),
        scratch_shapes=[pltpu.VMEM((D_MODEL, bn), BF16), pltpu.VMEM((D_MODEL, bn), BF16)],
        compiler_params=_params("arbitrary", "arbitrary"),
        name="gate_up",
    )(h, w3, w3)


def _down_kernel(a_ref, w_ref, h_ref, g_ref, ho_ref, hn_ref, acc_ref):
    k = pl.program_id(1)
    last = pl.num_programs(1) - 1

    def product():
        return jnp.dot(a_ref[...], w_ref[...], preferred_element_type=F32)

    @pl.when(k == 0)
    def _():
        acc_ref[...] = product()

    @pl.when((k > 0) & (k < last))
    def _():
        acc_ref[...] += product()

    @pl.when(k == last)
    def _():
        hn = h_ref[...] + (acc_ref[...] + product())
        ho_ref[...] = hn
        ms = jnp.mean(hn * hn, axis=-1, keepdims=True)
        hn_ref[...] = (hn * lax.rsqrt(ms + EPS) * g_ref[...]).astype(hn_ref.dtype)


def _down(a, w3, layer, h, g, bm, bk):
    m = a.shape[0]
    assert FF_HIDDEN // bk >= 2
    return pl.pallas_call(
        _down_kernel,
        grid=(m // bm, FF_HIDDEN // bk),
        in_specs=[pl.BlockSpec((bm, bk), lambda i, k: (i, k)),
                  pl.BlockSpec((None, bk, D_MODEL), lambda i, k: (layer, k, 0)),
                  pl.BlockSpec((bm, D_MODEL), lambda i, k: (i, 0)),
                  pl.BlockSpec((1, D_MODEL), lambda i, k: (0, 0))],
        out_specs=[pl.BlockSpec((bm, D_MODEL), lambda i, k: (i, 0)),
                   pl.BlockSpec((bm, D_MODEL), lambda i, k: (i, 0))],
        out_shape=[jax.ShapeDtypeStruct((m, D_MODEL), F32),
                   jax.ShapeDtypeStruct((m, D_MODEL), BF16)],
        scratch_shapes=[pltpu.VMEM((bm, D_MODEL), F32)],
        compiler_params=_params("parallel", "arbitrary"),
        name="down_proj",
    )(a, w3, h, g.reshape(1, -1))


def _rope_layout(t):
    zeros = jnp.zeros(t.shape[:-1] + (32,), t.dtype)
    return jnp.concatenate([t[..., :32], zeros, t[..., 32:], zeros], axis=-1)


def _head_layout(t):
    return jnp.concatenate([t[..., :MLA_NOPE], _rope_layout(t[..., MLA_NOPE:])], axis=-1)


def _rope_tables():
    length = N_META + SEQ
    inv = 1.0 / (ROPE_THETA ** (jnp.arange(0, MLA_ROPE, 2, dtype=F32) / MLA_ROPE))
    ang = jnp.arange(length, dtype=F32)[:, None] * inv[None, :]
    cos, sin = jnp.cos(ang), jnp.sin(ang)
    zeros = jnp.zeros_like(cos)
    ctab = jnp.concatenate([cos, zeros, cos, zeros], axis=-1)
    stab = jnp.concatenate([-sin, zeros, sin, zeros], axis=-1)
    return ctab, stab, cos.T, sin.T


def kernel(x, meta_tokens, norm1_g, w_in, gla_gate_w2, gla_gate_b, gla_onorm_g, w_branch_a,
           q_a_norm_g, w_uq, kv_a_norm_g, w_ukv, q_norm_g, k_norm_g, w_branch_b, w_out,
           norm2_g, w_gate_up, w_down):
    ctab, stab, cos_t, sin_t = _rope_tables()
    ctab_m, stab_m = ctab[:N_META], stab[:N_META]
    ctab_r, stab_r = ctab[N_META:], stab[N_META:]
    cos_tm, sin_tm = cos_t[:, :N_META], sin_t[:, :N_META]
    cos_tr, sin_tr = cos_t[:, N_META:], sin_t[:, N_META:]

    h_res = x.reshape(ROWS, D_MODEL)
    hm_res = meta_tokens.astype(F32)
    h = _rms(h_res, norm1_g[0], 512)
    hm = _rms(hm_res, norm1_g[0], N_META)
    s_zero = jnp.zeros((GLA_HEADS, GLA_DV, GLA_DK), F32)

    w_in_t = jnp.swapaxes(w_in, 1, 2)
    w_kr = w_in_t[:, 7184:7248]
    zrow = jnp.zeros((DEPTH, 32, D_MODEL), F32)
    w_small = jnp.concatenate(
        [w_in_t[:, 6144:6160], jnp.zeros((DEPTH, 128 - GLA_GATE_RANK, D_MODEL), F32),
         w_kr[:, :32], zrow, w_kr[:, 32:], zrow], axis=1)
    wa = w_branch_a.astype(BF16)
    wb = w_branch_b.astype(BF16)
    wo = w_out.astype(BF16)
    wd = w_down.astype(BF16)

    for l in range(DEPTH):
        last = l == DEPTH - 1
        w2p = jnp.concatenate(
            [gla_gate_w2[l], jnp.zeros((128 - GLA_GATE_RANK, GLA_HEADS * GLA_DK), F32)],
            axis=0).astype(BF16)
        wqt = _head_layout(w_uq[l].reshape(MLA_LORA, MLA_HEADS, MLA_QK)).reshape(
            MLA_LORA, MLA_HEADS * MLA_HEAD_PAD).T.astype(BF16)
        wkv3 = w_ukv[l].reshape(MLA_LORA, MLA_HEADS, MLA_NOPE + MLA_V)
        wk = wkv3[:, :, :MLA_NOPE].reshape(MLA_LORA, MLA_HEADS * MLA_NOPE).astype(BF16)
        wvt = wkv3[:, :, MLA_NOPE:].reshape(MLA_LORA, MLA_HEADS * MLA_V).T.astype(BF16)
        qg = (_head_layout(q_norm_g[l]) * (MLA_QK ** -0.5 * LOG2E)).reshape(MLA_HEAD_PAD, 1)
        kg = _head_layout(k_norm_g[l]).reshape(1, MLA_HEAD_PAD)
        g_next = norm1_g[l + 1] if not last else norm1_g[l]

        z = _in_proj(h, w_in_t, l, 1024, 1024, name="in_proj")
        zs = _matmul_nt(h, w_small, l, 1024, N_SMALL, name="in_proj_small")
        zm = _in_proj(hm, w_in_t, l, N_META, 1024, name="in_proj_meta")
        zsm = _matmul_nt(hm, w_small, l, N_META, N_SMALL, name="in_proj_small_meta")

        gla_m, s_meta = _gla(zm, zsm, w2p, gla_gate_b[l], gla_onorm_g[l], s_zero,
                             batch=1, tokens=N_META, block=N_META, chunk=N_META, emit_state=True)
        (gla_r,) = _gla(z, zs, w2p, gla_gate_b[l], gla_onorm_g[l], s_meta,
                        batch=BATCH, tokens=SEQ, block=512, chunk=GLA_CHUNK, emit_state=False)

        km, vmt = _kvproj(zm, zsm, kv_a_norm_g[l], wk, wvt, kg, ctab_m, stab_m, N_META)
        kr, vrt = _kvproj(z, zs, kv_a_norm_g[l], wk, wvt, kg, ctab_r, stab_r, 512)
        kmp = jnp.pad(km, ((0, 128 - N_META), (0, 0)))
        vmtp = jnp.pad(vmt, ((0, 0), (0, 128 - N_META)))
        qr = _qproj(z, q_a_norm_g[l], wqt, qg, cos_tr, sin_tr, 512)
        att_r = _attention(qr, kr, vrt, km, vmt, 512, 4, 512, 4)

        merged = _merge(gla_r, att_r, wa, wb, l, z, 1024, 512)
        h_res, h2 = _outproj(merged, wo, l, h_res, norm2_g[l], 512)
        act = _gateup(h2, w_gate_up, l, 1024, 512)
        h_res, h = _down(act, wd, l, h_res, g_next, 512, 2816)

        if not last:
            qm = _qproj(zm, q_a_norm_g[l], wqt, qg, cos_tm, sin_tm, N_META)
            att_m = _meta_attention(qm.T, kmp, vmtp)
            merged_m = _merge(gla_m, att_m, wa, wb, l, zm, N_META, 512)
            hm_res, hm2 = _outproj(merged_m, wo, l, hm_res, norm2_g[l], N_META)
            act_m = _gateup(hm2, w_gate_up, l, N_META, 512)
            hm_res, hm = _down(act_m, wd, l, hm_res, g_next, N_META, 1408)

    return h_res.reshape(BATCH, SEQ, D_MODEL)
```

```python
import functools

import jax
import jax.numpy as jnp
from jax import lax
from jax.experimental import pallas as pl
from jax.experimental.pallas import tpu as pltpu

D_MODEL = 2048
BATCH = 4
SEQ = 4096
DEPTH = 2
N_META = 16
ROWS = BATCH * SEQ

GLA_HEADS = 4
GLA_DK = 256
GLA_DV = 512
GLA_GATE_RANK = 16
GLA_GATE_TAU = 16.0
GLA_CHUNK = 64

MLA_HEADS = 16
MLA_LORA = 512
MLA_NOPE = 128
MLA_ROPE = 64
MLA_QK = 192
MLA_V = 128
MLA_HEAD_PAD = 256
ROPE_THETA = 10000.0
FF_HIDDEN = 5632
EPS = 1e-6
LOG2E = 1.4426950408889634

W_IN_ZA = 6144
W_IN_C, W_IN_C_END = 6160, 7184
W_IN_G = 7248
Z_Q, Z_K, Z_V, Z_R, Z_CQ, Z_CKV, Z_A, Z_B = 0, 1024, 2048, 4096, 6144, 6656, 7168, 9216
N_Z = 11264
N_SMALL = 256

VMEM_LIMIT = 56 * 1024 * 1024
BF16 = jnp.bfloat16
F32 = jnp.float32

_NT = (((1,), (1,)), ((), ()))
_TN = (((0,), (0,)), ((), ()))


def _params(*sem):
    return pltpu.CompilerParams(dimension_semantics=sem, vmem_limit_bytes=VMEM_LIMIT)


def _sigmoid(x):
    return 1.0 / (1.0 + jnp.exp(-x))


def _rms_kernel(x_ref, g_ref, o_ref):
    x = x_ref[...]
    ms = jnp.mean(x * x, axis=-1, keepdims=True)
    o_ref[...] = (x * lax.rsqrt(ms + EPS) * g_ref[...]).astype(o_ref.dtype)


def _rms(x, g, bm):
    m, d = x.shape
    return pl.pallas_call(
        _rms_kernel,
        grid=(m // bm,),
        in_specs=[pl.BlockSpec((bm, d), lambda i: (i, 0)),
                  pl.BlockSpec((1, d), lambda i: (0, 0))],
        out_specs=pl.BlockSpec((bm, d), lambda i: (i, 0)),
        out_shape=jax.ShapeDtypeStruct((m, d), BF16),
        compiler_params=_params("parallel"),
        name="rmsnorm",
    )(x, g.reshape(1, d))


def _mm_nt_kernel(a_ref, wt_ref, o_ref):
    o_ref[...] = lax.dot_general(a_ref[...], wt_ref[...].astype(BF16), _NT,
                                 preferred_element_type=F32)


def _matmul_nt(a, wt3, layer, bm, bn, name):
    m, k = a.shape
    n = wt3.shape[1]
    return pl.pallas_call(
        _mm_nt_kernel,
        grid=(m // bm, n // bn),
        in_specs=[pl.BlockSpec((bm, k), lambda i, j: (i, 0)),
                  pl.BlockSpec((None, bn, k), lambda i, j: (layer, j, 0))],
        out_specs=pl.BlockSpec((bm, bn), lambda i, j: (i, j)),
        out_shape=jax.ShapeDtypeStruct((m, n), F32),
        compiler_params=_params("parallel", "parallel"),
        name=name,
    )(a, wt3)


def _mm_nt_wstat_kernel(a_ref, wt_ref, o_ref, wb_ref):
    @pl.when(pl.program_id(1) == 0)
    def _():
        wb_ref[...] = wt_ref[0].astype(BF16)

    o_ref[...] = lax.dot_general(a_ref[...], wb_ref[...], _NT, preferred_element_type=F32)


def _in_proj_row0(j, bn):
    n_a = W_IN_ZA // bn
    n_c = (W_IN_C_END - W_IN_C) // bn
    return jnp.where(j < n_a, j * bn,
                     jnp.where(j < n_a + n_c, W_IN_C + (j - n_a) * bn,
                               W_IN_G + (j - n_a - n_c) * bn))


def _in_proj(a, wt3, layer, bm, bn, name):
    m, k = a.shape
    return pl.pallas_call(
        _mm_nt_wstat_kernel,
        grid=(N_Z // bn, m // bm),
        in_specs=[pl.BlockSpec((bm, k), lambda j, i: (i, 0)),
                  pl.BlockSpec((pl.Element(1), pl.Element(bn), pl.Element(k)),
                               lambda j, i: (layer, pl.multiple_of(_in_proj_row0(j, bn), 16), 0))],
        out_specs=pl.BlockSpec((bm, bn), lambda j, i: (i, j)),
        out_shape=jax.ShapeDtypeStruct((m, N_Z), F32),
        scratch_shapes=[pltpu.VMEM((bn, k), BF16)],
        compiler_params=_params("arbitrary", "arbitrary"),
        name=name,
    )(a, wt3)


def _gla_kernel(q_ref, k_ref, v_ref, r_ref, zlr_ref, w2_ref, gb_ref, og_ref, s0_ref,
                o_ref, *rest, chunk, n_chunks, emit_state):
    if emit_state:
        sfin_ref, st_ref, b_sc = rest
    else:
        st_ref, b_sc = rest
    t = pl.program_id(1)
    heads = range(GLA_HEADS)

    @pl.when(t == 0)
    def _():
        st_ref[...] = s0_ref[...]

    row = lax.broadcasted_iota(jnp.int32, (chunk, chunk), 0)
    col = lax.broadcasted_iota(jnp.int32, (chunk, chunk), 1)
    causal = col <= row
    tri = jnp.where(causal, 1.0, 0.0).astype(BF16)
    w2 = w2_ref[...]
    gb = gb_ref[...]
    og = og_ref[...]
    kdim = GLA_HEADS * GLA_DK

    def log_gate_split(c):
        sl = pl.ds(pl.multiple_of(c * chunk, chunk), chunk)
        logit = jnp.dot(zlr_ref[sl, :].astype(BF16), w2, preferred_element_type=F32) + gb
        g = (jnp.minimum(logit, 0.0) - jnp.log1p(jnp.exp(-jnp.abs(logit)))) * (1.0 / GLA_GATE_TAU)
        g_hi = g.astype(BF16)
        g_lo = (g - g_hi.astype(F32)).astype(BF16)
        return jnp.concatenate([g_hi, g_lo], axis=1)

    def log_decay(g_split):
        cs = jnp.dot(tri, g_split, preferred_element_type=F32)
        return cs[:, :kdim] + cs[:, kdim:]

    b_sc[...] = log_decay(log_gate_split(0))

    def body(c, carry):
        sl = pl.ds(pl.multiple_of(c * chunk, chunk), chunk)
        g_split_next = log_gate_split(jnp.minimum(c + 1, n_chunks - 1))
        r = r_ref[sl, :]
        out_gate = r * _sigmoid(r)
        st = [st_ref[h] for h in heads]
        st_bf = [st[h].astype(BF16) for h in heads]
        b = b_sc[...]
        b_last = b[chunk - 1:chunk, :]
        eb = jnp.exp(b)
        enb = jnp.exp(-b)
        erel = jnp.exp(b_last - b)
        elast = jnp.exp(b_last)
        k = k_ref[sl, :]
        qd = (q_ref[sl, :] * (GLA_DK ** -0.5) * eb).astype(BF16)
        kd = (k * enb).astype(BF16)
        k2 = (k * erel).astype(BF16)
        v = v_ref[sl, :].astype(BF16)

        def hk(x, h):
            return x[:, h * GLA_DK:(h + 1) * GLA_DK]

        def hv(x, h):
            return x[:, h * GLA_DV:(h + 1) * GLA_DV]

        a = [lax.dot_general(hk(qd, h), hk(kd, h), _NT, preferred_element_type=F32) for h in heads]
        o_state = [jnp.dot(hk(qd, h), st_bf[h], preferred_element_type=F32) for h in heads]
        o_local = [jnp.dot(jnp.where(causal, a[h], 0.0).astype(BF16), hv(v, h),
                           preferred_element_type=F32) for h in heads]
        b_sc[...] = log_decay(g_split_next)
        upd = [lax.dot_general(hk(k2, h), hv(v, h), _TN, preferred_element_type=F32)
               for h in heads]
        decay_col = jnp.broadcast_to(elast, (128, kdim)).T
        for h in heads:
            col = decay_col[h * GLA_DK:(h + 1) * GLA_DK]
            st_ref[h] = st[h] * jnp.concatenate([col] * (GLA_DV // 128), axis=1) + upd[h]
        for h in heads:
            o = o_local[h] + o_state[h]
            ms = jnp.mean(o * o, axis=-1, keepdims=True)
            on = o * lax.rsqrt(ms + EPS) * og
            o_ref[sl, h * GLA_DV:(h + 1) * GLA_DV] = (on * hv(out_gate, h)).astype(o_ref.dtype)
        return carry

    lax.fori_loop(0, n_chunks, body, 0)

    if emit_state:
        @pl.when(t == pl.num_programs(1) - 1)
        def _():
            sfin_ref[...] = st_ref[...]


def _gla(z, zs, w2p, gate_b, onorm_g, s0, *, batch, tokens, block, chunk, emit_state):
    nt = tokens // block
    rows = batch * tokens
    kern = functools.partial(_gla_kernel, chunk=chunk, n_chunks=block // chunk,
                             emit_state=emit_state)
    kdim = GLA_HEADS * GLA_DK
    vdim = GLA_HEADS * GLA_DV
    in_specs = [
        pl.BlockSpec((block, kdim), lambda b, t: (b * nt + t, Z_Q // kdim)),
        pl.BlockSpec((block, kdim), lambda b, t: (b * nt + t, Z_K // kdim)),
        pl.BlockSpec((block, vdim), lambda b, t: (b * nt + t, Z_V // vdim)),
        pl.BlockSpec((block, vdim), lambda b, t: (b * nt + t, Z_R // vdim)),
        pl.BlockSpec((block, 128), lambda b, t: (b * nt + t, 0)),
        pl.BlockSpec((128, kdim), lambda b, t: (0, 0)),
        pl.BlockSpec((1, kdim), lambda b, t: (0, 0)),
        pl.BlockSpec((1, GLA_DV), lambda b, t: (0, 0)),
        pl.BlockSpec((GLA_HEADS, GLA_DK, GLA_DV), lambda b, t: (0, 0, 0)),
    ]
    out_specs = [pl.BlockSpec((block, vdim), lambda b, t: (b * nt + t, 0))]
    out_shape = [jax.ShapeDtypeStruct((rows, vdim), BF16)]
    if emit_state:
        out_specs.append(pl.BlockSpec((GLA_HEADS, GLA_DK, GLA_DV), lambda b, t: (b, 0, 0)))
        out_shape.append(jax.ShapeDtypeStruct((batch * GLA_HEADS, GLA_DK, GLA_DV), F32))
    res = pl.pallas_call(
        kern,
        grid=(batch, nt),
        in_specs=in_specs,
        out_specs=out_specs,
        out_shape=out_shape,
        scratch_shapes=[pltpu.VMEM((GLA_HEADS, GLA_DK, GLA_DV), F32),
                        pltpu.VMEM((chunk, kdim), F32)],
        compiler_params=_params("parallel", "arbitrary"),
        name="gla",
    )(z, z, z, z, zs, w2p, gate_b.reshape(1, -1), onorm_g.reshape(1, -1), s0)
    return res


def _rope(x, c, s):
    return x * c + pltpu.roll(x, 64, 1) * s


def _qproj_kernel(zc_ref, ng_ref, wt_ref, hg_ref, c_ref, s_ref, o_ref, *, heads_per_dot):
    x = zc_ref[...]
    ms = jnp.mean(x * x, axis=-1, keepdims=True)
    xn = (x * lax.rsqrt(ms + EPS) * ng_ref[...]).astype(BF16)
    bm = x.shape[0]
    c = c_ref[...]
    s = s_ref[...]
    hg = jnp.broadcast_to(hg_ref[...], (MLA_HEAD_PAD, bm))
    for hc in range(MLA_HEADS // heads_per_dot):
        base = hc * heads_per_dot * MLA_HEAD_PAD
        yt = lax.dot_general(wt_ref[base:base + heads_per_dot * MLA_HEAD_PAD, :], xn, _NT,
                             preferred_element_type=F32)
        for j in range(heads_per_dot):
            lo = base + j * MLA_HEAD_PAD
            y = yt[j * MLA_HEAD_PAD:(j + 1) * MLA_HEAD_PAD]
            ms = jnp.sum(y * y, axis=0, keepdims=True) * (1.0 / MLA_QK)
            yn = y * lax.rsqrt(ms + EPS) * hg
            x1 = yn[128:160]
            x2 = yn[192:224]
            o_ref[lo:lo + 128, :] = yn[:128].astype(o_ref.dtype)
            o_ref[lo + 128:lo + 160, :] = (x1 * c - x2 * s).astype(o_ref.dtype)
            o_ref[lo + 160:lo + 192, :] = yn[160:192].astype(o_ref.dtype)
            o_ref[lo + 192:lo + 224, :] = (x2 * c + x1 * s).astype(o_ref.dtype)
            o_ref[lo + 224:lo + 256, :] = yn[224:256].astype(o_ref.dtype)


def _qproj(z, ng, wt, hg_col, cos_t, sin_t, bm):
    m = z.shape[0]
    nt = cos_t.shape[1] // bm
    kern = functools.partial(_qproj_kernel, heads_per_dot=4)
    return pl.pallas_call(
        kern,
        grid=(m // bm,),
        in_specs=[pl.BlockSpec((bm, MLA_LORA), lambda i: (i, Z_CQ // MLA_LORA)),
                  pl.BlockSpec((1, MLA_LORA), lambda i: (0, 0)),
                  pl.BlockSpec((MLA_HEADS * MLA_HEAD_PAD, MLA_LORA), lambda i: (0, 0)),
                  pl.BlockSpec((MLA_HEAD_PAD, 1), lambda i: (0, 0)),
                  pl.BlockSpec((MLA_ROPE // 2, bm), lambda i: (0, i % nt)),
                  pl.BlockSpec((MLA_ROPE // 2, bm), lambda i: (0, i % nt))],
        out_specs=pl.BlockSpec((MLA_HEADS * MLA_HEAD_PAD, bm), lambda i: (0, i)),
        out_shape=jax.ShapeDtypeStruct((MLA_HEADS * MLA_HEAD_PAD, m), BF16),
        compiler_params=_params("parallel"),
        name="q_proj",
    )(z, ng.reshape(1, -1), wt, hg_col, cos_t, sin_t)


def _kvproj_kernel(zc_ref, kr_ref, ng_ref, wk_ref, wvt_ref, hg_ref, c_ref, s_ref, k_ref, vt_ref):
    x = zc_ref[...]
    ms = jnp.mean(x * x, axis=-1, keepdims=True)
    xn = (x * lax.rsqrt(ms + EPS) * ng_ref[...]).astype(BF16)
    c = c_ref[...]
    s = s_ref[...]
    hg = hg_ref[...]
    kr = kr_ref[...]
    kr_ss = jnp.sum(kr * kr, axis=-1, keepdims=True)
    kr_rot = _rope(kr * hg[:, 128:], c, s)
    for hp in range(MLA_HEADS // 2):
        y = jnp.dot(xn, wk_ref[:, hp * 256:(hp + 1) * 256], preferred_element_type=F32)
        for sub in range(2):
            lo = (2 * hp + sub) * MLA_HEAD_PAD
            kn = y[:, sub * MLA_NOPE:(sub + 1) * MLA_NOPE]
            ms = (jnp.sum(kn * kn, axis=-1, keepdims=True) + kr_ss) * (1.0 / MLA_QK)
            rs = lax.rsqrt(ms + EPS)
            k_ref[:, lo:lo + 128] = (kn * rs * hg[:, :128]).astype(k_ref.dtype)
            k_ref[:, lo + 128:lo + 256] = (kr_rot * rs).astype(k_ref.dtype)
    rows_per_dot = 512
    for j in range(MLA_HEADS * MLA_V // rows_per_dot):
        sl = slice(j * rows_per_dot, (j + 1) * rows_per_dot)
        vt_ref[sl, :] = lax.dot_general(wvt_ref[sl, :], xn, _NT,
                                        preferred_element_type=F32).astype(vt_ref.dtype)


def _kvproj(z, zs, ng, wk, wvt, hg, ctab, stab, bm):
    m = z.shape[0]
    nt = ctab.shape[0] // bm
    return pl.pallas_call(
        _kvproj_kernel,
        grid=(m // bm,),
        in_specs=[pl.BlockSpec((bm, MLA_LORA), lambda i: (i, Z_CKV // MLA_LORA)),
                  pl.BlockSpec((bm, 128), lambda i: (i, 1)),
                  pl.BlockSpec((1, MLA_LORA), lambda i: (0, 0)),
                  pl.BlockSpec((MLA_LORA, MLA_HEADS * MLA_NOPE), lambda i: (0, 0)),
                  pl.BlockSpec((MLA_HEADS * MLA_V, MLA_LORA), lambda i: (0, 0)),
                  pl.BlockSpec((1, MLA_HEAD_PAD), lambda i: (0, 0)),
                  pl.BlockSpec((bm, 128), lambda i: (i % nt, 0)),
                  pl.BlockSpec((bm, 128), lambda i: (i % nt, 0))],
        out_specs=[pl.BlockSpec((bm, MLA_HEADS * MLA_HEAD_PAD), lambda i: (i, 0)),
                   pl.BlockSpec((MLA_HEADS * MLA_V, bm), lambda i: (0, i))],
        out_shape=[jax.ShapeDtypeStruct((m, MLA_HEADS * MLA_HEAD_PAD), BF16),
                   jax.ShapeDtypeStruct((MLA_HEADS * MLA_V, m), BF16)],
        compiler_params=_params("parallel"),
        name="kv_proj",
    )(z, zs, ng.reshape(1, -1), wk, wvt, hg, ctab, stab)


def _attn_kernel(qt_ref, k_ref, vt_ref, km_ref, vmt_ref, o_ref, m_sc, l_sc, acc_sc, s_sc, *, blk, heads,
                 qw, ahead, nq):
    i = pl.program_id(2)

    def queries(qi, h, lo, width):
        return qt_ref[h * MLA_HEAD_PAD:(h + 1) * MLA_HEAD_PAD,
                      pl.ds(pl.multiple_of(qi * blk + lo, width), width)]

    for h in range(heads):
        s = jnp.dot(km_ref[:, h * MLA_HEAD_PAD:(h + 1) * MLA_HEAD_PAD], queries(i, h, 0, blk),
                    preferred_element_type=F32)
        m = jnp.max(s, axis=0, keepdims=True)
        p = jnp.exp2(s - m)
        m_sc[h] = m
        l_sc[h] = jnp.sum(p, axis=0, keepdims=True)
        acc_sc[h] = jnp.dot(vmt_ref[h * MLA_V:(h + 1) * MLA_V, :], p.astype(BF16),
                            preferred_element_type=F32)

    tasks = [(h, c) for c in range(blk // qw) for h in range(heads)]

    def scores(qi, kb, h, c):
        ksl = pl.ds(pl.multiple_of(kb * blk, blk), blk)
        return jnp.dot(k_ref[ksl, h * MLA_HEAD_PAD:(h + 1) * MLA_HEAD_PAD],
                       queries(qi, h, c * qw, qw),
                       preferred_element_type=F32)

    @pl.when(i == 0)
    def _():
        for n in range(ahead):
            s_sc[n] = scores(0, 0, *tasks[n])

    def step(kb, masked):
        ksl = pl.ds(pl.multiple_of(kb * blk, blk), blk)
        pending = [s_sc[n] for n in range(ahead)]
        handover = []
        for n, (h, c) in enumerate(tasks):
            csl = slice(c * qw, (c + 1) * qw)
            s = pending.pop(0)
            if n + ahead < len(tasks):
                pending.append(scores(i, kb, *tasks[n + ahead]))
            elif not masked:
                handover.append(scores(i, kb + 1, *tasks[n + ahead - len(tasks)]))
            else:
                handover.append(scores(jnp.minimum(i + 1, nq - 1), 0,
                                       *tasks[n + ahead - len(tasks)]))
            if masked:
                key = lax.broadcasted_iota(jnp.int32, s.shape, 0)
                qry = lax.broadcasted_iota(jnp.int32, s.shape, 1) + c * qw
                s = jnp.where(key <= qry, s, -1e30)
            m_prev = m_sc[h, :, csl]
            m_new = jnp.maximum(m_prev, jnp.max(s, axis=0, keepdims=True))
            alpha = jnp.exp2(m_prev - m_new)
            p = jnp.exp2(s - m_new)
            l_sc[h, :, csl] = alpha * l_sc[h, :, csl] + jnp.sum(p, axis=0, keepdims=True)
            acc_sc[h, :, csl] = alpha * acc_sc[h, :, csl] + jnp.dot(
                vt_ref[h * MLA_V:(h + 1) * MLA_V, ksl], p.astype(BF16),
                preferred_element_type=F32)
            m_sc[h, :, csl] = m_new
        for n, s_next in enumerate(handover):
            s_sc[n] = s_next

    def body(kb, carry):
        step(kb, False)
        return carry

    lax.fori_loop(0, i, body, 0)
    step(i, True)
    for h in range(heads):
        o_ref[:, h * MLA_V:(h + 1) * MLA_V] = (acc_sc[h] * (1.0 / l_sc[h])).T.astype(o_ref.dtype)


def _attention(q, k, vt, km, vmt, blk, heads, qw, ahead):
    nq = SEQ // blk
    kern = functools.partial(_attn_kernel, blk=blk, heads=heads, qw=qw, ahead=ahead, nq=nq)
    return pl.pallas_call(
        kern,
        grid=(BATCH, MLA_HEADS // heads, nq),
        in_specs=[pl.BlockSpec((heads * MLA_HEAD_PAD, SEQ), lambda b, h, i: (h, b)),
                  pl.BlockSpec((SEQ, heads * MLA_HEAD_PAD), lambda b, h, i: (b, h)),
                  pl.BlockSpec((heads * MLA_V, SEQ), lambda b, h, i: (h, b)),
                  pl.BlockSpec((N_META, heads * MLA_HEAD_PAD), lambda b, h, i: (0, h)),
                  pl.BlockSpec((heads * MLA_V, N_META), lambda b, h, i: (h, 0))],
        out_specs=pl.BlockSpec((blk, heads * MLA_V), lambda b, h, i: (b * nq + i, h)),
        out_shape=jax.ShapeDtypeStruct((ROWS, MLA_HEADS * MLA_V), BF16),
        scratch_shapes=[pltpu.VMEM((heads, 1, blk), F32), pltpu.VMEM((heads, 1, blk), F32),
                        pltpu.VMEM((heads, MLA_V, blk), F32), pltpu.VMEM((ahead, blk, qw), F32)],
        compiler_params=_params("parallel", "parallel", "arbitrary"),
        name="mla_attention",
    )(q, k, vt, km, vmt)


def _meta_attn_kernel(q_ref, km_ref, vmt_ref, o_ref):
    s = lax.dot_general(q_ref[...], km_ref[...], _NT, preferred_element_type=F32)
    row = lax.broadcasted_iota(jnp.int32, s.shape, 0)
    col = lax.broadcasted_iota(jnp.int32, s.shape, 1)
    s = jnp.where(col <= row, s, -1e30)
    m = jnp.max(s, axis=-1, keepdims=True)
    p = jnp.exp2(s - m)
    l = jnp.sum(p, axis=-1, keepdims=True)
    o = lax.dot_general(p.astype(BF16), vmt_ref[...], _NT, preferred_element_type=F32)
    o_ref[...] = (o / l).astype(o_ref.dtype)


def _meta_attention(qm, kmp, vmtp):
    return pl.pallas_call(
        _meta_attn_kernel,
        grid=(MLA_HEADS,),
        in_specs=[pl.BlockSpec((N_META, MLA_HEAD_PAD), lambda h: (0, h)),
                  pl.BlockSpec((128, MLA_HEAD_PAD), lambda h: (0, h)),
                  pl.BlockSpec((MLA_V, 128), lambda h: (h, 0))],
        out_specs=pl.BlockSpec((N_META, MLA_V), lambda h: (0, h)),
        out_shape=jax.ShapeDtypeStruct((N_META, MLA_HEADS * MLA_V), BF16),
        compiler_params=_params("parallel"),
        name="meta_attention",
    )(qm, kmp, vmtp)


def _merge_kernel(a_ref, b_ref, wa_ref, wb_ref, za_ref, zb_ref, o_ref):
    ga = _sigmoid(za_ref[...])
    gb = _sigmoid(zb_ref[...])
    ya = jnp.dot(a_ref[...], wa_ref[...], preferred_element_type=F32)
    yb = jnp.dot(b_ref[...], wb_ref[...], preferred_element_type=F32)
    o_ref[...] = (ga * ya + gb * yb).astype(o_ref.dtype)


def _merge(a, b, wa3, wb3, layer, z, bm, bn):
    m = a.shape[0]
    return pl.pallas_call(
        _merge_kernel,
        grid=(m // bm, D_MODEL // bn),
        in_specs=[pl.BlockSpec((bm, D_MODEL), lambda i, j: (i, 0)),
                  pl.BlockSpec((bm, D_MODEL), lambda i, j: (i, 0)),
                  pl.BlockSpec((None, D_MODEL, bn), lambda i, j: (layer, 0, j)),
                  pl.BlockSpec((None, D_MODEL, bn), lambda i, j: (layer, 0, j)),
                  pl.BlockSpec((bm, bn), lambda i, j: (i, Z_A // bn + j)),
                  pl.BlockSpec((bm, bn), lambda i, j: (i, Z_B // bn + j))],
        out_specs=pl.BlockSpec((bm, bn), lambda i, j: (i, j)),
        out_shape=jax.ShapeDtypeStruct((m, D_MODEL), BF16),
        compiler_params=_params("parallel", "parallel"),
        name="branch_merge",
    )(a, b, wa3, wb3, z, z)


def _outproj_kernel(m_ref, w_ref, h_ref, g_ref, ho_ref, hn_ref):
    hn = h_ref[...] + jnp.dot(m_ref[...], w_ref[...], preferred_element_type=F32)
    ho_ref[...] = hn
    ms = jnp.mean(hn * hn, axis=-1, keepdims=True)
    hn_ref[...] = (hn * lax.rsqrt(ms + EPS) * g_ref[...]).astype(hn_ref.dtype)


def _outproj(mg, w3, layer, h, g, bm):
    m = mg.shape[0]
    return pl.pallas_call(
        _outproj_kernel,
        grid=(m // bm,),
        in_specs=[pl.BlockSpec((bm, D_MODEL), lambda i: (i, 0)),
                  pl.BlockSpec((None, D_MODEL, D_MODEL), lambda i: (layer, 0, 0)),
                  pl.BlockSpec((bm, D_MODEL), lambda i: (i, 0)),
                  pl.BlockSpec((1, D_MODEL), lambda i: (0, 0))],
        out_specs=[pl.BlockSpec((bm, D_MODEL), lambda i: (i, 0)),
                   pl.BlockSpec((bm, D_MODEL), lambda i: (i, 0))],
        out_shape=[jax.ShapeDtypeStruct((m, D_MODEL), F32),
                   jax.ShapeDtypeStruct((m, D_MODEL), BF16)],
        compiler_params=_params("parallel"),
        name="out_proj",
    )(mg, w3, h, g.reshape(1, -1))


def _gateup_kernel(h_ref, wg_ref, wu_ref, o_ref, wgb_ref, wub_ref):
    @pl.when(pl.program_id(1) == 0)
    def _():
        wgb_ref[...] = wg_ref[...].astype(BF16)
        wub_ref[...] = wu_ref[...].astype(BF16)

    h = h_ref[...]
    g = jnp.dot(h, wgb_ref[...], preferred_element_type=F32)
    u = jnp.dot(h, wub_ref[...], preferred_element_type=F32)
    o_ref[...] = (g * _sigmoid(g) * u).astype(o_ref.dtype)


def _gateup(h, w3, layer, bm, bn):
    m = h.shape[0]
    nb = FF_HIDDEN // bn
    return pl.pallas_call(
        _gateup_kernel,
        grid=(nb, m // bm),
        in_specs=[pl.BlockSpec((bm, D_MODEL), lambda j, i: (i, 0)),
                  pl.BlockSpec((None, D_MODEL, bn), lambda j, i: (layer, 0, j)),
                  pl.BlockSpec((None, D_MODEL, bn), lambda j, i: (layer, 0, nb + j))],
        out_specs=pl.BlockSpec((bm, bn), lambda j, i: (i, j)),
        out_shape=jax.ShapeDtypeStruct((m, FF_HIDDEN), BF16),
        scratch_shapes=[pltpu.VMEM((D_MODEL, bn), BF16), pltpu.VMEM((D_MODEL, bn), BF16)],
        compiler_params=_params("arbitrary", "arbitrary"),
        name="gate_up",
    )(h, w3, w3)


def _down_kernel(a_ref, w_ref, h_ref, g_ref, ho_ref, hn_ref, acc_ref):
    k = pl.program_id(1)
    last = pl.num_programs(1) - 1

    def product():
        return jnp.dot(a_ref[...], w_ref[...], preferred_element_type=F32)

    @pl.when(k == 0)
    def _():
        acc_ref[...] = product()

    @pl.when((k > 0) & (k < last))
    def _():
        acc_ref[...] += product()

    @pl.when(k == last)
    def _():
        hn = h_ref[...] + (acc_ref[...] + product())
        ho_ref[...] = hn
        ms = jnp.mean(hn * hn, axis=-1, keepdims=True)
        hn_ref[...] = (hn * lax.rsqrt(ms + EPS) * g_ref[...]).astype(hn_ref.dtype)


def _down(a, w3, layer, h, g, bm, bk):
    m = a.shape[0]
    assert FF_HIDDEN // bk >= 2
    return pl.pallas_call(
        _down_kernel,
        grid=(m // bm, FF_HIDDEN // bk),
        in_specs=[pl.BlockSpec((bm, bk), lambda i, k: (i, k)),
                  pl.BlockSpec((None, bk, D_MODEL), lambda i, k: (layer, k, 0)),
                  pl.BlockSpec((bm, D_MODEL), lambda i, k: (i, 0)),
                  pl.BlockSpec((1, D_MODEL), lambda i, k: (0, 0))],
        out_specs=[pl.BlockSpec((bm, D_MODEL), lambda i, k: (i, 0)),
                   pl.BlockSpec((bm, D_MODEL), lambda i, k: (i, 0))],
        out_shape=[jax.ShapeDtypeStruct((m, D_MODEL), F32),
                   jax.ShapeDtypeStruct((m, D_MODEL), BF16)],
        scratch_shapes=[pltpu.VMEM((bm, D_MODEL), F32)],
        compiler_params=_params("parallel", "arbitrary"),
        name="down_proj",
    )(a, w3, h, g.reshape(1, -1))


def _rope_layout(t):
    zeros = jnp.zeros(t.shape[:-1] + (32,), t.dtype)
    return jnp.concatenate([t[..., :32], zeros, t[..., 32:], zeros], axis=-1)


def _head_layout(t):
    return jnp.concatenate([t[..., :MLA_NOPE], _rope_layout(t[..., MLA_NOPE:])], axis=-1)


def _rope_tables():
    length = N_META + SEQ
    inv = 1.0 / (ROPE_THETA ** (jnp.arange(0, MLA_ROPE, 2, dtype=F32) / MLA_ROPE))
    ang = jnp.arange(length, dtype=F32)[:, None] * inv[None, :]
    cos, sin = jnp.cos(ang), jnp.sin(ang)
    zeros = jnp.zeros_like(cos)
    ctab = jnp.concatenate([cos, zeros, cos, zeros], axis=-1)
    stab = jnp.concatenate([-sin, zeros, sin, zeros], axis=-1)
    return ctab, stab, cos.T, sin.T


def kernel(x, meta_tokens, norm1_g, w_in, gla_gate_w2, gla_gate_b, gla_onorm_g, w_branch_a,
           q_a_norm_g, w_uq, kv_a_norm_g, w_ukv, q_norm_g, k_norm_g, w_branch_b, w_out,
           norm2_g, w_gate_up, w_down):
    ctab, stab, cos_t, sin_t = _rope_tables()
    ctab_m, stab_m = ctab[:N_META], stab[:N_META]
    ctab_r, stab_r = ctab[N_META:], stab[N_META:]
    cos_tm, sin_tm = cos_t[:, :N_META], sin_t[:, :N_META]
    cos_tr, sin_tr = cos_t[:, N_META:], sin_t[:, N_META:]

    h_res = x.reshape(ROWS, D_MODEL)
    hm_res = meta_tokens.astype(F32)
    h = _rms(h_res, norm1_g[0], 512)
    hm = _rms(hm_res, norm1_g[0], N_META)
    s_zero = jnp.zeros((GLA_HEADS, GLA_DK, GLA_DV), F32)

    w_in_t = jnp.swapaxes(w_in, 1, 2)
    w_kr = w_in_t[:, 7184:7248]
    zrow = jnp.zeros((DEPTH, 32, D_MODEL), F32)
    w_small = jnp.concatenate(
        [w_in_t[:, 6144:6160], jnp.zeros((DEPTH, 128 - GLA_GATE_RANK, D_MODEL), F32),
         w_kr[:, :32], zrow, w_kr[:, 32:], zrow], axis=1)
    wa = w_branch_a.astype(BF16)
    wb = w_branch_b.astype(BF16)
    wo = w_out.astype(BF16)
    wd = w_down.astype(BF16)

    for l in range(DEPTH):
        last = l == DEPTH - 1
        w2p = jnp.concatenate(
            [gla_gate_w2[l], jnp.zeros((128 - GLA_GATE_RANK, GLA_HEADS * GLA_DK), F32)],
            axis=0).astype(BF16)
        wqt = _head_layout(w_uq[l].reshape(MLA_LORA, MLA_HEADS, MLA_QK)).reshape(
            MLA_LORA, MLA_HEADS * MLA_HEAD_PAD).T.astype(BF16)
        wkv3 = w_ukv[l].reshape(MLA_LORA, MLA_HEADS, MLA_NOPE + MLA_V)
        wk = wkv3[:, :, :MLA_NOPE].reshape(MLA_LORA, MLA_HEADS * MLA_NOPE).astype(BF16)
        wvt = wkv3[:, :, MLA_NOPE:].reshape(MLA_LORA, MLA_HEADS * MLA_V).T.astype(BF16)
        qg = (_head_layout(q_norm_g[l]) * (MLA_QK ** -0.5 * LOG2E)).reshape(MLA_HEAD_PAD, 1)
        kg = _head_layout(k_norm_g[l]).reshape(1, MLA_HEAD_PAD)
        g_next = norm1_g[l + 1] if not last else norm1_g[l]

        z = _in_proj(h, w_in_t, l, 1024, 1024, name="in_proj")
        zs = _matmul_nt(h, w_small, l, 1024, N_SMALL, name="in_proj_small")
        zm = _in_proj(hm, w_in_t, l, N_META, 1024, name="in_proj_meta")
        zsm = _matmul_nt(hm, w_small, l, N_META, N_SMALL, name="in_proj_small_meta")

        gla_m, s_meta = _gla(zm, zsm, w2p, gla_gate_b[l], gla_onorm_g[l], s_zero,
                             batch=1, tokens=N_META, block=N_META, chunk=N_META, emit_state=True)
        (gla_r,) = _gla(z, zs, w2p, gla_gate_b[l], gla_onorm_g[l], s_meta,
                        batch=BATCH, tokens=SEQ, block=512, chunk=GLA_CHUNK, emit_state=False)

        km, vmt = _kvproj(zm, zsm, kv_a_norm_g[l], wk, wvt, kg, ctab_m, stab_m, N_META)
        kr, vrt = _kvproj(z, zs, kv_a_norm_g[l], wk, wvt, kg, ctab_r, stab_r, 512)
        kmp = jnp.pad(km, ((0, 128 - N_META), (0, 0)))
        vmtp = jnp.pad(vmt, ((0, 0), (0, 128 - N_META)))
        qr = _qproj(z, q_a_norm_g[l], wqt, qg, cos_tr, sin_tr, 512)
        att_r = _attention(qr, kr, vrt, km, vmt, 512, 4, 512, 4)

        merged = _merge(gla_r, att_r, wa, wb, l, z, 1024, 512)
        h_res, h2 = _outproj(merged, wo, l, h_res, norm2_g[l], 512)
        act = _gateup(h2, w_gate_up, l, 1024, 512)
        h_res, h = _down(act, wd, l, h_res, g_next, 512, 2816)

        if not last:
            qm = _qproj(zm, q_a_norm_g[l], wqt, qg, cos_tm, sin_tm, N_META)
            att_m = _meta_attention(qm.T, kmp, vmtp)
            merged_m = _merge(gla_m, att_m, wa, wb, l, zm, N_META, 512)
            hm_res, hm2 = _outproj(merged_m, wo, l, hm_res, norm2_g[l], N_META)
            act_m = _gateup(hm2, w_gate_up, l, N_META, 512)
            hm_res, hm = _down(act_m, wd, l, hm_res, g_next, N_META, 1408)

    return h_res.reshape(BATCH, SEQ, D_MODEL)
```

```python
import functools

import jax
import jax.numpy as jnp
from jax import lax
from jax.experimental import pallas as pl
from jax.experimental.pallas import tpu as pltpu

D_MODEL = 2048
BATCH = 4
SEQ = 4096
DEPTH = 2
N_META = 16
ROWS = BATCH * SEQ

GLA_HEADS = 4
GLA_DK = 256
GLA_DV = 512
GLA_GATE_RANK = 16
GLA_GATE_TAU = 16.0
GLA_CHUNK = 64

MLA_HEADS = 16
MLA_LORA = 512
MLA_NOPE = 128
MLA_ROPE = 64
MLA_QK = 192
MLA_V = 128
MLA_HEAD_PAD = 256
ROPE_THETA = 10000.0
FF_HIDDEN = 5632
EPS = 1e-6
LOG2E = 1.4426950408889634

W_IN_ZA = 6144
W_IN_C, W_IN_C_END = 6160, 7184
W_IN_G = 7248
Z_Q, Z_K, Z_V, Z_R, Z_CQ, Z_CKV, Z_A, Z_B = 0, 1024, 2048, 4096, 6144, 6656, 7168, 9216
N_Z = 11264
N_SMALL = 256

VMEM_LIMIT = 56 * 1024 * 1024
BF16 = jnp.bfloat16
F32 = jnp.float32

_NT = (((1,), (1,)), ((), ()))
_TN = (((0,), (0,)), ((), ()))


def _params(*sem):
    return pltpu.CompilerParams(dimension_semantics=sem, vmem_limit_bytes=VMEM_LIMIT)


def _sigmoid(x):
    return 1.0 / (1.0 + jnp.exp(-x))


def _rms_kernel(x_ref, g_ref, o_ref):
    x = x_ref[...]
    ms = jnp.mean(x * x, axis=-1, keepdims=True)
    o_ref[...] = (x * lax.rsqrt(ms + EPS) * g_ref[...]).astype(o_ref.dtype)


def _rms(x, g, bm):
    m, d = x.shape
    return pl.pallas_call(
        _rms_kernel,
        grid=(m // bm,),
        in_specs=[pl.BlockSpec((bm, d), lambda i: (i, 0)),
                  pl.BlockSpec((1, d), lambda i: (0, 0))],
        out_specs=pl.BlockSpec((bm, d), lambda i: (i, 0)),
        out_shape=jax.ShapeDtypeStruct((m, d), BF16),
        compiler_params=_params("parallel"),
        name="rmsnorm",
    )(x, g.reshape(1, d))


def _mm_nt_kernel(a_ref, wt_ref, o_ref):
    o_ref[...] = lax.dot_general(a_ref[...], wt_ref[...].astype(BF16), _NT,
                                 preferred_element_type=F32)


def _matmul_nt(a, wt3, layer, bm, bn, name):
    m, k = a.shape
    n = wt3.shape[1]
    return pl.pallas_call(
        _mm_nt_kernel,
        grid=(m // bm, n // bn),
        in_specs=[pl.BlockSpec((bm, k), lambda i, j: (i, 0)),
                  pl.BlockSpec((None, bn, k), lambda i, j: (layer, j, 0))],
        out_specs=pl.BlockSpec((bm, bn), lambda i, j: (i, j)),
        out_shape=jax.ShapeDtypeStruct((m, n), F32),
        compiler_params=_params("parallel", "parallel"),
        name=name,
    )(a, wt3)


def _mm_nt_wstat_kernel(a_ref, wt_ref, o_ref, wb_ref):
    @pl.when(pl.program_id(1) == 0)
    def _():
        wb_ref[...] = wt_ref[0].astype(BF16)

    o_ref[...] = lax.dot_general(a_ref[...], wb_ref[...], _NT, preferred_element_type=F32)


def _in_proj_row0(j, bn):
    n_a = W_IN_ZA // bn
    n_c = (W_IN_C_END - W_IN_C) // bn
    return jnp.where(j < n_a, j * bn,
                     jnp.where(j < n_a + n_c, W_IN_C + (j - n_a) * bn,
                               W_IN_G + (j - n_a - n_c) * bn))


def _in_proj(a, wt3, layer, bm, bn, name):
    m, k = a.shape
    return pl.pallas_call(
        _mm_nt_wstat_kernel,
        grid=(N_Z // bn, m // bm),
        in_specs=[pl.BlockSpec((bm, k), lambda j, i: (i, 0)),
                  pl.BlockSpec((pl.Element(1), pl.Element(bn), pl.Element(k)),
                               lambda j, i: (layer, pl.multiple_of(_in_proj_row0(j, bn), 16), 0))],
        out_specs=pl.BlockSpec((bm, bn), lambda j, i: (i, j)),
        out_shape=jax.ShapeDtypeStruct((m, N_Z), F32),
        scratch_shapes=[pltpu.VMEM((bn, k), BF16)],
        compiler_params=_params("arbitrary", "arbitrary"),
        name=name,
    )(a, wt3)


def _gla_kernel(q_ref, k_ref, v_ref, r_ref, zlr_ref, w2_ref, gb_ref, og_ref, s0_ref,
                o_ref, *rest, chunk, n_chunks, emit_state):
    if emit_state:
        sfin_ref, st_ref, b_sc = rest
    else:
        st_ref, b_sc = rest
    t = pl.program_id(1)
    heads = range(GLA_HEADS)

    @pl.when(t == 0)
    def _():
        st_ref[...] = s0_ref[...]

    row = lax.broadcasted_iota(jnp.int32, (chunk, chunk), 0)
    col = lax.broadcasted_iota(jnp.int32, (chunk, chunk), 1)
    causal = col <= row
    tri = jnp.where(causal, 1.0, 0.0).astype(BF16)
    w2 = w2_ref[...]
    gb = gb_ref[...]
    og = og_ref[...]
    kdim = GLA_HEADS * GLA_DK

    def log_gate_split(c):
        sl = pl.ds(pl.multiple_of(c * chunk, chunk), chunk)
        logit = jnp.dot(zlr_ref[sl, :].astype(BF16), w2, preferred_element_type=F32) + gb
        g = (jnp.minimum(logit, 0.0) - jnp.log1p(jnp.exp(-jnp.abs(logit)))) * (1.0 / GLA_GATE_TAU)
        g_hi = g.astype(BF16)
        g_lo = (g - g_hi.astype(F32)).astype(BF16)
        return jnp.concatenate([g_hi, g_lo], axis=1)

    def log_decay(g_split):
        cs = jnp.dot(tri, g_split, preferred_element_type=F32)
        return cs[:, :kdim] + cs[:, kdim:]

    b_sc[...] = log_decay(log_gate_split(0))

    def body(c, carry):
        sl = pl.ds(pl.multiple_of(c * chunk, chunk), chunk)
        g_split_next = log_gate_split(jnp.minimum(c + 1, n_chunks - 1))
        r = r_ref[sl, :]
        out_gate = r * _sigmoid(r)
        st = [st_ref[h] for h in heads]
        st_bf = [st[h].astype(BF16) for h in heads]
        b = b_sc[...]
        b_last = b[chunk - 1:chunk, :]
        eb = jnp.exp(b)
        enb = jnp.exp(-b)
        erel = jnp.exp(b_last - b)
        elast = jnp.exp(b_last)
        k = k_ref[sl, :]
        qd = (q_ref[sl, :] * (GLA_DK ** -0.5) * eb).astype(BF16)
        kd = (k * enb).astype(BF16)
        k2 = (k * erel).astype(BF16)
        v = v_ref[sl, :].astype(BF16)

        def hk(x, h):
            return x[:, h * GLA_DK:(h + 1) * GLA_DK]

        def hv(x, h):
            return x[:, h * GLA_DV:(h + 1) * GLA_DV]

        a = [lax.dot_general(hk(qd, h), hk(kd, h), _NT, preferred_element_type=F32) for h in heads]
        o_state = [jnp.dot(hk(qd, h), st_bf[h], preferred_element_type=F32) for h in heads]
        o_local = [jnp.dot(jnp.where(causal, a[h], 0.0).astype(BF16), hv(v, h),
                           preferred_element_type=F32) for h in heads]
        b_sc[...] = log_decay(g_split_next)
        upd = [lax.dot_general(hk(k2, h), hv(v, h), _TN, preferred_element_type=F32)
               for h in heads]
        decay_col = jnp.broadcast_to(elast, (128, kdim)).T
        for h in heads:
            col = decay_col[h * GLA_DK:(h + 1) * GLA_DK]
            st_ref[h] = st[h] * jnp.concatenate([col] * (GLA_DV // 128), axis=1) + upd[h]
        for h in heads:
            o = o_local[h] + o_state[h]
            ms = jnp.mean(o * o, axis=-1, keepdims=True)
            on = o * lax.rsqrt(ms + EPS) * og
            o_ref[sl, h * GLA_DV:(h + 1) * GLA_DV] = (on * hv(out_gate, h)).astype(o_ref.dtype)
        return carry

    lax.fori_loop(0, n_chunks, body, 0)

    if emit_state:
        @pl.when(t == pl.num_programs(1) - 1)
        def _():
            sfin_ref[...] = st_ref[...]


def _gla(z, zs, w2p, gate_b, onorm_g, s0, *, batch, tokens, block, chunk, emit_state):
    nt = tokens // block
    rows = batch * tokens
    kern = functools.partial(_gla_kernel, chunk=chunk, n_chunks=block // chunk,
                             emit_state=emit_state)
    kdim = GLA_HEADS * GLA_DK
    vdim = GLA_HEADS * GLA_DV
    in_specs = [
        pl.BlockSpec((block, kdim), lambda b, t: (b * nt + t, Z_Q // kdim)),
        pl.BlockSpec((block, kdim), lambda b, t: (b * nt + t, Z_K // kdim)),
        pl.BlockSpec((block, vdim), lambda b, t: (b * nt + t, Z_V // vdim)),
        pl.BlockSpec((block, vdim), lambda b, t: (b * nt + t, Z_R // vdim)),
        pl.BlockSpec((block, 128), lambda b, t: (b * nt + t, 0)),
        pl.BlockSpec((128, kdim), lambda b, t: (0, 0)),
        pl.BlockSpec((1, kdim), lambda b, t: (0, 0)),
        pl.BlockSpec((1, GLA_DV), lambda b, t: (0, 0)),
        pl.BlockSpec((GLA_HEADS, GLA_DK, GLA_DV), lambda b, t: (0, 0, 0)),
    ]
    out_specs = [pl.BlockSpec((block, vdim), lambda b, t: (b * nt + t, 0))]
    out_shape = [jax.ShapeDtypeStruct((rows, vdim), BF16)]
    if emit_state:
        out_specs.append(pl.BlockSpec((GLA_HEADS, GLA_DK, GLA_DV), lambda b, t: (b, 0, 0)))
        out_shape.append(jax.ShapeDtypeStruct((batch * GLA_HEADS, GLA_DK, GLA_DV), F32))
    res = pl.pallas_call(
        kern,
        grid=(batch, nt),
        in_specs=in_specs,
        out_specs=out_specs,
        out_shape=out_shape,
        scratch_shapes=[pltpu.VMEM((GLA_HEADS, GLA_DK, GLA_DV), F32),
                        pltpu.VMEM((chunk, kdim), F32)],
        compiler_params=_params("parallel", "arbitrary"),
        name="gla",
    )(z, z, z, z, zs, w2p, gate_b.reshape(1, -1), onorm_g.reshape(1, -1), s0)
    return res


def _rope(x, c, s):
    return x * c + pltpu.roll(x, 64, 1) * s


def _qproj_kernel(zc_ref, ng_ref, wt_ref, hg_ref, c_ref, s_ref, o_ref, *, heads_per_dot):
    x = zc_ref[...]
    ms = jnp.mean(x * x, axis=-1, keepdims=True)
    xn = (x * lax.rsqrt(ms + EPS) * ng_ref[...]).astype(BF16)
    bm = x.shape[0]
    c = c_ref[...]
    s = s_ref[...]
    hg = jnp.broadcast_to(hg_ref[...], (MLA_HEAD_PAD, bm))
    for hc in range(MLA_HEADS // heads_per_dot):
        base = hc * heads_per_dot * MLA_HEAD_PAD
        yt = lax.dot_general(wt_ref[base:base + heads_per_dot * MLA_HEAD_PAD, :], xn, _NT,
                             preferred_element_type=F32)
        for j in range(heads_per_dot):
            lo = base + j * MLA_HEAD_PAD
            y = yt[j * MLA_HEAD_PAD:(j + 1) * MLA_HEAD_PAD]
            ms = jnp.sum(y * y, axis=0, keepdims=True) * (1.0 / MLA_QK)
            yn = y * lax.rsqrt(ms + EPS) * hg
            x1 = yn[128:160]
            x2 = yn[192:224]
            o_ref[lo:lo + 128, :] = yn[:128].astype(o_ref.dtype)
            o_ref[lo + 128:lo + 160, :] = (x1 * c - x2 * s).astype(o_ref.dtype)
            o_ref[lo + 160:lo + 192, :] = yn[160:192].astype(o_ref.dtype)
            o_ref[lo + 192:lo + 224, :] = (x2 * c + x1 * s).astype(o_ref.dtype)
            o_ref[lo + 224:lo + 256, :] = yn[224:256].astype(o_ref.dtype)


def _qproj(z, ng, wt, hg_col, cos_t, sin_t, bm):
    m = z.shape[0]
    nt = cos_t.shape[1] // bm
    kern = functools.partial(_qproj_kernel, heads_per_dot=1)
    return pl.pallas_call(
        kern,
        grid=(m // bm,),
        in_specs=[pl.BlockSpec((bm, MLA_LORA), lambda i: (i, Z_CQ // MLA_LORA)),
                  pl.BlockSpec((1, MLA_LORA), lambda i: (0, 0)),
                  pl.BlockSpec((MLA_HEADS * MLA_HEAD_PAD, MLA_LORA), lambda i: (0, 0)),
                  pl.BlockSpec((MLA_HEAD_PAD, 1), lambda i: (0, 0)),
                  pl.BlockSpec((MLA_ROPE // 2, bm), lambda i: (0, i % nt)),
                  pl.BlockSpec((MLA_ROPE // 2, bm), lambda i: (0, i % nt))],
        out_specs=pl.BlockSpec((MLA_HEADS * MLA_HEAD_PAD, bm), lambda i: (0, i)),
        out_shape=jax.ShapeDtypeStruct((MLA_HEADS * MLA_HEAD_PAD, m), BF16),
        compiler_params=_params("parallel"),
        name="q_proj",
    )(z, ng.reshape(1, -1), wt, hg_col, cos_t, sin_t)


def _kvproj_kernel(zc_ref, kr_ref, ng_ref, wk_ref, wvt_ref, hg_ref, c_ref, s_ref, k_ref, vt_ref):
    x = zc_ref[...]
    ms = jnp.mean(x * x, axis=-1, keepdims=True)
    xn = (x * lax.rsqrt(ms + EPS) * ng_ref[...]).astype(BF16)
    c = c_ref[...]
    s = s_ref[...]
    hg = hg_ref[...]
    kr = kr_ref[...]
    kr_ss = jnp.sum(kr * kr, axis=-1, keepdims=True)
    kr_rot = _rope(kr * hg[:, 128:], c, s)
    for hp in range(MLA_HEADS // 2):
        y = jnp.dot(xn, wk_ref[:, hp * 256:(hp + 1) * 256], preferred_element_type=F32)
        for sub in range(2):
            lo = (2 * hp + sub) * MLA_HEAD_PAD
            kn = y[:, sub * MLA_NOPE:(sub + 1) * MLA_NOPE]
            ms = (jnp.sum(kn * kn, axis=-1, keepdims=True) + kr_ss) * (1.0 / MLA_QK)
            rs = lax.rsqrt(ms + EPS)
            k_ref[:, lo:lo + 128] = (kn * rs * hg[:, :128]).astype(k_ref.dtype)
            k_ref[:, lo + 128:lo + 256] = (kr_rot * rs).astype(k_ref.dtype)
    rows_per_dot = 512
    for j in range(MLA_HEADS * MLA_V // rows_per_dot):
        sl = slice(j * rows_per_dot, (j + 1) * rows_per_dot)
        vt_ref[sl, :] = lax.dot_general(wvt_ref[sl, :], xn, _NT,
                                        preferred_element_type=F32).astype(vt_ref.dtype)


def _kvproj(z, zs, ng, wk, wvt, hg, ctab, stab, bm):
    m = z.shape[0]
    nt = ctab.shape[0] // bm
    return pl.pallas_call(
        _kvproj_kernel,
        grid=(m // bm,),
        in_specs=[pl.BlockSpec((bm, MLA_LORA), lambda i: (i, Z_CKV // MLA_LORA)),
                  pl.BlockSpec((bm, 128), lambda i: (i, 1)),
                  pl.BlockSpec((1, MLA_LORA), lambda i: (0, 0)),
                  pl.BlockSpec((MLA_LORA, MLA_HEADS * MLA_NOPE), lambda i: (0, 0)),
                  pl.BlockSpec((MLA_HEADS * MLA_V, MLA_LORA), lambda i: (0, 0)),
                  pl.BlockSpec((1, MLA_HEAD_PAD), lambda i: (0, 0)),
                  pl.BlockSpec((bm, 128), lambda i: (i % nt, 0)),
                  pl.BlockSpec((bm, 128), lambda i: (i % nt, 0))],
        out_specs=[pl.BlockSpec((bm, MLA_HEADS * MLA_HEAD_PAD), lambda i: (i, 0)),
                   pl.BlockSpec((MLA_HEADS * MLA_V, bm), lambda i: (0, i))],
        out_shape=[jax.ShapeDtypeStruct((m, MLA_HEADS * MLA_HEAD_PAD), BF16),
                   jax.ShapeDtypeStruct((MLA_HEADS * MLA_V, m), BF16)],
        compiler_params=_params("parallel"),
        name="kv_proj",
    )(z, zs, ng.reshape(1, -1), wk, wvt, hg, ctab, stab)


def _attn_kernel(qt_ref, k_ref, vt_ref, km_ref, vmt_ref, o_ref, m_sc, l_sc, acc_sc, s_sc, *, blk, heads,
                 qw, ahead, nq):
    i = pl.program_id(2)

    def queries(qi, h, lo, width):
        return qt_ref[h * MLA_HEAD_PAD:(h + 1) * MLA_HEAD_PAD,
                      pl.ds(pl.multiple_of(qi * blk + lo, width), width)]

    for h in range(heads):
        s = jnp.dot(km_ref[:, h * MLA_HEAD_PAD:(h + 1) * MLA_HEAD_PAD], queries(i, h, 0, blk),
                    preferred_element_type=F32)
        m = jnp.max(s, axis=0, keepdims=True)
        p = jnp.exp2(s - m)
        m_sc[h] = m
        l_sc[h] = jnp.sum(p, axis=0, keepdims=True)
        acc_sc[h] = jnp.dot(vmt_ref[h * MLA_V:(h + 1) * MLA_V, :], p.astype(BF16),
                            preferred_element_type=F32)

    tasks = [(h, c) for c in range(blk // qw) for h in range(heads)]

    def scores(qi, kb, h, c):
        ksl = pl.ds(pl.multiple_of(kb * blk, blk), blk)
        return jnp.dot(k_ref[ksl, h * MLA_HEAD_PAD:(h + 1) * MLA_HEAD_PAD],
                       queries(qi, h, c * qw, qw),
                       preferred_element_type=F32)

    @pl.when(i == 0)
    def _():
        for n in range(ahead):
            s_sc[n] = scores(0, 0, *tasks[n])

    def step(kb, masked):
        ksl = pl.ds(pl.multiple_of(kb * blk, blk), blk)
        pending = [s_sc[n] for n in range(ahead)]
        handover = []
        for n, (h, c) in enumerate(tasks):
            csl = slice(c * qw, (c + 1) * qw)
            s = pending.pop(0)
            if n + ahead < len(tasks):
                pending.append(scores(i, kb, *tasks[n + ahead]))
            elif not masked:
                handover.append(scores(i, kb + 1, *tasks[n + ahead - len(tasks)]))
            else:
                handover.append(scores(jnp.minimum(i + 1, nq - 1), 0,
                                       *tasks[n + ahead - len(tasks)]))
            if masked:
                key = lax.broadcasted_iota(jnp.int32, s.shape, 0)
                qry = lax.broadcasted_iota(jnp.int32, s.shape, 1) + c * qw
                s = jnp.where(key <= qry, s, -1e30)
            m_prev = m_sc[h, :, csl]
            m_new = jnp.maximum(m_prev, jnp.max(s, axis=0, keepdims=True))
            alpha = jnp.exp2(m_prev - m_new)
            p = jnp.exp2(s - m_new)
            l_sc[h, :, csl] = alpha * l_sc[h, :, csl] + jnp.sum(p, axis=0, keepdims=True)
            acc_sc[h, :, csl] = alpha * acc_sc[h, :, csl] + jnp.dot(
                vt_ref[h * MLA_V:(h + 1) * MLA_V, ksl], p.astype(BF16),
                preferred_element_type=F32)
            m_sc[h, :, csl] = m_new
        for n, s_next in enumerate(handover):
            s_sc[n] = s_next

    def body(kb, carry):
        step(kb, False)
        return carry

    lax.fori_loop(0, i, body, 0)
    step(i, True)
    for h in range(heads):
        o_ref[:, h * MLA_V:(h + 1) * MLA_V] = (acc_sc[h] * (1.0 / l_sc[h])).T.astype(o_ref.dtype)


def _attention(q, k, vt, km, vmt, blk, heads, qw, ahead):
    nq = SEQ // blk
    kern = functools.partial(_attn_kernel, blk=blk, heads=heads, qw=qw, ahead=ahead, nq=nq)
    return pl.pallas_call(
        kern,
        grid=(BATCH, MLA_HEADS // heads, nq),
        in_specs=[pl.BlockSpec((heads * MLA_HEAD_PAD, SEQ), lambda b, h, i: (h, b)),
                  pl.BlockSpec((SEQ, heads * MLA_HEAD_PAD), lambda b, h, i: (b, h)),
                  pl.BlockSpec((heads * MLA_V, SEQ), lambda b, h, i: (h, b)),
                  pl.BlockSpec((N_META, heads * MLA_HEAD_PAD), lambda b, h, i: (0, h)),
                  pl.BlockSpec((heads * MLA_V, N_META), lambda b, h, i: (h, 0))],
        out_specs=pl.BlockSpec((blk, heads * MLA_V), lambda b, h, i: (b * nq + i, h)),
        out_shape=jax.ShapeDtypeStruct((ROWS, MLA_HEADS * MLA_V), BF16),
        scratch_shapes=[pltpu.VMEM((heads, 1, blk), F32), pltpu.VMEM((heads, 1, blk), F32),
                        pltpu.VMEM((heads, MLA_V, blk), F32), pltpu.VMEM((ahead, blk, qw), F32)],
        compiler_params=_params("parallel", "parallel", "arbitrary"),
        name="mla_attention",
    )(q, k, vt, km, vmt)


def _meta_attn_kernel(q_ref, km_ref, vmt_ref, o_ref):
    s = lax.dot_general(q_ref[...], km_ref[...], _NT, preferred_element_type=F32)
    row = lax.broadcasted_iota(jnp.int32, s.shape, 0)
    col = lax.broadcasted_iota(jnp.int32, s.shape, 1)
    s = jnp.where(col <= row, s, -1e30)
    m = jnp.max(s, axis=-1, keepdims=True)
    p = jnp.exp2(s - m)
    l = jnp.sum(p, axis=-1, keepdims=True)
    o = lax.dot_general(p.astype(BF16), vmt_ref[...], _NT, preferred_element_type=F32)
    o_ref[...] = (o / l).astype(o_ref.dtype)


def _meta_attention(qm, kmp, vmtp):
    return pl.pallas_call(
        _meta_attn_kernel,
        grid=(MLA_HEADS,),
        in_specs=[pl.BlockSpec((N_META, MLA_HEAD_PAD), lambda h: (0, h)),
                  pl.BlockSpec((128, MLA_HEAD_PAD), lambda h: (0, h)),
                  pl.BlockSpec((MLA_V, 128), lambda h: (h, 0))],
        out_specs=pl.BlockSpec((N_META, MLA_V), lambda h: (0, h)),
        out_shape=jax.ShapeDtypeStruct((N_META, MLA_HEADS * MLA_V), BF16),
        compiler_params=_params("parallel"),
        name="meta_attention",
    )(qm, kmp, vmtp)


def _merge_kernel(a_ref, b_ref, wa_ref, wb_ref, za_ref, zb_ref, o_ref):
    ga = _sigmoid(za_ref[...])
    gb = _sigmoid(zb_ref[...])
    ya = jnp.dot(a_ref[...], wa_ref[...], preferred_element_type=F32)
    yb = jnp.dot(b_ref[...], wb_ref[...], preferred_element_type=F32)
    o_ref[...] = (ga * ya + gb * yb).astype(o_ref.dtype)


def _merge(a, b, wa3, wb3, layer, z, bm, bn):
    m = a.shape[0]
    return pl.pallas_call(
        _merge_kernel,
        grid=(m // bm, D_MODEL // bn),
        in_specs=[pl.BlockSpec((bm, D_MODEL), lambda i, j: (i, 0)),
                  pl.BlockSpec((bm, D_MODEL), lambda i, j: (i, 0)),
                  pl.BlockSpec((None, D_MODEL, bn), lambda i, j: (layer, 0, j)),
                  pl.BlockSpec((None, D_MODEL, bn), lambda i, j: (layer, 0, j)),
                  pl.BlockSpec((bm, bn), lambda i, j: (i, Z_A // bn + j)),
                  pl.BlockSpec((bm, bn), lambda i, j: (i, Z_B // bn + j))],
        out_specs=pl.BlockSpec((bm, bn), lambda i, j: (i, j)),
        out_shape=jax.ShapeDtypeStruct((m, D_MODEL), BF16),
        compiler_params=_params("parallel", "parallel"),
        name="branch_merge",
    )(a, b, wa3, wb3, z, z)


def _outproj_kernel(m_ref, w_ref, h_ref, g_ref, ho_ref, hn_ref):
    hn = h_ref[...] + jnp.dot(m_ref[...], w_ref[...], preferred_element_type=F32)
    ho_ref[...] = hn
    ms = jnp.mean(hn * hn, axis=-1, keepdims=True)
    hn_ref[...] = (hn * lax.rsqrt(ms + EPS) * g_ref[...]).astype(hn_ref.dtype)


def _outproj(mg, w3, layer, h, g, bm):
    m = mg.shape[0]
    return pl.pallas_call(
        _outproj_kernel,
        grid=(m // bm,),
        in_specs=[pl.BlockSpec((bm, D_MODEL), lambda i: (i, 0)),
                  pl.BlockSpec((None, D_MODEL, D_MODEL), lambda i: (layer, 0, 0)),
                  pl.BlockSpec((bm, D_MODEL), lambda i: (i, 0)),
                  pl.BlockSpec((1, D_MODEL), lambda i: (0, 0))],
        out_specs=[pl.BlockSpec((bm, D_MODEL), lambda i: (i, 0)),
                   pl.BlockSpec((bm, D_MODEL), lambda i: (i, 0))],
        out_shape=[jax.ShapeDtypeStruct((m, D_MODEL), F32),
                   jax.ShapeDtypeStruct((m, D_MODEL), BF16)],
        compiler_params=_params("parallel"),
        name="out_proj",
    )(mg, w3, h, g.reshape(1, -1))


def _gateup_kernel(h_ref, wg_ref, wu_ref, o_ref, wgb_ref, wub_ref):
    @pl.when(pl.program_id(1) == 0)
    def _():
        wgb_ref[...] = wg_ref[...].astype(BF16)
        wub_ref[...] = wu_ref[...].astype(BF16)

    h = h_ref[...]
    g = jnp.dot(h, wgb_ref[...], preferred_element_type=F32)
    u = jnp.dot(h, wub_ref[...], preferred_element_type=F32)
    o_ref[...] = (g * _sigmoid(g) * u).astype(o_ref.dtype)


def _gateup(h, w3, layer, bm, bn):
    m = h.shape[0]
    nb = FF_HIDDEN // bn
    return pl.pallas_call(
        _gateup_kernel,
        grid=(nb, m // bm),
        in_specs=[pl.BlockSpec((bm, D_MODEL), lambda j, i: (i, 0)),
                  pl.BlockSpec((None, D_MODEL, bn), lambda j, i: (layer, 0, j)),
                  pl.BlockSpec((None, D_MODEL, bn), lambda j, i: (layer, 0, nb + j))],
        out_specs=pl.BlockSpec((bm, bn), lambda j, i: (i, j)),
        out_shape=jax.ShapeDtypeStruct((m, FF_HIDDEN), BF16),
        scratch_shapes=[pltpu.VMEM((D_MODEL, bn), BF16), pltpu.VMEM((D_MODEL, bn), BF16)],
        compiler_params=_params("arbitrary", "arbitrary"),
        name="gate_up",
    )(h, w3, w3)


def _down_kernel(a_ref, w_ref, h_ref, *rest, emit_norm):
    if emit_norm:
        g_ref, ho_ref, hn_ref, acc_ref = rest
    else:
        ho_ref, acc_ref = rest
    k = pl.program_id(1)
    last = pl.num_programs(1) - 1

    def product():
        return jnp.dot(a_ref[...], w_ref[...], preferred_element_type=F32)

    @pl.when(k == 0)
    def _():
        acc_ref[...] = product()

    @pl.when((k > 0) & (k < last))
    def _():
        acc_ref[...] += product()

    @pl.when(k == last)
    def _():
        hn = h_ref[...] + (acc_ref[...] + product())
        ho_ref[...] = hn
        if emit_norm:
            ms = jnp.mean(hn * hn, axis=-1, keepdims=True)
            hn_ref[...] = (hn * lax.rsqrt(ms + EPS) * g_ref[...]).astype(hn_ref.dtype)


def _down(a, w3, layer, h, g, bm, bk):
    m = a.shape[0]
    assert FF_HIDDEN // bk >= 2
    emit_norm = g is not None
    row_block = pl.BlockSpec((bm, D_MODEL), lambda i, k: (i, 0))
    in_specs = [pl.BlockSpec((bm, bk), lambda i, k: (i, k)),
                pl.BlockSpec((None, bk, D_MODEL), lambda i, k: (layer, k, 0)),
                row_block]
    operands = [a, w3, h]
    out_specs = [row_block]
    out_shape = [jax.ShapeDtypeStruct((m, D_MODEL), F32)]
    if emit_norm:
        in_specs.append(pl.BlockSpec((1, D_MODEL), lambda i, k: (0, 0)))
        operands.append(g.reshape(1, -1))
        out_specs.append(row_block)
        out_shape.append(jax.ShapeDtypeStruct((m, D_MODEL), BF16))
    return pl.pallas_call(
        functools.partial(_down_kernel, emit_norm=emit_norm),
        grid=(m // bm, FF_HIDDEN // bk),
        in_specs=in_specs,
        out_specs=out_specs,
        out_shape=out_shape,
        scratch_shapes=[pltpu.VMEM((bm, D_MODEL), F32)],
        compiler_params=_params("parallel", "arbitrary"),
        name="down_proj",
    )(*operands)


def _rope_layout(t):
    zeros = jnp.zeros(t.shape[:-1] + (32,), t.dtype)
    return jnp.concatenate([t[..., :32], zeros, t[..., 32:], zeros], axis=-1)


def _head_layout(t):
    return jnp.concatenate([t[..., :MLA_NOPE], _rope_layout(t[..., MLA_NOPE:])], axis=-1)


def _rope_tables():
    length = N_META + SEQ
    inv = 1.0 / (ROPE_THETA ** (jnp.arange(0, MLA_ROPE, 2, dtype=F32) / MLA_ROPE))
    ang = jnp.arange(length, dtype=F32)[:, None] * inv[None, :]
    cos, sin = jnp.cos(ang), jnp.sin(ang)
    zeros = jnp.zeros_like(cos)
    ctab = jnp.concatenate([cos, zeros, cos, zeros], axis=-1)
    stab = jnp.concatenate([-sin, zeros, sin, zeros], axis=-1)
    return ctab, stab, cos.T, sin.T


def kernel(x, meta_tokens, norm1_g, w_in, gla_gate_w2, gla_gate_b, gla_onorm_g, w_branch_a,
           q_a_norm_g, w_uq, kv_a_norm_g, w_ukv, q_norm_g, k_norm_g, w_branch_b, w_out,
           norm2_g, w_gate_up, w_down):
    ctab, stab, cos_t, sin_t = _rope_tables()
    ctab_m, stab_m = ctab[:N_META], stab[:N_META]
    ctab_r, stab_r = ctab[N_META:], stab[N_META:]
    cos_tm, sin_tm = cos_t[:, :N_META], sin_t[:, :N_META]
    cos_tr, sin_tr = cos_t[:, N_META:], sin_t[:, N_META:]

    h_res = x.reshape(ROWS, D_MODEL)
    hm_res = meta_tokens.astype(F32)
    h = _rms(h_res, norm1_g[0], 512)
    hm = _rms(hm_res, norm1_g[0], N_META)
    s_zero = jnp.zeros((GLA_HEADS, GLA_DK, GLA_DV), F32)

    w_in_t = jnp.swapaxes(w_in, 1, 2)
    w_kr = w_in_t[:, 7184:7248]
    zrow = jnp.zeros((DEPTH, 32, D_MODEL), F32)
    w_small = jnp.concatenate(
        [w_in_t[:, 6144:6160], jnp.zeros((DEPTH, 128 - GLA_GATE_RANK, D_MODEL), F32),
         w_kr[:, :32], zrow, w_kr[:, 32:], zrow], axis=1)
    wa = w_branch_a.astype(BF16)
    wb = w_branch_b.astype(BF16)
    wo = w_out.astype(BF16)
    wd = w_down.astype(BF16)

    for l in range(DEPTH):
        last = l == DEPTH - 1
        w2p = jnp.concatenate(
            [gla_gate_w2[l], jnp.zeros((128 - GLA_GATE_RANK, GLA_HEADS * GLA_DK), F32)],
            axis=0).astype(BF16)
        wqt = _head_layout(w_uq[l].reshape(MLA_LORA, MLA_HEADS, MLA_QK)).reshape(
            MLA_LORA, MLA_HEADS * MLA_HEAD_PAD).T.astype(BF16)
        wkv3 = w_ukv[l].reshape(MLA_LORA, MLA_HEADS, MLA_NOPE + MLA_V)
        wk = wkv3[:, :, :MLA_NOPE].reshape(MLA_LORA, MLA_HEADS * MLA_NOPE).astype(BF16)
        wvt = wkv3[:, :, MLA_NOPE:].reshape(MLA_LORA, MLA_HEADS * MLA_V).T.astype(BF16)
        qg = (_head_layout(q_norm_g[l]) * (MLA_QK ** -0.5 * LOG2E)).reshape(MLA_HEAD_PAD, 1)
        kg = _head_layout(k_norm_g[l]).reshape(1, MLA_HEAD_PAD)
        g_next = None if last else norm1_g[l + 1]

        z = _in_proj(h, w_in_t, l, 1024, 1024, name="in_proj")
        zs = _matmul_nt(h, w_small, l, 1024, N_SMALL, name="in_proj_small")
        zm = _in_proj(hm, w_in_t, l, N_META, 1024, name="in_proj_meta")
        zsm = _matmul_nt(hm, w_small, l, N_META, N_SMALL, name="in_proj_small_meta")

        gla_m, s_meta = _gla(zm, zsm, w2p, gla_gate_b[l], gla_onorm_g[l], s_zero,
                             batch=1, tokens=N_META, block=N_META, chunk=N_META, emit_state=True)
        (gla_r,) = _gla(z, zs, w2p, gla_gate_b[l], gla_onorm_g[l], s_meta,
                        batch=BATCH, tokens=SEQ, block=512, chunk=GLA_CHUNK, emit_state=False)

        km, vmt = _kvproj(zm, zsm, kv_a_norm_g[l], wk, wvt, kg, ctab_m, stab_m, N_META)
        kr, vrt = _kvproj(z, zs, kv_a_norm_g[l], wk, wvt, kg, ctab_r, stab_r, 512)
        kmp = jnp.pad(km, ((0, 128 - N_META), (0, 0)))
        vmtp = jnp.pad(vmt, ((0, 0), (0, 128 - N_META)))
        qr = _qproj(z, q_a_norm_g[l], wqt, qg, cos_tr, sin_tr, 512)
        att_r = _attention(qr, kr, vrt, km, vmt, 512, 4, 512, 4)

        merged = _merge(gla_r, att_r, wa, wb, l, z, 1024, 512)
        h_res, h2 = _outproj(merged, wo, l, h_res, norm2_g[l], 512)
        act = _gateup(h2, w_gate_up, l, 1024, 512)
        h_res, *h_next = _down(act, wd, l, h_res, g_next, 512, 2816)
        h = h_next[0] if h_next else None

        if not last:
            qm = _qproj(zm, q_a_norm_g[l], wqt, qg, cos_tm, sin_tm, N_META)
            att_m = _meta_attention(qm.T, kmp, vmtp)
            merged_m = _merge(gla_m, att_m, wa, wb, l, zm, N_META, 512)
            hm_res, hm2 = _outproj(merged_m, wo, l, hm_res, norm2_g[l], N_META)
            act_m = _gateup(hm2, w_gate_up, l, N_META, 512)
            hm_res, hm = _down(act_m, wd, l, hm_res, g_next, N_META, 1408)

    return h_res.reshape(BATCH, SEQ, D_MODEL)
```

```python
import functools

import jax
import jax.numpy as jnp
from jax import lax
from jax.experimental import pallas as pl
from jax.experimental.pallas import tpu as pltpu

D_MODEL = 2048
BATCH = 4
SEQ = 4096
DEPTH = 2
N_META = 16
ROWS = BATCH * SEQ

GLA_HEADS = 4
GLA_DK = 256
GLA_DV = 512
GLA_GATE_RANK = 16
GLA_GATE_TAU = 16.0
GLA_CHUNK = 64

MLA_HEADS = 16
MLA_LORA = 512
MLA_NOPE = 128
MLA_ROPE = 64
MLA_QK = 192
MLA_V = 128
MLA_HEAD_PAD = 256
ROPE_THETA = 10000.0
FF_HIDDEN = 5632
EPS = 1e-6
LOG2E = 1.4426950408889634

W_IN_ZA = 6144
W_IN_C, W_IN_C_END = 6160, 7184
W_IN_G = 7248
Z_Q, Z_K, Z_V, Z_R, Z_CQ, Z_CKV, Z_A, Z_B = 0, 1024, 2048, 4096, 6144, 6656, 7168, 9216
N_Z = 11264
N_SMALL = 256

VMEM_LIMIT = 56 * 1024 * 1024
BF16 = jnp.bfloat16
F32 = jnp.float32

_NT = (((1,), (1,)), ((), ()))
_TN = (((0,), (0,)), ((), ()))


def _params(*sem):
    return pltpu.CompilerParams(dimension_semantics=sem, vmem_limit_bytes=VMEM_LIMIT)


def _sigmoid(x):
    return 1.0 / (1.0 + jnp.exp(-x))


def _rms_kernel(x_ref, g_ref, o_ref):
    x = x_ref[...]
    ms = jnp.mean(x * x, axis=-1, keepdims=True)
    o_ref[...] = (x * lax.rsqrt(ms + EPS) * g_ref[...]).astype(o_ref.dtype)


def _rms(x, g, bm):
    m, d = x.shape
    return pl.pallas_call(
        _rms_kernel,
        grid=(m // bm,),
        in_specs=[pl.BlockSpec((bm, d), lambda i: (i, 0)),
                  pl.BlockSpec((1, d), lambda i: (0, 0))],
        out_specs=pl.BlockSpec((bm, d), lambda i: (i, 0)),
        out_shape=jax.ShapeDtypeStruct((m, d), BF16),
        compiler_params=_params("parallel"),
        name="rmsnorm",
    )(x, g.reshape(1, d))


def _mm_nt_kernel(a_ref, wt_ref, o_ref):
    o_ref[...] = lax.dot_general(a_ref[...], wt_ref[...].astype(BF16), _NT,
                                 preferred_element_type=F32)


def _matmul_nt(a, wt3, layer, bm, bn, name):
    m, k = a.shape
    n = wt3.shape[1]
    return pl.pallas_call(
        _mm_nt_kernel,
        grid=(m // bm, n // bn),
        in_specs=[pl.BlockSpec((bm, k), lambda i, j: (i, 0)),
                  pl.BlockSpec((None, bn, k), lambda i, j: (layer, j, 0))],
        out_specs=pl.BlockSpec((bm, bn), lambda i, j: (i, j)),
        out_shape=jax.ShapeDtypeStruct((m, n), F32),
        compiler_params=_params("parallel", "parallel"),
        name=name,
    )(a, wt3)


def _mm_nt_wstat_kernel(a_ref, am_ref, wt_ref, o_ref, om_ref, wb_ref):
    @pl.when(pl.program_id(1) == 0)
    def _():
        wb_ref[...] = wt_ref[0].astype(BF16)
        om_ref[...] = lax.dot_general(am_ref[...], wb_ref[...], _NT, preferred_element_type=F32)

    o_ref[...] = lax.dot_general(a_ref[...], wb_ref[...], _NT, preferred_element_type=F32)


def _in_proj_row0(j, bn):
    n_a = W_IN_ZA // bn
    n_c = (W_IN_C_END - W_IN_C) // bn
    return jnp.where(j < n_a, j * bn,
                     jnp.where(j < n_a + n_c, W_IN_C + (j - n_a) * bn,
                               W_IN_G + (j - n_a - n_c) * bn))


def _in_proj(a, a_meta, wt3, layer, bm, bn, name):
    m, k = a.shape
    return pl.pallas_call(
        _mm_nt_wstat_kernel,
        grid=(N_Z // bn, m // bm),
        in_specs=[pl.BlockSpec((bm, k), lambda j, i: (i, 0)),
                  pl.BlockSpec((N_META, k), lambda j, i: (0, 0)),
                  pl.BlockSpec((pl.Element(1), pl.Element(bn), pl.Element(k)),
                               lambda j, i: (layer, pl.multiple_of(_in_proj_row0(j, bn), 16), 0))],
        out_specs=[pl.BlockSpec((bm, bn), lambda j, i: (i, j)),
                   pl.BlockSpec((N_META, bn), lambda j, i: (0, j))],
        out_shape=[jax.ShapeDtypeStruct((m, N_Z), F32),
                   jax.ShapeDtypeStruct((N_META, N_Z), F32)],
        scratch_shapes=[pltpu.VMEM((bn, k), BF16)],
        compiler_params=_params("arbitrary", "arbitrary"),
        name=name,
    )(a, a_meta, wt3)


def _gla_kernel(q_ref, k_ref, v_ref, r_ref, zlr_ref, w2_ref, gb_ref, og_ref, s0_ref,
                o_ref, *rest, chunk, n_chunks, emit_state):
    if emit_state:
        sfin_ref, st_ref, b_sc = rest
    else:
        st_ref, b_sc = rest
    t = pl.program_id(1)
    heads = range(GLA_HEADS)

    @pl.when(t == 0)
    def _():
        st_ref[...] = s0_ref[...]

    row = lax.broadcasted_iota(jnp.int32, (chunk, chunk), 0)
    col = lax.broadcasted_iota(jnp.int32, (chunk, chunk), 1)
    causal = col <= row
    tri = jnp.where(causal, 1.0, 0.0).astype(BF16)
    w2 = w2_ref[...]
    gb = gb_ref[...]
    og = og_ref[...]
    kdim = GLA_HEADS * GLA_DK

    def log_gate_split(c):
        sl = pl.ds(pl.multiple_of(c * chunk, chunk), chunk)
        logit = jnp.dot(zlr_ref[sl, :].astype(BF16), w2, preferred_element_type=F32) + gb
        g = (jnp.minimum(logit, 0.0) - jnp.log1p(jnp.exp(-jnp.abs(logit)))) * (1.0 / GLA_GATE_TAU)
        g_hi = g.astype(BF16)
        g_lo = (g - g_hi.astype(F32)).astype(BF16)
        return jnp.concatenate([g_hi, g_lo], axis=1)

    def log_decay(g_split):
        cs = jnp.dot(tri, g_split, preferred_element_type=F32)
        return cs[:, :kdim] + cs[:, kdim:]

    b_sc[...] = log_decay(log_gate_split(0))

    def body(c, carry):
        sl = pl.ds(pl.multiple_of(c * chunk, chunk), chunk)
        g_split_next = log_gate_split(jnp.minimum(c + 1, n_chunks - 1))
        r = r_ref[sl, :]
        out_gate = r * _sigmoid(r)
        st = [st_ref[h] for h in heads]
        st_bf = [st[h].astype(BF16) for h in heads]
        b = b_sc[...]
        b_last = b[chunk - 1:chunk, :]
        eb = jnp.exp(b)
        enb = jnp.exp(-b)
        erel = jnp.exp(b_last - b)
        elast = jnp.exp(b_last)
        k = k_ref[sl, :]
        qd = (q_ref[sl, :] * (GLA_DK ** -0.5) * eb).astype(BF16)
        kd = (k * enb).astype(BF16)
        k2 = (k * erel).astype(BF16)
        v = v_ref[sl, :].astype(BF16)

        def hk(x, h):
            return x[:, h * GLA_DK:(h + 1) * GLA_DK]

        def hv(x, h):
            return x[:, h * GLA_DV:(h + 1) * GLA_DV]

        a = [lax.dot_general(hk(qd, h), hk(kd, h), _NT, preferred_element_type=F32) for h in heads]
        o_state = [jnp.dot(hk(qd, h), st_bf[h], preferred_element_type=F32) for h in heads]
        o_local = [jnp.dot(jnp.where(causal, a[h], 0.0).astype(BF16), hv(v, h),
                           preferred_element_type=F32) for h in heads]
        b_sc[...] = log_decay(g_split_next)
        upd = [lax.dot_general(hk(k2, h), hv(v, h), _TN, preferred_element_type=F32)
               for h in heads]
        decay_col = jnp.broadcast_to(elast, (128, kdim)).T
        for h in heads:
            col = decay_col[h * GLA_DK:(h + 1) * GLA_DK]
            st_ref[h] = st[h] * jnp.concatenate([col] * (GLA_DV // 128), axis=1) + upd[h]
        for h in heads:
            o = o_local[h] + o_state[h]
            ms = jnp.mean(o * o, axis=-1, keepdims=True)
            on = o * lax.rsqrt(ms + EPS) * og
            o_ref[sl, h * GLA_DV:(h + 1) * GLA_DV] = (on * hv(out_gate, h)).astype(o_ref.dtype)
        return carry

    lax.fori_loop(0, n_chunks, body, 0)

    if emit_state:
        @pl.when(t == pl.num_programs(1) - 1)
        def _():
            sfin_ref[...] = st_ref[...]


def _gla(z, zs, w2p, gate_b, onorm_g, s0, *, batch, tokens, block, chunk, emit_state):
    nt = tokens // block
    rows = batch * tokens
    kern = functools.partial(_gla_kernel, chunk=chunk, n_chunks=block // chunk,
                             emit_state=emit_state)
    kdim = GLA_HEADS * GLA_DK
    vdim = GLA_HEADS * GLA_DV
    in_specs = [
        pl.BlockSpec((block, kdim), lambda b, t: (b * nt + t, Z_Q // kdim)),
        pl.BlockSpec((block, kdim), lambda b, t: (b * nt + t, Z_K // kdim)),
        pl.BlockSpec((block, vdim), lambda b, t: (b * nt + t, Z_V // vdim)),
        pl.BlockSpec((block, vdim), lambda b, t: (b * nt + t, Z_R // vdim)),
        pl.BlockSpec((block, 128), lambda b, t: (b * nt + t, 0)),
        pl.BlockSpec((128, kdim), lambda b, t: (0, 0)),
        pl.BlockSpec((1, kdim), lambda b, t: (0, 0)),
        pl.BlockSpec((1, GLA_DV), lambda b, t: (0, 0)),
        pl.BlockSpec((GLA_HEADS, GLA_DK, GLA_DV), lambda b, t: (0, 0, 0)),
    ]
    out_specs = [pl.BlockSpec((block, vdim), lambda b, t: (b * nt + t, 0))]
    out_shape = [jax.ShapeDtypeStruct((rows, vdim), BF16)]
    if emit_state:
        out_specs.append(pl.BlockSpec((GLA_HEADS, GLA_DK, GLA_DV), lambda b, t: (b, 0, 0)))
        out_shape.append(jax.ShapeDtypeStruct((batch * GLA_HEADS, GLA_DK, GLA_DV), F32))
    res = pl.pallas_call(
        kern,
        grid=(batch, nt),
        in_specs=in_specs,
        out_specs=out_specs,
        out_shape=out_shape,
        scratch_shapes=[pltpu.VMEM((GLA_HEADS, GLA_DK, GLA_DV), F32),
                        pltpu.VMEM((chunk, kdim), F32)],
        compiler_params=_params("parallel", "arbitrary"),
        name="gla",
    )(z, z, z, z, zs, w2p, gate_b.reshape(1, -1), onorm_g.reshape(1, -1), s0)
    return res


def _rope(x, c, s):
    return x * c + pltpu.roll(x, 64, 1) * s


def _qproj_kernel(zc_ref, ng_ref, wt_ref, hg_ref, c_ref, s_ref, o_ref, *, heads_per_dot):
    x = zc_ref[...]
    ms = jnp.mean(x * x, axis=-1, keepdims=True)
    xn = (x * lax.rsqrt(ms + EPS) * ng_ref[...]).astype(BF16)
    bm = x.shape[0]
    c = c_ref[...]
    s = s_ref[...]
    hg = jnp.broadcast_to(hg_ref[...], (MLA_HEAD_PAD, bm))
    for hc in range(MLA_HEADS // heads_per_dot):
        base = hc * heads_per_dot * MLA_HEAD_PAD
        yt = lax.dot_general(wt_ref[base:base + heads_per_dot * MLA_HEAD_PAD, :], xn, _NT,
                             preferred_element_type=F32)
        for j in range(heads_per_dot):
            lo = base + j * MLA_HEAD_PAD
            y = yt[j * MLA_HEAD_PAD:(j + 1) * MLA_HEAD_PAD]
            ms = jnp.sum(y * y, axis=0, keepdims=True) * (1.0 / MLA_QK)
            yn = y * lax.rsqrt(ms + EPS) * hg
            x1 = yn[128:160]
            x2 = yn[192:224]
            o_ref[lo:lo + 128, :] = yn[:128].astype(o_ref.dtype)
            o_ref[lo + 128:lo + 160, :] = (x1 * c - x2 * s).astype(o_ref.dtype)
            o_ref[lo + 160:lo + 192, :] = yn[160:192].astype(o_ref.dtype)
            o_ref[lo + 192:lo + 224, :] = (x2 * c + x1 * s).astype(o_ref.dtype)
            o_ref[lo + 224:lo + 256, :] = yn[224:256].astype(o_ref.dtype)


def _qproj(z, ng, wt, hg_col, cos_t, sin_t, bm):
    m = z.shape[0]
    nt = cos_t.shape[1] // bm
    kern = functools.partial(_qproj_kernel, heads_per_dot=1)
    return pl.pallas_call(
        kern,
        grid=(m // bm,),
        in_specs=[pl.BlockSpec((bm, MLA_LORA), lambda i: (i, Z_CQ // MLA_LORA)),
                  pl.BlockSpec((1, MLA_LORA), lambda i: (0, 0)),
                  pl.BlockSpec((MLA_HEADS * MLA_HEAD_PAD, MLA_LORA), lambda i: (0, 0)),
                  pl.BlockSpec((MLA_HEAD_PAD, 1), lambda i: (0, 0)),
                  pl.BlockSpec((MLA_ROPE // 2, bm), lambda i: (0, i % nt)),
                  pl.BlockSpec((MLA_ROPE // 2, bm), lambda i: (0, i % nt))],
        out_specs=pl.BlockSpec((MLA_HEADS * MLA_HEAD_PAD, bm), lambda i: (0, i)),
        out_shape=jax.ShapeDtypeStruct((MLA_HEADS * MLA_HEAD_PAD, m), BF16),
        compiler_params=_params("parallel"),
        name="q_proj",
    )(z, ng.reshape(1, -1), wt, hg_col, cos_t, sin_t)


def _kvproj_kernel(zc_ref, kr_ref, ng_ref, wk_ref, wvt_ref, hg_ref, c_ref, s_ref, k_ref, vt_ref):
    x = zc_ref[...]
    ms = jnp.mean(x * x, axis=-1, keepdims=True)
    xn = (x * lax.rsqrt(ms + EPS) * ng_ref[...]).astype(BF16)
    c = c_ref[...]
    s = s_ref[...]
    hg = hg_ref[...]
    kr = kr_ref[...]
    kr_ss = jnp.sum(kr * kr, axis=-1, keepdims=True)
    kr_rot = _rope(kr * hg[:, 128:], c, s)
    for hp in range(MLA_HEADS // 2):
        y = jnp.dot(xn, wk_ref[:, hp * 256:(hp + 1) * 256], preferred_element_type=F32)
        for sub in range(2):
            lo = (2 * hp + sub) * MLA_HEAD_PAD
            kn = y[:, sub * MLA_NOPE:(sub + 1) * MLA_NOPE]
            ms = (jnp.sum(kn * kn, axis=-1, keepdims=True) + kr_ss) * (1.0 / MLA_QK)
            rs = lax.rsqrt(ms + EPS)
            k_ref[:, lo:lo + 128] = (kn * rs * hg[:, :128]).astype(k_ref.dtype)
            k_ref[:, lo + 128:lo + 256] = (kr_rot * rs).astype(k_ref.dtype)
    rows_per_dot = 512
    for j in range(MLA_HEADS * MLA_V // rows_per_dot):
        sl = slice(j * rows_per_dot, (j + 1) * rows_per_dot)
        vt_ref[sl, :] = lax.dot_general(wvt_ref[sl, :], xn, _NT,
                                        preferred_element_type=F32).astype(vt_ref.dtype)


def _kvproj(z, zs, ng, wk, wvt, hg, ctab, stab, bm):
    m = z.shape[0]
    nt = ctab.shape[0] // bm
    return pl.pallas_call(
        _kvproj_kernel,
        grid=(m // bm,),
        in_specs=[pl.BlockSpec((bm, MLA_LORA), lambda i: (i, Z_CKV // MLA_LORA)),
                  pl.BlockSpec((bm, 128), lambda i: (i, 1)),
                  pl.BlockSpec((1, MLA_LORA), lambda i: (0, 0)),
                  pl.BlockSpec((MLA_LORA, MLA_HEADS * MLA_NOPE), lambda i: (0, 0)),
                  pl.BlockSpec((MLA_HEADS * MLA_V, MLA_LORA), lambda i: (0, 0)),
                  pl.BlockSpec((1, MLA_HEAD_PAD), lambda i: (0, 0)),
                  pl.BlockSpec((bm, 128), lambda i: (i % nt, 0)),
                  pl.BlockSpec((bm, 128), lambda i: (i % nt, 0))],
        out_specs=[pl.BlockSpec((bm, MLA_HEADS * MLA_HEAD_PAD), lambda i: (i, 0)),
                   pl.BlockSpec((MLA_HEADS * MLA_V, bm), lambda i: (0, i))],
        out_shape=[jax.ShapeDtypeStruct((m, MLA_HEADS * MLA_HEAD_PAD), BF16),
                   jax.ShapeDtypeStruct((MLA_HEADS * MLA_V, m), BF16)],
        compiler_params=_params("parallel"),
        name="kv_proj",
    )(z, zs, ng.reshape(1, -1), wk, wvt, hg, ctab, stab)


def _attn_kernel(qt_ref, k_ref, vt_ref, km_ref, vmt_ref, o_ref, m_sc, l_sc, acc_sc, s_sc, *, blk, heads,
                 qw, ahead, nq):
    i = pl.program_id(2)

    def queries(qi, h, lo, width):
        return qt_ref[h * MLA_HEAD_PAD:(h + 1) * MLA_HEAD_PAD,
                      pl.ds(pl.multiple_of(qi * blk + lo, width), width)]

    for h in range(heads):
        s = jnp.dot(km_ref[:, h * MLA_HEAD_PAD:(h + 1) * MLA_HEAD_PAD], queries(i, h, 0, blk),
                    preferred_element_type=F32)
        m = jnp.max(s, axis=0, keepdims=True)
        p = jnp.exp2(s - m)
        m_sc[h] = m
        l_sc[h] = jnp.sum(p, axis=0, keepdims=True)
        acc_sc[h] = jnp.dot(vmt_ref[h * MLA_V:(h + 1) * MLA_V, :], p.astype(BF16),
                            preferred_element_type=F32)

    tasks = [(h, c) for c in range(blk // qw) for h in range(heads)]

    def scores(qi, kb, h, c):
        ksl = pl.ds(pl.multiple_of(kb * blk, blk), blk)
        return jnp.dot(k_ref[ksl, h * MLA_HEAD_PAD:(h + 1) * MLA_HEAD_PAD],
                       queries(qi, h, c * qw, qw),
                       preferred_element_type=F32)

    @pl.when(i == 0)
    def _():
        for n in range(ahead):
            s_sc[n] = scores(0, 0, *tasks[n])

    def step(kb, masked):
        ksl = pl.ds(pl.multiple_of(kb * blk, blk), blk)
        pending = [s_sc[n] for n in range(ahead)]
        handover = []
        for n, (h, c) in enumerate(tasks):
            csl = slice(c * qw, (c + 1) * qw)
            s = pending.pop(0)
            if n + ahead < len(tasks):
                pending.append(scores(i, kb, *tasks[n + ahead]))
            elif not masked:
                handover.append(scores(i, kb + 1, *tasks[n + ahead - len(tasks)]))
            else:
                handover.append(scores(jnp.minimum(i + 1, nq - 1), 0,
                                       *tasks[n + ahead - len(tasks)]))
            if masked:
                key = lax.broadcasted_iota(jnp.int32, s.shape, 0)
                qry = lax.broadcasted_iota(jnp.int32, s.shape, 1) + c * qw
                s = jnp.where(key <= qry, s, -1e30)
            m_prev = m_sc[h, :, csl]
            m_new = jnp.maximum(m_prev, jnp.max(s, axis=0, keepdims=True))
            alpha = jnp.exp2(m_prev - m_new)
            p = jnp.exp2(s - m_new)
            l_sc[h, :, csl] = alpha * l_sc[h, :, csl] + jnp.sum(p, axis=0, keepdims=True)
            acc_sc[h, :, csl] = alpha * acc_sc[h, :, csl] + jnp.dot(
                vt_ref[h * MLA_V:(h + 1) * MLA_V, ksl], p.astype(BF16),
                preferred_element_type=F32)
            m_sc[h, :, csl] = m_new
        for n, s_next in enumerate(handover):
            s_sc[n] = s_next

    def body(kb, carry):
        step(kb, False)
        return carry

    lax.fori_loop(0, i, body, 0)
    step(i, True)
    for h in range(heads):
        o_ref[:, h * MLA_V:(h + 1) * MLA_V] = (acc_sc[h] * (1.0 / l_sc[h])).T.astype(o_ref.dtype)


def _attention(q, k, vt, km, vmt, blk, heads, qw, ahead):
    nq = SEQ // blk
    kern = functools.partial(_attn_kernel, blk=blk, heads=heads, qw=qw, ahead=ahead, nq=nq)
    return pl.pallas_call(
        kern,
        grid=(BATCH, MLA_HEADS // heads, nq),
        in_specs=[pl.BlockSpec((heads * MLA_HEAD_PAD, SEQ), lambda b, h, i: (h, b)),
                  pl.BlockSpec((SEQ, heads * MLA_HEAD_PAD), lambda b, h, i: (b, h)),
                  pl.BlockSpec((heads * MLA_V, SEQ), lambda b, h, i: (h, b)),
                  pl.BlockSpec((N_META, heads * MLA_HEAD_PAD), lambda b, h, i: (0, h)),
                  pl.BlockSpec((heads * MLA_V, N_META), lambda b, h, i: (h, 0))],
        out_specs=pl.BlockSpec((blk, heads * MLA_V), lambda b, h, i: (b * nq + i, h)),
        out_shape=jax.ShapeDtypeStruct((ROWS, MLA_HEADS * MLA_V), BF16),
        scratch_shapes=[pltpu.VMEM((heads, 1, blk), F32), pltpu.VMEM((heads, 1, blk), F32),
                        pltpu.VMEM((heads, MLA_V, blk), F32), pltpu.VMEM((ahead, blk, qw), F32)],
        compiler_params=_params("parallel", "parallel", "arbitrary"),
        name="mla_attention",
    )(q, k, vt, km, vmt)


def _meta_attn_kernel(q_ref, km_ref, vmt_ref, o_ref):
    s = lax.dot_general(q_ref[...], km_ref[...], _NT, preferred_element_type=F32)
    row = lax.broadcasted_iota(jnp.int32, s.shape, 0)
    col = lax.broadcasted_iota(jnp.int32, s.shape, 1)
    s = jnp.where(col <= row, s, -1e30)
    m = jnp.max(s, axis=-1, keepdims=True)
    p = jnp.exp2(s - m)
    l = jnp.sum(p, axis=-1, keepdims=True)
    o = lax.dot_general(p.astype(BF16), vmt_ref[...], _NT, preferred_element_type=F32)
    o_ref[...] = (o / l).astype(o_ref.dtype)


def _meta_attention(qm, kmp, vmtp):
    return pl.pallas_call(
        _meta_attn_kernel,
        grid=(MLA_HEADS,),
        in_specs=[pl.BlockSpec((N_META, MLA_HEAD_PAD), lambda h: (0, h)),
                  pl.BlockSpec((128, MLA_HEAD_PAD), lambda h: (0, h)),
                  pl.BlockSpec((MLA_V, 128), lambda h: (h, 0))],
        out_specs=pl.BlockSpec((N_META, MLA_V), lambda h: (0, h)),
        out_shape=jax.ShapeDtypeStruct((N_META, MLA_HEADS * MLA_V), BF16),
        compiler_params=_params("parallel"),
        name="meta_attention",
    )(qm, kmp, vmtp)


def _merge_kernel(a_ref, b_ref, wa_ref, wb_ref, za_ref, zb_ref, o_ref):
    ga = _sigmoid(za_ref[...])
    gb = _sigmoid(zb_ref[...])
    ya = jnp.dot(a_ref[...], wa_ref[...], preferred_element_type=F32)
    yb = jnp.dot(b_ref[...], wb_ref[...], preferred_element_type=F32)
    o_ref[...] = (ga * ya + gb * yb).astype(o_ref.dtype)


def _merge(a, b, wa3, wb3, layer, z, bm, bn):
    m = a.shape[0]
    return pl.pallas_call(
        _merge_kernel,
        grid=(m // bm, D_MODEL // bn),
        in_specs=[pl.BlockSpec((bm, D_MODEL), lambda i, j: (i, 0)),
                  pl.BlockSpec((bm, D_MODEL), lambda i, j: (i, 0)),
                  pl.BlockSpec((None, D_MODEL, bn), lambda i, j: (layer, 0, j)),
                  pl.BlockSpec((None, D_MODEL, bn), lambda i, j: (layer, 0, j)),
                  pl.BlockSpec((bm, bn), lambda i, j: (i, Z_A // bn + j)),
                  pl.BlockSpec((bm, bn), lambda i, j: (i, Z_B // bn + j))],
        out_specs=pl.BlockSpec((bm, bn), lambda i, j: (i, j)),
        out_shape=jax.ShapeDtypeStruct((m, D_MODEL), BF16),
        compiler_params=_params("parallel", "parallel"),
        name="branch_merge",
    )(a, b, wa3, wb3, z, z)


def _outproj_kernel(m_ref, w_ref, h_ref, g_ref, ho_ref, hn_ref):
    hn = h_ref[...] + jnp.dot(m_ref[...], w_ref[...], preferred_element_type=F32)
    ho_ref[...] = hn
    ms = jnp.mean(hn * hn, axis=-1, keepdims=True)
    hn_ref[...] = (hn * lax.rsqrt(ms + EPS) * g_ref[...]).astype(hn_ref.dtype)


def _outproj(mg, w3, layer, h, g, bm):
    m = mg.shape[0]
    return pl.pallas_call(
        _outproj_kernel,
        grid=(m // bm,),
        in_specs=[pl.BlockSpec((bm, D_MODEL), lambda i: (i, 0)),
                  pl.BlockSpec((None, D_MODEL, D_MODEL), lambda i: (layer, 0, 0)),
                  pl.BlockSpec((bm, D_MODEL), lambda i: (i, 0)),
                  pl.BlockSpec((1, D_MODEL), lambda i: (0, 0))],
        out_specs=[pl.BlockSpec((bm, D_MODEL), lambda i: (i, 0)),
                   pl.BlockSpec((bm, D_MODEL), lambda i: (i, 0))],
        out_shape=[jax.ShapeDtypeStruct((m, D_MODEL), F32),
                   jax.ShapeDtypeStruct((m, D_MODEL), BF16)],
        compiler_params=_params("parallel"),
        name="out_proj",
    )(mg, w3, h, g.reshape(1, -1))


def _gateup_kernel(*refs, with_meta):
    if with_meta:
        h_ref, hm_ref, wg_ref, wu_ref, o_ref, om_ref, wgb_ref, wub_ref = refs
    else:
        h_ref, wg_ref, wu_ref, o_ref, wgb_ref, wub_ref = refs

    def swiglu(h):
        g = jnp.dot(h, wgb_ref[...], preferred_element_type=F32)
        u = jnp.dot(h, wub_ref[...], preferred_element_type=F32)
        return (g * _sigmoid(g) * u).astype(o_ref.dtype)

    @pl.when(pl.program_id(1) == 0)
    def _():
        wgb_ref[...] = wg_ref[...].astype(BF16)
        wub_ref[...] = wu_ref[...].astype(BF16)
        if with_meta:
            om_ref[...] = swiglu(hm_ref[...])

    o_ref[...] = swiglu(h_ref[...])


def _gateup(h, h_meta, w3, layer, bm, bn):
    m = h.shape[0]
    nb = FF_HIDDEN // bn
    with_meta = h_meta is not None
    weight_specs = [pl.BlockSpec((None, D_MODEL, bn), lambda j, i: (layer, 0, j)),
                    pl.BlockSpec((None, D_MODEL, bn), lambda j, i: (layer, 0, nb + j))]
    in_specs = [pl.BlockSpec((bm, D_MODEL), lambda j, i: (i, 0))]
    operands = [h]
    out_specs = [pl.BlockSpec((bm, bn), lambda j, i: (i, j))]
    out_shape = [jax.ShapeDtypeStruct((m, FF_HIDDEN), BF16)]
    if with_meta:
        in_specs.append(pl.BlockSpec((N_META, D_MODEL), lambda j, i: (0, 0)))
        operands.append(h_meta)
        out_specs.append(pl.BlockSpec((N_META, bn), lambda j, i: (0, j)))
        out_shape.append(jax.ShapeDtypeStruct((N_META, FF_HIDDEN), BF16))
    return pl.pallas_call(
        functools.partial(_gateup_kernel, with_meta=with_meta),
        grid=(nb, m // bm),
        in_specs=in_specs + weight_specs,
        out_specs=out_specs,
        out_shape=out_shape,
        scratch_shapes=[pltpu.VMEM((D_MODEL, bn), BF16), pltpu.VMEM((D_MODEL, bn), BF16)],
        compiler_params=_params("arbitrary", "arbitrary"),
        name="gate_up",
    )(*operands, w3, w3)


def _down_kernel(a_ref, w_ref, h_ref, *rest, emit_norm):
    if emit_norm:
        g_ref, ho_ref, hn_ref, acc_ref = rest
    else:
        ho_ref, acc_ref = rest
    k = pl.program_id(1)
    last = pl.num_programs(1) - 1

    def product():
        return jnp.dot(a_ref[...], w_ref[...], preferred_element_type=F32)

    @pl.when(k == 0)
    def _():
        acc_ref[...] = product()

    @pl.when((k > 0) & (k < last))
    def _():
        acc_ref[...] += product()

    @pl.when(k == last)
    def _():
        hn = h_ref[...] + (acc_ref[...] + product())
        ho_ref[...] = hn
        if emit_norm:
            ms = jnp.mean(hn * hn, axis=-1, keepdims=True)
            hn_ref[...] = (hn * lax.rsqrt(ms + EPS) * g_ref[...]).astype(hn_ref.dtype)


def _down(a, w3, layer, h, g, bm, bk):
    m = a.shape[0]
    assert FF_HIDDEN // bk >= 2
    emit_norm = g is not None
    row_block = pl.BlockSpec((bm, D_MODEL), lambda i, k: (i, 0))
    in_specs = [pl.BlockSpec((bm, bk), lambda i, k: (i, k)),
                pl.BlockSpec((None, bk, D_MODEL), lambda i, k: (layer, k, 0)),
                row_block]
    operands = [a, w3, h]
    out_specs = [row_block]
    out_shape = [jax.ShapeDtypeStruct((m, D_MODEL), F32)]
    if emit_norm:
        in_specs.append(pl.BlockSpec((1, D_MODEL), lambda i, k: (0, 0)))
        operands.append(g.reshape(1, -1))
        out_specs.append(row_block)
        out_shape.append(jax.ShapeDtypeStruct((m, D_MODEL), BF16))
    return pl.pallas_call(
        functools.partial(_down_kernel, emit_norm=emit_norm),
        grid=(m // bm, FF_HIDDEN // bk),
        in_specs=in_specs,
        out_specs=out_specs,
        out_shape=out_shape,
        scratch_shapes=[pltpu.VMEM((bm, D_MODEL), F32)],
        compiler_params=_params("parallel", "arbitrary"),
        name="down_proj",
    )(*operands)


def _rope_layout(t):
    zeros = jnp.zeros(t.shape[:-1] + (32,), t.dtype)
    return jnp.concatenate([t[..., :32], zeros, t[..., 32:], zeros], axis=-1)


def _head_layout(t):
    return jnp.concatenate([t[..., :MLA_NOPE], _rope_layout(t[..., MLA_NOPE:])], axis=-1)


def _rope_tables():
    length = N_META + SEQ
    inv = 1.0 / (ROPE_THETA ** (jnp.arange(0, MLA_ROPE, 2, dtype=F32) / MLA_ROPE))
    ang = jnp.arange(length, dtype=F32)[:, None] * inv[None, :]
    cos, sin = jnp.cos(ang), jnp.sin(ang)
    zeros = jnp.zeros_like(cos)
    ctab = jnp.concatenate([cos, zeros, cos, zeros], axis=-1)
    stab = jnp.concatenate([-sin, zeros, sin, zeros], axis=-1)
    return ctab, stab, cos.T, sin.T


def kernel(x, meta_tokens, norm1_g, w_in, gla_gate_w2, gla_gate_b, gla_onorm_g, w_branch_a,
           q_a_norm_g, w_uq, kv_a_norm_g, w_ukv, q_norm_g, k_norm_g, w_branch_b, w_out,
           norm2_g, w_gate_up, w_down):
    ctab, stab, cos_t, sin_t = _rope_tables()
    ctab_m, stab_m = ctab[:N_META], stab[:N_META]
    ctab_r, stab_r = ctab[N_META:], stab[N_META:]
    cos_tm, sin_tm = cos_t[:, :N_META], sin_t[:, :N_META]
    cos_tr, sin_tr = cos_t[:, N_META:], sin_t[:, N_META:]

    h_res = x.reshape(ROWS, D_MODEL)
    hm_res = meta_tokens.astype(F32)
    h = _rms(h_res, norm1_g[0], 512)
    hm = _rms(hm_res, norm1_g[0], N_META)
    s_zero = jnp.zeros((GLA_HEADS, GLA_DK, GLA_DV), F32)

    w_in_t = jnp.swapaxes(w_in, 1, 2)
    w_kr = w_in_t[:, 7184:7248]
    zrow = jnp.zeros((DEPTH, 32, D_MODEL), F32)
    w_small = jnp.concatenate(
        [w_in_t[:, 6144:6160], jnp.zeros((DEPTH, 128 - GLA_GATE_RANK, D_MODEL), F32),
         w_kr[:, :32], zrow, w_kr[:, 32:], zrow], axis=1)
    wa = w_branch_a.astype(BF16)
    wb = w_branch_b.astype(BF16)
    wo = w_out.astype(BF16)
    wd = w_down.astype(BF16)

    for l in range(DEPTH):
        last = l == DEPTH - 1
        w2p = jnp.concatenate(
            [gla_gate_w2[l], jnp.zeros((128 - GLA_GATE_RANK, GLA_HEADS * GLA_DK), F32)],
            axis=0).astype(BF16)
        wqt = _head_layout(w_uq[l].reshape(MLA_LORA, MLA_HEADS, MLA_QK)).reshape(
            MLA_LORA, MLA_HEADS * MLA_HEAD_PAD).T.astype(BF16)
        wkv3 = w_ukv[l].reshape(MLA_LORA, MLA_HEADS, MLA_NOPE + MLA_V)
        wk = wkv3[:, :, :MLA_NOPE].reshape(MLA_LORA, MLA_HEADS * MLA_NOPE).astype(BF16)
        wvt = wkv3[:, :, MLA_NOPE:].reshape(MLA_LORA, MLA_HEADS * MLA_V).T.astype(BF16)
        qg = (_head_layout(q_norm_g[l]) * (MLA_QK ** -0.5 * LOG2E)).reshape(MLA_HEAD_PAD, 1)
        kg = _head_layout(k_norm_g[l]).reshape(1, MLA_HEAD_PAD)

        z, zm = _in_proj(h, hm, w_in_t, l, 1024, 1024, name="in_proj")
        zs = _matmul_nt(h, w_small, l, 1024, N_SMALL, name="in_proj_small")
        zsm = _matmul_nt(hm, w_small, l, N_META, N_SMALL, name="in_proj_small_meta")

        gla_m, s_meta = _gla(zm, zsm, w2p, gla_gate_b[l], gla_onorm_g[l], s_zero,
                             batch=1, tokens=N_META, block=N_META, chunk=N_META, emit_state=True)
        (gla_r,) = _gla(z, zs, w2p, gla_gate_b[l], gla_onorm_g[l], s_meta,
                        batch=BATCH, tokens=SEQ, block=512, chunk=GLA_CHUNK, emit_state=False)

        km, vmt = _kvproj(zm, zsm, kv_a_norm_g[l], wk, wvt, kg, ctab_m, stab_m, N_META)
        kr, vrt = _kvproj(z, zs, kv_a_norm_g[l], wk, wvt, kg, ctab_r, stab_r, 512)
        kmp = jnp.pad(km, ((0, 128 - N_META), (0, 0)))
        vmtp = jnp.pad(vmt, ((0, 0), (0, 128 - N_META)))
        qr = _qproj(z, q_a_norm_g[l], wqt, qg, cos_tr, sin_tr, 512)
        att_r = _attention(qr, kr, vrt, km, vmt, 512, 4, 512, 4)

        merged = _merge(gla_r, att_r, wa, wb, l, z, 1024, 512)
        h_res, h2 = _outproj(merged, wo, l, h_res, norm2_g[l], 512)
        if last:
            (act,) = _gateup(h2, None, w_gate_up, l, 1024, 512)
            (h_res,) = _down(act, wd, l, h_res, None, 512, 2816)
        else:
            qm = _qproj(zm, q_a_norm_g[l], wqt, qg, cos_tm, sin_tm, N_META)
            att_m = _meta_attention(qm.T, kmp, vmtp)
            merged_m = _merge(gla_m, att_m, wa, wb, l, zm, N_META, 512)
            hm_res, hm2 = _outproj(merged_m, wo, l, hm_res, norm2_g[l], N_META)
            act, act_m = _gateup(h2, hm2, w_gate_up, l, 1024, 512)
            h_res, h = _down(act, wd, l, h_res, norm1_g[l + 1], 512, 2816)
            hm_res, hm = _down(act_m, wd, l, hm_res, norm1_g[l + 1], N_META, 1408)

    return h_res.reshape(BATCH, SEQ, D_MODEL)
```

```python
import functools

import jax
import jax.numpy as jnp
from jax import lax
from jax.experimental import pallas as pl
from jax.experimental.pallas import tpu as pltpu

D_MODEL = 2048
BATCH = 4
SEQ = 4096
DEPTH = 2
N_META = 16
ROWS = BATCH * SEQ

GLA_HEADS = 4
GLA_DK = 256
GLA_DV = 512
GLA_GATE_RANK = 16
GLA_GATE_TAU = 16.0
GLA_CHUNK = 64

MLA_HEADS = 16
MLA_LORA = 512
MLA_NOPE = 128
MLA_ROPE = 64
MLA_QK = 192
MLA_V = 128
VT_ROWS = 144
MLA_HEAD_PAD = 256
ROPE_THETA = 10000.0
FF_HIDDEN = 5632
EPS = 1e-6
LOG2E = 1.4426950408889634

W_IN_ZA = 6144
W_IN_C, W_IN_C_END = 6160, 7184
W_IN_G = 7248
Z_Q, Z_K, Z_V, Z_R, Z_CQ, Z_CKV, Z_A, Z_B = 0, 1024, 2048, 4096, 6144, 6656, 7168, 9216
N_Z = 11264
N_SMALL = 256

VMEM_LIMIT = 56 * 1024 * 1024
BF16 = jnp.bfloat16
F32 = jnp.float32

_NT = (((1,), (1,)), ((), ()))
_TN = (((0,), (0,)), ((), ()))


def _params(*sem):
    return pltpu.CompilerParams(dimension_semantics=sem, vmem_limit_bytes=VMEM_LIMIT)


def _sigmoid(x):
    return 1.0 / (1.0 + jnp.exp(-x))


def _rms_kernel(x_ref, g_ref, o_ref):
    x = x_ref[...]
    ms = jnp.mean(x * x, axis=-1, keepdims=True)
    o_ref[...] = (x * lax.rsqrt(ms + EPS) * g_ref[...]).astype(o_ref.dtype)


def _rms(x, g, bm):
    m, d = x.shape
    return pl.pallas_call(
        _rms_kernel,
        grid=(m // bm,),
        in_specs=[pl.BlockSpec((bm, d), lambda i: (i, 0)),
                  pl.BlockSpec((1, d), lambda i: (0, 0))],
        out_specs=pl.BlockSpec((bm, d), lambda i: (i, 0)),
        out_shape=jax.ShapeDtypeStruct((m, d), BF16),
        compiler_params=_params("parallel"),
        name="rmsnorm",
    )(x, g.reshape(1, d))


def _mm_nt_kernel(a_ref, wt_ref, o_ref):
    o_ref[...] = lax.dot_general(a_ref[...], wt_ref[...].astype(BF16), _NT,
                                 preferred_element_type=F32)


def _matmul_nt(a, wt3, layer, bm, bn, name):
    m, k = a.shape
    n = wt3.shape[1]
    return pl.pallas_call(
        _mm_nt_kernel,
        grid=(m // bm, n // bn),
        in_specs=[pl.BlockSpec((bm, k), lambda i, j: (i, 0)),
                  pl.BlockSpec((None, bn, k), lambda i, j: (layer, j, 0))],
        out_specs=pl.BlockSpec((bm, bn), lambda i, j: (i, j)),
        out_shape=jax.ShapeDtypeStruct((m, n), F32),
        compiler_params=_params("parallel", "parallel"),
        name=name,
    )(a, wt3)


def _mm_nt_wstat_kernel(a_ref, am_ref, wt_ref, o_ref, om_ref, wb_ref):
    @pl.when(pl.program_id(1) == 0)
    def _():
        wb_ref[...] = wt_ref[0].astype(BF16)
        om_ref[...] = lax.dot_general(am_ref[...], wb_ref[...], _NT, preferred_element_type=F32)

    o_ref[...] = lax.dot_general(a_ref[...], wb_ref[...], _NT, preferred_element_type=F32)


def _in_proj_row0(j, bn):
    n_a = W_IN_ZA // bn
    n_c = (W_IN_C_END - W_IN_C) // bn
    return jnp.where(j < n_a, j * bn,
                     jnp.where(j < n_a + n_c, W_IN_C + (j - n_a) * bn,
                               W_IN_G + (j - n_a - n_c) * bn))


def _in_proj(a, a_meta, wt3, layer, bm, bn, name):
    m, k = a.shape
    return pl.pallas_call(
        _mm_nt_wstat_kernel,
        grid=(N_Z // bn, m // bm),
        in_specs=[pl.BlockSpec((bm, k), lambda j, i: (i, 0)),
                  pl.BlockSpec((N_META, k), lambda j, i: (0, 0)),
                  pl.BlockSpec((pl.Element(1), pl.Element(bn), pl.Element(k)),
                               lambda j, i: (layer, pl.multiple_of(_in_proj_row0(j, bn), 16), 0))],
        out_specs=[pl.BlockSpec((bm, bn), lambda j, i: (i, j)),
                   pl.BlockSpec((N_META, bn), lambda j, i: (0, j))],
        out_shape=[jax.ShapeDtypeStruct((m, N_Z), F32),
                   jax.ShapeDtypeStruct((N_META, N_Z), F32)],
        scratch_shapes=[pltpu.VMEM((bn, k), BF16)],
        compiler_params=_params("arbitrary", "arbitrary"),
        name=name,
    )(a, a_meta, wt3)


def _gla_kernel(q_ref, k_ref, v_ref, r_ref, zlr_ref, w2_ref, gb_ref, og_ref, s0_ref,
                o_ref, *rest, chunk, n_chunks, emit_state):
    if emit_state:
        sfin_ref, st_ref, b_sc = rest
    else:
        st_ref, b_sc = rest
    t = pl.program_id(1)
    heads = range(GLA_HEADS)

    @pl.when(t == 0)
    def _():
        st_ref[...] = s0_ref[...]

    row = lax.broadcasted_iota(jnp.int32, (chunk, chunk), 0)
    col = lax.broadcasted_iota(jnp.int32, (chunk, chunk), 1)
    causal = col <= row
    tri = jnp.where(causal, 1.0, 0.0).astype(BF16)
    w2 = w2_ref[...]
    gb = gb_ref[...]
    og = og_ref[...]
    kdim = GLA_HEADS * GLA_DK

    def log_gate_split(c):
        sl = pl.ds(pl.multiple_of(c * chunk, chunk), chunk)
        logit = jnp.dot(zlr_ref[sl, :].astype(BF16), w2, preferred_element_type=F32) + gb
        g = (jnp.minimum(logit, 0.0) - jnp.log1p(jnp.exp(-jnp.abs(logit)))) * (1.0 / GLA_GATE_TAU)
        g_hi = g.astype(BF16)
        g_lo = (g - g_hi.astype(F32)).astype(BF16)
        return jnp.concatenate([g_hi, g_lo], axis=1)

    def log_decay(g_split):
        cs = jnp.dot(tri, g_split, preferred_element_type=F32)
        return cs[:, :kdim] + cs[:, kdim:]

    b_sc[...] = log_decay(log_gate_split(0))

    def body(c, carry):
        sl = pl.ds(pl.multiple_of(c * chunk, chunk), chunk)
        g_split_next = log_gate_split(jnp.minimum(c + 1, n_chunks - 1))
        r = r_ref[sl, :]
        out_gate = r * _sigmoid(r)
        st = [st_ref[h] for h in heads]
        st_bf = [st[h].astype(BF16) for h in heads]
        b = b_sc[...]
        b_last = b[chunk - 1:chunk, :]
        eb = jnp.exp(b)
        enb = jnp.exp(-b)
        erel = jnp.exp(b_last - b)
        elast = jnp.exp(b_last)
        k = k_ref[sl, :]
        qd = (q_ref[sl, :] * (GLA_DK ** -0.5) * eb).astype(BF16)
        kd = (k * enb).astype(BF16)
        k2 = (k * erel).astype(BF16)
        v = v_ref[sl, :].astype(BF16)

        def hk(x, h):
            return x[:, h * GLA_DK:(h + 1) * GLA_DK]

        def hv(x, h):
            return x[:, h * GLA_DV:(h + 1) * GLA_DV]

        a = [lax.dot_general(hk(qd, h), hk(kd, h), _NT, preferred_element_type=F32) for h in heads]
        o_state = [jnp.dot(hk(qd, h), st_bf[h], preferred_element_type=F32) for h in heads]
        o_local = [jnp.dot(jnp.where(causal, a[h], 0.0).astype(BF16), hv(v, h),
                           preferred_element_type=F32) for h in heads]
        b_sc[...] = log_decay(g_split_next)
        upd = [lax.dot_general(hk(k2, h), hv(v, h), _TN, preferred_element_type=F32)
               for h in heads]
        decay_col = jnp.broadcast_to(elast, (128, kdim)).T
        for h in heads:
            col = decay_col[h * GLA_DK:(h + 1) * GLA_DK]
            st_ref[h] = st[h] * jnp.concatenate([col] * (GLA_DV // 128), axis=1) + upd[h]
        for h in heads:
            o = o_local[h] + o_state[h]
            ms = jnp.mean(o * o, axis=-1, keepdims=True)
            on = o * lax.rsqrt(ms + EPS) * og
            o_ref[sl, h * GLA_DV:(h + 1) * GLA_DV] = (on * hv(out_gate, h)).astype(o_ref.dtype)
        return carry

    lax.fori_loop(0, n_chunks, body, 0)

    if emit_state:
        @pl.when(t == pl.num_programs(1) - 1)
        def _():
            sfin_ref[...] = st_ref[...]


def _gla(z, zs, w2p, gate_b, onorm_g, s0, *, batch, tokens, block, chunk, emit_state):
    nt = tokens // block
    rows = batch * tokens
    kern = functools.partial(_gla_kernel, chunk=chunk, n_chunks=block // chunk,
                             emit_state=emit_state)
    kdim = GLA_HEADS * GLA_DK
    vdim = GLA_HEADS * GLA_DV
    in_specs = [
        pl.BlockSpec((block, kdim), lambda b, t: (b * nt + t, Z_Q // kdim)),
        pl.BlockSpec((block, kdim), lambda b, t: (b * nt + t, Z_K // kdim)),
        pl.BlockSpec((block, vdim), lambda b, t: (b * nt + t, Z_V // vdim)),
        pl.BlockSpec((block, vdim), lambda b, t: (b * nt + t, Z_R // vdim)),
        pl.BlockSpec((block, 128), lambda b, t: (b * nt + t, 0)),
        pl.BlockSpec((128, kdim), lambda b, t: (0, 0)),
        pl.BlockSpec((1, kdim), lambda b, t: (0, 0)),
        pl.BlockSpec((1, GLA_DV), lambda b, t: (0, 0)),
        pl.BlockSpec((GLA_HEADS, GLA_DK, GLA_DV), lambda b, t: (0, 0, 0)),
    ]
    out_specs = [pl.BlockSpec((block, vdim), lambda b, t: (b * nt + t, 0))]
    out_shape = [jax.ShapeDtypeStruct((rows, vdim), BF16)]
    if emit_state:
        out_specs.append(pl.BlockSpec((GLA_HEADS, GLA_DK, GLA_DV), lambda b, t: (b, 0, 0)))
        out_shape.append(jax.ShapeDtypeStruct((batch * GLA_HEADS, GLA_DK, GLA_DV), F32))
    res = pl.pallas_call(
        kern,
        grid=(batch, nt),
        in_specs=in_specs,
        out_specs=out_specs,
        out_shape=out_shape,
        scratch_shapes=[pltpu.VMEM((GLA_HEADS, GLA_DK, GLA_DV), F32),
                        pltpu.VMEM((chunk, kdim), F32)],
        compiler_params=_params("parallel", "arbitrary"),
        name="gla",
    )(z, z, z, z, zs, w2p, gate_b.reshape(1, -1), onorm_g.reshape(1, -1), s0)
    return res


def _rope(x, c, s):
    return x * c + pltpu.roll(x, 64, 1) * s


def _qproj_kernel(zc_ref, ng_ref, wt_ref, hg_ref, c_ref, s_ref, o_ref, *, heads_per_dot):
    x = zc_ref[...]
    ms = jnp.mean(x * x, axis=-1, keepdims=True)
    xn = (x * lax.rsqrt(ms + EPS) * ng_ref[...]).astype(BF16)
    bm = x.shape[0]
    c = c_ref[...]
    s = s_ref[...]
    hg = jnp.broadcast_to(hg_ref[...], (MLA_HEAD_PAD, bm))
    for hc in range(MLA_HEADS // heads_per_dot):
        base = hc * heads_per_dot * MLA_HEAD_PAD
        yt = lax.dot_general(wt_ref[base:base + heads_per_dot * MLA_HEAD_PAD, :], xn, _NT,
                             preferred_element_type=F32)
        for j in range(heads_per_dot):
            lo = base + j * MLA_HEAD_PAD
            y = yt[j * MLA_HEAD_PAD:(j + 1) * MLA_HEAD_PAD]
            ms = jnp.sum(y * y, axis=0, keepdims=True) * (1.0 / MLA_QK)
            yn = y * lax.rsqrt(ms + EPS) * hg
            x1 = yn[128:160]
            x2 = yn[192:224]
            o_ref[lo:lo + 128, :] = yn[:128].astype(o_ref.dtype)
            o_ref[lo + 128:lo + 160, :] = (x1 * c - x2 * s).astype(o_ref.dtype)
            o_ref[lo + 160:lo + 192, :] = yn[160:192].astype(o_ref.dtype)
            o_ref[lo + 192:lo + 224, :] = (x2 * c + x1 * s).astype(o_ref.dtype)
            o_ref[lo + 224:lo + 256, :] = yn[224:256].astype(o_ref.dtype)


def _qproj(z, ng, wt, hg_col, cos_t, sin_t, bm):
    m = z.shape[0]
    nt = cos_t.shape[1] // bm
    kern = functools.partial(_qproj_kernel, heads_per_dot=1)
    return pl.pallas_call(
        kern,
        grid=(m // bm,),
        in_specs=[pl.BlockSpec((bm, MLA_LORA), lambda i: (i, Z_CQ // MLA_LORA)),
                  pl.BlockSpec((1, MLA_LORA), lambda i: (0, 0)),
                  pl.BlockSpec((MLA_HEADS * MLA_HEAD_PAD, MLA_LORA), lambda i: (0, 0)),
                  pl.BlockSpec((MLA_HEAD_PAD, 1), lambda i: (0, 0)),
                  pl.BlockSpec((MLA_ROPE // 2, bm), lambda i: (0, i % nt)),
                  pl.BlockSpec((MLA_ROPE // 2, bm), lambda i: (0, i % nt))],
        out_specs=pl.BlockSpec((MLA_HEADS * MLA_HEAD_PAD, bm), lambda i: (0, i)),
        out_shape=jax.ShapeDtypeStruct((MLA_HEADS * MLA_HEAD_PAD, m), BF16),
        compiler_params=_params("parallel"),
        name="q_proj",
    )(z, ng.reshape(1, -1), wt, hg_col, cos_t, sin_t)


def _kvproj_kernel(zc_ref, kr_ref, ng_ref, wk_ref, wvt_ref, hg_ref, c_ref, s_ref, k_ref, vt_ref):
    x = zc_ref[...]
    ms = jnp.mean(x * x, axis=-1, keepdims=True)
    xn = (x * lax.rsqrt(ms + EPS) * ng_ref[...]).astype(BF16)
    c = c_ref[...]
    s = s_ref[...]
    hg = hg_ref[...]
    kr = kr_ref[...]
    kr_ss = jnp.sum(kr * kr, axis=-1, keepdims=True)
    kr_rot = _rope(kr * hg[:, 128:], c, s)
    for hp in range(MLA_HEADS // 2):
        y = jnp.dot(xn, wk_ref[:, hp * 256:(hp + 1) * 256], preferred_element_type=F32)
        for sub in range(2):
            lo = (2 * hp + sub) * MLA_HEAD_PAD
            kn = y[:, sub * MLA_NOPE:(sub + 1) * MLA_NOPE]
            ms = (jnp.sum(kn * kn, axis=-1, keepdims=True) + kr_ss) * (1.0 / MLA_QK)
            rs = lax.rsqrt(ms + EPS)
            k_ref[:, lo:lo + 128] = (kn * rs * hg[:, :128]).astype(k_ref.dtype)
            k_ref[:, lo + 128:lo + 256] = (kr_rot * rs).astype(k_ref.dtype)
    rows_per_dot = 4 * VT_ROWS
    for j in range(MLA_HEADS * VT_ROWS // rows_per_dot):
        sl = slice(j * rows_per_dot, (j + 1) * rows_per_dot)
        vt_ref[sl, :] = lax.dot_general(wvt_ref[sl, :], xn, _NT,
                                        preferred_element_type=F32).astype(vt_ref.dtype)
    ones = jnp.ones((VT_ROWS - MLA_V, x.shape[0]), vt_ref.dtype)
    for h in range(MLA_HEADS):
        vt_ref[h * VT_ROWS + MLA_V:(h + 1) * VT_ROWS, :] = ones


def _kvproj(z, zs, ng, wk, wvt, hg, ctab, stab, bm):
    m = z.shape[0]
    nt = ctab.shape[0] // bm
    return pl.pallas_call(
        _kvproj_kernel,
        grid=(m // bm,),
        in_specs=[pl.BlockSpec((bm, MLA_LORA), lambda i: (i, Z_CKV // MLA_LORA)),
                  pl.BlockSpec((bm, 128), lambda i: (i, 1)),
                  pl.BlockSpec((1, MLA_LORA), lambda i: (0, 0)),
                  pl.BlockSpec((MLA_LORA, MLA_HEADS * MLA_NOPE), lambda i: (0, 0)),
                  pl.BlockSpec((MLA_HEADS * VT_ROWS, MLA_LORA), lambda i: (0, 0)),
                  pl.BlockSpec((1, MLA_HEAD_PAD), lambda i: (0, 0)),
                  pl.BlockSpec((bm, 128), lambda i: (i % nt, 0)),
                  pl.BlockSpec((bm, 128), lambda i: (i % nt, 0))],
        out_specs=[pl.BlockSpec((bm, MLA_HEADS * MLA_HEAD_PAD), lambda i: (i, 0)),
                   pl.BlockSpec((MLA_HEADS * VT_ROWS, bm), lambda i: (0, i))],
        out_shape=[jax.ShapeDtypeStruct((m, MLA_HEADS * MLA_HEAD_PAD), BF16),
                   jax.ShapeDtypeStruct((MLA_HEADS * VT_ROWS, m), BF16)],
        compiler_params=_params("parallel"),
        name="kv_proj",
    )(z, zs, ng.reshape(1, -1), wk, wvt, hg, ctab, stab)


def _attn_kernel(qt_ref, k_ref, vt_ref, km_ref, vmt_ref, o_ref, m_sc, acc_sc, s_sc, *, blk, heads,
                 qw, ahead, nq):
    i = pl.program_id(2)

    def queries(qi, h, lo, width):
        return qt_ref[h * MLA_HEAD_PAD:(h + 1) * MLA_HEAD_PAD,
                      pl.ds(pl.multiple_of(qi * blk + lo, width), width)]

    for h in range(heads):
        s = jnp.dot(km_ref[:, h * MLA_HEAD_PAD:(h + 1) * MLA_HEAD_PAD], queries(i, h, 0, blk),
                    preferred_element_type=F32)
        m = jnp.max(s, axis=0, keepdims=True)
        p = jnp.exp2(s - m)
        m_sc[h] = m
        acc_sc[h] = jnp.dot(vmt_ref[h * VT_ROWS:(h + 1) * VT_ROWS, :], p.astype(BF16),
                            preferred_element_type=F32)

    tasks = [(h, c) for c in range(blk // qw) for h in range(heads)]

    def scores(qi, kb, h, c):
        ksl = pl.ds(pl.multiple_of(kb * blk, blk), blk)
        return jnp.dot(k_ref[ksl, h * MLA_HEAD_PAD:(h + 1) * MLA_HEAD_PAD],
                       queries(qi, h, c * qw, qw),
                       preferred_element_type=F32)

    @pl.when(i == 0)
    def _():
        for n in range(ahead):
            s_sc[n] = scores(0, 0, *tasks[n])

    def step(kb, masked):
        ksl = pl.ds(pl.multiple_of(kb * blk, blk), blk)
        pending = [s_sc[n] for n in range(ahead)]
        handover = []
        for n, (h, c) in enumerate(tasks):
            csl = slice(c * qw, (c + 1) * qw)
            s = pending.pop(0)
            if n + ahead < len(tasks):
                pending.append(scores(i, kb, *tasks[n + ahead]))
            elif not masked:
                handover.append(scores(i, kb + 1, *tasks[n + ahead - len(tasks)]))
            else:
                handover.append(scores(jnp.minimum(i + 1, nq - 1), 0,
                                       *tasks[n + ahead - len(tasks)]))
            if masked:
                key = lax.broadcasted_iota(jnp.int32, s.shape, 0)
                qry = lax.broadcasted_iota(jnp.int32, s.shape, 1) + c * qw
                s = jnp.where(key <= qry, s, -1e30)
            m_prev = m_sc[h, :, csl]
            m_new = jnp.maximum(m_prev, jnp.max(s, axis=0, keepdims=True))
            alpha = jnp.exp2(m_prev - m_new)
            p = jnp.exp2(s - m_new)
            acc_sc[h, :, csl] = alpha * acc_sc[h, :, csl] + jnp.dot(
                vt_ref[h * VT_ROWS:(h + 1) * VT_ROWS, ksl], p.astype(BF16),
                preferred_element_type=F32)
            m_sc[h, :, csl] = m_new
        for n, s_next in enumerate(handover):
            s_sc[n] = s_next

    def body(kb, carry):
        step(kb, False)
        return carry

    lax.fori_loop(0, i, body, 0)
    step(i, True)
    for h in range(heads):
        acc = acc_sc[h]
        o_ref[:, h * MLA_V:(h + 1) * MLA_V] = (
            acc[:MLA_V] * (1.0 / acc[MLA_V:MLA_V + 1])).T.astype(o_ref.dtype)


def _attention(q, k, vt, km, vmt, blk, heads, qw, ahead):
    nq = SEQ // blk
    kern = functools.partial(_attn_kernel, blk=blk, heads=heads, qw=qw, ahead=ahead, nq=nq)
    return pl.pallas_call(
        kern,
        grid=(BATCH, MLA_HEADS // heads, nq),
        in_specs=[pl.BlockSpec((heads * MLA_HEAD_PAD, SEQ), lambda b, h, i: (h, b)),
                  pl.BlockSpec((SEQ, heads * MLA_HEAD_PAD), lambda b, h, i: (b, h)),
                  pl.BlockSpec((heads * VT_ROWS, SEQ), lambda b, h, i: (h, b)),
                  pl.BlockSpec((N_META, heads * MLA_HEAD_PAD), lambda b, h, i: (0, h)),
                  pl.BlockSpec((heads * VT_ROWS, N_META), lambda b, h, i: (h, 0))],
        out_specs=pl.BlockSpec((blk, heads * MLA_V), lambda b, h, i: (b * nq + i, h)),
        out_shape=jax.ShapeDtypeStruct((ROWS, MLA_HEADS * MLA_V), BF16),
        scratch_shapes=[pltpu.VMEM((heads, 1, blk), F32),
                        pltpu.VMEM((heads, VT_ROWS, blk), F32), pltpu.VMEM((ahead, blk, qw), F32)],
        compiler_params=_params("parallel", "parallel", "arbitrary"),
        name="mla_attention",
    )(q, k, vt, km, vmt)


def _meta_attn_kernel(q_ref, km_ref, vmt_ref, o_ref):
    s = lax.dot_general(q_ref[...], km_ref[...], _NT, preferred_element_type=F32)
    row = lax.broadcasted_iota(jnp.int32, s.shape, 0)
    col = lax.broadcasted_iota(jnp.int32, s.shape, 1)
    s = jnp.where(col <= row, s, -1e30)
    m = jnp.max(s, axis=-1, keepdims=True)
    p = jnp.exp2(s - m)
    l = jnp.sum(p, axis=-1, keepdims=True)
    o = lax.dot_general(p.astype(BF16), vmt_ref[:MLA_V, :], _NT, preferred_element_type=F32)
    o_ref[...] = (o / l).astype(o_ref.dtype)


def _meta_attention(qm, kmp, vmtp):
    return pl.pallas_call(
        _meta_attn_kernel,
        grid=(MLA_HEADS,),
        in_specs=[pl.BlockSpec((N_META, MLA_HEAD_PAD), lambda h: (0, h)),
                  pl.BlockSpec((128, MLA_HEAD_PAD), lambda h: (0, h)),
                  pl.BlockSpec((VT_ROWS, 128), lambda h: (h, 0))],
        out_specs=pl.BlockSpec((N_META, MLA_V), lambda h: (0, h)),
        out_shape=jax.ShapeDtypeStruct((N_META, MLA_HEADS * MLA_V), BF16),
        compiler_params=_params("parallel"),
        name="meta_attention",
    )(qm, kmp, vmtp)


def _merge_kernel(a_ref, b_ref, wa_ref, wb_ref, za_ref, zb_ref, o_ref):
    ga = _sigmoid(za_ref[...])
    gb = _sigmoid(zb_ref[...])
    ya = jnp.dot(a_ref[...], wa_ref[...], preferred_element_type=F32)
    yb = jnp.dot(b_ref[...], wb_ref[...], preferred_element_type=F32)
    o_ref[...] = (ga * ya + gb * yb).astype(o_ref.dtype)


def _merge(a, b, wa3, wb3, layer, z, bm, bn):
    m = a.shape[0]
    return pl.pallas_call(
        _merge_kernel,
        grid=(m // bm, D_MODEL // bn),
        in_specs=[pl.BlockSpec((bm, D_MODEL), lambda i, j: (i, 0)),
                  pl.BlockSpec((bm, D_MODEL), lambda i, j: (i, 0)),
                  pl.BlockSpec((None, D_MODEL, bn), lambda i, j: (layer, 0, j)),
                  pl.BlockSpec((None, D_MODEL, bn), lambda i, j: (layer, 0, j)),
                  pl.BlockSpec((bm, bn), lambda i, j: (i, Z_A // bn + j)),
                  pl.BlockSpec((bm, bn), lambda i, j: (i, Z_B // bn + j))],
        out_specs=pl.BlockSpec((bm, bn), lambda i, j: (i, j)),
        out_shape=jax.ShapeDtypeStruct((m, D_MODEL), BF16),
        compiler_params=_params("parallel", "parallel"),
        name="branch_merge",
    )(a, b, wa3, wb3, z, z)


def _outproj_kernel(m_ref, w_ref, h_ref, g_ref, ho_ref, hn_ref):
    hn = h_ref[...] + jnp.dot(m_ref[...], w_ref[...], preferred_element_type=F32)
    ho_ref[...] = hn
    ms = jnp.mean(hn * hn, axis=-1, keepdims=True)
    hn_ref[...] = (hn * lax.rsqrt(ms + EPS) * g_ref[...]).astype(hn_ref.dtype)


def _outproj(mg, w3, layer, h, g, bm):
    m = mg.shape[0]
    return pl.pallas_call(
        _outproj_kernel,
        grid=(m // bm,),
        in_specs=[pl.BlockSpec((bm, D_MODEL), lambda i: (i, 0)),
                  pl.BlockSpec((None, D_MODEL, D_MODEL), lambda i: (layer, 0, 0)),
                  pl.BlockSpec((bm, D_MODEL), lambda i: (i, 0)),
                  pl.BlockSpec((1, D_MODEL), lambda i: (0, 0))],
        out_specs=[pl.BlockSpec((bm, D_MODEL), lambda i: (i, 0)),
                   pl.BlockSpec((bm, D_MODEL), lambda i: (i, 0))],
        out_shape=[jax.ShapeDtypeStruct((m, D_MODEL), F32),
                   jax.ShapeDtypeStruct((m, D_MODEL), BF16)],
        compiler_params=_params("parallel"),
        name="out_proj",
    )(mg, w3, h, g.reshape(1, -1))


def _gateup_kernel(*refs, with_meta):
    if with_meta:
        h_ref, hm_ref, wg_ref, wu_ref, o_ref, om_ref, wgb_ref, wub_ref = refs
    else:
        h_ref, wg_ref, wu_ref, o_ref, wgb_ref, wub_ref = refs

    def swiglu(h):
        g = jnp.dot(h, wgb_ref[...], preferred_element_type=F32)
        u = jnp.dot(h, wub_ref[...], preferred_element_type=F32)
        return (g * _sigmoid(g) * u).astype(o_ref.dtype)

    @pl.when(pl.program_id(1) == 0)
    def _():
        wgb_ref[...] = wg_ref[...].astype(BF16)
        wub_ref[...] = wu_ref[...].astype(BF16)
        if with_meta:
            om_ref[...] = swiglu(hm_ref[...])

    o_ref[...] = swiglu(h_ref[...])


def _gateup(h, h_meta, w3, layer, bm, bn):
    m = h.shape[0]
    nb = FF_HIDDEN // bn
    with_meta = h_meta is not None
    weight_specs = [pl.BlockSpec((None, D_MODEL, bn), lambda j, i: (layer, 0, j)),
                    pl.BlockSpec((None, D_MODEL, bn), lambda j, i: (layer, 0, nb + j))]
    in_specs = [pl.BlockSpec((bm, D_MODEL), lambda j, i: (i, 0))]
    operands = [h]
    out_specs = [pl.BlockSpec((bm, bn), lambda j, i: (i, j))]
    out_shape = [jax.ShapeDtypeStruct((m, FF_HIDDEN), BF16)]
    if with_meta:
        in_specs.append(pl.BlockSpec((N_META, D_MODEL), lambda j, i: (0, 0)))
        operands.append(h_meta)
        out_specs.append(pl.BlockSpec((N_META, bn), lambda j, i: (0, j)))
        out_shape.append(jax.ShapeDtypeStruct((N_META, FF_HIDDEN), BF16))
    return pl.pallas_call(
        functools.partial(_gateup_kernel, with_meta=with_meta),
        grid=(nb, m // bm),
        in_specs=in_specs + weight_specs,
        out_specs=out_specs,
        out_shape=out_shape,
        scratch_shapes=[pltpu.VMEM((D_MODEL, bn), BF16), pltpu.VMEM((D_MODEL, bn), BF16)],
        compiler_params=_params("arbitrary", "arbitrary"),
        name="gate_up",
    )(*operands, w3, w3)


def _down_kernel(a_ref, w_ref, h_ref, *rest, emit_norm):
    if emit_norm:
        g_ref, ho_ref, hn_ref, acc_ref = rest
    else:
        ho_ref, acc_ref = rest
    k = pl.program_id(1)
    last = pl.num_programs(1) - 1

    def product():
        return jnp.dot(a_ref[...], w_ref[...], preferred_element_type=F32)

    @pl.when(k == 0)
    def _():
        acc_ref[...] = product()

    @pl.when((k > 0) & (k < last))
    def _():
        acc_ref[...] += product()

    @pl.when(k == last)
    def _():
        hn = h_ref[...] + (acc_ref[...] + product())
        ho_ref[...] = hn
        if emit_norm:
            ms = jnp.mean(hn * hn, axis=-1, keepdims=True)
            hn_ref[...] = (hn * lax.rsqrt(ms + EPS) * g_ref[...]).astype(hn_ref.dtype)


def _down(a, w3, layer, h, g, bm, bk):
    m = a.shape[0]
    assert FF_HIDDEN // bk >= 2
    emit_norm = g is not None
    row_block = pl.BlockSpec((bm, D_MODEL), lambda i, k: (i, 0))
    in_specs = [pl.BlockSpec((bm, bk), lambda i, k: (i, k)),
                pl.BlockSpec((None, bk, D_MODEL), lambda i, k: (layer, k, 0)),
                row_block]
    operands = [a, w3, h]
    out_specs = [row_block]
    out_shape = [jax.ShapeDtypeStruct((m, D_MODEL), F32)]
    if emit_norm:
        in_specs.append(pl.BlockSpec((1, D_MODEL), lambda i, k: (0, 0)))
        operands.append(g.reshape(1, -1))
        out_specs.append(row_block)
        out_shape.append(jax.ShapeDtypeStruct((m, D_MODEL), BF16))
    return pl.pallas_call(
        functools.partial(_down_kernel, emit_norm=emit_norm),
        grid=(m // bm, FF_HIDDEN // bk),
        in_specs=in_specs,
        out_specs=out_specs,
        out_shape=out_shape,
        scratch_shapes=[pltpu.VMEM((bm, D_MODEL), F32)],
        compiler_params=_params("parallel", "arbitrary"),
        name="down_proj",
    )(*operands)


def _rope_layout(t):
    zeros = jnp.zeros(t.shape[:-1] + (32,), t.dtype)
    return jnp.concatenate([t[..., :32], zeros, t[..., 32:], zeros], axis=-1)


def _head_layout(t):
    return jnp.concatenate([t[..., :MLA_NOPE], _rope_layout(t[..., MLA_NOPE:])], axis=-1)


def _rope_tables():
    length = N_META + SEQ
    inv = 1.0 / (ROPE_THETA ** (jnp.arange(0, MLA_ROPE, 2, dtype=F32) / MLA_ROPE))
    ang = jnp.arange(length, dtype=F32)[:, None] * inv[None, :]
    cos, sin = jnp.cos(ang), jnp.sin(ang)
    zeros = jnp.zeros_like(cos)
    ctab = jnp.concatenate([cos, zeros, cos, zeros], axis=-1)
    stab = jnp.concatenate([-sin, zeros, sin, zeros], axis=-1)
    return ctab, stab, cos.T, sin.T


def kernel(x, meta_tokens, norm1_g, w_in, gla_gate_w2, gla_gate_b, gla_onorm_g, w_branch_a,
           q_a_norm_g, w_uq, kv_a_norm_g, w_ukv, q_norm_g, k_norm_g, w_branch_b, w_out,
           norm2_g, w_gate_up, w_down):
    ctab, stab, cos_t, sin_t = _rope_tables()
    ctab_m, stab_m = ctab[:N_META], stab[:N_META]
    ctab_r, stab_r = ctab[N_META:], stab[N_META:]
    cos_tm, sin_tm = cos_t[:, :N_META], sin_t[:, :N_META]
    cos_tr, sin_tr = cos_t[:, N_META:], sin_t[:, N_META:]

    h_res = x.reshape(ROWS, D_MODEL)
    hm_res = meta_tokens.astype(F32)
    h = _rms(h_res, norm1_g[0], 512)
    hm = _rms(hm_res, norm1_g[0], N_META)
    s_zero = jnp.zeros((GLA_HEADS, GLA_DK, GLA_DV), F32)

    w_in_t = jnp.swapaxes(w_in, 1, 2)
    w_kr = w_in_t[:, 7184:7248]
    zrow = jnp.zeros((DEPTH, 32, D_MODEL), F32)
    w_small = jnp.concatenate(
        [w_in_t[:, 6144:6160], jnp.zeros((DEPTH, 128 - GLA_GATE_RANK, D_MODEL), F32),
         w_kr[:, :32], zrow, w_kr[:, 32:], zrow], axis=1)
    wa = w_branch_a.astype(BF16)
    wb = w_branch_b.astype(BF16)
    wo = w_out.astype(BF16)
    wd = w_down.astype(BF16)

    for l in range(DEPTH):
        last = l == DEPTH - 1
        w2p = jnp.concatenate(
            [gla_gate_w2[l], jnp.zeros((128 - GLA_GATE_RANK, GLA_HEADS * GLA_DK), F32)],
            axis=0).astype(BF16)
        wqt = _head_layout(w_uq[l].reshape(MLA_LORA, MLA_HEADS, MLA_QK)).reshape(
            MLA_LORA, MLA_HEADS * MLA_HEAD_PAD).T.astype(BF16)
        wkv3 = w_ukv[l].reshape(MLA_LORA, MLA_HEADS, MLA_NOPE + MLA_V)
        wk = wkv3[:, :, :MLA_NOPE].reshape(MLA_LORA, MLA_HEADS * MLA_NOPE).astype(BF16)
        wvt = jnp.pad(jnp.transpose(wkv3[:, :, MLA_NOPE:], (1, 2, 0)),
                      ((0, 0), (0, VT_ROWS - MLA_V), (0, 0))).reshape(
            MLA_HEADS * VT_ROWS, MLA_LORA).astype(BF16)
        qg = (_head_layout(q_norm_g[l]) * (MLA_QK ** -0.5 * LOG2E)).reshape(MLA_HEAD_PAD, 1)
        kg = _head_layout(k_norm_g[l]).reshape(1, MLA_HEAD_PAD)

        z, zm = _in_proj(h, hm, w_in_t, l, 1024, 1024, name="in_proj")
        zs = _matmul_nt(h, w_small, l, 1024, N_SMALL, name="in_proj_small")
        zsm = _matmul_nt(hm, w_small, l, N_META, N_SMALL, name="in_proj_small_meta")

        gla_m, s_meta = _gla(zm, zsm, w2p, gla_gate_b[l], gla_onorm_g[l], s_zero,
                             batch=1, tokens=N_META, block=N_META, chunk=N_META, emit_state=True)
        (gla_r,) = _gla(z, zs, w2p, gla_gate_b[l], gla_onorm_g[l], s_meta,
                        batch=BATCH, tokens=SEQ, block=512, chunk=GLA_CHUNK, emit_state=False)

        km, vmt = _kvproj(zm, zsm, kv_a_norm_g[l], wk, wvt, kg, ctab_m, stab_m, N_META)
        kr, vrt = _kvproj(z, zs, kv_a_norm_g[l], wk, wvt, kg, ctab_r, stab_r, 512)
        kmp = jnp.pad(km, ((0, 128 - N_META), (0, 0)))
        vmtp = jnp.pad(vmt, ((0, 0), (0, 128 - N_META)))
        qr = _qproj(z, q_a_norm_g[l], wqt, qg, cos_tr, sin_tr, 512)
        att_r = _attention(qr, kr, vrt, km, vmt, 512, 4, 512, 4)

        merged = _merge(gla_r, att_r, wa, wb, l, z, 1024, 512)
        h_res, h2 = _outproj(merged, wo, l, h_res, norm2_g[l], 512)
        if last:
            (act,) = _gateup(h2, None, w_gate_up, l, 1024, 512)
            (h_res,) = _down(act, wd, l, h_res, None, 512, 2816)
        else:
            qm = _qproj(zm, q_a_norm_g[l], wqt, qg, cos_tm, sin_tm, N_META)
            att_m = _meta_attention(qm.T, kmp, vmtp)
            merged_m = _merge(gla_m, att_m, wa, wb, l, zm, N_META, 512)
            hm_res, hm2 = _outproj(merged_m, wo, l, hm_res, norm2_g[l], N_META)
            act, act_m = _gateup(h2, hm2, w_gate_up, l, 1024, 512)
            h_res, h = _down(act, wd, l, h_res, norm1_g[l + 1], 512, 2816)
            hm_res, hm = _down(act_m, wd, l, hm_res, norm1_g[l + 1], N_META, 1408)

    return h_res.reshape(BATCH, SEQ, D_MODEL)
```

```python
import functools

import jax
import jax.numpy as jnp
from jax import lax
from jax.experimental import pallas as pl
from jax.experimental.pallas import tpu as pltpu

D_MODEL = 2048
BATCH = 4
SEQ = 4096
DEPTH = 2
N_META = 16
ROWS = BATCH * SEQ

GLA_HEADS = 4
GLA_DK = 256
GLA_DV = 512
GLA_GATE_RANK = 16
GLA_GATE_TAU = 16.0
GLA_CHUNK = 64

MLA_HEADS = 16
MLA_LORA = 512
MLA_NOPE = 128
MLA_ROPE = 64
MLA_QK = 192
MLA_V = 128
VT_ROWS = 144
MLA_HEAD_PAD = 256
ROPE_THETA = 10000.0
FF_HIDDEN = 5632
EPS = 1e-6
LOG2E = 1.4426950408889634

W_IN_ZA = 6144
W_IN_C, W_IN_C_END = 6160, 7184
W_IN_G = 7248
Z_Q, Z_K, Z_V, Z_R, Z_CQ, Z_CKV, Z_A, Z_B = 0, 1024, 2048, 4096, 6144, 6656, 7168, 9216
N_Z = 11264
N_SMALL = 256

VMEM_LIMIT = 56 * 1024 * 1024
BF16 = jnp.bfloat16
F32 = jnp.float32

_NT = (((1,), (1,)), ((), ()))
_TN = (((0,), (0,)), ((), ()))


def _params(*sem):
    return pltpu.CompilerParams(dimension_semantics=sem, vmem_limit_bytes=VMEM_LIMIT)


def _sigmoid(x):
    return 1.0 / (1.0 + jnp.exp(-x))


def _rms_kernel(x_ref, g_ref, o_ref):
    x = x_ref[...]
    ms = jnp.mean(x * x, axis=-1, keepdims=True)
    o_ref[...] = (x * lax.rsqrt(ms + EPS) * g_ref[...]).astype(o_ref.dtype)


def _rms(x, g, bm):
    m, d = x.shape
    return pl.pallas_call(
        _rms_kernel,
        grid=(m // bm,),
        in_specs=[pl.BlockSpec((bm, d), lambda i: (i, 0)),
                  pl.BlockSpec((1, d), lambda i: (0, 0))],
        out_specs=pl.BlockSpec((bm, d), lambda i: (i, 0)),
        out_shape=jax.ShapeDtypeStruct((m, d), BF16),
        compiler_params=_params("parallel"),
        name="rmsnorm",
    )(x, g.reshape(1, d))


def _mm_nt_kernel(a_ref, wt_ref, o_ref):
    o_ref[...] = lax.dot_general(a_ref[...], wt_ref[...].astype(BF16), _NT,
                                 preferred_element_type=F32)


def _matmul_nt(a, wt3, layer, bm, bn, name):
    m, k = a.shape
    n = wt3.shape[1]
    return pl.pallas_call(
        _mm_nt_kernel,
        grid=(m // bm, n // bn),
        in_specs=[pl.BlockSpec((bm, k), lambda i, j: (i, 0)),
                  pl.BlockSpec((None, bn, k), lambda i, j: (layer, j, 0))],
        out_specs=pl.BlockSpec((bm, bn), lambda i, j: (i, j)),
        out_shape=jax.ShapeDtypeStruct((m, n), F32),
        compiler_params=_params("parallel", "parallel"),
        name=name,
    )(a, wt3)


def _mm_nt_wstat_kernel(a_ref, am_ref, wt_ref, o_ref, om_ref, wb_ref):
    @pl.when(pl.program_id(1) == 0)
    def _():
        wb_ref[...] = wt_ref[0].astype(BF16)
        om_ref[...] = lax.dot_general(am_ref[...], wb_ref[...], _NT, preferred_element_type=F32)

    o_ref[...] = lax.dot_general(a_ref[...], wb_ref[...], _NT, preferred_element_type=F32)


def _in_proj_row0(j, bn):
    n_a = W_IN_ZA // bn
    n_c = (W_IN_C_END - W_IN_C) // bn
    return jnp.where(j < n_a, j * bn,
                     jnp.where(j < n_a + n_c, W_IN_C + (j - n_a) * bn,
                               W_IN_G + (j - n_a - n_c) * bn))


def _in_proj(a, a_meta, wt3, layer, bm, bn, name):
    m, k = a.shape
    return pl.pallas_call(
        _mm_nt_wstat_kernel,
        grid=(N_Z // bn, m // bm),
        in_specs=[pl.BlockSpec((bm, k), lambda j, i: (i, 0)),
                  pl.BlockSpec((N_META, k), lambda j, i: (0, 0)),
                  pl.BlockSpec((pl.Element(1), pl.Element(bn), pl.Element(k)),
                               lambda j, i: (layer, pl.multiple_of(_in_proj_row0(j, bn), 16), 0))],
        out_specs=[pl.BlockSpec((bm, bn), lambda j, i: (i, j)),
                   pl.BlockSpec((N_META, bn), lambda j, i: (0, j))],
        out_shape=[jax.ShapeDtypeStruct((m, N_Z), F32),
                   jax.ShapeDtypeStruct((N_META, N_Z), F32)],
        scratch_shapes=[pltpu.VMEM((bn, k), BF16)],
        compiler_params=_params("arbitrary", "arbitrary"),
        name=name,
    )(a, a_meta, wt3)


def _gla_kernel(q_ref, k_ref, v_ref, r_ref, zlr_ref, w2_ref, gb_ref, og_ref, s0_ref,
                o_ref, *rest, chunk, n_chunks, emit_state):
    if emit_state:
        sfin_ref, st_ref, b_sc = rest
    else:
        st_ref, b_sc = rest
    t = pl.program_id(1)
    heads = range(GLA_HEADS)

    @pl.when(t == 0)
    def _():
        st_ref[...] = s0_ref[...]

    row = lax.broadcasted_iota(jnp.int32, (chunk, chunk), 0)
    col = lax.broadcasted_iota(jnp.int32, (chunk, chunk), 1)
    causal = col <= row
    tri = jnp.where(causal, 1.0, 0.0).astype(BF16)
    w2 = w2_ref[...]
    gb = gb_ref[...]
    og = og_ref[...]
    kdim = GLA_HEADS * GLA_DK

    def log_gate_split(c):
        sl = pl.ds(pl.multiple_of(c * chunk, chunk), chunk)
        logit = jnp.dot(zlr_ref[sl, :].astype(BF16), w2, preferred_element_type=F32) + gb
        g = (jnp.minimum(logit, 0.0) - jnp.log1p(jnp.exp(-jnp.abs(logit)))) * (1.0 / GLA_GATE_TAU)
        g_hi = g.astype(BF16)
        g_lo = (g - g_hi.astype(F32)).astype(BF16)
        return jnp.concatenate([g_hi, g_lo], axis=1)

    def log_decay(g_split):
        cs = jnp.dot(tri, g_split, preferred_element_type=F32)
        return cs[:, :kdim] + cs[:, kdim:]

    b_sc[...] = log_decay(log_gate_split(0))

    def body(c, carry):
        sl = pl.ds(pl.multiple_of(c * chunk, chunk), chunk)
        g_split_next = log_gate_split(jnp.minimum(c + 1, n_chunks - 1))
        r = r_ref[sl, :]
        out_gate = r * _sigmoid(r)
        st = [st_ref[h] for h in heads]
        st_bf = [st[h].astype(BF16) for h in heads]
        b = b_sc[...]
        b_last = b[chunk - 1:chunk, :]
        eb = jnp.exp(b)
        enb = jnp.exp(-b)
        erel = jnp.exp(b_last - b)
        elast = jnp.exp(b_last)
        k = k_ref[sl, :]
        qd = (q_ref[sl, :] * (GLA_DK ** -0.5) * eb).astype(BF16)
        kd = (k * enb).astype(BF16)
        k2 = (k * erel).astype(BF16)
        v = v_ref[sl, :].astype(BF16)

        def hk(x, h):
            return x[:, h * GLA_DK:(h + 1) * GLA_DK]

        def hv(x, h):
            return x[:, h * GLA_DV:(h + 1) * GLA_DV]

        a = [lax.dot_general(hk(qd, h), hk(kd, h), _NT, preferred_element_type=F32) for h in heads]
        o_state = [jnp.dot(hk(qd, h), st_bf[h], preferred_element_type=F32) for h in heads]
        o_local = [jnp.dot(jnp.where(causal, a[h], 0.0).astype(BF16), hv(v, h),
                           preferred_element_type=F32) for h in heads]
        b_sc[...] = log_decay(g_split_next)
        upd = [lax.dot_general(hk(k2, h), hv(v, h), _TN, preferred_element_type=F32)
               for h in heads]
        decay_col = jnp.broadcast_to(elast, (128, kdim)).T
        for h in heads:
            col = decay_col[h * GLA_DK:(h + 1) * GLA_DK]
            st_ref[h] = st[h] * jnp.concatenate([col] * (GLA_DV // 128), axis=1) + upd[h]
        for h in heads:
            o = o_local[h] + o_state[h]
            ms = jnp.mean(o * o, axis=-1, keepdims=True)
            on = o * lax.rsqrt(ms + EPS) * og
            o_ref[sl, h * GLA_DV:(h + 1) * GLA_DV] = (on * hv(out_gate, h)).astype(o_ref.dtype)
        return carry

    lax.fori_loop(0, n_chunks, body, 0)

    if emit_state:
        @pl.when(t == pl.num_programs(1) - 1)
        def _():
            sfin_ref[...] = st_ref[...]


def _gla(z, zs, w2p, gate_b, onorm_g, s0, *, batch, tokens, block, chunk, emit_state):
    nt = tokens // block
    rows = batch * tokens
    kern = functools.partial(_gla_kernel, chunk=chunk, n_chunks=block // chunk,
                             emit_state=emit_state)
    kdim = GLA_HEADS * GLA_DK
    vdim = GLA_HEADS * GLA_DV
    in_specs = [
        pl.BlockSpec((block, kdim), lambda b, t: (b * nt + t, Z_Q // kdim)),
        pl.BlockSpec((block, kdim), lambda b, t: (b * nt + t, Z_K // kdim)),
        pl.BlockSpec((block, vdim), lambda b, t: (b * nt + t, Z_V // vdim)),
        pl.BlockSpec((block, vdim), lambda b, t: (b * nt + t, Z_R // vdim)),
        pl.BlockSpec((block, 128), lambda b, t: (b * nt + t, 0)),
        pl.BlockSpec((128, kdim), lambda b, t: (0, 0)),
        pl.BlockSpec((1, kdim), lambda b, t: (0, 0)),
        pl.BlockSpec((1, GLA_DV), lambda b, t: (0, 0)),
        pl.BlockSpec((GLA_HEADS, GLA_DK, GLA_DV), lambda b, t: (0, 0, 0)),
    ]
    out_specs = [pl.BlockSpec((block, vdim), lambda b, t: (b * nt + t, 0))]
    out_shape = [jax.ShapeDtypeStruct((rows, vdim), BF16)]
    if emit_state:
        out_specs.append(pl.BlockSpec((GLA_HEADS, GLA_DK, GLA_DV), lambda b, t: (b, 0, 0)))
        out_shape.append(jax.ShapeDtypeStruct((batch * GLA_HEADS, GLA_DK, GLA_DV), F32))
    res = pl.pallas_call(
        kern,
        grid=(batch, nt),
        in_specs=in_specs,
        out_specs=out_specs,
        out_shape=out_shape,
        scratch_shapes=[pltpu.VMEM((GLA_HEADS, GLA_DK, GLA_DV), F32),
                        pltpu.VMEM((chunk, kdim), F32)],
        compiler_params=_params("parallel", "arbitrary"),
        name="gla",
    )(z, z, z, z, zs, w2p, gate_b.reshape(1, -1), onorm_g.reshape(1, -1), s0)
    return res


def _rope(x, c, s):
    return x * c + pltpu.roll(x, 64, 1) * s


def _qproj_kernel(zc_ref, ng_ref, wt_ref, hg_ref, c_ref, s_ref, o_ref, *, heads_per_dot):
    x = zc_ref[...]
    ms = jnp.mean(x * x, axis=-1, keepdims=True)
    xn = (x * lax.rsqrt(ms + EPS) * ng_ref[...]).astype(BF16)
    bm = x.shape[0]
    c = c_ref[...]
    s = s_ref[...]
    hg = jnp.broadcast_to(hg_ref[...], (MLA_HEAD_PAD, bm))
    for hc in range(MLA_HEADS // heads_per_dot):
        base = hc * heads_per_dot * MLA_HEAD_PAD
        yt = lax.dot_general(wt_ref[base:base + heads_per_dot * MLA_HEAD_PAD, :], xn, _NT,
                             preferred_element_type=F32)
        for j in range(heads_per_dot):
            lo = base + j * MLA_HEAD_PAD
            y = yt[j * MLA_HEAD_PAD:(j + 1) * MLA_HEAD_PAD]
            ms = jnp.sum(y * y, axis=0, keepdims=True) * (1.0 / MLA_QK)
            yn = y * lax.rsqrt(ms + EPS) * hg
            x1 = yn[128:160]
            x2 = yn[192:224]
            o_ref[lo:lo + 128, :] = yn[:128].astype(o_ref.dtype)
            o_ref[lo + 128:lo + 160, :] = (x1 * c - x2 * s).astype(o_ref.dtype)
            o_ref[lo + 160:lo + 192, :] = yn[160:192].astype(o_ref.dtype)
            o_ref[lo + 192:lo + 224, :] = (x2 * c + x1 * s).astype(o_ref.dtype)
            o_ref[lo + 224:lo + 256, :] = yn[224:256].astype(o_ref.dtype)


def _qproj(z, ng, wt, hg_col, cos_t, sin_t, bm):
    m = z.shape[0]
    nt = cos_t.shape[1] // bm
    kern = functools.partial(_qproj_kernel, heads_per_dot=1)
    return pl.pallas_call(
        kern,
        grid=(m // bm,),
        in_specs=[pl.BlockSpec((bm, MLA_LORA), lambda i: (i, Z_CQ // MLA_LORA)),
                  pl.BlockSpec((1, MLA_LORA), lambda i: (0, 0)),
                  pl.BlockSpec((MLA_HEADS * MLA_HEAD_PAD, MLA_LORA), lambda i: (0, 0)),
                  pl.BlockSpec((MLA_HEAD_PAD, 1), lambda i: (0, 0)),
                  pl.BlockSpec((MLA_ROPE // 2, bm), lambda i: (0, i % nt)),
                  pl.BlockSpec((MLA_ROPE // 2, bm), lambda i: (0, i % nt))],
        out_specs=pl.BlockSpec((MLA_HEADS * MLA_HEAD_PAD, bm), lambda i: (0, i)),
        out_shape=jax.ShapeDtypeStruct((MLA_HEADS * MLA_HEAD_PAD, m), BF16),
        compiler_params=_params("parallel"),
        name="q_proj",
    )(z, ng.reshape(1, -1), wt, hg_col, cos_t, sin_t)


def _kvproj_kernel(zc_ref, kr_ref, ng_ref, wk_ref, wvt_ref, hg_ref, c_ref, s_ref, k_ref, vt_ref):
    x = zc_ref[...]
    ms = jnp.mean(x * x, axis=-1, keepdims=True)
    xn = (x * lax.rsqrt(ms + EPS) * ng_ref[...]).astype(BF16)
    c = c_ref[...]
    s = s_ref[...]
    hg = hg_ref[...]
    kr = kr_ref[...]
    kr_ss = jnp.sum(kr * kr, axis=-1, keepdims=True)
    kr_rot = _rope(kr * hg[:, 128:], c, s)
    for hp in range(MLA_HEADS // 2):
        y = jnp.dot(xn, wk_ref[:, hp * 256:(hp + 1) * 256], preferred_element_type=F32)
        for sub in range(2):
            lo = (2 * hp + sub) * MLA_HEAD_PAD
            kn = y[:, sub * MLA_NOPE:(sub + 1) * MLA_NOPE]
            ms = (jnp.sum(kn * kn, axis=-1, keepdims=True) + kr_ss) * (1.0 / MLA_QK)
            rs = lax.rsqrt(ms + EPS)
            k_ref[:, lo:lo + 128] = (kn * rs * hg[:, :128]).astype(k_ref.dtype)
            k_ref[:, lo + 128:lo + 256] = (kr_rot * rs).astype(k_ref.dtype)
    rows_per_dot = 4 * VT_ROWS
    for j in range(MLA_HEADS * VT_ROWS // rows_per_dot):
        sl = slice(j * rows_per_dot, (j + 1) * rows_per_dot)
        vt_ref[sl, :] = lax.dot_general(wvt_ref[sl, :], xn, _NT,
                                        preferred_element_type=F32).astype(vt_ref.dtype)
    ones = jnp.ones((VT_ROWS - MLA_V, x.shape[0]), vt_ref.dtype)
    for h in range(MLA_HEADS):
        vt_ref[h * VT_ROWS + MLA_V:(h + 1) * VT_ROWS, :] = ones


def _kvproj(z, zs, ng, wk, wvt, hg, ctab, stab, bm):
    m = z.shape[0]
    nt = ctab.shape[0] // bm
    return pl.pallas_call(
        _kvproj_kernel,
        grid=(m // bm,),
        in_specs=[pl.BlockSpec((bm, MLA_LORA), lambda i: (i, Z_CKV // MLA_LORA)),
                  pl.BlockSpec((bm, 128), lambda i: (i, 1)),
                  pl.BlockSpec((1, MLA_LORA), lambda i: (0, 0)),
                  pl.BlockSpec((MLA_LORA, MLA_HEADS * MLA_NOPE), lambda i: (0, 0)),
                  pl.BlockSpec((MLA_HEADS * VT_ROWS, MLA_LORA), lambda i: (0, 0)),
                  pl.BlockSpec((1, MLA_HEAD_PAD), lambda i: (0, 0)),
                  pl.BlockSpec((bm, 128), lambda i: (i % nt, 0)),
                  pl.BlockSpec((bm, 128), lambda i: (i % nt, 0))],
        out_specs=[pl.BlockSpec((bm, MLA_HEADS * MLA_HEAD_PAD), lambda i: (i, 0)),
                   pl.BlockSpec((MLA_HEADS * VT_ROWS, bm), lambda i: (0, i))],
        out_shape=[jax.ShapeDtypeStruct((m, MLA_HEADS * MLA_HEAD_PAD), BF16),
                   jax.ShapeDtypeStruct((MLA_HEADS * VT_ROWS, m), BF16)],
        compiler_params=_params("parallel"),
        name="kv_proj",
    )(z, zs, ng.reshape(1, -1), wk, wvt, hg, ctab, stab)


def _attn_kernel(qt_ref, k_ref, vt_ref, km_ref, vmt_ref, o_ref, m_sc, acc_sc, s_sc, *, blk, heads,
                 qw, ahead, nq):
    i = pl.program_id(2)

    def queries(qi, h, lo, width):
        return qt_ref[h * MLA_HEAD_PAD:(h + 1) * MLA_HEAD_PAD,
                      pl.ds(pl.multiple_of(qi * blk + lo, width), width)]

    for h in range(heads):
        s = jnp.dot(km_ref[:, h * MLA_HEAD_PAD:(h + 1) * MLA_HEAD_PAD], queries(i, h, 0, blk),
                    preferred_element_type=F32)
        m = jnp.max(s, axis=0, keepdims=True)
        p = jnp.exp2(s - m)
        m_sc[h] = m
        acc_sc[h] = jnp.dot(vmt_ref[h * VT_ROWS:(h + 1) * VT_ROWS, :], p.astype(BF16),
                            preferred_element_type=F32)

    tasks = [(h, c) for c in range(blk // qw) for h in range(heads)]

    def scores(qi, kb, h, c):
        ksl = pl.ds(pl.multiple_of(kb * blk, blk), blk)
        return jnp.dot(k_ref[ksl, h * MLA_HEAD_PAD:(h + 1) * MLA_HEAD_PAD],
                       queries(qi, h, c * qw, qw),
                       preferred_element_type=F32)

    @pl.when(i == 0)
    def _():
        for n in range(ahead):
            s_sc[n] = scores(0, 0, *tasks[n])

    def step(kb, masked):
        ksl = pl.ds(pl.multiple_of(kb * blk, blk), blk)
        pending = [s_sc[n] for n in range(ahead)]
        handover = []
        for n, (h, c) in enumerate(tasks):
            csl = slice(c * qw, (c + 1) * qw)
            s = pending.pop(0)
            if n + ahead < len(tasks):
                pending.append(scores(i, kb, *tasks[n + ahead]))
            elif not masked:
                handover.append(scores(i, kb + 1, *tasks[n + ahead - len(tasks)]))
            else:
                handover.append(scores(jnp.minimum(i + 1, nq - 1), 0,
                                       *tasks[n + ahead - len(tasks)]))
            if masked:
                key = lax.broadcasted_iota(jnp.int32, s.shape, 0)
                qry = lax.broadcasted_iota(jnp.int32, s.shape, 1) + c * qw
                s = jnp.where(key <= qry, s, -1e30)
            m_prev = m_sc[h, :, csl]
            m_new = jnp.maximum(m_prev, jnp.max(s, axis=0, keepdims=True))
            alpha = jnp.exp2(m_prev - m_new)
            p = jnp.exp2(s - m_new)
            acc_sc[h, :, csl] = alpha * acc_sc[h, :, csl] + jnp.dot(
                vt_ref[h * VT_ROWS:(h + 1) * VT_ROWS, ksl], p.astype(BF16),
                preferred_element_type=F32)
            m_sc[h, :, csl] = m_new
        for n, s_next in enumerate(handover):
            s_sc[n] = s_next

    def body(kb, carry):
        step(kb, False)
        return carry

    lax.fori_loop(0, i, body, 0)
    step(i, True)
    for h in range(heads):
        acc = acc_sc[h]
        o_ref[:, h * MLA_V:(h + 1) * MLA_V] = (
            acc[:MLA_V] * (1.0 / acc[MLA_V:MLA_V + 1])).T.astype(o_ref.dtype)


def _attention(q, k, vt, km, vmt, blk, heads, qw, ahead):
    nq = SEQ // blk
    kern = functools.partial(_attn_kernel, blk=blk, heads=heads, qw=qw, ahead=ahead, nq=nq)
    return pl.pallas_call(
        kern,
        grid=(BATCH, MLA_HEADS // heads, nq),
        in_specs=[pl.BlockSpec((heads * MLA_HEAD_PAD, SEQ), lambda b, h, i: (h, b)),
                  pl.BlockSpec((SEQ, heads * MLA_HEAD_PAD), lambda b, h, i: (b, h)),
                  pl.BlockSpec((heads * VT_ROWS, SEQ), lambda b, h, i: (h, b)),
                  pl.BlockSpec((N_META, heads * MLA_HEAD_PAD), lambda b, h, i: (0, h)),
                  pl.BlockSpec((heads * VT_ROWS, N_META), lambda b, h, i: (h, 0))],
        out_specs=pl.BlockSpec((blk, heads * MLA_V), lambda b, h, i: (b * nq + i, h)),
        out_shape=jax.ShapeDtypeStruct((ROWS, MLA_HEADS * MLA_V), BF16),
        scratch_shapes=[pltpu.VMEM((heads, 1, blk), F32),
                        pltpu.VMEM((heads, VT_ROWS, blk), F32), pltpu.VMEM((ahead, blk, qw), F32)],
        compiler_params=_params("parallel", "parallel", "arbitrary"),
        name="mla_attention",
    )(q, k, vt, km, vmt)


def _meta_attn_kernel(q_ref, km_ref, vmt_ref, o_ref):
    s = lax.dot_general(q_ref[...], km_ref[...], _NT, preferred_element_type=F32)
    row = lax.broadcasted_iota(jnp.int32, s.shape, 0)
    col = lax.broadcasted_iota(jnp.int32, s.shape, 1)
    s = jnp.where(col <= row, s, -1e30)
    m = jnp.max(s, axis=-1, keepdims=True)
    p = jnp.exp2(s - m)
    l = jnp.sum(p, axis=-1, keepdims=True)
    o = lax.dot_general(p.astype(BF16), vmt_ref[:MLA_V, :], _NT, preferred_element_type=F32)
    o_ref[...] = (o / l).astype(o_ref.dtype)


def _meta_attention(qm, kmp, vmtp):
    return pl.pallas_call(
        _meta_attn_kernel,
        grid=(MLA_HEADS,),
        in_specs=[pl.BlockSpec((N_META, MLA_HEAD_PAD), lambda h: (0, h)),
                  pl.BlockSpec((128, MLA_HEAD_PAD), lambda h: (0, h)),
                  pl.BlockSpec((VT_ROWS, 128), lambda h: (h, 0))],
        out_specs=pl.BlockSpec((N_META, MLA_V), lambda h: (0, h)),
        out_shape=jax.ShapeDtypeStruct((N_META, MLA_HEADS * MLA_V), BF16),
        compiler_params=_params("parallel"),
        name="meta_attention",
    )(qm, kmp, vmtp)


def _merge_kernel(a_ref, b_ref, wa_ref, wb_ref, za_ref, zb_ref, o_ref):
    ga = _sigmoid(za_ref[...])
    gb = _sigmoid(zb_ref[...])
    ya = jnp.dot(a_ref[...], wa_ref[...], preferred_element_type=F32)
    yb = jnp.dot(b_ref[...], wb_ref[...], preferred_element_type=F32)
    o_ref[...] = (ga * ya + gb * yb).astype(o_ref.dtype)


def _merge(a, b, wa3, wb3, layer, z, bm, bn):
    m = a.shape[0]
    return pl.pallas_call(
        _merge_kernel,
        grid=(m // bm, D_MODEL // bn),
        in_specs=[pl.BlockSpec((bm, D_MODEL), lambda i, j: (i, 0)),
                  pl.BlockSpec((bm, D_MODEL), lambda i, j: (i, 0)),
                  pl.BlockSpec((None, D_MODEL, bn), lambda i, j: (layer, 0, j)),
                  pl.BlockSpec((None, D_MODEL, bn), lambda i, j: (layer, 0, j)),
                  pl.BlockSpec((bm, bn), lambda i, j: (i, Z_A // bn + j)),
                  pl.BlockSpec((bm, bn), lambda i, j: (i, Z_B // bn + j))],
        out_specs=pl.BlockSpec((bm, bn), lambda i, j: (i, j)),
        out_shape=jax.ShapeDtypeStruct((m, D_MODEL), BF16),
        compiler_params=_params("parallel", "parallel"),
        name="branch_merge",
    )(a, b, wa3, wb3, z, z)


def _outproj_kernel(m_ref, w_ref, h_ref, g_ref, ho_ref, hn_ref):
    hn = h_ref[...] + jnp.dot(m_ref[...], w_ref[...], preferred_element_type=F32)
    ho_ref[...] = hn
    ms = jnp.mean(hn * hn, axis=-1, keepdims=True)
    hn_ref[...] = (hn * lax.rsqrt(ms + EPS) * g_ref[...]).astype(hn_ref.dtype)


def _outproj(mg, w3, layer, h, g, bm):
    m = mg.shape[0]
    return pl.pallas_call(
        _outproj_kernel,
        grid=(m // bm,),
        in_specs=[pl.BlockSpec((bm, D_MODEL), lambda i: (i, 0)),
                  pl.BlockSpec((None, D_MODEL, D_MODEL), lambda i: (layer, 0, 0)),
                  pl.BlockSpec((bm, D_MODEL), lambda i: (i, 0)),
                  pl.BlockSpec((1, D_MODEL), lambda i: (0, 0))],
        out_specs=[pl.BlockSpec((bm, D_MODEL), lambda i: (i, 0)),
                   pl.BlockSpec((bm, D_MODEL), lambda i: (i, 0))],
        out_shape=[jax.ShapeDtypeStruct((m, D_MODEL), F32),
                   jax.ShapeDtypeStruct((m, D_MODEL), BF16)],
        compiler_params=_params("parallel"),
        name="out_proj",
    )(mg, w3, h, g.reshape(1, -1))


def _gateup_kernel(*refs, with_meta):
    if with_meta:
        h_ref, hm_ref, wg_ref, wu_ref, o_ref, om_ref, wgb_ref, wub_ref = refs
    else:
        h_ref, wg_ref, wu_ref, o_ref, wgb_ref, wub_ref = refs

    def swiglu(h):
        g = jnp.dot(h, wgb_ref[...], preferred_element_type=F32)
        u = jnp.dot(h, wub_ref[...], preferred_element_type=F32)
        return (g * _sigmoid(g) * u).astype(o_ref.dtype)

    @pl.when(pl.program_id(1) == 0)
    def _():
        wgb_ref[...] = wg_ref[...].astype(BF16)
        wub_ref[...] = wu_ref[...].astype(BF16)
        if with_meta:
            om_ref[...] = swiglu(hm_ref[...])

    o_ref[...] = swiglu(h_ref[...])


def _gateup(h, h_meta, w3, layer, bm, bn):
    m = h.shape[0]
    nb = FF_HIDDEN // bn
    with_meta = h_meta is not None
    weight_specs = [pl.BlockSpec((None, D_MODEL, bn), lambda j, i: (layer, 0, j)),
                    pl.BlockSpec((None, D_MODEL, bn), lambda j, i: (layer, 0, nb + j))]
    in_specs = [pl.BlockSpec((bm, D_MODEL), lambda j, i: (i, 0))]
    operands = [h]
    out_specs = [pl.BlockSpec((bm, bn), lambda j, i: (i, j))]
    out_shape = [jax.ShapeDtypeStruct((m, FF_HIDDEN), BF16)]
    if with_meta:
        in_specs.append(pl.BlockSpec((N_META, D_MODEL), lambda j, i: (0, 0)))
        operands.append(h_meta)
        out_specs.append(pl.BlockSpec((N_META, bn), lambda j, i: (0, j)))
        out_shape.append(jax.ShapeDtypeStruct((N_META, FF_HIDDEN), BF16))
    return pl.pallas_call(
        functools.partial(_gateup_kernel, with_meta=with_meta),
        grid=(nb, m // bm),
        in_specs=in_specs + weight_specs,
        out_specs=out_specs,
        out_shape=out_shape,
        scratch_shapes=[pltpu.VMEM((D_MODEL, bn), BF16), pltpu.VMEM((D_MODEL, bn), BF16)],
        compiler_params=_params("arbitrary", "arbitrary"),
        name="gate_up",
    )(*operands, w3, w3)


def _down_kernel(a_ref, w_ref, h_ref, *rest, emit_norm):
    if emit_norm:
        g_ref, ho_ref, hn_ref, acc_ref = rest
    else:
        ho_ref, acc_ref = rest
    k = pl.program_id(1)
    last = pl.num_programs(1) - 1

    def product():
        return jnp.dot(a_ref[...], w_ref[...], preferred_element_type=F32)

    @pl.when(k == 0)
    def _():
        acc_ref[...] = product()

    @pl.when((k > 0) & (k < last))
    def _():
        acc_ref[...] += product()

    @pl.when(k == last)
    def _():
        hn = h_ref[...] + (acc_ref[...] + product())
        ho_ref[...] = hn
        if emit_norm:
            ms = jnp.mean(hn * hn, axis=-1, keepdims=True)
            hn_ref[...] = (hn * lax.rsqrt(ms + EPS) * g_ref[...]).astype(hn_ref.dtype)


def _down(a, w3, layer, h, g, bm, bk):
    m = a.shape[0]
    assert FF_HIDDEN // bk >= 2
    emit_norm = g is not None
    row_block = pl.BlockSpec((bm, D_MODEL), lambda i, k: (i, 0))
    in_specs = [pl.BlockSpec((bm, bk), lambda i, k: (i, k)),
                pl.BlockSpec((None, bk, D_MODEL), lambda i, k: (layer, k, 0)),
                row_block]
    operands = [a, w3, h]
    out_specs = [row_block]
    out_shape = [jax.ShapeDtypeStruct((m, D_MODEL), F32)]
    if emit_norm:
        in_specs.append(pl.BlockSpec((1, D_MODEL), lambda i, k: (0, 0)))
        operands.append(g.reshape(1, -1))
        out_specs.append(row_block)
        out_shape.append(jax.ShapeDtypeStruct((m, D_MODEL), BF16))
    return pl.pallas_call(
        functools.partial(_down_kernel, emit_norm=emit_norm),
        grid=(m // bm, FF_HIDDEN // bk),
        in_specs=in_specs,
        out_specs=out_specs,
        out_shape=out_shape,
        scratch_shapes=[pltpu.VMEM((bm, D_MODEL), F32)],
        compiler_params=_params("parallel", "arbitrary"),
        name="down_proj",
    )(*operands)


def _rope_layout(t):
    zeros = jnp.zeros(t.shape[:-1] + (32,), t.dtype)
    return jnp.concatenate([t[..., :32], zeros, t[..., 32:], zeros], axis=-1)


def _head_layout(t):
    return jnp.concatenate([t[..., :MLA_NOPE], _rope_layout(t[..., MLA_NOPE:])], axis=-1)


def _rope_tables():
    length = N_META + SEQ
    inv = 1.0 / (ROPE_THETA ** (jnp.arange(0, MLA_ROPE, 2, dtype=F32) / MLA_ROPE))
    ang = jnp.arange(length, dtype=F32)[:, None] * inv[None, :]
    cos, sin = jnp.cos(ang), jnp.sin(ang)
    zeros = jnp.zeros_like(cos)
    ctab = jnp.concatenate([cos, zeros, cos, zeros], axis=-1)
    stab = jnp.concatenate([-sin, zeros, sin, zeros], axis=-1)
    return ctab, stab, cos.T, sin.T


def kernel(x, meta_tokens, norm1_g, w_in, gla_gate_w2, gla_gate_b, gla_onorm_g, w_branch_a,
           q_a_norm_g, w_uq, kv_a_norm_g, w_ukv, q_norm_g, k_norm_g, w_branch_b, w_out,
           norm2_g, w_gate_up, w_down):
    ctab, stab, cos_t, sin_t = _rope_tables()
    ctab_m, stab_m = ctab[:N_META], stab[:N_META]
    ctab_r, stab_r = ctab[N_META:], stab[N_META:]
    cos_tm, sin_tm = cos_t[:, :N_META], sin_t[:, :N_META]
    cos_tr, sin_tr = cos_t[:, N_META:], sin_t[:, N_META:]

    h_res = x.reshape(ROWS, D_MODEL)
    hm_res = meta_tokens.astype(F32)
    h = _rms(h_res, norm1_g[0], 512)
    hm = _rms(hm_res, norm1_g[0], N_META)
    s_zero = jnp.zeros((GLA_HEADS, GLA_DK, GLA_DV), F32)

    w_in_t = jnp.swapaxes(w_in, 1, 2)
    w_kr = w_in_t[:, 7184:7248]
    zrow = jnp.zeros((DEPTH, 32, D_MODEL), F32)
    w_small = jnp.concatenate(
        [w_in_t[:, 6144:6160], jnp.zeros((DEPTH, 128 - GLA_GATE_RANK, D_MODEL), F32),
         w_kr[:, :32], zrow, w_kr[:, 32:], zrow], axis=1)
    wa = w_branch_a.astype(BF16)
    wb = w_branch_b.astype(BF16)
    wo = w_out.astype(BF16)
    wd = w_down.astype(BF16)

    for l in range(DEPTH):
        last = l == DEPTH - 1
        w2p = jnp.concatenate(
            [gla_gate_w2[l], jnp.zeros((128 - GLA_GATE_RANK, GLA_HEADS * GLA_DK), F32)],
            axis=0).astype(BF16)
        wqt = _head_layout(w_uq[l].reshape(MLA_LORA, MLA_HEADS, MLA_QK)).reshape(
            MLA_LORA, MLA_HEADS * MLA_HEAD_PAD).T.astype(BF16)
        wkv3 = w_ukv[l].reshape(MLA_LORA, MLA_HEADS, MLA_NOPE + MLA_V)
        wk = wkv3[:, :, :MLA_NOPE].reshape(MLA_LORA, MLA_HEADS * MLA_NOPE).astype(BF16)
        wvt = jnp.pad(jnp.transpose(wkv3[:, :, MLA_NOPE:], (1, 2, 0)),
                      ((0, 0), (0, VT_ROWS - MLA_V), (0, 0))).reshape(
            MLA_HEADS * VT_ROWS, MLA_LORA).astype(BF16)
        qg = (_head_layout(q_norm_g[l]) * (MLA_QK ** -0.5 * LOG2E)).reshape(MLA_HEAD_PAD, 1)
        kg = _head_layout(k_norm_g[l]).reshape(1, MLA_HEAD_PAD)

        z, zm = _in_proj(h, hm, w_in_t, l, 1024, 1024, name="in_proj")
        zs = _matmul_nt(h, w_small, l, 2048, N_SMALL, name="in_proj_small")
        zsm = _matmul_nt(hm, w_small, l, N_META, N_SMALL, name="in_proj_small_meta")

        gla_m, s_meta = _gla(zm, zsm, w2p, gla_gate_b[l], gla_onorm_g[l], s_zero,
                             batch=1, tokens=N_META, block=N_META, chunk=N_META, emit_state=True)
        (gla_r,) = _gla(z, zs, w2p, gla_gate_b[l], gla_onorm_g[l], s_meta,
                        batch=BATCH, tokens=SEQ, block=512, chunk=GLA_CHUNK, emit_state=False)

        km, vmt = _kvproj(zm, zsm, kv_a_norm_g[l], wk, wvt, kg, ctab_m, stab_m, N_META)
        kr, vrt = _kvproj(z, zs, kv_a_norm_g[l], wk, wvt, kg, ctab_r, stab_r, 1024)
        kmp = jnp.pad(km, ((0, 128 - N_META), (0, 0)))
        vmtp = jnp.pad(vmt, ((0, 0), (0, 128 - N_META)))
        qr = _qproj(z, q_a_norm_g[l], wqt, qg, cos_tr, sin_tr, 1024)
        att_r = _attention(qr, kr, vrt, km, vmt, 512, 4, 512, 4)

        merged = _merge(gla_r, att_r, wa, wb, l, z, 1024, 512)
        h_res, h2 = _outproj(merged, wo, l, h_res, norm2_g[l], 512)
        if last:
            (act,) = _gateup(h2, None, w_gate_up, l, 1024, 512)
            (h_res,) = _down(act, wd, l, h_res, None, 512, 2816)
        else:
            qm = _qproj(zm, q_a_norm_g[l], wqt, qg, cos_tm, sin_tm, N_META)
            att_m = _meta_attention(qm.T, kmp, vmtp)
            merged_m = _merge(gla_m, att_m, wa, wb, l, zm, N_META, 512)
            hm_res, hm2 = _outproj(merged_m, wo, l, hm_res, norm2_g[l], N_META)
            act, act_m = _gateup(h2, hm2, w_gate_up, l, 1024, 512)
            h_res, h = _down(act, wd, l, h_res, norm1_g[l + 1], 512, 2816)
            hm_res, hm = _down(act_m, wd, l, hm_res, norm1_g[l + 1], N_META, 1408)

    return h_res.reshape(BATCH, SEQ, D_MODEL)
```

```python
import functools

import jax
import jax.numpy as jnp
from jax import lax
from jax.experimental import pallas as pl
from jax.experimental.pallas import tpu as pltpu

D_MODEL = 2048
BATCH = 4
SEQ = 4096
DEPTH = 2
N_META = 16
ROWS = BATCH * SEQ

GLA_HEADS = 4
GLA_DK = 256
GLA_DV = 512
GLA_GATE_RANK = 16
GLA_GATE_TAU = 16.0
GLA_CHUNK = 64

MLA_HEADS = 16
MLA_LORA = 512
MLA_NOPE = 128
MLA_ROPE = 64
MLA_QK = 192
MLA_V = 128
VT_ROWS = 144
MLA_HEAD_PAD = 256
ROPE_THETA = 10000.0
FF_HIDDEN = 5632
EPS = 1e-6
LOG2E = 1.4426950408889634

W_IN_ZA = 6144
W_IN_C, W_IN_C_END = 6160, 7184
W_IN_G = 7248
Z_Q, Z_K, Z_V, Z_R, Z_CQ, Z_CKV, Z_A, Z_B = 0, 1024, 2048, 4096, 6144, 6656, 7168, 9216
N_Z = 11264
N_SMALL = 256

RMS_ROWS = 512
IN_PROJ_BLOCK = (1024, 1024)
IN_PROJ_SMALL_ROWS = 2048
GLA_TOKENS = 512
PROJ_ROWS = 1024
ATTN_BLOCK = 512
ATTN_HEADS = 4
MERGE_BLOCK = (1024, 512)
OUT_PROJ_ROWS = 512
GATE_UP_BLOCK = (1024, 512)
DOWN_BLOCK = (512, 2816)
DOWN_META_K = 1408

VMEM_LIMIT = 56 * 1024 * 1024
BF16 = jnp.bfloat16
F32 = jnp.float32

_NT = (((1,), (1,)), ((), ()))
_TN = (((0,), (0,)), ((), ()))


def _params(*sem):
    return pltpu.CompilerParams(dimension_semantics=sem, vmem_limit_bytes=VMEM_LIMIT)


def _sigmoid(x):
    return 1.0 / (1.0 + jnp.exp(-x))


def _rms_kernel(x_ref, g_ref, o_ref):
    x = x_ref[...]
    ms = jnp.mean(x * x, axis=-1, keepdims=True)
    o_ref[...] = (x * lax.rsqrt(ms + EPS) * g_ref[...]).astype(o_ref.dtype)


def _rms(x, g, bm):
    m, d = x.shape
    return pl.pallas_call(
        _rms_kernel,
        grid=(m // bm,),
        in_specs=[pl.BlockSpec((bm, d), lambda i: (i, 0)),
                  pl.BlockSpec((1, d), lambda i: (0, 0))],
        out_specs=pl.BlockSpec((bm, d), lambda i: (i, 0)),
        out_shape=jax.ShapeDtypeStruct((m, d), BF16),
        compiler_params=_params("parallel"),
        name="rmsnorm",
    )(x, g.reshape(1, d))


def _mm_nt_kernel(a_ref, wt_ref, o_ref):
    o_ref[...] = lax.dot_general(a_ref[...], wt_ref[...].astype(BF16), _NT,
                                 preferred_element_type=F32)


def _matmul_nt(a, wt3, layer, bm, bn, name):
    m, k = a.shape
    n = wt3.shape[1]
    return pl.pallas_call(
        _mm_nt_kernel,
        grid=(m // bm, n // bn),
        in_specs=[pl.BlockSpec((bm, k), lambda i, j: (i, 0)),
                  pl.BlockSpec((None, bn, k), lambda i, j: (layer, j, 0))],
        out_specs=pl.BlockSpec((bm, bn), lambda i, j: (i, j)),
        out_shape=jax.ShapeDtypeStruct((m, n), F32),
        compiler_params=_params("parallel", "parallel"),
        name=name,
    )(a, wt3)


def _mm_nt_wstat_kernel(a_ref, am_ref, wt_ref, o_ref, om_ref, wb_ref):
    @pl.when(pl.program_id(1) == 0)
    def _():
        wb_ref[...] = wt_ref[0].astype(BF16)
        om_ref[...] = lax.dot_general(am_ref[...], wb_ref[...], _NT, preferred_element_type=F32)

    o_ref[...] = lax.dot_general(a_ref[...], wb_ref[...], _NT, preferred_element_type=F32)


def _in_proj_row0(j, bn):
    n_a = W_IN_ZA // bn
    n_c = (W_IN_C_END - W_IN_C) // bn
    return jnp.where(j < n_a, j * bn,
                     jnp.where(j < n_a + n_c, W_IN_C + (j - n_a) * bn,
                               W_IN_G + (j - n_a - n_c) * bn))


def _in_proj(a, a_meta, wt3, layer, bm, bn, name):
    m, k = a.shape
    return pl.pallas_call(
        _mm_nt_wstat_kernel,
        grid=(N_Z // bn, m // bm),
        in_specs=[pl.BlockSpec((bm, k), lambda j, i: (i, 0)),
                  pl.BlockSpec((N_META, k), lambda j, i: (0, 0)),
                  pl.BlockSpec((pl.Element(1), pl.Element(bn), pl.Element(k)),
                               lambda j, i: (layer, pl.multiple_of(_in_proj_row0(j, bn), 16), 0))],
        out_specs=[pl.BlockSpec((bm, bn), lambda j, i: (i, j)),
                   pl.BlockSpec((N_META, bn), lambda j, i: (0, j))],
        out_shape=[jax.ShapeDtypeStruct((m, N_Z), F32),
                   jax.ShapeDtypeStruct((N_META, N_Z), F32)],
        scratch_shapes=[pltpu.VMEM((bn, k), BF16)],
        compiler_params=_params("arbitrary", "arbitrary"),
        name=name,
    )(a, a_meta, wt3)


def _gla_kernel(q_ref, k_ref, v_ref, r_ref, zlr_ref, w2_ref, gb_ref, og_ref, s0_ref,
                o_ref, *rest, chunk, n_chunks, emit_state):
    if emit_state:
        sfin_ref, st_ref, b_sc = rest
    else:
        st_ref, b_sc = rest
    t = pl.program_id(1)
    heads = range(GLA_HEADS)

    @pl.when(t == 0)
    def _():
        st_ref[...] = s0_ref[...]

    row = lax.broadcasted_iota(jnp.int32, (chunk, chunk), 0)
    col = lax.broadcasted_iota(jnp.int32, (chunk, chunk), 1)
    causal = col <= row
    tri = jnp.where(causal, 1.0, 0.0).astype(BF16)
    w2 = w2_ref[...]
    gb = gb_ref[...]
    og = og_ref[...]
    kdim = GLA_HEADS * GLA_DK

    def log_gate_split(c):
        sl = pl.ds(pl.multiple_of(c * chunk, chunk), chunk)
        logit = jnp.dot(zlr_ref[sl, :].astype(BF16), w2, preferred_element_type=F32) + gb
        g = (jnp.minimum(logit, 0.0) - jnp.log1p(jnp.exp(-jnp.abs(logit)))) * (1.0 / GLA_GATE_TAU)
        g_hi = g.astype(BF16)
        g_lo = (g - g_hi.astype(F32)).astype(BF16)
        return jnp.concatenate([g_hi, g_lo], axis=1)

    def log_decay(g_split):
        cs = jnp.dot(tri, g_split, preferred_element_type=F32)
        return cs[:, :kdim] + cs[:, kdim:]

    b_sc[...] = log_decay(log_gate_split(0))

    def body(c, carry):
        sl = pl.ds(pl.multiple_of(c * chunk, chunk), chunk)
        g_split_next = log_gate_split(jnp.minimum(c + 1, n_chunks - 1))
        r = r_ref[sl, :]
        out_gate = r * _sigmoid(r)
        st = [st_ref[h] for h in heads]
        st_bf = [st[h].astype(BF16) for h in heads]
        b = b_sc[...]
        b_last = b[chunk - 1:chunk, :]
        eb = jnp.exp(b)
        enb = jnp.exp(-b)
        erel = jnp.exp(b_last - b)
        elast = jnp.exp(b_last)
        k = k_ref[sl, :]
        qd = (q_ref[sl, :] * (GLA_DK ** -0.5) * eb).astype(BF16)
        kd = (k * enb).astype(BF16)
        k2 = (k * erel).astype(BF16)
        v = v_ref[sl, :].astype(BF16)

        def hk(x, h):
            return x[:, h * GLA_DK:(h + 1) * GLA_DK]

        def hv(x, h):
            return x[:, h * GLA_DV:(h + 1) * GLA_DV]

        a = [lax.dot_general(hk(qd, h), hk(kd, h), _NT, preferred_element_type=F32) for h in heads]
        o_state = [jnp.dot(hk(qd, h), st_bf[h], preferred_element_type=F32) for h in heads]
        o_local = [jnp.dot(jnp.where(causal, a[h], 0.0).astype(BF16), hv(v, h),
                           preferred_element_type=F32) for h in heads]
        b_sc[...] = log_decay(g_split_next)
        upd = [lax.dot_general(hk(k2, h), hv(v, h), _TN, preferred_element_type=F32)
               for h in heads]
        decay_col = jnp.broadcast_to(elast, (128, kdim)).T
        for h in heads:
            col = decay_col[h * GLA_DK:(h + 1) * GLA_DK]
            st_ref[h] = st[h] * jnp.concatenate([col] * (GLA_DV // 128), axis=1) + upd[h]
        for h in heads:
            o = o_local[h] + o_state[h]
            ms = jnp.mean(o * o, axis=-1, keepdims=True)
            on = o * lax.rsqrt(ms + EPS) * og
            o_ref[sl, h * GLA_DV:(h + 1) * GLA_DV] = (on * hv(out_gate, h)).astype(o_ref.dtype)
        return carry

    lax.fori_loop(0, n_chunks, body, 0)

    if emit_state:
        @pl.when(t == pl.num_programs(1) - 1)
        def _():
            sfin_ref[...] = st_ref[...]


def _gla(z, zs, w2p, gate_b, onorm_g, s0, *, batch, tokens, block, chunk, emit_state):
    nt = tokens // block
    rows = batch * tokens
    kern = functools.partial(_gla_kernel, chunk=chunk, n_chunks=block // chunk,
                             emit_state=emit_state)
    kdim = GLA_HEADS * GLA_DK
    vdim = GLA_HEADS * GLA_DV
    in_specs = [
        pl.BlockSpec((block, kdim), lambda b, t: (b * nt + t, Z_Q // kdim)),
        pl.BlockSpec((block, kdim), lambda b, t: (b * nt + t, Z_K // kdim)),
        pl.BlockSpec((block, vdim), lambda b, t: (b * nt + t, Z_V // vdim)),
        pl.BlockSpec((block, vdim), lambda b, t: (b * nt + t, Z_R // vdim)),
        pl.BlockSpec((block, 128), lambda b, t: (b * nt + t, 0)),
        pl.BlockSpec((128, kdim), lambda b, t: (0, 0)),
        pl.BlockSpec((1, kdim), lambda b, t: (0, 0)),
        pl.BlockSpec((1, GLA_DV), lambda b, t: (0, 0)),
        pl.BlockSpec((GLA_HEADS, GLA_DK, GLA_DV), lambda b, t: (0, 0, 0)),
    ]
    out_specs = [pl.BlockSpec((block, vdim), lambda b, t: (b * nt + t, 0))]
    out_shape = [jax.ShapeDtypeStruct((rows, vdim), BF16)]
    if emit_state:
        out_specs.append(pl.BlockSpec((GLA_HEADS, GLA_DK, GLA_DV), lambda b, t: (b, 0, 0)))
        out_shape.append(jax.ShapeDtypeStruct((batch * GLA_HEADS, GLA_DK, GLA_DV), F32))
    res = pl.pallas_call(
        kern,
        grid=(batch, nt),
        in_specs=in_specs,
        out_specs=out_specs,
        out_shape=out_shape,
        scratch_shapes=[pltpu.VMEM((GLA_HEADS, GLA_DK, GLA_DV), F32),
                        pltpu.VMEM((chunk, kdim), F32)],
        compiler_params=_params("parallel", "arbitrary"),
        name="gla",
    )(z, z, z, z, zs, w2p, gate_b.reshape(1, -1), onorm_g.reshape(1, -1), s0)
    return res


def _rope(x, c, s):
    return x * c + pltpu.roll(x, 64, 1) * s


def _qproj_kernel(zc_ref, ng_ref, wt_ref, hg_ref, c_ref, s_ref, o_ref, *, heads_per_dot):
    x = zc_ref[...]
    ms = jnp.mean(x * x, axis=-1, keepdims=True)
    xn = (x * lax.rsqrt(ms + EPS) * ng_ref[...]).astype(BF16)
    bm = x.shape[0]
    c = c_ref[...]
    s = s_ref[...]
    hg = jnp.broadcast_to(hg_ref[...], (MLA_HEAD_PAD, bm))
    for hc in range(MLA_HEADS // heads_per_dot):
        base = hc * heads_per_dot * MLA_HEAD_PAD
        yt = lax.dot_general(wt_ref[base:base + heads_per_dot * MLA_HEAD_PAD, :], xn, _NT,
                             preferred_element_type=F32)
        for j in range(heads_per_dot):
            lo = base + j * MLA_HEAD_PAD
            y = yt[j * MLA_HEAD_PAD:(j + 1) * MLA_HEAD_PAD]
            ms = jnp.sum(y * y, axis=0, keepdims=True) * (1.0 / MLA_QK)
            yn = y * lax.rsqrt(ms + EPS) * hg
            x1 = yn[128:160]
            x2 = yn[192:224]
            o_ref[lo:lo + 128, :] = yn[:128].astype(o_ref.dtype)
            o_ref[lo + 128:lo + 160, :] = (x1 * c - x2 * s).astype(o_ref.dtype)
            o_ref[lo + 160:lo + 192, :] = yn[160:192].astype(o_ref.dtype)
            o_ref[lo + 192:lo + 224, :] = (x2 * c + x1 * s).astype(o_ref.dtype)
            o_ref[lo + 224:lo + 256, :] = yn[224:256].astype(o_ref.dtype)


def _qproj(z, ng, wt, hg_col, cos_t, sin_t, bm):
    m = z.shape[0]
    nt = cos_t.shape[1] // bm
    kern = functools.partial(_qproj_kernel, heads_per_dot=1)
    return pl.pallas_call(
        kern,
        grid=(m // bm,),
        in_specs=[pl.BlockSpec((bm, MLA_LORA), lambda i: (i, Z_CQ // MLA_LORA)),
                  pl.BlockSpec((1, MLA_LORA), lambda i: (0, 0)),
                  pl.BlockSpec((MLA_HEADS * MLA_HEAD_PAD, MLA_LORA), lambda i: (0, 0)),
                  pl.BlockSpec((MLA_HEAD_PAD, 1), lambda i: (0, 0)),
                  pl.BlockSpec((MLA_ROPE // 2, bm), lambda i: (0, i % nt)),
                  pl.BlockSpec((MLA_ROPE // 2, bm), lambda i: (0, i % nt))],
        out_specs=pl.BlockSpec((MLA_HEADS * MLA_HEAD_PAD, bm), lambda i: (0, i)),
        out_shape=jax.ShapeDtypeStruct((MLA_HEADS * MLA_HEAD_PAD, m), BF16),
        compiler_params=_params("parallel"),
        name="q_proj",
    )(z, ng.reshape(1, -1), wt, hg_col, cos_t, sin_t)


def _kvproj_kernel(zc_ref, kr_ref, ng_ref, wk_ref, wvt_ref, hg_ref, c_ref, s_ref, k_ref, vt_ref):
    x = zc_ref[...]
    ms = jnp.mean(x * x, axis=-1, keepdims=True)
    xn = (x * lax.rsqrt(ms + EPS) * ng_ref[...]).astype(BF16)
    c = c_ref[...]
    s = s_ref[...]
    hg = hg_ref[...]
    kr = kr_ref[...]
    kr_ss = jnp.sum(kr * kr, axis=-1, keepdims=True)
    kr_rot = _rope(kr * hg[:, 128:], c, s)
    for hp in range(MLA_HEADS // 2):
        y = jnp.dot(xn, wk_ref[:, hp * 256:(hp + 1) * 256], preferred_element_type=F32)
        for sub in range(2):
            lo = (2 * hp + sub) * MLA_HEAD_PAD
            kn = y[:, sub * MLA_NOPE:(sub + 1) * MLA_NOPE]
            ms = (jnp.sum(kn * kn, axis=-1, keepdims=True) + kr_ss) * (1.0 / MLA_QK)
            rs = lax.rsqrt(ms + EPS)
            k_ref[:, lo:lo + 128] = (kn * rs * hg[:, :128]).astype(k_ref.dtype)
            k_ref[:, lo + 128:lo + 256] = (kr_rot * rs).astype(k_ref.dtype)
    rows_per_dot = 4 * VT_ROWS
    for j in range(MLA_HEADS * VT_ROWS // rows_per_dot):
        sl = slice(j * rows_per_dot, (j + 1) * rows_per_dot)
        vt_ref[sl, :] = lax.dot_general(wvt_ref[sl, :], xn, _NT,
                                        preferred_element_type=F32).astype(vt_ref.dtype)
    ones = jnp.ones((VT_ROWS - MLA_V, x.shape[0]), vt_ref.dtype)
    for h in range(MLA_HEADS):
        vt_ref[h * VT_ROWS + MLA_V:(h + 1) * VT_ROWS, :] = ones


def _kvproj(z, zs, ng, wk, wvt, hg, ctab, stab, bm):
    m = z.shape[0]
    nt = ctab.shape[0] // bm
    return pl.pallas_call(
        _kvproj_kernel,
        grid=(m // bm,),
        in_specs=[pl.BlockSpec((bm, MLA_LORA), lambda i: (i, Z_CKV // MLA_LORA)),
                  pl.BlockSpec((bm, 128), lambda i: (i, 1)),
                  pl.BlockSpec((1, MLA_LORA), lambda i: (0, 0)),
                  pl.BlockSpec((MLA_LORA, MLA_HEADS * MLA_NOPE), lambda i: (0, 0)),
                  pl.BlockSpec((MLA_HEADS * VT_ROWS, MLA_LORA), lambda i: (0, 0)),
                  pl.BlockSpec((1, MLA_HEAD_PAD), lambda i: (0, 0)),
                  pl.BlockSpec((bm, 128), lambda i: (i % nt, 0)),
                  pl.BlockSpec((bm, 128), lambda i: (i % nt, 0))],
        out_specs=[pl.BlockSpec((bm, MLA_HEADS * MLA_HEAD_PAD), lambda i: (i, 0)),
                   pl.BlockSpec((MLA_HEADS * VT_ROWS, bm), lambda i: (0, i))],
        out_shape=[jax.ShapeDtypeStruct((m, MLA_HEADS * MLA_HEAD_PAD), BF16),
                   jax.ShapeDtypeStruct((MLA_HEADS * VT_ROWS, m), BF16)],
        compiler_params=_params("parallel"),
        name="kv_proj",
    )(z, zs, ng.reshape(1, -1), wk, wvt, hg, ctab, stab)


def _attn_kernel(qt_ref, k_ref, vt_ref, km_ref, vmt_ref, o_ref, m_sc, acc_sc, s_sc, *, blk, heads,
                 nq):
    i = pl.program_id(2)

    def queries(qi, h):
        return qt_ref[h * MLA_HEAD_PAD:(h + 1) * MLA_HEAD_PAD,
                      pl.ds(pl.multiple_of(qi * blk, blk), blk)]

    for h in range(heads):
        s = jnp.dot(km_ref[:, h * MLA_HEAD_PAD:(h + 1) * MLA_HEAD_PAD], queries(i, h),
                    preferred_element_type=F32)
        m = jnp.max(s, axis=0, keepdims=True)
        p = jnp.exp2(s - m)
        m_sc[h] = m
        acc_sc[h] = jnp.dot(vmt_ref[h * VT_ROWS:(h + 1) * VT_ROWS, :], p.astype(BF16),
                            preferred_element_type=F32)

    def scores(qi, kb, h):
        ksl = pl.ds(pl.multiple_of(kb * blk, blk), blk)
        return jnp.dot(k_ref[ksl, h * MLA_HEAD_PAD:(h + 1) * MLA_HEAD_PAD], queries(qi, h),
                       preferred_element_type=F32)

    @pl.when(i == 0)
    def _():
        for h in range(heads):
            s_sc[h] = scores(0, 0, h)

    def step(kb, masked):
        ksl = pl.ds(pl.multiple_of(kb * blk, blk), blk)
        tiles = [s_sc[h] for h in range(heads)]
        for h in range(heads):
            s = tiles[h]
            if masked:
                s_sc[h] = scores(jnp.minimum(i + 1, nq - 1), 0, h)
                key = lax.broadcasted_iota(jnp.int32, s.shape, 0)
                qry = lax.broadcasted_iota(jnp.int32, s.shape, 1)
                s = jnp.where(key <= qry, s, -1e30)
            else:
                s_sc[h] = scores(i, kb + 1, h)
            m_prev = m_sc[h]
            m_new = jnp.maximum(m_prev, jnp.max(s, axis=0, keepdims=True))
            alpha = jnp.exp2(m_prev - m_new)
            p = jnp.exp2(s - m_new)
            acc_sc[h] = alpha * acc_sc[h] + jnp.dot(
                vt_ref[h * VT_ROWS:(h + 1) * VT_ROWS, ksl], p.astype(BF16),
                preferred_element_type=F32)
            m_sc[h] = m_new

    def body(kb, carry):
        step(kb, False)
        return carry

    lax.fori_loop(0, i, body, 0)
    step(i, True)
    for h in range(heads):
        acc = acc_sc[h]
        o_ref[:, h * MLA_V:(h + 1) * MLA_V] = (
            acc[:MLA_V] * (1.0 / acc[MLA_V:MLA_V + 1])).T.astype(o_ref.dtype)


def _attention(q, k, vt, km, vmt, blk, heads):
    nq = SEQ // blk
    kern = functools.partial(_attn_kernel, blk=blk, heads=heads, nq=nq)
    return pl.pallas_call(
        kern,
        grid=(BATCH, MLA_HEADS // heads, nq),
        in_specs=[pl.BlockSpec((heads * MLA_HEAD_PAD, SEQ), lambda b, h, i: (h, b)),
                  pl.BlockSpec((SEQ, heads * MLA_HEAD_PAD), lambda b, h, i: (b, h)),
                  pl.BlockSpec((heads * VT_ROWS, SEQ), lambda b, h, i: (h, b)),
                  pl.BlockSpec((N_META, heads * MLA_HEAD_PAD), lambda b, h, i: (0, h)),
                  pl.BlockSpec((heads * VT_ROWS, N_META), lambda b, h, i: (h, 0))],
        out_specs=pl.BlockSpec((blk, heads * MLA_V), lambda b, h, i: (b * nq + i, h)),
        out_shape=jax.ShapeDtypeStruct((ROWS, MLA_HEADS * MLA_V), BF16),
        scratch_shapes=[pltpu.VMEM((heads, 1, blk), F32),
                        pltpu.VMEM((heads, VT_ROWS, blk), F32), pltpu.VMEM((heads, blk, blk), F32)],
        compiler_params=_params("parallel", "parallel", "arbitrary"),
        name="mla_attention",
    )(q, k, vt, km, vmt)


def _meta_attn_kernel(q_ref, km_ref, vmt_ref, o_ref):
    s = lax.dot_general(q_ref[...], km_ref[...], _NT, preferred_element_type=F32)
    row = lax.broadcasted_iota(jnp.int32, s.shape, 0)
    col = lax.broadcasted_iota(jnp.int32, s.shape, 1)
    s = jnp.where(col <= row, s, -1e30)
    m = jnp.max(s, axis=-1, keepdims=True)
    p = jnp.exp2(s - m)
    l = jnp.sum(p, axis=-1, keepdims=True)
    o = lax.dot_general(p.astype(BF16), vmt_ref[:MLA_V, :], _NT, preferred_element_type=F32)
    o_ref[...] = (o / l).astype(o_ref.dtype)


def _meta_attention(qm, kmp, vmtp):
    return pl.pallas_call(
        _meta_attn_kernel,
        grid=(MLA_HEADS,),
        in_specs=[pl.BlockSpec((N_META, MLA_HEAD_PAD), lambda h: (0, h)),
                  pl.BlockSpec((128, MLA_HEAD_PAD), lambda h: (0, h)),
                  pl.BlockSpec((VT_ROWS, 128), lambda h: (h, 0))],
        out_specs=pl.BlockSpec((N_META, MLA_V), lambda h: (0, h)),
        out_shape=jax.ShapeDtypeStruct((N_META, MLA_HEADS * MLA_V), BF16),
        compiler_params=_params("parallel"),
        name="meta_attention",
    )(qm, kmp, vmtp)


def _merge_kernel(a_ref, b_ref, wa_ref, wb_ref, za_ref, zb_ref, o_ref):
    ga = _sigmoid(za_ref[...])
    gb = _sigmoid(zb_ref[...])
    ya = jnp.dot(a_ref[...], wa_ref[...], preferred_element_type=F32)
    yb = jnp.dot(b_ref[...], wb_ref[...], preferred_element_type=F32)
    o_ref[...] = (ga * ya + gb * yb).astype(o_ref.dtype)


def _merge(a, b, wa3, wb3, layer, z, bm, bn):
    m = a.shape[0]
    return pl.pallas_call(
        _merge_kernel,
        grid=(m // bm, D_MODEL // bn),
        in_specs=[pl.BlockSpec((bm, D_MODEL), lambda i, j: (i, 0)),
                  pl.BlockSpec((bm, D_MODEL), lambda i, j: (i, 0)),
                  pl.BlockSpec((None, D_MODEL, bn), lambda i, j: (layer, 0, j)),
                  pl.BlockSpec((None, D_MODEL, bn), lambda i, j: (layer, 0, j)),
                  pl.BlockSpec((bm, bn), lambda i, j: (i, Z_A // bn + j)),
                  pl.BlockSpec((bm, bn), lambda i, j: (i, Z_B // bn + j))],
        out_specs=pl.BlockSpec((bm, bn), lambda i, j: (i, j)),
        out_shape=jax.ShapeDtypeStruct((m, D_MODEL), BF16),
        compiler_params=_params("parallel", "parallel"),
        name="branch_merge",
    )(a, b, wa3, wb3, z, z)


def _outproj_kernel(m_ref, w_ref, h_ref, g_ref, ho_ref, hn_ref):
    hn = h_ref[...] + jnp.dot(m_ref[...], w_ref[...], preferred_element_type=F32)
    ho_ref[...] = hn
    ms = jnp.mean(hn * hn, axis=-1, keepdims=True)
    hn_ref[...] = (hn * lax.rsqrt(ms + EPS) * g_ref[...]).astype(hn_ref.dtype)


def _outproj(mg, w3, layer, h, g, bm):
    m = mg.shape[0]
    return pl.pallas_call(
        _outproj_kernel,
        grid=(m // bm,),
        in_specs=[pl.BlockSpec((bm, D_MODEL), lambda i: (i, 0)),
                  pl.BlockSpec((None, D_MODEL, D_MODEL), lambda i: (layer, 0, 0)),
                  pl.BlockSpec((bm, D_MODEL), lambda i: (i, 0)),
                  pl.BlockSpec((1, D_MODEL), lambda i: (0, 0))],
        out_specs=[pl.BlockSpec((bm, D_MODEL), lambda i: (i, 0)),
                   pl.BlockSpec((bm, D_MODEL), lambda i: (i, 0))],
        out_shape=[jax.ShapeDtypeStruct((m, D_MODEL), F32),
                   jax.ShapeDtypeStruct((m, D_MODEL), BF16)],
        compiler_params=_params("parallel"),
        name="out_proj",
    )(mg, w3, h, g.reshape(1, -1))


def _gateup_kernel(*refs, with_meta):
    if with_meta:
        h_ref, hm_ref, wg_ref, wu_ref, o_ref, om_ref, wgb_ref, wub_ref = refs
    else:
        h_ref, wg_ref, wu_ref, o_ref, wgb_ref, wub_ref = refs

    def swiglu(h):
        g = jnp.dot(h, wgb_ref[...], preferred_element_type=F32)
        u = jnp.dot(h, wub_ref[...], preferred_element_type=F32)
        return (g * _sigmoid(g) * u).astype(o_ref.dtype)

    @pl.when(pl.program_id(1) == 0)
    def _():
        wgb_ref[...] = wg_ref[...].astype(BF16)
        wub_ref[...] = wu_ref[...].astype(BF16)
        if with_meta:
            om_ref[...] = swiglu(hm_ref[...])

    o_ref[...] = swiglu(h_ref[...])


def _gateup(h, h_meta, w3, layer, bm, bn):
    m = h.shape[0]
    nb = FF_HIDDEN // bn
    with_meta = h_meta is not None
    weight_specs = [pl.BlockSpec((None, D_MODEL, bn), lambda j, i: (layer, 0, j)),
                    pl.BlockSpec((None, D_MODEL, bn), lambda j, i: (layer, 0, nb + j))]
    in_specs = [pl.BlockSpec((bm, D_MODEL), lambda j, i: (i, 0))]
    operands = [h]
    out_specs = [pl.BlockSpec((bm, bn), lambda j, i: (i, j))]
    out_shape = [jax.ShapeDtypeStruct((m, FF_HIDDEN), BF16)]
    if with_meta:
        in_specs.append(pl.BlockSpec((N_META, D_MODEL), lambda j, i: (0, 0)))
        operands.append(h_meta)
        out_specs.append(pl.BlockSpec((N_META, bn), lambda j, i: (0, j)))
        out_shape.append(jax.ShapeDtypeStruct((N_META, FF_HIDDEN), BF16))
    return pl.pallas_call(
        functools.partial(_gateup_kernel, with_meta=with_meta),
        grid=(nb, m // bm),
        in_specs=in_specs + weight_specs,
        out_specs=out_specs,
        out_shape=out_shape,
        scratch_shapes=[pltpu.VMEM((D_MODEL, bn), BF16), pltpu.VMEM((D_MODEL, bn), BF16)],
        compiler_params=_params("arbitrary", "arbitrary"),
        name="gate_up",
    )(*operands, w3, w3)


def _down_kernel(a_ref, w_ref, h_ref, *rest, emit_norm):
    if emit_norm:
        g_ref, ho_ref, hn_ref, acc_ref = rest
    else:
        ho_ref, acc_ref = rest
    k = pl.program_id(1)
    last = pl.num_programs(1) - 1

    def product():
        return jnp.dot(a_ref[...], w_ref[...], preferred_element_type=F32)

    @pl.when(k == 0)
    def _():
        acc_ref[...] = product()

    @pl.when((k > 0) & (k < last))
    def _():
        acc_ref[...] += product()

    @pl.when(k == last)
    def _():
        hn = h_ref[...] + (acc_ref[...] + product())
        ho_ref[...] = hn
        if emit_norm:
            ms = jnp.mean(hn * hn, axis=-1, keepdims=True)
            hn_ref[...] = (hn * lax.rsqrt(ms + EPS) * g_ref[...]).astype(hn_ref.dtype)


def _down(a, w3, layer, h, g, bm, bk):
    m = a.shape[0]
    assert FF_HIDDEN // bk >= 2
    emit_norm = g is not None
    row_block = pl.BlockSpec((bm, D_MODEL), lambda i, k: (i, 0))
    in_specs = [pl.BlockSpec((bm, bk), lambda i, k: (i, k)),
                pl.BlockSpec((None, bk, D_MODEL), lambda i, k: (layer, k, 0)),
                row_block]
    operands = [a, w3, h]
    out_specs = [row_block]
    out_shape = [jax.ShapeDtypeStruct((m, D_MODEL), F32)]
    if emit_norm:
        in_specs.append(pl.BlockSpec((1, D_MODEL), lambda i, k: (0, 0)))
        operands.append(g.reshape(1, -1))
        out_specs.append(row_block)
        out_shape.append(jax.ShapeDtypeStruct((m, D_MODEL), BF16))
    return pl.pallas_call(
        functools.partial(_down_kernel, emit_norm=emit_norm),
        grid=(m // bm, FF_HIDDEN // bk),
        in_specs=in_specs,
        out_specs=out_specs,
        out_shape=out_shape,
        scratch_shapes=[pltpu.VMEM((bm, D_MODEL), F32)],
        compiler_params=_params("parallel", "arbitrary"),
        name="down_proj",
    )(*operands)


def _rope_layout(t):
    zeros = jnp.zeros(t.shape[:-1] + (32,), t.dtype)
    return jnp.concatenate([t[..., :32], zeros, t[..., 32:], zeros], axis=-1)


def _head_layout(t):
    return jnp.concatenate([t[..., :MLA_NOPE], _rope_layout(t[..., MLA_NOPE:])], axis=-1)


def _rope_tables():
    length = N_META + SEQ
    inv = 1.0 / (ROPE_THETA ** (jnp.arange(0, MLA_ROPE, 2, dtype=F32) / MLA_ROPE))
    ang = jnp.arange(length, dtype=F32)[:, None] * inv[None, :]
    cos, sin = jnp.cos(ang), jnp.sin(ang)
    zeros = jnp.zeros_like(cos)
    ctab = jnp.concatenate([cos, zeros, cos, zeros], axis=-1)
    stab = jnp.concatenate([-sin, zeros, sin, zeros], axis=-1)
    return ctab, stab, cos.T, sin.T


def kernel(x, meta_tokens, norm1_g, w_in, gla_gate_w2, gla_gate_b, gla_onorm_g, w_branch_a,
           q_a_norm_g, w_uq, kv_a_norm_g, w_ukv, q_norm_g, k_norm_g, w_branch_b, w_out,
           norm2_g, w_gate_up, w_down):
    ctab, stab, cos_t, sin_t = _rope_tables()
    ctab_m, stab_m = ctab[:N_META], stab[:N_META]
    ctab_r, stab_r = ctab[N_META:], stab[N_META:]
    cos_tm, sin_tm = cos_t[:, :N_META], sin_t[:, :N_META]
    cos_tr, sin_tr = cos_t[:, N_META:], sin_t[:, N_META:]

    h_res = x.reshape(ROWS, D_MODEL)
    hm_res = meta_tokens.astype(F32)
    h = _rms(h_res, norm1_g[0], RMS_ROWS)
    hm = _rms(hm_res, norm1_g[0], N_META)
    s_zero = jnp.zeros((GLA_HEADS, GLA_DK, GLA_DV), F32)

    w_in_t = jnp.swapaxes(w_in, 1, 2)
    w_kr = w_in_t[:, 7184:7248]
    zrow = jnp.zeros((DEPTH, 32, D_MODEL), F32)
    w_small = jnp.concatenate(
        [w_in_t[:, 6144:6160], jnp.zeros((DEPTH, 128 - GLA_GATE_RANK, D_MODEL), F32),
         w_kr[:, :32], zrow, w_kr[:, 32:], zrow], axis=1)
    wa = w_branch_a.astype(BF16)
    wb = w_branch_b.astype(BF16)
    wo = w_out.astype(BF16)
    wd = w_down.astype(BF16)

    for l in range(DEPTH):
        last = l == DEPTH - 1
        w2p = jnp.concatenate(
            [gla_gate_w2[l], jnp.zeros((128 - GLA_GATE_RANK, GLA_HEADS * GLA_DK), F32)],
            axis=0).astype(BF16)
        wqt = _head_layout(w_uq[l].reshape(MLA_LORA, MLA_HEADS, MLA_QK)).reshape(
            MLA_LORA, MLA_HEADS * MLA_HEAD_PAD).T.astype(BF16)
        wkv3 = w_ukv[l].reshape(MLA_LORA, MLA_HEADS, MLA_NOPE + MLA_V)
        wk = wkv3[:, :, :MLA_NOPE].reshape(MLA_LORA, MLA_HEADS * MLA_NOPE).astype(BF16)
        wvt = jnp.pad(jnp.transpose(wkv3[:, :, MLA_NOPE:], (1, 2, 0)),
                      ((0, 0), (0, VT_ROWS - MLA_V), (0, 0))).reshape(
            MLA_HEADS * VT_ROWS, MLA_LORA).astype(BF16)
        qg = (_head_layout(q_norm_g[l]) * (MLA_QK ** -0.5 * LOG2E)).reshape(MLA_HEAD_PAD, 1)
        kg = _head_layout(k_norm_g[l]).reshape(1, MLA_HEAD_PAD)

        z, zm = _in_proj(h, hm, w_in_t, l, *IN_PROJ_BLOCK, name="in_proj")
        zs = _matmul_nt(h, w_small, l, IN_PROJ_SMALL_ROWS, N_SMALL, name="in_proj_small")
        zsm = _matmul_nt(hm, w_small, l, N_META, N_SMALL, name="in_proj_small_meta")

        gla_m, s_meta = _gla(zm, zsm, w2p, gla_gate_b[l], gla_onorm_g[l], s_zero,
                             batch=1, tokens=N_META, block=N_META, chunk=N_META, emit_state=True)
        (gla_r,) = _gla(z, zs, w2p, gla_gate_b[l], gla_onorm_g[l], s_meta,
                        batch=BATCH, tokens=SEQ, block=GLA_TOKENS, chunk=GLA_CHUNK, emit_state=False)

        km, vmt = _kvproj(zm, zsm, kv_a_norm_g[l], wk, wvt, kg, ctab_m, stab_m, N_META)
        kr, vrt = _kvproj(z, zs, kv_a_norm_g[l], wk, wvt, kg, ctab_r, stab_r, PROJ_ROWS)
        kmp = jnp.pad(km, ((0, 128 - N_META), (0, 0)))
        vmtp = jnp.pad(vmt, ((0, 0), (0, 128 - N_META)))
        qr = _qproj(z, q_a_norm_g[l], wqt, qg, cos_tr, sin_tr, PROJ_ROWS)
        att_r = _attention(qr, kr, vrt, km, vmt, ATTN_BLOCK, ATTN_HEADS)

        merged = _merge(gla_r, att_r, wa, wb, l, z, *MERGE_BLOCK)
        h_res, h2 = _outproj(merged, wo, l, h_res, norm2_g[l], OUT_PROJ_ROWS)
        if last:
            (act,) = _gateup(h2, None, w_gate_up, l, *GATE_UP_BLOCK)
            (h_res,) = _down(act, wd, l, h_res, None, *DOWN_BLOCK)
        else:
            qm = _qproj(zm, q_a_norm_g[l], wqt, qg, cos_tm, sin_tm, N_META)
            att_m = _meta_attention(qm.T, kmp, vmtp)
            merged_m = _merge(gla_m, att_m, wa, wb, l, zm, N_META, MERGE_BLOCK[1])
            hm_res, hm2 = _outproj(merged_m, wo, l, hm_res, norm2_g[l], N_META)
            act, act_m = _gateup(h2, hm2, w_gate_up, l, *GATE_UP_BLOCK)
            h_res, h = _down(act, wd, l, h_res, norm1_g[l + 1], *DOWN_BLOCK)
            hm_res, hm = _down(act_m, wd, l, hm_res, norm1_g[l + 1], N_META, DOWN_META_K)

    return h_res.reshape(BATCH, SEQ, D_MODEL)
```

```python
import functools

import jax
import jax.numpy as jnp
from jax import lax
from jax.experimental import pallas as pl
from jax.experimental.pallas import tpu as pltpu

D_MODEL = 2048
BATCH = 4
SEQ = 4096
DEPTH = 2
N_META = 16
ROWS = BATCH * SEQ

GLA_HEADS = 4
GLA_DK = 256
GLA_DV = 512
GLA_GATE_RANK = 16
GLA_GATE_TAU = 16.0
GLA_CHUNK = 64

MLA_HEADS = 16
MLA_LORA = 512
MLA_NOPE = 128
MLA_ROPE = 64
MLA_QK = 192
MLA_V = 128
VT_ROWS = 144
MLA_HEAD_PAD = 256
ROPE_THETA = 10000.0
FF_HIDDEN = 5632
EPS = 1e-6
LOG2E = 1.4426950408889634

W_IN_ZA = 6144
W_IN_C, W_IN_C_END = 6160, 7184
W_IN_G = 7248
Z_Q, Z_K, Z_V, Z_R, Z_CQ, Z_CKV, Z_A, Z_B = 0, 1024, 2048, 4096, 6144, 6656, 7168, 9216
N_Z = 11264
N_SMALL = 256

RMS_ROWS = 512
IN_PROJ_BLOCK = (1024, 1024)
IN_PROJ_SMALL_ROWS = 2048
GLA_TOKENS = 512
PROJ_ROWS = 1024
ATTN_BLOCK = 512
ATTN_HEADS = 4
MERGE_BLOCK = (1024, 512)
OUT_PROJ_ROWS = 512
GATE_UP_BLOCK = (1024, 512)
DOWN_BLOCK = (512, 2816)
DOWN_META_K = 1408

VMEM_LIMIT = 56 * 1024 * 1024
BF16 = jnp.bfloat16
F32 = jnp.float32

_NT = (((1,), (1,)), ((), ()))
_TN = (((0,), (0,)), ((), ()))


def _params(*sem):
    return pltpu.CompilerParams(dimension_semantics=sem, vmem_limit_bytes=VMEM_LIMIT)


def _sigmoid(x):
    return 1.0 / (1.0 + jnp.exp(-x))


def _rms_kernel(x_ref, g_ref, o_ref):
    x = x_ref[...]
    ms = jnp.mean(x * x, axis=-1, keepdims=True)
    o_ref[...] = (x * lax.rsqrt(ms + EPS) * g_ref[...]).astype(o_ref.dtype)


def _rms(x, g, bm):
    m, d = x.shape
    return pl.pallas_call(
        _rms_kernel,
        grid=(m // bm,),
        in_specs=[pl.BlockSpec((bm, d), lambda i: (i, 0)),
                  pl.BlockSpec((1, d), lambda i: (0, 0))],
        out_specs=pl.BlockSpec((bm, d), lambda i: (i, 0)),
        out_shape=jax.ShapeDtypeStruct((m, d), BF16),
        compiler_params=_params("parallel"),
        name="rmsnorm",
    )(x, g.reshape(1, d))


def _mm_nt_kernel(a_ref, wt_ref, o_ref):
    o_ref[...] = lax.dot_general(a_ref[...], wt_ref[...].astype(BF16), _NT,
                                 preferred_element_type=F32)


def _matmul_nt(a, wt3, layer, bm, bn, name):
    m, k = a.shape
    n = wt3.shape[1]
    return pl.pallas_call(
        _mm_nt_kernel,
        grid=(m // bm, n // bn),
        in_specs=[pl.BlockSpec((bm, k), lambda i, j: (i, 0)),
                  pl.BlockSpec((None, bn, k), lambda i, j: (layer, j, 0))],
        out_specs=pl.BlockSpec((bm, bn), lambda i, j: (i, j)),
        out_shape=jax.ShapeDtypeStruct((m, n), F32),
        compiler_params=_params("parallel", "parallel"),
        name=name,
    )(a, wt3)


def _mm_nt_wstat_kernel(a_ref, am_ref, wt_ref, o_ref, om_ref, wb_ref):
    @pl.when(pl.program_id(1) == 0)
    def _():
        wb_ref[...] = wt_ref[0].astype(BF16)
        om_ref[...] = lax.dot_general(am_ref[...], wb_ref[...], _NT, preferred_element_type=F32)

    o_ref[...] = lax.dot_general(a_ref[...], wb_ref[...], _NT, preferred_element_type=F32)


def _in_proj_row0(j, bn):
    n_a = W_IN_ZA // bn
    n_c = (W_IN_C_END - W_IN_C) // bn
    return jnp.where(j < n_a, j * bn,
                     jnp.where(j < n_a + n_c, W_IN_C + (j - n_a) * bn,
                               W_IN_G + (j - n_a - n_c) * bn))


def _in_proj(a, a_meta, wt3, layer, bm, bn, name):
    m, k = a.shape
    return pl.pallas_call(
        _mm_nt_wstat_kernel,
        grid=(N_Z // bn, m // bm),
        in_specs=[pl.BlockSpec((bm, k), lambda j, i: (i, 0)),
                  pl.BlockSpec((N_META, k), lambda j, i: (0, 0)),
                  pl.BlockSpec((pl.Element(1), pl.Element(bn), pl.Element(k)),
                               lambda j, i: (layer, pl.multiple_of(_in_proj_row0(j, bn), 16), 0))],
        out_specs=[pl.BlockSpec((bm, bn), lambda j, i: (i, j)),
                   pl.BlockSpec((N_META, bn), lambda j, i: (0, j))],
        out_shape=[jax.ShapeDtypeStruct((m, N_Z), F32),
                   jax.ShapeDtypeStruct((N_META, N_Z), F32)],
        scratch_shapes=[pltpu.VMEM((bn, k), BF16)],
        compiler_params=_params("arbitrary", "arbitrary"),
        name=name,
    )(a, a_meta, wt3)


def _gla_kernel(q_ref, k_ref, v_ref, r_ref, zlr_ref, w2_ref, gb_ref, og_ref, s0_ref,
                o_ref, *rest, chunk, n_chunks, emit_state):
    if emit_state:
        sfin_ref, st_ref, b_sc = rest
    else:
        st_ref, b_sc = rest
    t = pl.program_id(1)
    heads = range(GLA_HEADS)

    @pl.when(t == 0)
    def _():
        st_ref[...] = s0_ref[...]

    row = lax.broadcasted_iota(jnp.int32, (chunk, chunk), 0)
    col = lax.broadcasted_iota(jnp.int32, (chunk, chunk), 1)
    causal = col <= row
    tri = jnp.where(causal, 1.0, 0.0).astype(BF16)
    w2 = w2_ref[...]
    gb = gb_ref[...]
    og = og_ref[...]
    kdim = GLA_HEADS * GLA_DK

    def log_gate_split(c):
        sl = pl.ds(pl.multiple_of(c * chunk, chunk), chunk)
        logit = jnp.dot(zlr_ref[sl, :].astype(BF16), w2, preferred_element_type=F32) + gb
        g = (jnp.minimum(logit, 0.0) - jnp.log1p(jnp.exp(-jnp.abs(logit)))) * (1.0 / GLA_GATE_TAU)
        g_hi = g.astype(BF16)
        g_lo = (g - g_hi.astype(F32)).astype(BF16)
        return jnp.concatenate([g_hi, g_lo], axis=1)

    def log_decay(g_split):
        cs = jnp.dot(tri, g_split, preferred_element_type=F32)
        return cs[:, :kdim] + cs[:, kdim:]

    b_sc[...] = log_decay(log_gate_split(0))

    def body(c, carry):
        sl = pl.ds(pl.multiple_of(c * chunk, chunk), chunk)
        g_split_next = log_gate_split(jnp.minimum(c + 1, n_chunks - 1))
        r = r_ref[sl, :]
        out_gate = r * _sigmoid(r)
        st = [st_ref[h] for h in heads]
        st_bf = [st[h].astype(BF16) for h in heads]
        b = b_sc[...]
        b_last = b[chunk - 1:chunk, :]
        eb = jnp.exp(b)
        enb = jnp.exp(-b)
        erel = jnp.exp(b_last - b)
        elast = jnp.exp(b_last)
        k = k_ref[sl, :]
        qd = (q_ref[sl, :] * (GLA_DK ** -0.5) * eb).astype(BF16)
        kd = (k * enb).astype(BF16)
        k2 = (k * erel).astype(BF16)
        v = v_ref[sl, :].astype(BF16)

        def hk(x, h):
            return x[:, h * GLA_DK:(h + 1) * GLA_DK]

        def hv(x, h):
            return x[:, h * GLA_DV:(h + 1) * GLA_DV]

        a = [lax.dot_general(hk(qd, h), hk(kd, h), _NT, preferred_element_type=F32) for h in heads]
        o_state = [jnp.dot(hk(qd, h), st_bf[h], preferred_element_type=F32) for h in heads]
        o_local = [jnp.dot(jnp.where(causal, a[h], 0.0).astype(BF16), hv(v, h),
                           preferred_element_type=F32) for h in heads]
        b_sc[...] = log_decay(g_split_next)
        upd = [lax.dot_general(hk(k2, h), hv(v, h), _TN, preferred_element_type=F32)
               for h in heads]
        decay_col = jnp.broadcast_to(elast, (128, kdim)).T
        for h in heads:
            col = decay_col[h * GLA_DK:(h + 1) * GLA_DK]
            st_ref[h] = st[h] * jnp.concatenate([col] * (GLA_DV // 128), axis=1) + upd[h]
        for h in heads:
            o = o_local[h] + o_state[h]
            ms = jnp.mean(o * o, axis=-1, keepdims=True)
            on = o * lax.rsqrt(ms + EPS) * og
            o_ref[sl, h * GLA_DV:(h + 1) * GLA_DV] = (on * hv(out_gate, h)).astype(o_ref.dtype)
        return carry

    lax.fori_loop(0, n_chunks, body, 0)

    if emit_state:
        @pl.when(t == pl.num_programs(1) - 1)
        def _():
            sfin_ref[...] = st_ref[...]


def _gla(z, zs, w2p, gate_b, onorm_g, s0, *, batch, tokens, block, chunk, emit_state):
    nt = tokens // block
    rows = batch * tokens
    kern = functools.partial(_gla_kernel, chunk=chunk, n_chunks=block // chunk,
                             emit_state=emit_state)
    kdim = GLA_HEADS * GLA_DK
    vdim = GLA_HEADS * GLA_DV
    in_specs = [
        pl.BlockSpec((block, kdim), lambda b, t: (b * nt + t, Z_Q // kdim)),
        pl.BlockSpec((block, kdim), lambda b, t: (b * nt + t, Z_K // kdim)),
        pl.BlockSpec((block, vdim), lambda b, t: (b * nt + t, Z_V // vdim)),
        pl.BlockSpec((block, vdim), lambda b, t: (b * nt + t, Z_R // vdim)),
        pl.BlockSpec((block, 128), lambda b, t: (b * nt + t, 0)),
        pl.BlockSpec((128, kdim), lambda b, t: (0, 0)),
        pl.BlockSpec((1, kdim), lambda b, t: (0, 0)),
        pl.BlockSpec((1, GLA_DV), lambda b, t: (0, 0)),
        pl.BlockSpec((GLA_HEADS, GLA_DK, GLA_DV), lambda b, t: (0, 0, 0)),
    ]
    out_specs = [pl.BlockSpec((block, vdim), lambda b, t: (b * nt + t, 0))]
    out_shape = [jax.ShapeDtypeStruct((rows, vdim), BF16)]
    if emit_state:
        out_specs.append(pl.BlockSpec((GLA_HEADS, GLA_DK, GLA_DV), lambda b, t: (b, 0, 0)))
        out_shape.append(jax.ShapeDtypeStruct((batch * GLA_HEADS, GLA_DK, GLA_DV), F32))
    res = pl.pallas_call(
        kern,
        grid=(batch, nt),
        in_specs=in_specs,
        out_specs=out_specs,
        out_shape=out_shape,
        scratch_shapes=[pltpu.VMEM((GLA_HEADS, GLA_DK, GLA_DV), F32),
                        pltpu.VMEM((chunk, kdim), F32)],
        compiler_params=_params("parallel", "arbitrary"),
        name="gla",
    )(z, z, z, z, zs, w2p, gate_b.reshape(1, -1), onorm_g.reshape(1, -1), s0)
    return res


def _rope(x, c, s):
    return x * c + pltpu.roll(x, 64, 1) * s


def _qproj_kernel(zc_ref, ng_ref, wt_ref, hg_ref, c_ref, s_ref, o_ref, *, heads_per_dot):
    x = zc_ref[...]
    ms = jnp.mean(x * x, axis=-1, keepdims=True)
    xn = (x * lax.rsqrt(ms + EPS) * ng_ref[...]).astype(BF16)
    bm = x.shape[0]
    c = c_ref[...]
    s = s_ref[...]
    hg = jnp.broadcast_to(hg_ref[...], (MLA_HEAD_PAD, bm))
    for hc in range(MLA_HEADS // heads_per_dot):
        base = hc * heads_per_dot * MLA_HEAD_PAD
        yt = lax.dot_general(wt_ref[base:base + heads_per_dot * MLA_HEAD_PAD, :], xn, _NT,
                             preferred_element_type=F32)
        for j in range(heads_per_dot):
            lo = base + j * MLA_HEAD_PAD
            y = yt[j * MLA_HEAD_PAD:(j + 1) * MLA_HEAD_PAD]
            ms = jnp.sum(y * y, axis=0, keepdims=True) * (1.0 / MLA_QK)
            yn = y * lax.rsqrt(ms + EPS) * hg
            x1 = yn[128:160]
            x2 = yn[192:224]
            o_ref[lo:lo + 128, :] = yn[:128].astype(o_ref.dtype)
            o_ref[lo + 128:lo + 160, :] = (x1 * c - x2 * s).astype(o_ref.dtype)
            o_ref[lo + 160:lo + 192, :] = yn[160:192].astype(o_ref.dtype)
            o_ref[lo + 192:lo + 224, :] = (x2 * c + x1 * s).astype(o_ref.dtype)
            o_ref[lo + 224:lo + 256, :] = yn[224:256].astype(o_ref.dtype)


def _qproj(z, ng, wt, hg_col, cos_t, sin_t, bm):
    m = z.shape[0]
    nt = cos_t.shape[1] // bm
    kern = functools.partial(_qproj_kernel, heads_per_dot=1)
    return pl.pallas_call(
        kern,
        grid=(m // bm,),
        in_specs=[pl.BlockSpec((bm, MLA_LORA), lambda i: (i, Z_CQ // MLA_LORA)),
                  pl.BlockSpec((1, MLA_LORA), lambda i: (0, 0)),
                  pl.BlockSpec((MLA_HEADS * MLA_HEAD_PAD, MLA_LORA), lambda i: (0, 0)),
                  pl.BlockSpec((MLA_HEAD_PAD, 1), lambda i: (0, 0)),
                  pl.BlockSpec((MLA_ROPE // 2, bm), lambda i: (0, i % nt)),
                  pl.BlockSpec((MLA_ROPE // 2, bm), lambda i: (0, i % nt))],
        out_specs=pl.BlockSpec((MLA_HEADS * MLA_HEAD_PAD, bm), lambda i: (0, i)),
        out_shape=jax.ShapeDtypeStruct((MLA_HEADS * MLA_HEAD_PAD, m), BF16),
        compiler_params=_params("parallel"),
        name="q_proj",
    )(z, ng.reshape(1, -1), wt, hg_col, cos_t, sin_t)


def _kvproj_kernel(zc_ref, kr_ref, ng_ref, wk_ref, wvt_ref, hg_ref, c_ref, s_ref, k_ref, vt_ref):
    x = zc_ref[...]
    ms = jnp.mean(x * x, axis=-1, keepdims=True)
    xn = (x * lax.rsqrt(ms + EPS) * ng_ref[...]).astype(BF16)
    c = c_ref[...]
    s = s_ref[...]
    hg = hg_ref[...]
    kr = kr_ref[...]
    kr_ss = jnp.sum(kr * kr, axis=-1, keepdims=True)
    kr_rot = _rope(kr * hg[:, 128:], c, s)
    for hp in range(MLA_HEADS // 2):
        y = jnp.dot(xn, wk_ref[:, hp * 256:(hp + 1) * 256], preferred_element_type=F32)
        for sub in range(2):
            lo = (2 * hp + sub) * MLA_HEAD_PAD
            kn = y[:, sub * MLA_NOPE:(sub + 1) * MLA_NOPE]
            ms = (jnp.sum(kn * kn, axis=-1, keepdims=True) + kr_ss) * (1.0 / MLA_QK)
            rs = lax.rsqrt(ms + EPS)
            k_ref[:, lo:lo + 128] = (kn * rs * hg[:, :128]).astype(k_ref.dtype)
            k_ref[:, lo + 128:lo + 256] = (kr_rot * rs).astype(k_ref.dtype)
    rows_per_dot = 4 * VT_ROWS
    for j in range(MLA_HEADS * VT_ROWS // rows_per_dot):
        sl = slice(j * rows_per_dot, (j + 1) * rows_per_dot)
        vt_ref[sl, :] = lax.dot_general(wvt_ref[sl, :], xn, _NT,
                                        preferred_element_type=F32).astype(vt_ref.dtype)
    ones = jnp.ones((VT_ROWS - MLA_V, x.shape[0]), vt_ref.dtype)
    for h in range(MLA_HEADS):
        vt_ref[h * VT_ROWS + MLA_V:(h + 1) * VT_ROWS, :] = ones


def _kvproj(z, zs, ng, wk, wvt, hg, ctab, stab, bm):
    m = z.shape[0]
    nt = ctab.shape[0] // bm
    return pl.pallas_call(
        _kvproj_kernel,
        grid=(m // bm,),
        in_specs=[pl.BlockSpec((bm, MLA_LORA), lambda i: (i, Z_CKV // MLA_LORA)),
                  pl.BlockSpec((bm, 128), lambda i: (i, 1)),
                  pl.BlockSpec((1, MLA_LORA), lambda i: (0, 0)),
                  pl.BlockSpec((MLA_LORA, MLA_HEADS * MLA_NOPE), lambda i: (0, 0)),
                  pl.BlockSpec((MLA_HEADS * VT_ROWS, MLA_LORA), lambda i: (0, 0)),
                  pl.BlockSpec((1, MLA_HEAD_PAD), lambda i: (0, 0)),
                  pl.BlockSpec((bm, 128), lambda i: (i % nt, 0)),
                  pl.BlockSpec((bm, 128), lambda i: (i % nt, 0))],
        out_specs=[pl.BlockSpec((bm, MLA_HEADS * MLA_HEAD_PAD), lambda i: (i, 0)),
                   pl.BlockSpec((MLA_HEADS * VT_ROWS, bm), lambda i: (0, i))],
        out_shape=[jax.ShapeDtypeStruct((m, MLA_HEADS * MLA_HEAD_PAD), BF16),
                   jax.ShapeDtypeStruct((MLA_HEADS * VT_ROWS, m), BF16)],
        compiler_params=_params("parallel"),
        name="kv_proj",
    )(z, zs, ng.reshape(1, -1), wk, wvt, hg, ctab, stab)


def _attn_kernel(qt_ref, k_ref, vt_ref, km_ref, vmt_ref, o_ref, m_sc, acc_sc, s_sc, *, blk, heads,
                 nq):
    i = pl.program_id(2)

    def queries(qi, h):
        return qt_ref[h * MLA_HEAD_PAD:(h + 1) * MLA_HEAD_PAD,
                      pl.ds(pl.multiple_of(qi * blk, blk), blk)]

    s_meta = [jnp.dot(km_ref[:, h * MLA_HEAD_PAD:(h + 1) * MLA_HEAD_PAD], queries(i, h),
                      preferred_element_type=F32) for h in range(heads)]
    p_meta = []
    for h in range(heads):
        m = jnp.max(s_meta[h], axis=0, keepdims=True)
        p_meta.append(jnp.exp2(s_meta[h] - m).astype(BF16))
        m_sc[h] = m
    for h in range(heads):
        acc_sc[h] = jnp.dot(vmt_ref[h * VT_ROWS:(h + 1) * VT_ROWS, :], p_meta[h],
                            preferred_element_type=F32)

    def scores(qi, kb, h):
        ksl = pl.ds(pl.multiple_of(kb * blk, blk), blk)
        return jnp.dot(k_ref[ksl, h * MLA_HEAD_PAD:(h + 1) * MLA_HEAD_PAD], queries(qi, h),
                       preferred_element_type=F32)

    @pl.when(i == 0)
    def _():
        for h in range(heads):
            s_sc[h] = scores(0, 0, h)

    def step(kb, masked):
        ksl = pl.ds(pl.multiple_of(kb * blk, blk), blk)
        tiles = [s_sc[h] for h in range(heads)]
        for h in range(heads):
            s = tiles[h]
            if masked:
                s_sc[h] = scores(jnp.minimum(i + 1, nq - 1), 0, h)
                key = lax.broadcasted_iota(jnp.int32, s.shape, 0)
                qry = lax.broadcasted_iota(jnp.int32, s.shape, 1)
                s = jnp.where(key <= qry, s, -1e30)
            else:
                s_sc[h] = scores(i, kb + 1, h)
            m_prev = m_sc[h]
            m_new = jnp.maximum(m_prev, jnp.max(s, axis=0, keepdims=True))
            alpha = jnp.exp2(m_prev - m_new)
            p = jnp.exp2(s - m_new)
            acc_sc[h] = alpha * acc_sc[h] + jnp.dot(
                vt_ref[h * VT_ROWS:(h + 1) * VT_ROWS, ksl], p.astype(BF16),
                preferred_element_type=F32)
            m_sc[h] = m_new

    def body(kb, carry):
        step(kb, False)
        return carry

    lax.fori_loop(0, i, body, 0)
    step(i, True)
    for h in range(heads):
        acc = acc_sc[h]
        o_ref[:, h * MLA_V:(h + 1) * MLA_V] = (
            acc[:MLA_V] * (1.0 / acc[MLA_V:MLA_V + 1])).T.astype(o_ref.dtype)


def _attention(q, k, vt, km, vmt, blk, heads):
    nq = SEQ // blk
    kern = functools.partial(_attn_kernel, blk=blk, heads=heads, nq=nq)
    return pl.pallas_call(
        kern,
        grid=(BATCH, MLA_HEADS // heads, nq),
        in_specs=[pl.BlockSpec((heads * MLA_HEAD_PAD, SEQ), lambda b, h, i: (h, b)),
                  pl.BlockSpec((SEQ, heads * MLA_HEAD_PAD), lambda b, h, i: (b, h)),
                  pl.BlockSpec((heads * VT_ROWS, SEQ), lambda b, h, i: (h, b)),
                  pl.BlockSpec((N_META, heads * MLA_HEAD_PAD), lambda b, h, i: (0, h)),
                  pl.BlockSpec((heads * VT_ROWS, N_META), lambda b, h, i: (h, 0))],
        out_specs=pl.BlockSpec((blk, heads * MLA_V), lambda b, h, i: (b * nq + i, h)),
        out_shape=jax.ShapeDtypeStruct((ROWS, MLA_HEADS * MLA_V), BF16),
        scratch_shapes=[pltpu.VMEM((heads, 1, blk), F32),
                        pltpu.VMEM((heads, VT_ROWS, blk), F32), pltpu.VMEM((heads, blk, blk), F32)],
        compiler_params=_params("parallel", "parallel", "arbitrary"),
        name="mla_attention",
    )(q, k, vt, km, vmt)


def _meta_attn_kernel(q_ref, km_ref, vmt_ref, o_ref):
    s = lax.dot_general(q_ref[...], km_ref[...], _NT, preferred_element_type=F32)
    row = lax.broadcasted_iota(jnp.int32, s.shape, 0)
    col = lax.broadcasted_iota(jnp.int32, s.shape, 1)
    s = jnp.where(col <= row, s, -1e30)
    m = jnp.max(s, axis=-1, keepdims=True)
    p = jnp.exp2(s - m)
    l = jnp.sum(p, axis=-1, keepdims=True)
    o = lax.dot_general(p.astype(BF16), vmt_ref[:MLA_V, :], _NT, preferred_element_type=F32)
    o_ref[...] = (o / l).astype(o_ref.dtype)


def _meta_attention(qm, kmp, vmtp):
    return pl.pallas_call(
        _meta_attn_kernel,
        grid=(MLA_HEADS,),
        in_specs=[pl.BlockSpec((N_META, MLA_HEAD_PAD), lambda h: (0, h)),
                  pl.BlockSpec((128, MLA_HEAD_PAD), lambda h: (0, h)),
                  pl.BlockSpec((VT_ROWS, 128), lambda h: (h, 0))],
        out_specs=pl.BlockSpec((N_META, MLA_V), lambda h: (0, h)),
        out_shape=jax.ShapeDtypeStruct((N_META, MLA_HEADS * MLA_V), BF16),
        compiler_params=_params("parallel"),
        name="meta_attention",
    )(qm, kmp, vmtp)


def _merge_kernel(a_ref, b_ref, wa_ref, wb_ref, za_ref, zb_ref, o_ref):
    ga = _sigmoid(za_ref[...])
    gb = _sigmoid(zb_ref[...])
    ya = jnp.dot(a_ref[...], wa_ref[...], preferred_element_type=F32)
    yb = jnp.dot(b_ref[...], wb_ref[...], preferred_element_type=F32)
    o_ref[...] = (ga * ya + gb * yb).astype(o_ref.dtype)


def _merge(a, b, wa3, wb3, layer, z, bm, bn):
    m = a.shape[0]
    return pl.pallas_call(
        _merge_kernel,
        grid=(m // bm, D_MODEL // bn),
        in_specs=[pl.BlockSpec((bm, D_MODEL), lambda i, j: (i, 0)),
                  pl.BlockSpec((bm, D_MODEL), lambda i, j: (i, 0)),
                  pl.BlockSpec((None, D_MODEL, bn), lambda i, j: (layer, 0, j)),
                  pl.BlockSpec((None, D_MODEL, bn), lambda i, j: (layer, 0, j)),
                  pl.BlockSpec((bm, bn), lambda i, j: (i, Z_A // bn + j)),
                  pl.BlockSpec((bm, bn), lambda i, j: (i, Z_B // bn + j))],
        out_specs=pl.BlockSpec((bm, bn), lambda i, j: (i, j)),
        out_shape=jax.ShapeDtypeStruct((m, D_MODEL), BF16),
        compiler_params=_params("parallel", "parallel"),
        name="branch_merge",
    )(a, b, wa3, wb3, z, z)


def _outproj_kernel(m_ref, w_ref, h_ref, g_ref, ho_ref, hn_ref):
    hn = h_ref[...] + jnp.dot(m_ref[...], w_ref[...], preferred_element_type=F32)
    ho_ref[...] = hn
    ms = jnp.mean(hn * hn, axis=-1, keepdims=True)
    hn_ref[...] = (hn * lax.rsqrt(ms + EPS) * g_ref[...]).astype(hn_ref.dtype)


def _outproj(mg, w3, layer, h, g, bm):
    m = mg.shape[0]
    return pl.pallas_call(
        _outproj_kernel,
        grid=(m // bm,),
        in_specs=[pl.BlockSpec((bm, D_MODEL), lambda i: (i, 0)),
                  pl.BlockSpec((None, D_MODEL, D_MODEL), lambda i: (layer, 0, 0)),
                  pl.BlockSpec((bm, D_MODEL), lambda i: (i, 0)),
                  pl.BlockSpec((1, D_MODEL), lambda i: (0, 0))],
        out_specs=[pl.BlockSpec((bm, D_MODEL), lambda i: (i, 0)),
                   pl.BlockSpec((bm, D_MODEL), lambda i: (i, 0))],
        out_shape=[jax.ShapeDtypeStruct((m, D_MODEL), F32),
                   jax.ShapeDtypeStruct((m, D_MODEL), BF16)],
        compiler_params=_params("parallel"),
        name="out_proj",
    )(mg, w3, h, g.reshape(1, -1))


def _gateup_kernel(*refs, with_meta):
    if with_meta:
        h_ref, hm_ref, wg_ref, wu_ref, o_ref, om_ref, wgb_ref, wub_ref = refs
    else:
        h_ref, wg_ref, wu_ref, o_ref, wgb_ref, wub_ref = refs

    def swiglu(h):
        g = jnp.dot(h, wgb_ref[...], preferred_element_type=F32)
        u = jnp.dot(h, wub_ref[...], preferred_element_type=F32)
        return (g * _sigmoid(g) * u).astype(o_ref.dtype)

    @pl.when(pl.program_id(1) == 0)
    def _():
        wgb_ref[...] = wg_ref[...].astype(BF16)
        wub_ref[...] = wu_ref[...].astype(BF16)
        if with_meta:
            om_ref[...] = swiglu(hm_ref[...])

    o_ref[...] = swiglu(h_ref[...])


def _gateup(h, h_meta, w3, layer, bm, bn):
    m = h.shape[0]
    nb = FF_HIDDEN // bn
    with_meta = h_meta is not None
    weight_specs = [pl.BlockSpec((None, D_MODEL, bn), lambda j, i: (layer, 0, j)),
                    pl.BlockSpec((None, D_MODEL, bn), lambda j, i: (layer, 0, nb + j))]
    in_specs = [pl.BlockSpec((bm, D_MODEL), lambda j, i: (i, 0))]
    operands = [h]
    out_specs = [pl.BlockSpec((bm, bn), lambda j, i: (i, j))]
    out_shape = [jax.ShapeDtypeStruct((m, FF_HIDDEN), BF16)]
    if with_meta:
        in_specs.append(pl.BlockSpec((N_META, D_MODEL), lambda j, i: (0, 0)))
        operands.append(h_meta)
        out_specs.append(pl.BlockSpec((N_META, bn), lambda j, i: (0, j)))
        out_shape.append(jax.ShapeDtypeStruct((N_META, FF_HIDDEN), BF16))
    return pl.pallas_call(
        functools.partial(_gateup_kernel, with_meta=with_meta),
        grid=(nb, m // bm),
        in_specs=in_specs + weight_specs,
        out_specs=out_specs,
        out_shape=out_shape,
        scratch_shapes=[pltpu.VMEM((D_MODEL, bn), BF16), pltpu.VMEM((D_MODEL, bn), BF16)],
        compiler_params=_params("arbitrary", "arbitrary"),
        name="gate_up",
    )(*operands, w3, w3)


def _down_kernel(a_ref, w_ref, h_ref, *rest, emit_norm):
    if emit_norm:
        g_ref, ho_ref, hn_ref, acc_ref = rest
    else:
        ho_ref, acc_ref = rest
    k = pl.program_id(1)
    last = pl.num_programs(1) - 1

    def product():
        return jnp.dot(a_ref[...], w_ref[...], preferred_element_type=F32)

    @pl.when(k == 0)
    def _():
        acc_ref[...] = product()

    @pl.when((k > 0) & (k < last))
    def _():
        acc_ref[...] += product()

    @pl.when(k == last)
    def _():
        hn = h_ref[...] + (acc_ref[...] + product())
        ho_ref[...] = hn
        if emit_norm:
            ms = jnp.mean(hn * hn, axis=-1, keepdims=True)
            hn_ref[...] = (hn * lax.rsqrt(ms + EPS) * g_ref[...]).astype(hn_ref.dtype)


def _down(a, w3, layer, h, g, bm, bk):
    m = a.shape[0]
    assert FF_HIDDEN // bk >= 2
    emit_norm = g is not None
    row_block = pl.BlockSpec((bm, D_MODEL), lambda i, k: (i, 0))
    in_specs = [pl.BlockSpec((bm, bk), lambda i, k: (i, k)),
                pl.BlockSpec((None, bk, D_MODEL), lambda i, k: (layer, k, 0)),
                row_block]
    operands = [a, w3, h]
    out_specs = [row_block]
    out_shape = [jax.ShapeDtypeStruct((m, D_MODEL), F32)]
    if emit_norm:
        in_specs.append(pl.BlockSpec((1, D_MODEL), lambda i, k: (0, 0)))
        operands.append(g.reshape(1, -1))
        out_specs.append(row_block)
        out_shape.append(jax.ShapeDtypeStruct((m, D_MODEL), BF16))
    return pl.pallas_call(
        functools.partial(_down_kernel, emit_norm=emit_norm),
        grid=(m // bm, FF_HIDDEN // bk),
        in_specs=in_specs,
        out_specs=out_specs,
        out_shape=out_shape,
        scratch_shapes=[pltpu.VMEM((bm, D_MODEL), F32)],
        compiler_params=_params("parallel", "arbitrary"),
        name="down_proj",
    )(*operands)


def _rope_layout(t):
    zeros = jnp.zeros(t.shape[:-1] + (32,), t.dtype)
    return jnp.concatenate([t[..., :32], zeros, t[..., 32:], zeros], axis=-1)


def _head_layout(t):
    return jnp.concatenate([t[..., :MLA_NOPE], _rope_layout(t[..., MLA_NOPE:])], axis=-1)


def _rope_tables():
    length = N_META + SEQ
    inv = 1.0 / (ROPE_THETA ** (jnp.arange(0, MLA_ROPE, 2, dtype=F32) / MLA_ROPE))
    ang = jnp.arange(length, dtype=F32)[:, None] * inv[None, :]
    cos, sin = jnp.cos(ang), jnp.sin(ang)
    zeros = jnp.zeros_like(cos)
    ctab = jnp.concatenate([cos, zeros, cos, zeros], axis=-1)
    stab = jnp.concatenate([-sin, zeros, sin, zeros], axis=-1)
    return ctab, stab, cos.T, sin.T


def kernel(x, meta_tokens, norm1_g, w_in, gla_gate_w2, gla_gate_b, gla_onorm_g, w_branch_a,
           q_a_norm_g, w_uq, kv_a_norm_g, w_ukv, q_norm_g, k_norm_g, w_branch_b, w_out,
           norm2_g, w_gate_up, w_down):
    ctab, stab, cos_t, sin_t = _rope_tables()
    ctab_m, stab_m = ctab[:N_META], stab[:N_META]
    ctab_r, stab_r = ctab[N_META:], stab[N_META:]
    cos_tm, sin_tm = cos_t[:, :N_META], sin_t[:, :N_META]
    cos_tr, sin_tr = cos_t[:, N_META:], sin_t[:, N_META:]

    h_res = x.reshape(ROWS, D_MODEL)
    hm_res = meta_tokens.astype(F32)
    h = _rms(h_res, norm1_g[0], RMS_ROWS)
    hm = _rms(hm_res, norm1_g[0], N_META)
    s_zero = jnp.zeros((GLA_HEADS, GLA_DK, GLA_DV), F32)

    w_in_t = jnp.swapaxes(w_in, 1, 2)
    w_kr = w_in_t[:, 7184:7248]
    zrow = jnp.zeros((DEPTH, 32, D_MODEL), F32)
    w_small = jnp.concatenate(
        [w_in_t[:, 6144:6160], jnp.zeros((DEPTH, 128 - GLA_GATE_RANK, D_MODEL), F32),
         w_kr[:, :32], zrow, w_kr[:, 32:], zrow], axis=1)
    wa = w_branch_a.astype(BF16)
    wb = w_branch_b.astype(BF16)
    wo = w_out.astype(BF16)
    wd = w_down.astype(BF16)

    for l in range(DEPTH):
        last = l == DEPTH - 1
        w2p = jnp.concatenate(
            [gla_gate_w2[l], jnp.zeros((128 - GLA_GATE_RANK, GLA_HEADS * GLA_DK), F32)],
            axis=0).astype(BF16)
        wqt = _head_layout(w_uq[l].reshape(MLA_LORA, MLA_HEADS, MLA_QK)).reshape(
            MLA_LORA, MLA_HEADS * MLA_HEAD_PAD).T.astype(BF16)
        wkv3 = w_ukv[l].reshape(MLA_LORA, MLA_HEADS, MLA_NOPE + MLA_V)
        wk = wkv3[:, :, :MLA_NOPE].reshape(MLA_LORA, MLA_HEADS * MLA_NOPE).astype(BF16)
        wvt = jnp.pad(jnp.transpose(wkv3[:, :, MLA_NOPE:], (1, 2, 0)),
                      ((0, 0), (0, VT_ROWS - MLA_V), (0, 0))).reshape(
            MLA_HEADS * VT_ROWS, MLA_LORA).astype(BF16)
        qg = (_head_layout(q_norm_g[l]) * (MLA_QK ** -0.5 * LOG2E)).reshape(MLA_HEAD_PAD, 1)
        kg = _head_layout(k_norm_g[l]).reshape(1, MLA_HEAD_PAD)

        z, zm = _in_proj(h, hm, w_in_t, l, *IN_PROJ_BLOCK, name="in_proj")
        zs = _matmul_nt(h, w_small, l, IN_PROJ_SMALL_ROWS, N_SMALL, name="in_proj_small")
        zsm = _matmul_nt(hm, w_small, l, N_META, N_SMALL, name="in_proj_small_meta")

        gla_m, s_meta = _gla(zm, zsm, w2p, gla_gate_b[l], gla_onorm_g[l], s_zero,
                             batch=1, tokens=N_META, block=N_META, chunk=N_META, emit_state=True)
        (gla_r,) = _gla(z, zs, w2p, gla_gate_b[l], gla_onorm_g[l], s_meta,
                        batch=BATCH, tokens=SEQ, block=GLA_TOKENS, chunk=GLA_CHUNK, emit_state=False)

        km, vmt = _kvproj(zm, zsm, kv_a_norm_g[l], wk, wvt, kg, ctab_m, stab_m, N_META)
        kr, vrt = _kvproj(z, zs, kv_a_norm_g[l], wk, wvt, kg, ctab_r, stab_r, PROJ_ROWS)
        kmp = jnp.pad(km, ((0, 128 - N_META), (0, 0)))
        vmtp = jnp.pad(vmt, ((0, 0), (0, 128 - N_META)))
        qr = _qproj(z, q_a_norm_g[l], wqt, qg, cos_tr, sin_tr, PROJ_ROWS)
        att_r = _attention(qr, kr, vrt, km, vmt, ATTN_BLOCK, ATTN_HEADS)

        merged = _merge(gla_r, att_r, wa, wb, l, z, *MERGE_BLOCK)
        h_res, h2 = _outproj(merged, wo, l, h_res, norm2_g[l], OUT_PROJ_ROWS)
        if last:
            (act,) = _gateup(h2, None, w_gate_up, l, *GATE_UP_BLOCK)
            (h_res,) = _down(act, wd, l, h_res, None, *DOWN_BLOCK)
        else:
            qm = _qproj(zm, q_a_norm_g[l], wqt, qg, cos_tm, sin_tm, N_META)
            att_m = _meta_attention(qm.T, kmp, vmtp)
            merged_m = _merge(gla_m, att_m, wa, wb, l, zm, N_META, MERGE_BLOCK[1])
            hm_res, hm2 = _outproj(merged_m, wo, l, hm_res, norm2_g[l], N_META)
            act, act_m = _gateup(h2, hm2, w_gate_up, l, *GATE_UP_BLOCK)
            h_res, h = _down(act, wd, l, h_res, norm1_g[l + 1], *DOWN_BLOCK)
            hm_res, hm = _down(act_m, wd, l, hm_res, norm1_g[l + 1], N_META, DOWN_META_K)

    return h_res.reshape(BATCH, SEQ, D_MODEL)
```

```python
import functools

import jax
import jax.numpy as jnp
from jax import lax
from jax.experimental import pallas as pl
from jax.experimental.pallas import tpu as pltpu

D_MODEL = 2048
BATCH = 4
SEQ = 4096
DEPTH = 2
N_META = 16
ROWS = BATCH * SEQ

GLA_HEADS = 4
GLA_DK = 256
GLA_DV = 512
GLA_GATE_RANK = 16
GLA_GATE_TAU = 16.0
GLA_CHUNK = 64

MLA_HEADS = 16
MLA_LORA = 512
MLA_NOPE = 128
MLA_ROPE = 64
MLA_QK = 192
MLA_V = 128
VT_ROWS = 144
MLA_HEAD_PAD = 256
ROPE_THETA = 10000.0
FF_HIDDEN = 5632
EPS = 1e-6
LOG2E = 1.4426950408889634

W_IN_ZA = 6144
W_IN_C, W_IN_C_END = 6160, 7184
W_IN_G = 7248
Z_Q, Z_K, Z_V, Z_R, Z_CQ, Z_CKV, Z_A, Z_B = 0, 1024, 2048, 4096, 6144, 6656, 7168, 9216
N_Z = 11264
N_SMALL = 256

RMS_SMALL_ROWS = 1024
IN_PROJ_BLOCK = (1024, 1024)
IN_PROJ_SMALL_ROWS = 2048
GLA_TOKENS = 512
PROJ_ROWS = 1024
ATTN_BLOCK = 512
ATTN_HEADS = 4
MERGE_BLOCK = (1024, 512)
OUT_PROJ_ROWS = 512
GATE_UP_BLOCK = (1024, 512)
DOWN_BLOCK = (512, 2816)
DOWN_META_K = 1408

VMEM_LIMIT = 56 * 1024 * 1024
BF16 = jnp.bfloat16
F32 = jnp.float32

_NT = (((1,), (1,)), ((), ()))
_TN = (((0,), (0,)), ((), ()))


def _params(*sem):
    return pltpu.CompilerParams(dimension_semantics=sem, vmem_limit_bytes=VMEM_LIMIT)


def _sigmoid(x):
    return 1.0 / (1.0 + jnp.exp(-x))


def _rms_small_kernel(x_ref, g_ref, wt_ref, h_ref, zs_ref):
    x = x_ref[...]
    ms = jnp.mean(x * x, axis=-1, keepdims=True)
    h = (x * lax.rsqrt(ms + EPS) * g_ref[...]).astype(BF16)
    h_ref[...] = h
    zs_ref[...] = lax.dot_general(h, wt_ref[...].astype(BF16), _NT, preferred_element_type=F32)


def _rms_small(x, g, wt3, layer, bm):
    m, d = x.shape
    n = wt3.shape[1]
    return pl.pallas_call(
        _rms_small_kernel,
        grid=(m // bm,),
        in_specs=[pl.BlockSpec((bm, d), lambda i: (i, 0)),
                  pl.BlockSpec((1, d), lambda i: (0, 0)),
                  pl.BlockSpec((None, n, d), lambda i: (layer, 0, 0))],
        out_specs=[pl.BlockSpec((bm, d), lambda i: (i, 0)),
                   pl.BlockSpec((bm, n), lambda i: (i, 0))],
        out_shape=[jax.ShapeDtypeStruct((m, d), BF16),
                   jax.ShapeDtypeStruct((m, n), F32)],
        compiler_params=_params("parallel"),
        name="rmsnorm_small_proj",
    )(x, g.reshape(1, d), wt3)


def _mm_nt_kernel(a_ref, wt_ref, o_ref):
    o_ref[...] = lax.dot_general(a_ref[...], wt_ref[...].astype(BF16), _NT,
                                 preferred_element_type=F32)


def _matmul_nt(a, wt3, layer, bm, bn, name):
    m, k = a.shape
    n = wt3.shape[1]
    return pl.pallas_call(
        _mm_nt_kernel,
        grid=(m // bm, n // bn),
        in_specs=[pl.BlockSpec((bm, k), lambda i, j: (i, 0)),
                  pl.BlockSpec((None, bn, k), lambda i, j: (layer, j, 0))],
        out_specs=pl.BlockSpec((bm, bn), lambda i, j: (i, j)),
        out_shape=jax.ShapeDtypeStruct((m, n), F32),
        compiler_params=_params("parallel", "parallel"),
        name=name,
    )(a, wt3)


def _mm_nt_wstat_kernel(a_ref, am_ref, wt_ref, o_ref, om_ref, wb_ref):
    @pl.when(pl.program_id(1) == 0)
    def _():
        wb_ref[...] = wt_ref[0].astype(BF16)
        om_ref[...] = lax.dot_general(am_ref[...], wb_ref[...], _NT, preferred_element_type=F32)

    o_ref[...] = lax.dot_general(a_ref[...], wb_ref[...], _NT, preferred_element_type=F32)


def _in_proj_row0(j, bn):
    n_a = W_IN_ZA // bn
    n_c = (W_IN_C_END - W_IN_C) // bn
    return jnp.where(j < n_a, j * bn,
                     jnp.where(j < n_a + n_c, W_IN_C + (j - n_a) * bn,
                               W_IN_G + (j - n_a - n_c) * bn))


def _in_proj(a, a_meta, wt3, layer, bm, bn, name):
    m, k = a.shape
    return pl.pallas_call(
        _mm_nt_wstat_kernel,
        grid=(N_Z // bn, m // bm),
        in_specs=[pl.BlockSpec((bm, k), lambda j, i: (i, 0)),
                  pl.BlockSpec((N_META, k), lambda j, i: (0, 0)),
                  pl.BlockSpec((pl.Element(1), pl.Element(bn), pl.Element(k)),
                               lambda j, i: (layer, pl.multiple_of(_in_proj_row0(j, bn), 16), 0))],
        out_specs=[pl.BlockSpec((bm, bn), lambda j, i: (i, j)),
                   pl.BlockSpec((N_META, bn), lambda j, i: (0, j))],
        out_shape=[jax.ShapeDtypeStruct((m, N_Z), F32),
                   jax.ShapeDtypeStruct((N_META, N_Z), F32)],
        scratch_shapes=[pltpu.VMEM((bn, k), BF16)],
        compiler_params=_params("arbitrary", "arbitrary"),
        name=name,
    )(a, a_meta, wt3)


def _gla_kernel(q_ref, k_ref, v_ref, r_ref, zlr_ref, w2_ref, gb_ref, og_ref, s0_ref,
                o_ref, *rest, chunk, n_chunks, n_total, emit_state):
    if emit_state:
        sfin_ref, st_ref, b_sc = rest
    else:
        st_ref, b_sc = rest
    t = pl.program_id(1)
    heads = range(GLA_HEADS)

    row = lax.broadcasted_iota(jnp.int32, (chunk, chunk), 0)
    col = lax.broadcasted_iota(jnp.int32, (chunk, chunk), 1)
    causal = col <= row
    tri = jnp.where(causal, 1.0, 0.0).astype(BF16)
    w2 = w2_ref[...]
    gb = gb_ref[...]
    og = og_ref[...]
    kdim = GLA_HEADS * GLA_DK

    def log_gate_split(gc):
        sl = pl.ds(pl.multiple_of(gc * chunk, chunk), chunk)
        logit = jnp.dot(zlr_ref[sl, :].astype(BF16), w2, preferred_element_type=F32) + gb
        g = (jnp.minimum(logit, 0.0) - jnp.log1p(jnp.exp(-jnp.abs(logit)))) * (1.0 / GLA_GATE_TAU)
        g_hi = g.astype(BF16)
        g_lo = (g - g_hi.astype(F32)).astype(BF16)
        return jnp.concatenate([g_hi, g_lo], axis=1)

    def log_decay(g_split):
        cs = jnp.dot(tri, g_split, preferred_element_type=F32)
        return cs[:, :kdim] + cs[:, kdim:]

    @pl.when(t == 0)
    def _():
        st_ref[...] = s0_ref[...]
        b_sc[...] = log_decay(log_gate_split(0))

    def body(c, carry):
        sl = pl.ds(pl.multiple_of(c * chunk, chunk), chunk)
        g_split_next = log_gate_split(jnp.minimum(t * n_chunks + c + 1, n_total - 1))
        r = r_ref[sl, :]
        out_gate = r * _sigmoid(r)
        st = [st_ref[h] for h in heads]
        st_bf = [st[h].astype(BF16) for h in heads]
        b = b_sc[...]
        b_last = b[chunk - 1:chunk, :]
        eb = jnp.exp(b)
        enb = jnp.exp(-b)
        erel = jnp.exp(b_last - b)
        elast = jnp.exp(b_last)
        k = k_ref[sl, :]
        qd = (q_ref[sl, :] * (GLA_DK ** -0.5) * eb).astype(BF16)
        kd = (k * enb).astype(BF16)
        k2 = (k * erel).astype(BF16)
        v = v_ref[sl, :].astype(BF16)

        def hk(x, h):
            return x[:, h * GLA_DK:(h + 1) * GLA_DK]

        def hv(x, h):
            return x[:, h * GLA_DV:(h + 1) * GLA_DV]

        a = [lax.dot_general(hk(qd, h), hk(kd, h), _NT, preferred_element_type=F32) for h in heads]
        o_state = [jnp.dot(hk(qd, h), st_bf[h], preferred_element_type=F32) for h in heads]
        upd = [lax.dot_general(hk(k2, h), hv(v, h), _TN, preferred_element_type=F32)
               for h in heads]
        b_sc[...] = log_decay(g_split_next)
        o_local = [jnp.dot(jnp.where(causal, a[h], 0.0).astype(BF16), hv(v, h),
                           preferred_element_type=F32) for h in heads]
        decay_col = jnp.broadcast_to(elast, (128, kdim)).T
        for h in heads:
            col = decay_col[h * GLA_DK:(h + 1) * GLA_DK]
            st_ref[h] = st[h] * jnp.concatenate([col] * (GLA_DV // 128), axis=1) + upd[h]
        for h in heads:
            o = o_local[h] + o_state[h]
            ms = jnp.mean(o * o, axis=-1, keepdims=True)
            on = o * lax.rsqrt(ms + EPS) * og
            o_ref[sl, h * GLA_DV:(h + 1) * GLA_DV] = (on * hv(out_gate, h)).astype(o_ref.dtype)
        return carry

    lax.fori_loop(0, n_chunks, body, 0)

    if emit_state:
        @pl.when(t == pl.num_programs(1) - 1)
        def _():
            sfin_ref[...] = st_ref[...]


def _gla(z, zs, w2p, gate_b, onorm_g, s0, *, batch, tokens, block, chunk, emit_state):
    nt = tokens // block
    rows = batch * tokens
    kern = functools.partial(_gla_kernel, chunk=chunk, n_chunks=block // chunk,
                             n_total=tokens // chunk, emit_state=emit_state)
    kdim = GLA_HEADS * GLA_DK
    vdim = GLA_HEADS * GLA_DV
    in_specs = [
        pl.BlockSpec((block, kdim), lambda b, t: (b * nt + t, Z_Q // kdim)),
        pl.BlockSpec((block, kdim), lambda b, t: (b * nt + t, Z_K // kdim)),
        pl.BlockSpec((block, vdim), lambda b, t: (b * nt + t, Z_V // vdim)),
        pl.BlockSpec((block, vdim), lambda b, t: (b * nt + t, Z_R // vdim)),
        pl.BlockSpec((tokens, 128), lambda b, t: (b, 0)),
        pl.BlockSpec((128, kdim), lambda b, t: (0, 0)),
        pl.BlockSpec((1, kdim), lambda b, t: (0, 0)),
        pl.BlockSpec((1, GLA_DV), lambda b, t: (0, 0)),
        pl.BlockSpec((GLA_HEADS, GLA_DK, GLA_DV), lambda b, t: (0, 0, 0)),
    ]
    out_specs = [pl.BlockSpec((block, vdim), lambda b, t: (b * nt + t, 0))]
    out_shape = [jax.ShapeDtypeStruct((rows, vdim), BF16)]
    if emit_state:
        out_specs.append(pl.BlockSpec((GLA_HEADS, GLA_DK, GLA_DV), lambda b, t: (b, 0, 0)))
        out_shape.append(jax.ShapeDtypeStruct((batch * GLA_HEADS, GLA_DK, GLA_DV), F32))
    res = pl.pallas_call(
        kern,
        grid=(batch, nt),
        in_specs=in_specs,
        out_specs=out_specs,
        out_shape=out_shape,
        scratch_shapes=[pltpu.VMEM((GLA_HEADS, GLA_DK, GLA_DV), F32),
                        pltpu.VMEM((chunk, kdim), F32)],
        compiler_params=_params("parallel", "arbitrary"),
        name="gla",
    )(z, z, z, z, zs, w2p, gate_b.reshape(1, -1), onorm_g.reshape(1, -1), s0)
    return res


def _rope(x, c, s):
    return x * c + pltpu.roll(x, 64, 1) * s


def _qproj_kernel(zc_ref, ng_ref, wt_ref, hg_ref, c_ref, s_ref, o_ref, *, heads_per_dot):
    x = zc_ref[...]
    ms = jnp.mean(x * x, axis=-1, keepdims=True)
    xn = (x * lax.rsqrt(ms + EPS) * ng_ref[...]).astype(BF16)
    bm = x.shape[0]
    c = c_ref[...]
    s = s_ref[...]
    hg = jnp.broadcast_to(hg_ref[...], (MLA_HEAD_PAD, bm))
    for hc in range(MLA_HEADS // heads_per_dot):
        base = hc * heads_per_dot * MLA_HEAD_PAD
        yt = lax.dot_general(wt_ref[base:base + heads_per_dot * MLA_HEAD_PAD, :], xn, _NT,
                             preferred_element_type=F32)
        for j in range(heads_per_dot):
            lo = base + j * MLA_HEAD_PAD
            y = yt[j * MLA_HEAD_PAD:(j + 1) * MLA_HEAD_PAD]
            ms = jnp.sum(y * y, axis=0, keepdims=True) * (1.0 / MLA_QK)
            yn = y * lax.rsqrt(ms + EPS) * hg
            x1 = yn[128:160]
            x2 = yn[192:224]
            o_ref[lo:lo + 128, :] = yn[:128].astype(o_ref.dtype)
            o_ref[lo + 128:lo + 160, :] = (x1 * c - x2 * s).astype(o_ref.dtype)
            o_ref[lo + 160:lo + 192, :] = yn[160:192].astype(o_ref.dtype)
            o_ref[lo + 192:lo + 224, :] = (x2 * c + x1 * s).astype(o_ref.dtype)
            o_ref[lo + 224:lo + 256, :] = yn[224:256].astype(o_ref.dtype)


def _qproj(z, ng, wt, hg_col, cos_t, sin_t, bm):
    m = z.shape[0]
    nt = cos_t.shape[1] // bm
    kern = functools.partial(_qproj_kernel, heads_per_dot=1)
    return pl.pallas_call(
        kern,
        grid=(m // bm,),
        in_specs=[pl.BlockSpec((bm, MLA_LORA), lambda i: (i, Z_CQ // MLA_LORA)),
                  pl.BlockSpec((1, MLA_LORA), lambda i: (0, 0)),
                  pl.BlockSpec((MLA_HEADS * MLA_HEAD_PAD, MLA_LORA), lambda i: (0, 0)),
                  pl.BlockSpec((MLA_HEAD_PAD, 1), lambda i: (0, 0)),
                  pl.BlockSpec((MLA_ROPE // 2, bm), lambda i: (0, i % nt)),
                  pl.BlockSpec((MLA_ROPE // 2, bm), lambda i: (0, i % nt))],
        out_specs=pl.BlockSpec((MLA_HEADS * MLA_HEAD_PAD, bm), lambda i: (0, i)),
        out_shape=jax.ShapeDtypeStruct((MLA_HEADS * MLA_HEAD_PAD, m), BF16),
        compiler_params=_params("parallel"),
        name="q_proj",
    )(z, ng.reshape(1, -1), wt, hg_col, cos_t, sin_t)


def _kvproj_kernel(zc_ref, kr_ref, ng_ref, wk_ref, wvt_ref, hg_ref, c_ref, s_ref, k_ref, vt_ref):
    x = zc_ref[...]
    ms = jnp.mean(x * x, axis=-1, keepdims=True)
    xn = (x * lax.rsqrt(ms + EPS) * ng_ref[...]).astype(BF16)
    c = c_ref[...]
    s = s_ref[...]
    hg = hg_ref[...]
    kr = kr_ref[...]
    kr_ss = jnp.sum(kr * kr, axis=-1, keepdims=True)
    kr_rot = _rope(kr * hg[:, 128:], c, s)
    for hp in range(MLA_HEADS // 2):
        y = jnp.dot(xn, wk_ref[:, hp * 256:(hp + 1) * 256], preferred_element_type=F32)
        for sub in range(2):
            lo = (2 * hp + sub) * MLA_HEAD_PAD
            kn = y[:, sub * MLA_NOPE:(sub + 1) * MLA_NOPE]
            ms = (jnp.sum(kn * kn, axis=-1, keepdims=True) + kr_ss) * (1.0 / MLA_QK)
            rs = lax.rsqrt(ms + EPS)
            k_ref[:, lo:lo + 128] = (kn * rs * hg[:, :128]).astype(k_ref.dtype)
            k_ref[:, lo + 128:lo + 256] = (kr_rot * rs).astype(k_ref.dtype)
    rows_per_dot = 4 * VT_ROWS
    for j in range(MLA_HEADS * VT_ROWS // rows_per_dot):
        sl = slice(j * rows_per_dot, (j + 1) * rows_per_dot)
        vt_ref[sl, :] = lax.dot_general(wvt_ref[sl, :], xn, _NT,
                                        preferred_element_type=F32).astype(vt_ref.dtype)
    ones = jnp.ones((VT_ROWS - MLA_V, x.shape[0]), vt_ref.dtype)
    for h in range(MLA_HEADS):
        vt_ref[h * VT_ROWS + MLA_V:(h + 1) * VT_ROWS, :] = ones


def _kvproj(z, zs, ng, wk, wvt, hg, ctab, stab, bm):
    m = z.shape[0]
    nt = ctab.shape[0] // bm
    return pl.pallas_call(
        _kvproj_kernel,
        grid=(m // bm,),
        in_specs=[pl.BlockSpec((bm, MLA_LORA), lambda i: (i, Z_CKV // MLA_LORA)),
                  pl.BlockSpec((bm, 128), lambda i: (i, 1)),
                  pl.BlockSpec((1, MLA_LORA), lambda i: (0, 0)),
                  pl.BlockSpec((MLA_LORA, MLA_HEADS * MLA_NOPE), lambda i: (0, 0)),
                  pl.BlockSpec((MLA_HEADS * VT_ROWS, MLA_LORA), lambda i: (0, 0)),
                  pl.BlockSpec((1, MLA_HEAD_PAD), lambda i: (0, 0)),
                  pl.BlockSpec((bm, 128), lambda i: (i % nt, 0)),
                  pl.BlockSpec((bm, 128), lambda i: (i % nt, 0))],
        out_specs=[pl.BlockSpec((bm, MLA_HEADS * MLA_HEAD_PAD), lambda i: (i, 0)),
                   pl.BlockSpec((MLA_HEADS * VT_ROWS, bm), lambda i: (0, i))],
        out_shape=[jax.ShapeDtypeStruct((m, MLA_HEADS * MLA_HEAD_PAD), BF16),
                   jax.ShapeDtypeStruct((MLA_HEADS * VT_ROWS, m), BF16)],
        compiler_params=_params("parallel"),
        name="kv_proj",
    )(z, zs, ng.reshape(1, -1), wk, wvt, hg, ctab, stab)


def _attn_kernel(qt_ref, k_ref, vt_ref, km_ref, vmt_ref, o_ref, m_sc, acc_sc, s_sc, *, blk, heads,
                 nq):
    i = pl.program_id(2)

    def queries(qi, h):
        return qt_ref[h * MLA_HEAD_PAD:(h + 1) * MLA_HEAD_PAD,
                      pl.ds(pl.multiple_of(qi * blk, blk), blk)]

    s_meta = [jnp.dot(km_ref[:, h * MLA_HEAD_PAD:(h + 1) * MLA_HEAD_PAD], queries(i, h),
                      preferred_element_type=F32) for h in range(heads)]
    p_meta = []
    for h in range(heads):
        m = jnp.max(s_meta[h], axis=0, keepdims=True)
        p_meta.append(jnp.exp2(s_meta[h] - m).astype(BF16))
        m_sc[h] = m
    for h in range(heads):
        acc_sc[h] = jnp.dot(vmt_ref[h * VT_ROWS:(h + 1) * VT_ROWS, :], p_meta[h],
                            preferred_element_type=F32)

    def scores(qi, kb, h):
        ksl = pl.ds(pl.multiple_of(kb * blk, blk), blk)
        return jnp.dot(k_ref[ksl, h * MLA_HEAD_PAD:(h + 1) * MLA_HEAD_PAD], queries(qi, h),
                       preferred_element_type=F32)

    @pl.when(i == 0)
    def _():
        for h in range(heads):
            s_sc[h] = scores(0, 0, h)

    def step(kb, masked):
        ksl = pl.ds(pl.multiple_of(kb * blk, blk), blk)
        tiles = [s_sc[h] for h in range(heads)]
        for h in range(heads):
            s = tiles[h]
            if masked:
                s_sc[h] = scores(jnp.minimum(i + 1, nq - 1), 0, h)
                key = lax.broadcasted_iota(jnp.int32, s.shape, 0)
                qry = lax.broadcasted_iota(jnp.int32, s.shape, 1)
                s = jnp.where(key <= qry, s, -1e30)
            else:
                s_sc[h] = scores(i, kb + 1, h)
            m_prev = m_sc[h]
            m_new = jnp.maximum(m_prev, jnp.max(s, axis=0, keepdims=True))
            alpha = jnp.exp2(m_prev - m_new)
            p = jnp.exp2(s - m_new)
            acc_sc[h] = alpha * acc_sc[h] + jnp.dot(
                vt_ref[h * VT_ROWS:(h + 1) * VT_ROWS, ksl], p.astype(BF16),
                preferred_element_type=F32)
            m_sc[h] = m_new

    def body(kb, carry):
        step(kb, False)
        return carry

    lax.fori_loop(0, i, body, 0)
    step(i, True)
    for h in range(heads):
        acc = acc_sc[h]
        o_ref[:, h * MLA_V:(h + 1) * MLA_V] = (
            acc[:MLA_V] * (1.0 / acc[MLA_V:MLA_V + 1])).T.astype(o_ref.dtype)


def _attention(q, k, vt, km, vmt, blk, heads):
    nq = SEQ // blk
    kern = functools.partial(_attn_kernel, blk=blk, heads=heads, nq=nq)
    return pl.pallas_call(
        kern,
        grid=(BATCH, MLA_HEADS // heads, nq),
        in_specs=[pl.BlockSpec((heads * MLA_HEAD_PAD, SEQ), lambda b, h, i: (h, b)),
                  pl.BlockSpec((SEQ, heads * MLA_HEAD_PAD), lambda b, h, i: (b, h)),
                  pl.BlockSpec((heads * VT_ROWS, SEQ), lambda b, h, i: (h, b)),
                  pl.BlockSpec((N_META, heads * MLA_HEAD_PAD), lambda b, h, i: (0, h)),
                  pl.BlockSpec((heads * VT_ROWS, N_META), lambda b, h, i: (h, 0))],
        out_specs=pl.BlockSpec((blk, heads * MLA_V), lambda b, h, i: (b * nq + i, h)),
        out_shape=jax.ShapeDtypeStruct((ROWS, MLA_HEADS * MLA_V), BF16),
        scratch_shapes=[pltpu.VMEM((heads, 1, blk), F32),
                        pltpu.VMEM((heads, VT_ROWS, blk), F32), pltpu.VMEM((heads, blk, blk), F32)],
        compiler_params=_params("parallel", "parallel", "arbitrary"),
        name="mla_attention",
    )(q, k, vt, km, vmt)


def _meta_attn_kernel(q_ref, km_ref, vmt_ref, o_ref):
    s = lax.dot_general(q_ref[...], km_ref[...], _NT, preferred_element_type=F32)
    row = lax.broadcasted_iota(jnp.int32, s.shape, 0)
    col = lax.broadcasted_iota(jnp.int32, s.shape, 1)
    s = jnp.where(col <= row, s, -1e30)
    m = jnp.max(s, axis=-1, keepdims=True)
    p = jnp.exp2(s - m)
    l = jnp.sum(p, axis=-1, keepdims=True)
    o = lax.dot_general(p.astype(BF16), vmt_ref[:MLA_V, :], _NT, preferred_element_type=F32)
    o_ref[...] = (o / l).astype(o_ref.dtype)


def _meta_attention(qm, kmp, vmtp):
    return pl.pallas_call(
        _meta_attn_kernel,
        grid=(MLA_HEADS,),
        in_specs=[pl.BlockSpec((N_META, MLA_HEAD_PAD), lambda h: (0, h)),
                  pl.BlockSpec((128, MLA_HEAD_PAD), lambda h: (0, h)),
                  pl.BlockSpec((VT_ROWS, 128), lambda h: (h, 0))],
        out_specs=pl.BlockSpec((N_META, MLA_V), lambda h: (0, h)),
        out_shape=jax.ShapeDtypeStruct((N_META, MLA_HEADS * MLA_V), BF16),
        compiler_params=_params("parallel"),
        name="meta_attention",
    )(qm, kmp, vmtp)


def _merge_kernel(a_ref, b_ref, wa_ref, wb_ref, za_ref, zb_ref, o_ref):
    ga = _sigmoid(za_ref[...])
    gb = _sigmoid(zb_ref[...])
    ya = jnp.dot(a_ref[...], wa_ref[...], preferred_element_type=F32)
    yb = jnp.dot(b_ref[...], wb_ref[...], preferred_element_type=F32)
    o_ref[...] = (ga * ya + gb * yb).astype(o_ref.dtype)


def _merge(a, b, wa3, wb3, layer, z, bm, bn):
    m = a.shape[0]
    return pl.pallas_call(
        _merge_kernel,
        grid=(m // bm, D_MODEL // bn),
        in_specs=[pl.BlockSpec((bm, D_MODEL), lambda i, j: (i, 0)),
                  pl.BlockSpec((bm, D_MODEL), lambda i, j: (i, 0)),
                  pl.BlockSpec((None, D_MODEL, bn), lambda i, j: (layer, 0, j)),
                  pl.BlockSpec((None, D_MODEL, bn), lambda i, j: (layer, 0, j)),
                  pl.BlockSpec((bm, bn), lambda i, j: (i, Z_A // bn + j)),
                  pl.BlockSpec((bm, bn), lambda i, j: (i, Z_B // bn + j))],
        out_specs=pl.BlockSpec((bm, bn), lambda i, j: (i, j)),
        out_shape=jax.ShapeDtypeStruct((m, D_MODEL), BF16),
        compiler_params=_params("parallel", "parallel"),
        name="branch_merge",
    )(a, b, wa3, wb3, z, z)


def _outproj_kernel(m_ref, w_ref, h_ref, g_ref, ho_ref, hn_ref):
    hn = h_ref[...] + jnp.dot(m_ref[...], w_ref[...], preferred_element_type=F32)
    ho_ref[...] = hn
    ms = jnp.mean(hn * hn, axis=-1, keepdims=True)
    hn_ref[...] = (hn * lax.rsqrt(ms + EPS) * g_ref[...]).astype(hn_ref.dtype)


def _outproj(mg, w3, layer, h, g, bm):
    m = mg.shape[0]
    return pl.pallas_call(
        _outproj_kernel,
        grid=(m // bm,),
        in_specs=[pl.BlockSpec((bm, D_MODEL), lambda i: (i, 0)),
                  pl.BlockSpec((None, D_MODEL, D_MODEL), lambda i: (layer, 0, 0)),
                  pl.BlockSpec((bm, D_MODEL), lambda i: (i, 0)),
                  pl.BlockSpec((1, D_MODEL), lambda i: (0, 0))],
        out_specs=[pl.BlockSpec((bm, D_MODEL), lambda i: (i, 0)),
                   pl.BlockSpec((bm, D_MODEL), lambda i: (i, 0))],
        out_shape=[jax.ShapeDtypeStruct((m, D_MODEL), F32),
                   jax.ShapeDtypeStruct((m, D_MODEL), BF16)],
        compiler_params=_params("parallel"),
        name="out_proj",
    )(mg, w3, h, g.reshape(1, -1))


def _gateup_kernel(*refs, with_meta):
    if with_meta:
        h_ref, hm_ref, wg_ref, wu_ref, o_ref, om_ref, wgb_ref, wub_ref = refs
    else:
        h_ref, wg_ref, wu_ref, o_ref, wgb_ref, wub_ref = refs

    def swiglu(h):
        g = jnp.dot(h, wgb_ref[...], preferred_element_type=F32)
        u = jnp.dot(h, wub_ref[...], preferred_element_type=F32)
        return (g * _sigmoid(g) * u).astype(o_ref.dtype)

    @pl.when(pl.program_id(1) == 0)
    def _():
        wgb_ref[...] = wg_ref[...].astype(BF16)
        wub_ref[...] = wu_ref[...].astype(BF16)
        if with_meta:
            om_ref[...] = swiglu(hm_ref[...])

    o_ref[...] = swiglu(h_ref[...])


def _gateup(h, h_meta, w3, layer, bm, bn):
    m = h.shape[0]
    nb = FF_HIDDEN // bn
    with_meta = h_meta is not None
    weight_specs = [pl.BlockSpec((None, D_MODEL, bn), lambda j, i: (layer, 0, j)),
                    pl.BlockSpec((None, D_MODEL, bn), lambda j, i: (layer, 0, nb + j))]
    in_specs = [pl.BlockSpec((bm, D_MODEL), lambda j, i: (i, 0))]
    operands = [h]
    out_specs = [pl.BlockSpec((bm, bn), lambda j, i: (i, j))]
    out_shape = [jax.ShapeDtypeStruct((m, FF_HIDDEN), BF16)]
    if with_meta:
        in_specs.append(pl.BlockSpec((N_META, D_MODEL), lambda j, i: (0, 0)))
        operands.append(h_meta)
        out_specs.append(pl.BlockSpec((N_META, bn), lambda j, i: (0, j)))
        out_shape.append(jax.ShapeDtypeStruct((N_META, FF_HIDDEN), BF16))
    return pl.pallas_call(
        functools.partial(_gateup_kernel, with_meta=with_meta),
        grid=(nb, m // bm),
        in_specs=in_specs + weight_specs,
        out_specs=out_specs,
        out_shape=out_shape,
        scratch_shapes=[pltpu.VMEM((D_MODEL, bn), BF16), pltpu.VMEM((D_MODEL, bn), BF16)],
        compiler_params=_params("arbitrary", "arbitrary"),
        name="gate_up",
    )(*operands, w3, w3)


def _down_kernel(a_ref, w_ref, h_ref, *rest, emit_norm):
    if emit_norm:
        g_ref, ho_ref, hn_ref, acc_ref = rest
    else:
        ho_ref, acc_ref = rest
    k = pl.program_id(1)
    last = pl.num_programs(1) - 1

    def product():
        return jnp.dot(a_ref[...], w_ref[...], preferred_element_type=F32)

    @pl.when(k == 0)
    def _():
        acc_ref[...] = product()

    @pl.when((k > 0) & (k < last))
    def _():
        acc_ref[...] += product()

    @pl.when(k == last)
    def _():
        hn = h_ref[...] + (acc_ref[...] + product())
        ho_ref[...] = hn
        if emit_norm:
            ms = jnp.mean(hn * hn, axis=-1, keepdims=True)
            hn_ref[...] = (hn * lax.rsqrt(ms + EPS) * g_ref[...]).astype(hn_ref.dtype)


def _down(a, w3, layer, h, g, bm, bk):
    m = a.shape[0]
    assert FF_HIDDEN // bk >= 2
    emit_norm = g is not None
    row_block = pl.BlockSpec((bm, D_MODEL), lambda i, k: (i, 0))
    in_specs = [pl.BlockSpec((bm, bk), lambda i, k: (i, k)),
                pl.BlockSpec((None, bk, D_MODEL), lambda i, k: (layer, k, 0)),
                row_block]
    operands = [a, w3, h]
    out_specs = [row_block]
    out_shape = [jax.ShapeDtypeStruct((m, D_MODEL), F32)]
    if emit_norm:
        in_specs.append(pl.BlockSpec((1, D_MODEL), lambda i, k: (0, 0)))
        operands.append(g.reshape(1, -1))
        out_specs.append(row_block)
        out_shape.append(jax.ShapeDtypeStruct((m, D_MODEL), BF16))
    return pl.pallas_call(
        functools.partial(_down_kernel, emit_norm=emit_norm),
        grid=(m // bm, FF_HIDDEN // bk),
        in_specs=in_specs,
        out_specs=out_specs,
        out_shape=out_shape,
        scratch_shapes=[pltpu.VMEM((bm, D_MODEL), F32)],
        compiler_params=_params("parallel", "arbitrary"),
        name="down_proj",
    )(*operands)


def _rope_layout(t):
    zeros = jnp.zeros(t.shape[:-1] + (32,), t.dtype)
    return jnp.concatenate([t[..., :32], zeros, t[..., 32:], zeros], axis=-1)


def _head_layout(t):
    return jnp.concatenate([t[..., :MLA_NOPE], _rope_layout(t[..., MLA_NOPE:])], axis=-1)


def _rope_tables():
    length = N_META + SEQ
    inv = 1.0 / (ROPE_THETA ** (jnp.arange(0, MLA_ROPE, 2, dtype=F32) / MLA_ROPE))
    ang = jnp.arange(length, dtype=F32)[:, None] * inv[None, :]
    cos, sin = jnp.cos(ang), jnp.sin(ang)
    zeros = jnp.zeros_like(cos)
    ctab = jnp.concatenate([cos, zeros, cos, zeros], axis=-1)
    stab = jnp.concatenate([-sin, zeros, sin, zeros], axis=-1)
    return ctab, stab, cos.T, sin.T


def kernel(x, meta_tokens, norm1_g, w_in, gla_gate_w2, gla_gate_b, gla_onorm_g, w_branch_a,
           q_a_norm_g, w_uq, kv_a_norm_g, w_ukv, q_norm_g, k_norm_g, w_branch_b, w_out,
           norm2_g, w_gate_up, w_down):
    ctab, stab, cos_t, sin_t = _rope_tables()
    ctab_m, stab_m = ctab[:N_META], stab[:N_META]
    ctab_r, stab_r = ctab[N_META:], stab[N_META:]
    cos_tm, sin_tm = cos_t[:, :N_META], sin_t[:, :N_META]
    cos_tr, sin_tr = cos_t[:, N_META:], sin_t[:, N_META:]

    h_res = x.reshape(ROWS, D_MODEL)
    hm_res = meta_tokens.astype(F32)
    s_zero = jnp.zeros((GLA_HEADS, GLA_DK, GLA_DV), F32)

    w_in_t = jnp.swapaxes(w_in, 1, 2)
    w_kr = w_in_t[:, 7184:7248]
    zrow = jnp.zeros((DEPTH, 32, D_MODEL), F32)
    w_small = jnp.concatenate(
        [w_in_t[:, 6144:6160], jnp.zeros((DEPTH, 128 - GLA_GATE_RANK, D_MODEL), F32),
         w_kr[:, :32], zrow, w_kr[:, 32:], zrow], axis=1)
    wa = w_branch_a.astype(BF16)
    wb = w_branch_b.astype(BF16)
    wo = w_out.astype(BF16)
    wd = w_down.astype(BF16)

    for l in range(DEPTH):
        last = l == DEPTH - 1
        w2p = jnp.concatenate(
            [gla_gate_w2[l], jnp.zeros((128 - GLA_GATE_RANK, GLA_HEADS * GLA_DK), F32)],
            axis=0).astype(BF16)
        wqt = _head_layout(w_uq[l].reshape(MLA_LORA, MLA_HEADS, MLA_QK)).reshape(
            MLA_LORA, MLA_HEADS * MLA_HEAD_PAD).T.astype(BF16)
        wkv3 = w_ukv[l].reshape(MLA_LORA, MLA_HEADS, MLA_NOPE + MLA_V)
        wk = wkv3[:, :, :MLA_NOPE].reshape(MLA_LORA, MLA_HEADS * MLA_NOPE).astype(BF16)
        wvt = jnp.pad(jnp.transpose(wkv3[:, :, MLA_NOPE:], (1, 2, 0)),
                      ((0, 0), (0, VT_ROWS - MLA_V), (0, 0))).reshape(
            MLA_HEADS * VT_ROWS, MLA_LORA).astype(BF16)
        qg = (_head_layout(q_norm_g[l]) * (MLA_QK ** -0.5 * LOG2E)).reshape(MLA_HEAD_PAD, 1)
        kg = _head_layout(k_norm_g[l]).reshape(1, MLA_HEAD_PAD)

        if l == 0:
            h, zs = _rms_small(h_res, norm1_g[0], w_small, 0, RMS_SMALL_ROWS)
            hm, zsm = _rms_small(hm_res, norm1_g[0], w_small, 0, N_META)
        else:
            zs = _matmul_nt(h, w_small, l, IN_PROJ_SMALL_ROWS, N_SMALL, name="in_proj_small")
            zsm = _matmul_nt(hm, w_small, l, N_META, N_SMALL, name="in_proj_small_meta")
        z, zm = _in_proj(h, hm, w_in_t, l, *IN_PROJ_BLOCK, name="in_proj")

        gla_m, s_meta = _gla(zm, zsm, w2p, gla_gate_b[l], gla_onorm_g[l], s_zero,
                             batch=1, tokens=N_META, block=N_META, chunk=N_META, emit_state=True)
        (gla_r,) = _gla(z, zs, w2p, gla_gate_b[l], gla_onorm_g[l], s_meta,
                        batch=BATCH, tokens=SEQ, block=GLA_TOKENS, chunk=GLA_CHUNK, emit_state=False)

        km, vmt = _kvproj(zm, zsm, kv_a_norm_g[l], wk, wvt, kg, ctab_m, stab_m, N_META)
        kr, vrt = _kvproj(z, zs, kv_a_norm_g[l], wk, wvt, kg, ctab_r, stab_r, PROJ_ROWS)
        kmp = jnp.pad(km, ((0, 128 - N_META), (0, 0)))
        vmtp = jnp.pad(vmt, ((0, 0), (0, 128 - N_META)))
        qr = _qproj(z, q_a_norm_g[l], wqt, qg, cos_tr, sin_tr, PROJ_ROWS)
        att_r = _attention(qr, kr, vrt, km, vmt, ATTN_BLOCK, ATTN_HEADS)

        merged = _merge(gla_r, att_r, wa, wb, l, z, *MERGE_BLOCK)
        h_res, h2 = _outproj(merged, wo, l, h_res, norm2_g[l], OUT_PROJ_ROWS)
        if last:
            (act,) = _gateup(h2, None, w_gate_up, l, *GATE_UP_BLOCK)
            (h_res,) = _down(act, wd, l, h_res, None, *DOWN_BLOCK)
        else:
            qm = _qproj(zm, q_a_norm_g[l], wqt, qg, cos_tm, sin_tm, N_META)
            att_m = _meta_attention(qm.T, kmp, vmtp)
            merged_m = _merge(gla_m, att_m, wa, wb, l, zm, N_META, MERGE_BLOCK[1])
            hm_res, hm2 = _outproj(merged_m, wo, l, hm_res, norm2_g[l], N_META)
            act, act_m = _gateup(h2, hm2, w_gate_up, l, *GATE_UP_BLOCK)
            h_res, h = _down(act, wd, l, h_res, norm1_g[l + 1], *DOWN_BLOCK)
            hm_res, hm = _down(act_m, wd, l, hm_res, norm1_g[l + 1], N_META, DOWN_META_K)

    return h_res.reshape(BATCH, SEQ, D_MODEL)
```

```python
import functools

import jax
import jax.numpy as jnp
from jax import lax
from jax.experimental import pallas as pl
from jax.experimental.pallas import tpu as pltpu

D_MODEL = 2048
BATCH = 4
SEQ = 4096
DEPTH = 2
N_META = 16
ROWS = BATCH * SEQ

GLA_HEADS = 4
GLA_DK = 256
GLA_DV = 512
GLA_GATE_RANK = 16
GLA_GATE_TAU = 16.0
GLA_CHUNK = 64

MLA_HEADS = 16
MLA_LORA = 512
MLA_NOPE = 128
MLA_ROPE = 64
MLA_QK = 192
MLA_V = 128
VT_ROWS = 144
MLA_HEAD_PAD = 256
ROPE_THETA = 10000.0
FF_HIDDEN = 5632
EPS = 1e-6
LOG2E = 1.4426950408889634

W_IN_ZA = 6144
W_IN_C, W_IN_C_END = 6160, 7184
W_IN_G = 7248
Z_Q, Z_K, Z_V, Z_R, Z_CQ, Z_CKV, Z_A, Z_B = 0, 1024, 2048, 4096, 6144, 6656, 7168, 9216
N_Z = 11264
N_SMALL = 256

RMS_SMALL_ROWS = 1024
IN_PROJ_BLOCK = (1024, 1024)
IN_PROJ_SMALL_ROWS = 2048
GLA_TOKENS = 512
PROJ_ROWS = 1024
ATTN_BLOCK = 512
ATTN_HEADS = 4
MERGE_BLOCK = (1024, 512)
OUT_PROJ_ROWS = 512
GATE_UP_BLOCK = (1024, 512)
DOWN_BLOCK = (512, 2816)
DOWN_META_K = 1408

VMEM_LIMIT = 56 * 1024 * 1024
BF16 = jnp.bfloat16
F32 = jnp.float32

_NT = (((1,), (1,)), ((), ()))
_TN = (((0,), (0,)), ((), ()))


def _params(*sem):
    return pltpu.CompilerParams(dimension_semantics=sem, vmem_limit_bytes=VMEM_LIMIT)


def _sigmoid(x):
    return 1.0 / (1.0 + jnp.exp(-x))


def _rms_small_kernel(x_ref, g_ref, wt_ref, h_ref, zs_ref):
    x = x_ref[...]
    ms = jnp.mean(x * x, axis=-1, keepdims=True)
    h = (x * lax.rsqrt(ms + EPS) * g_ref[...]).astype(BF16)
    h_ref[...] = h
    zs_ref[...] = lax.dot_general(h, wt_ref[...].astype(BF16), _NT, preferred_element_type=F32)


def _rms_small(x, g, wt3, layer, bm):
    m, d = x.shape
    n = wt3.shape[1]
    return pl.pallas_call(
        _rms_small_kernel,
        grid=(m // bm,),
        in_specs=[pl.BlockSpec((bm, d), lambda i: (i, 0)),
                  pl.BlockSpec((1, d), lambda i: (0, 0)),
                  pl.BlockSpec((None, n, d), lambda i: (layer, 0, 0))],
        out_specs=[pl.BlockSpec((bm, d), lambda i: (i, 0)),
                   pl.BlockSpec((bm, n), lambda i: (i, 0))],
        out_shape=[jax.ShapeDtypeStruct((m, d), BF16),
                   jax.ShapeDtypeStruct((m, n), F32)],
        compiler_params=_params("parallel"),
        name="rmsnorm_small_proj",
    )(x, g.reshape(1, d), wt3)


def _mm_nt_kernel(a_ref, wt_ref, o_ref):
    o_ref[...] = lax.dot_general(a_ref[...], wt_ref[...].astype(BF16), _NT,
                                 preferred_element_type=F32)


def _matmul_nt(a, wt3, layer, bm, bn, name):
    m, k = a.shape
    n = wt3.shape[1]
    return pl.pallas_call(
        _mm_nt_kernel,
        grid=(m // bm, n // bn),
        in_specs=[pl.BlockSpec((bm, k), lambda i, j: (i, 0)),
                  pl.BlockSpec((None, bn, k), lambda i, j: (layer, j, 0))],
        out_specs=pl.BlockSpec((bm, bn), lambda i, j: (i, j)),
        out_shape=jax.ShapeDtypeStruct((m, n), F32),
        compiler_params=_params("parallel", "parallel"),
        name=name,
    )(a, wt3)


def _mm_nt_wstat_kernel(a_ref, am_ref, wt_ref, o_ref, om_ref, wb_ref):
    @pl.when(pl.program_id(1) == 0)
    def _():
        wb_ref[...] = wt_ref[0].astype(BF16)
        om_ref[...] = lax.dot_general(am_ref[...], wb_ref[...], _NT, preferred_element_type=F32)

    o_ref[...] = lax.dot_general(a_ref[...], wb_ref[...], _NT, preferred_element_type=F32)


def _in_proj_row0(j, bn):
    n_a = W_IN_ZA // bn
    n_c = (W_IN_C_END - W_IN_C) // bn
    return jnp.where(j < n_a, j * bn,
                     jnp.where(j < n_a + n_c, W_IN_C + (j - n_a) * bn,
                               W_IN_G + (j - n_a - n_c) * bn))


def _in_proj(a, a_meta, wt3, layer, bm, bn, name):
    m, k = a.shape
    return pl.pallas_call(
        _mm_nt_wstat_kernel,
        grid=(N_Z // bn, m // bm),
        in_specs=[pl.BlockSpec((bm, k), lambda j, i: (i, 0)),
                  pl.BlockSpec((N_META, k), lambda j, i: (0, 0)),
                  pl.BlockSpec((pl.Element(1), pl.Element(bn), pl.Element(k)),
                               lambda j, i: (layer, pl.multiple_of(_in_proj_row0(j, bn), 16), 0))],
        out_specs=[pl.BlockSpec((bm, bn), lambda j, i: (i, j)),
                   pl.BlockSpec((N_META, bn), lambda j, i: (0, j))],
        out_shape=[jax.ShapeDtypeStruct((m, N_Z), F32),
                   jax.ShapeDtypeStruct((N_META, N_Z), F32)],
        scratch_shapes=[pltpu.VMEM((bn, k), BF16)],
        compiler_params=_params("arbitrary", "arbitrary"),
        name=name,
    )(a, a_meta, wt3)


def _gla_kernel(q_ref, k_ref, v_ref, r_ref, zlr_ref, w2_ref, gb_ref, og_ref, s0_ref,
                o_ref, *rest, chunk, n_chunks, n_total, emit_state):
    if emit_state:
        sfin_ref, st_ref, b_sc = rest
    else:
        st_ref, b_sc = rest
    t = pl.program_id(1)
    heads = range(GLA_HEADS)

    row = lax.broadcasted_iota(jnp.int32, (chunk, chunk), 0)
    col = lax.broadcasted_iota(jnp.int32, (chunk, chunk), 1)
    causal = col <= row
    tri = jnp.where(causal, 1.0, 0.0).astype(BF16)
    w2 = w2_ref[...]
    gb = gb_ref[...]
    og = og_ref[...]
    kdim = GLA_HEADS * GLA_DK

    def log_gate_split(gc):
        sl = pl.ds(pl.multiple_of(gc * chunk, chunk), chunk)
        logit = jnp.dot(zlr_ref[sl, :].astype(BF16), w2, preferred_element_type=F32) + gb
        g = (jnp.minimum(logit, 0.0) - jnp.log1p(jnp.exp(-jnp.abs(logit)))) * (1.0 / GLA_GATE_TAU)
        g_hi = g.astype(BF16)
        g_lo = (g - g_hi.astype(F32)).astype(BF16)
        return jnp.concatenate([g_hi, g_lo], axis=1)

    def log_decay(g_split):
        cs = jnp.dot(tri, g_split, preferred_element_type=F32)
        return cs[:, :kdim] + cs[:, kdim:]

    @pl.when(t == 0)
    def _():
        st_ref[...] = s0_ref[...]
        b_sc[...] = log_decay(log_gate_split(0))

    def body(c, carry):
        sl = pl.ds(pl.multiple_of(c * chunk, chunk), chunk)
        g_split_next = log_gate_split(jnp.minimum(t * n_chunks + c + 1, n_total - 1))
        r = r_ref[sl, :]
        out_gate = r * _sigmoid(r)
        st = [st_ref[h] for h in heads]
        st_bf = [st[h].astype(BF16) for h in heads]
        b = b_sc[...]
        b_last = b[chunk - 1:chunk, :]
        eb = jnp.exp(b)
        enb = jnp.exp(-b)
        erel = jnp.exp(b_last - b)
        elast = jnp.exp(b_last)
        k = k_ref[sl, :]
        qd = (q_ref[sl, :] * (GLA_DK ** -0.5) * eb).astype(BF16)
        kd = (k * enb).astype(BF16)
        k2 = (k * erel).astype(BF16)
        v = v_ref[sl, :].astype(BF16)

        def hk(x, h):
            return x[:, h * GLA_DK:(h + 1) * GLA_DK]

        def hv(x, h):
            return x[:, h * GLA_DV:(h + 1) * GLA_DV]

        o_state = [jnp.dot(hk(qd, h), st_bf[h], preferred_element_type=F32) for h in heads]
        a = [lax.dot_general(hk(qd, h), hk(kd, h), _NT, preferred_element_type=F32) for h in heads]
        upd = [lax.dot_general(hk(k2, h), hv(v, h), _TN, preferred_element_type=F32)
               for h in heads]
        b_sc[...] = log_decay(g_split_next)
        o_local = [jnp.dot(jnp.where(causal, a[h], 0.0).astype(BF16), hv(v, h),
                           preferred_element_type=F32) for h in heads]
        decay_col = jnp.broadcast_to(elast, (128, kdim)).T
        for h in heads:
            col = decay_col[h * GLA_DK:(h + 1) * GLA_DK]
            st_ref[h] = st[h] * jnp.concatenate([col] * (GLA_DV // 128), axis=1) + upd[h]
        for h in heads:
            o = o_local[h] + o_state[h]
            ms = jnp.mean(o * o, axis=-1, keepdims=True)
            on = o * lax.rsqrt(ms + EPS) * og
            o_ref[sl, h * GLA_DV:(h + 1) * GLA_DV] = (on * hv(out_gate, h)).astype(o_ref.dtype)
        return carry

    lax.fori_loop(0, n_chunks, body, 0)

    if emit_state:
        @pl.when(t == pl.num_programs(1) - 1)
        def _():
            sfin_ref[...] = st_ref[...]


def _gla(z, zs, w2p, gate_b, onorm_g, s0, *, batch, tokens, block, chunk, emit_state):
    nt = tokens // block
    rows = batch * tokens
    kern = functools.partial(_gla_kernel, chunk=chunk, n_chunks=block // chunk,
                             n_total=tokens // chunk, emit_state=emit_state)
    kdim = GLA_HEADS * GLA_DK
    vdim = GLA_HEADS * GLA_DV
    in_specs = [
        pl.BlockSpec((block, kdim), lambda b, t: (b * nt + t, Z_Q // kdim)),
        pl.BlockSpec((block, kdim), lambda b, t: (b * nt + t, Z_K // kdim)),
        pl.BlockSpec((block, vdim), lambda b, t: (b * nt + t, Z_V // vdim)),
        pl.BlockSpec((block, vdim), lambda b, t: (b * nt + t, Z_R // vdim)),
        pl.BlockSpec((tokens, 128), lambda b, t: (b, 0)),
        pl.BlockSpec((128, kdim), lambda b, t: (0, 0)),
        pl.BlockSpec((1, kdim), lambda b, t: (0, 0)),
        pl.BlockSpec((1, GLA_DV), lambda b, t: (0, 0)),
        pl.BlockSpec((GLA_HEADS, GLA_DK, GLA_DV), lambda b, t: (0, 0, 0)),
    ]
    out_specs = [pl.BlockSpec((block, vdim), lambda b, t: (b * nt + t, 0))]
    out_shape = [jax.ShapeDtypeStruct((rows, vdim), BF16)]
    if emit_state:
        out_specs.append(pl.BlockSpec((GLA_HEADS, GLA_DK, GLA_DV), lambda b, t: (b, 0, 0)))
        out_shape.append(jax.ShapeDtypeStruct((batch * GLA_HEADS, GLA_DK, GLA_DV), F32))
    res = pl.pallas_call(
        kern,
        grid=(batch, nt),
        in_specs=in_specs,
        out_specs=out_specs,
        out_shape=out_shape,
        scratch_shapes=[pltpu.VMEM((GLA_HEADS, GLA_DK, GLA_DV), F32),
                        pltpu.VMEM((chunk, kdim), F32)],
        compiler_params=_params("parallel", "arbitrary"),
        name="gla",
    )(z, z, z, z, zs, w2p, gate_b.reshape(1, -1), onorm_g.reshape(1, -1), s0)
    return res


def _rope(x, c, s):
    return x * c + pltpu.roll(x, 64, 1) * s


def _qproj_kernel(zc_ref, ng_ref, wt_ref, hg_ref, c_ref, s_ref, o_ref, *, heads_per_dot):
    x = zc_ref[...]
    ms = jnp.mean(x * x, axis=-1, keepdims=True)
    xn = (x * lax.rsqrt(ms + EPS) * ng_ref[...]).astype(BF16)
    bm = x.shape[0]
    c = c_ref[...]
    s = s_ref[...]
    hg = jnp.broadcast_to(hg_ref[...], (MLA_HEAD_PAD, bm))
    for hc in range(MLA_HEADS // heads_per_dot):
        base = hc * heads_per_dot * MLA_HEAD_PAD
        yt = lax.dot_general(wt_ref[base:base + heads_per_dot * MLA_HEAD_PAD, :], xn, _NT,
                             preferred_element_type=F32)
        for j in range(heads_per_dot):
            lo = base + j * MLA_HEAD_PAD
            y = yt[j * MLA_HEAD_PAD:(j + 1) * MLA_HEAD_PAD]
            ms = jnp.sum(y * y, axis=0, keepdims=True) * (1.0 / MLA_QK)
            yn = y * lax.rsqrt(ms + EPS) * hg
            x1 = yn[128:160]
            x2 = yn[192:224]
            o_ref[lo:lo + 128, :] = yn[:128].astype(o_ref.dtype)
            o_ref[lo + 128:lo + 160, :] = (x1 * c - x2 * s).astype(o_ref.dtype)
            o_ref[lo + 160:lo + 192, :] = yn[160:192].astype(o_ref.dtype)
            o_ref[lo + 192:lo + 224, :] = (x2 * c + x1 * s).astype(o_ref.dtype)
            o_ref[lo + 224:lo + 256, :] = yn[224:256].astype(o_ref.dtype)


def _qproj(z, ng, wt, hg_col, cos_t, sin_t, bm):
    m = z.shape[0]
    nt = cos_t.shape[1] // bm
    kern = functools.partial(_qproj_kernel, heads_per_dot=1)
    return pl.pallas_call(
        kern,
        grid=(m // bm,),
        in_specs=[pl.BlockSpec((bm, MLA_LORA), lambda i: (i, Z_CQ // MLA_LORA)),
                  pl.BlockSpec((1, MLA_LORA), lambda i: (0, 0)),
                  pl.BlockSpec((MLA_HEADS * MLA_HEAD_PAD, MLA_LORA), lambda i: (0, 0)),
                  pl.BlockSpec((MLA_HEAD_PAD, 1), lambda i: (0, 0)),
                  pl.BlockSpec((MLA_ROPE // 2, bm), lambda i: (0, i % nt)),
                  pl.BlockSpec((MLA_ROPE // 2, bm), lambda i: (0, i % nt))],
        out_specs=pl.BlockSpec((MLA_HEADS * MLA_HEAD_PAD, bm), lambda i: (0, i)),
        out_shape=jax.ShapeDtypeStruct((MLA_HEADS * MLA_HEAD_PAD, m), BF16),
        compiler_params=_params("parallel"),
        name="q_proj",
    )(z, ng.reshape(1, -1), wt, hg_col, cos_t, sin_t)


def _kvproj_kernel(zc_ref, kr_ref, ng_ref, wk_ref, wvt_ref, hg_ref, c_ref, s_ref, k_ref, vt_ref):
    x = zc_ref[...]
    ms = jnp.mean(x * x, axis=-1, keepdims=True)
    xn = (x * lax.rsqrt(ms + EPS) * ng_ref[...]).astype(BF16)
    c = c_ref[...]
    s = s_ref[...]
    hg = hg_ref[...]
    kr = kr_ref[...]
    kr_ss = jnp.sum(kr * kr, axis=-1, keepdims=True)
    kr_rot = _rope(kr * hg[:, 128:], c, s)
    for hp in range(MLA_HEADS // 2):
        y = jnp.dot(xn, wk_ref[:, hp * 256:(hp + 1) * 256], preferred_element_type=F32)
        for sub in range(2):
            lo = (2 * hp + sub) * MLA_HEAD_PAD
            kn = y[:, sub * MLA_NOPE:(sub + 1) * MLA_NOPE]
            ms = (jnp.sum(kn * kn, axis=-1, keepdims=True) + kr_ss) * (1.0 / MLA_QK)
            rs = lax.rsqrt(ms + EPS)
            k_ref[:, lo:lo + 128] = (kn * rs * hg[:, :128]).astype(k_ref.dtype)
            k_ref[:, lo + 128:lo + 256] = (kr_rot * rs).astype(k_ref.dtype)
    rows_per_dot = 4 * VT_ROWS
    for j in range(MLA_HEADS * VT_ROWS // rows_per_dot):
        sl = slice(j * rows_per_dot, (j + 1) * rows_per_dot)
        vt_ref[sl, :] = lax.dot_general(wvt_ref[sl, :], xn, _NT,
                                        preferred_element_type=F32).astype(vt_ref.dtype)
    ones = jnp.ones((VT_ROWS - MLA_V, x.shape[0]), vt_ref.dtype)
    for h in range(MLA_HEADS):
        vt_ref[h * VT_ROWS + MLA_V:(h + 1) * VT_ROWS, :] = ones


def _kvproj(z, zs, ng, wk, wvt, hg, ctab, stab, bm):
    m = z.shape[0]
    nt = ctab.shape[0] // bm
    return pl.pallas_call(
        _kvproj_kernel,
        grid=(m // bm,),
        in_specs=[pl.BlockSpec((bm, MLA_LORA), lambda i: (i, Z_CKV // MLA_LORA)),
                  pl.BlockSpec((bm, 128), lambda i: (i, 1)),
                  pl.BlockSpec((1, MLA_LORA), lambda i: (0, 0)),
                  pl.BlockSpec((MLA_LORA, MLA_HEADS * MLA_NOPE), lambda i: (0, 0)),
                  pl.BlockSpec((MLA_HEADS * VT_ROWS, MLA_LORA), lambda i: (0, 0)),
                  pl.BlockSpec((1, MLA_HEAD_PAD), lambda i: (0, 0)),
                  pl.BlockSpec((bm, 128), lambda i: (i % nt, 0)),
                  pl.BlockSpec((bm, 128), lambda i: (i % nt, 0))],
        out_specs=[pl.BlockSpec((bm, MLA_HEADS * MLA_HEAD_PAD), lambda i: (i, 0)),
                   pl.BlockSpec((MLA_HEADS * VT_ROWS, bm), lambda i: (0, i))],
        out_shape=[jax.ShapeDtypeStruct((m, MLA_HEADS * MLA_HEAD_PAD), BF16),
                   jax.ShapeDtypeStruct((MLA_HEADS * VT_ROWS, m), BF16)],
        compiler_params=_params("parallel"),
        name="kv_proj",
    )(z, zs, ng.reshape(1, -1), wk, wvt, hg, ctab, stab)


def _attn_kernel(qt_ref, k_ref, vt_ref, km_ref, vmt_ref, o_ref, m_sc, acc_sc, s_sc, *, blk, heads,
                 nq):
    i = pl.program_id(2)

    def queries(qi, h):
        return qt_ref[h * MLA_HEAD_PAD:(h + 1) * MLA_HEAD_PAD,
                      pl.ds(pl.multiple_of(qi * blk, blk), blk)]

    s_meta = [jnp.dot(km_ref[:, h * MLA_HEAD_PAD:(h + 1) * MLA_HEAD_PAD], queries(i, h),
                      preferred_element_type=F32) for h in range(heads)]
    p_meta = []
    for h in range(heads):
        m = jnp.max(s_meta[h], axis=0, keepdims=True)
        p_meta.append(jnp.exp2(s_meta[h] - m).astype(BF16))
        m_sc[h] = m
    for h in range(heads):
        acc_sc[h] = jnp.dot(vmt_ref[h * VT_ROWS:(h + 1) * VT_ROWS, :], p_meta[h],
                            preferred_element_type=F32)

    def scores(qi, kb, h):
        ksl = pl.ds(pl.multiple_of(kb * blk, blk), blk)
        return jnp.dot(k_ref[ksl, h * MLA_HEAD_PAD:(h + 1) * MLA_HEAD_PAD], queries(qi, h),
                       preferred_element_type=F32)

    @pl.when(i == 0)
    def _():
        for h in range(heads):
            s_sc[h] = scores(0, 0, h)

    def step(kb, masked):
        ksl = pl.ds(pl.multiple_of(kb * blk, blk), blk)
        tiles = [s_sc[h] for h in range(heads)]
        for h in range(heads):
            s = tiles[h]
            if masked:
                s_sc[h] = scores(jnp.minimum(i + 1, nq - 1), 0, h)
                key = lax.broadcasted_iota(jnp.int32, s.shape, 0)
                qry = lax.broadcasted_iota(jnp.int32, s.shape, 1)
                s = jnp.where(key <= qry, s, -1e30)
            else:
                s_sc[h] = scores(i, kb + 1, h)
            m_prev = m_sc[h]
            m_new = jnp.maximum(m_prev, jnp.max(s, axis=0, keepdims=True))
            alpha = jnp.exp2(m_prev - m_new)
            p = jnp.exp2(s - m_new)
            acc_sc[h] = alpha * acc_sc[h] + jnp.dot(
                vt_ref[h * VT_ROWS:(h + 1) * VT_ROWS, ksl], p.astype(BF16),
                preferred_element_type=F32)
            m_sc[h] = m_new

    def body(kb, carry):
        step(kb, False)
        return carry

    lax.fori_loop(0, i, body, 0)
    step(i, True)
    for h in range(heads):
        acc = acc_sc[h]
        o_ref[:, h * MLA_V:(h + 1) * MLA_V] = (
            acc[:MLA_V] * (1.0 / acc[MLA_V:MLA_V + 1])).T.astype(o_ref.dtype)


def _attention(q, k, vt, km, vmt, blk, heads):
    nq = SEQ // blk
    kern = functools.partial(_attn_kernel, blk=blk, heads=heads, nq=nq)
    return pl.pallas_call(
        kern,
        grid=(BATCH, MLA_HEADS // heads, nq),
        in_specs=[pl.BlockSpec((heads * MLA_HEAD_PAD, SEQ), lambda b, h, i: (h, b)),
                  pl.BlockSpec((SEQ, heads * MLA_HEAD_PAD), lambda b, h, i: (b, h)),
                  pl.BlockSpec((heads * VT_ROWS, SEQ), lambda b, h, i: (h, b)),
                  pl.BlockSpec((N_META, heads * MLA_HEAD_PAD), lambda b, h, i: (0, h)),
                  pl.BlockSpec((heads * VT_ROWS, N_META), lambda b, h, i: (h, 0))],
        out_specs=pl.BlockSpec((blk, heads * MLA_V), lambda b, h, i: (b * nq + i, h)),
        out_shape=jax.ShapeDtypeStruct((ROWS, MLA_HEADS * MLA_V), BF16),
        scratch_shapes=[pltpu.VMEM((heads, 1, blk), F32),
                        pltpu.VMEM((heads, VT_ROWS, blk), F32), pltpu.VMEM((heads, blk, blk), F32)],
        compiler_params=_params("parallel", "parallel", "arbitrary"),
        name="mla_attention",
    )(q, k, vt, km, vmt)


def _meta_attn_kernel(q_ref, km_ref, vmt_ref, o_ref):
    s = lax.dot_general(q_ref[...], km_ref[...], _NT, preferred_element_type=F32)
    row = lax.broadcasted_iota(jnp.int32, s.shape, 0)
    col = lax.broadcasted_iota(jnp.int32, s.shape, 1)
    s = jnp.where(col <= row, s, -1e30)
    m = jnp.max(s, axis=-1, keepdims=True)
    p = jnp.exp2(s - m)
    l = jnp.sum(p, axis=-1, keepdims=True)
    o = lax.dot_general(p.astype(BF16), vmt_ref[:MLA_V, :], _NT, preferred_element_type=F32)
    o_ref[...] = (o / l).astype(o_ref.dtype)


def _meta_attention(qm, kmp, vmtp):
    return pl.pallas_call(
        _meta_attn_kernel,
        grid=(MLA_HEADS,),
        in_specs=[pl.BlockSpec((N_META, MLA_HEAD_PAD), lambda h: (0, h)),
                  pl.BlockSpec((128, MLA_HEAD_PAD), lambda h: (0, h)),
                  pl.BlockSpec((VT_ROWS, 128), lambda h: (h, 0))],
        out_specs=pl.BlockSpec((N_META, MLA_V), lambda h: (0, h)),
        out_shape=jax.ShapeDtypeStruct((N_META, MLA_HEADS * MLA_V), BF16),
        compiler_params=_params("parallel"),
        name="meta_attention",
    )(qm, kmp, vmtp)


def _merge_kernel(a_ref, b_ref, wa_ref, wb_ref, za_ref, zb_ref, o_ref):
    ga = _sigmoid(za_ref[...])
    gb = _sigmoid(zb_ref[...])
    ya = jnp.dot(a_ref[...], wa_ref[...], preferred_element_type=F32)
    yb = jnp.dot(b_ref[...], wb_ref[...], preferred_element_type=F32)
    o_ref[...] = (ga * ya + gb * yb).astype(o_ref.dtype)


def _merge(a, b, wa3, wb3, layer, z, bm, bn):
    m = a.shape[0]
    return pl.pallas_call(
        _merge_kernel,
        grid=(m // bm, D_MODEL // bn),
        in_specs=[pl.BlockSpec((bm, D_MODEL), lambda i, j: (i, 0)),
                  pl.BlockSpec((bm, D_MODEL), lambda i, j: (i, 0)),
                  pl.BlockSpec((None, D_MODEL, bn), lambda i, j: (layer, 0, j)),
                  pl.BlockSpec((None, D_MODEL, bn), lambda i, j: (layer, 0, j)),
                  pl.BlockSpec((bm, bn), lambda i, j: (i, Z_A // bn + j)),
                  pl.BlockSpec((bm, bn), lambda i, j: (i, Z_B // bn + j))],
        out_specs=pl.BlockSpec((bm, bn), lambda i, j: (i, j)),
        out_shape=jax.ShapeDtypeStruct((m, D_MODEL), BF16),
        compiler_params=_params("parallel", "parallel"),
        name="branch_merge",
    )(a, b, wa3, wb3, z, z)


def _outproj_kernel(m_ref, w_ref, h_ref, g_ref, ho_ref, hn_ref):
    hn = h_ref[...] + jnp.dot(m_ref[...], w_ref[...], preferred_element_type=F32)
    ho_ref[...] = hn
    ms = jnp.mean(hn * hn, axis=-1, keepdims=True)
    hn_ref[...] = (hn * lax.rsqrt(ms + EPS) * g_ref[...]).astype(hn_ref.dtype)


def _outproj(mg, w3, layer, h, g, bm):
    m = mg.shape[0]
    return pl.pallas_call(
        _outproj_kernel,
        grid=(m // bm,),
        in_specs=[pl.BlockSpec((bm, D_MODEL), lambda i: (i, 0)),
                  pl.BlockSpec((None, D_MODEL, D_MODEL), lambda i: (layer, 0, 0)),
                  pl.BlockSpec((bm, D_MODEL), lambda i: (i, 0)),
                  pl.BlockSpec((1, D_MODEL), lambda i: (0, 0))],
        out_specs=[pl.BlockSpec((bm, D_MODEL), lambda i: (i, 0)),
                   pl.BlockSpec((bm, D_MODEL), lambda i: (i, 0))],
        out_shape=[jax.ShapeDtypeStruct((m, D_MODEL), F32),
                   jax.ShapeDtypeStruct((m, D_MODEL), BF16)],
        compiler_params=_params("parallel"),
        name="out_proj",
    )(mg, w3, h, g.reshape(1, -1))


def _gateup_kernel(*refs, with_meta):
    if with_meta:
        h_ref, hm_ref, wg_ref, wu_ref, o_ref, om_ref, wgb_ref, wub_ref = refs
    else:
        h_ref, wg_ref, wu_ref, o_ref, wgb_ref, wub_ref = refs

    def swiglu(h):
        g = jnp.dot(h, wgb_ref[...], preferred_element_type=F32)
        u = jnp.dot(h, wub_ref[...], preferred_element_type=F32)
        return (g * _sigmoid(g) * u).astype(o_ref.dtype)

    @pl.when(pl.program_id(1) == 0)
    def _():
        wgb_ref[...] = wg_ref[...].astype(BF16)
        wub_ref[...] = wu_ref[...].astype(BF16)
        if with_meta:
            om_ref[...] = swiglu(hm_ref[...])

    o_ref[...] = swiglu(h_ref[...])


def _gateup(h, h_meta, w3, layer, bm, bn):
    m = h.shape[0]
    nb = FF_HIDDEN // bn
    with_meta = h_meta is not None
    weight_specs = [pl.BlockSpec((None, D_MODEL, bn), lambda j, i: (layer, 0, j)),
                    pl.BlockSpec((None, D_MODEL, bn), lambda j, i: (layer, 0, nb + j))]
    in_specs = [pl.BlockSpec((bm, D_MODEL), lambda j, i: (i, 0))]
    operands = [h]
    out_specs = [pl.BlockSpec((bm, bn), lambda j, i: (i, j))]
    out_shape = [jax.ShapeDtypeStruct((m, FF_HIDDEN), BF16)]
    if with_meta:
        in_specs.append(pl.BlockSpec((N_META, D_MODEL), lambda j, i: (0, 0)))
        operands.append(h_meta)
        out_specs.append(pl.BlockSpec((N_META, bn), lambda j, i: (0, j)))
        out_shape.append(jax.ShapeDtypeStruct((N_META, FF_HIDDEN), BF16))
    return pl.pallas_call(
        functools.partial(_gateup_kernel, with_meta=with_meta),
        grid=(nb, m // bm),
        in_specs=in_specs + weight_specs,
        out_specs=out_specs,
        out_shape=out_shape,
        scratch_shapes=[pltpu.VMEM((D_MODEL, bn), BF16), pltpu.VMEM((D_MODEL, bn), BF16)],
        compiler_params=_params("arbitrary", "arbitrary"),
        name="gate_up",
    )(*operands, w3, w3)


def _down_kernel(a_ref, w_ref, h_ref, *rest, emit_norm):
    if emit_norm:
        g_ref, ho_ref, hn_ref, acc_ref = rest
    else:
        ho_ref, acc_ref = rest
    k = pl.program_id(1)
    last = pl.num_programs(1) - 1

    def product():
        return jnp.dot(a_ref[...], w_ref[...], preferred_element_type=F32)

    @pl.when(k == 0)
    def _():
        acc_ref[...] = product()

    @pl.when((k > 0) & (k < last))
    def _():
        acc_ref[...] += product()

    @pl.when(k == last)
    def _():
        hn = h_ref[...] + (acc_ref[...] + product())
        ho_ref[...] = hn
        if emit_norm:
            ms = jnp.mean(hn * hn, axis=-1, keepdims=True)
            hn_ref[...] = (hn * lax.rsqrt(ms + EPS) * g_ref[...]).astype(hn_ref.dtype)


def _down(a, w3, layer, h, g, bm, bk):
    m = a.shape[0]
    assert FF_HIDDEN // bk >= 2
    emit_norm = g is not None
    row_block = pl.BlockSpec((bm, D_MODEL), lambda i, k: (i, 0))
    in_specs = [pl.BlockSpec((bm, bk), lambda i, k: (i, k)),
                pl.BlockSpec((None, bk, D_MODEL), lambda i, k: (layer, k, 0)),
                row_block]
    operands = [a, w3, h]
    out_specs = [row_block]
    out_shape = [jax.ShapeDtypeStruct((m, D_MODEL), F32)]
    if emit_norm:
        in_specs.append(pl.BlockSpec((1, D_MODEL), lambda i, k: (0, 0)))
        operands.append(g.reshape(1, -1))
        out_specs.append(row_block)
        out_shape.append(jax.ShapeDtypeStruct((m, D_MODEL), BF16))
    return pl.pallas_call(
        functools.partial(_down_kernel, emit_norm=emit_norm),
        grid=(m // bm, FF_HIDDEN // bk),
        in_specs=in_specs,
        out_specs=out_specs,
        out_shape=out_shape,
        scratch_shapes=[pltpu.VMEM((bm, D_MODEL), F32)],
        compiler_params=_params("parallel", "arbitrary"),
        name="down_proj",
    )(*operands)


def _rope_layout(t):
    zeros = jnp.zeros(t.shape[:-1] + (32,), t.dtype)
    return jnp.concatenate([t[..., :32], zeros, t[..., 32:], zeros], axis=-1)


def _head_layout(t):
    return jnp.concatenate([t[..., :MLA_NOPE], _rope_layout(t[..., MLA_NOPE:])], axis=-1)


def _rope_tables():
    length = N_META + SEQ
    inv = 1.0 / (ROPE_THETA ** (jnp.arange(0, MLA_ROPE, 2, dtype=F32) / MLA_ROPE))
    ang = jnp.arange(length, dtype=F32)[:, None] * inv[None, :]
    cos, sin = jnp.cos(ang), jnp.sin(ang)
    zeros = jnp.zeros_like(cos)
    ctab = jnp.concatenate([cos, zeros, cos, zeros], axis=-1)
    stab = jnp.concatenate([-sin, zeros, sin, zeros], axis=-1)
    return ctab, stab, cos.T, sin.T


def kernel(x, meta_tokens, norm1_g, w_in, gla_gate_w2, gla_gate_b, gla_onorm_g, w_branch_a,
           q_a_norm_g, w_uq, kv_a_norm_g, w_ukv, q_norm_g, k_norm_g, w_branch_b, w_out,
           norm2_g, w_gate_up, w_down):
    ctab, stab, cos_t, sin_t = _rope_tables()
    ctab_m, stab_m = ctab[:N_META], stab[:N_META]
    ctab_r, stab_r = ctab[N_META:], stab[N_META:]
    cos_tm, sin_tm = cos_t[:, :N_META], sin_t[:, :N_META]
    cos_tr, sin_tr = cos_t[:, N_META:], sin_t[:, N_META:]

    h_res = x.reshape(ROWS, D_MODEL)
    hm_res = meta_tokens.astype(F32)
    s_zero = jnp.zeros((GLA_HEADS, GLA_DK, GLA_DV), F32)

    w_in_t = jnp.swapaxes(w_in, 1, 2)
    w_kr = w_in_t[:, 7184:7248]
    zrow = jnp.zeros((DEPTH, 32, D_MODEL), F32)
    w_small = jnp.concatenate(
        [w_in_t[:, 6144:6160], jnp.zeros((DEPTH, 128 - GLA_GATE_RANK, D_MODEL), F32),
         w_kr[:, :32], zrow, w_kr[:, 32:], zrow], axis=1)
    wa = w_branch_a.astype(BF16)
    wb = w_branch_b.astype(BF16)
    wo = w_out.astype(BF16)
    wd = w_down.astype(BF16)

    for l in range(DEPTH):
        last = l == DEPTH - 1
        w2p = jnp.concatenate(
            [gla_gate_w2[l], jnp.zeros((128 - GLA_GATE_RANK, GLA_HEADS * GLA_DK), F32)],
            axis=0).astype(BF16)
        wqt = _head_layout(w_uq[l].reshape(MLA_LORA, MLA_HEADS, MLA_QK)).reshape(
            MLA_LORA, MLA_HEADS * MLA_HEAD_PAD).T.astype(BF16)
        wkv3 = w_ukv[l].reshape(MLA_LORA, MLA_HEADS, MLA_NOPE + MLA_V)
        wk = wkv3[:, :, :MLA_NOPE].reshape(MLA_LORA, MLA_HEADS * MLA_NOPE).astype(BF16)
        wvt = jnp.pad(jnp.transpose(wkv3[:, :, MLA_NOPE:], (1, 2, 0)),
                      ((0, 0), (0, VT_ROWS - MLA_V), (0, 0))).reshape(
            MLA_HEADS * VT_ROWS, MLA_LORA).astype(BF16)
        qg = (_head_layout(q_norm_g[l]) * (MLA_QK ** -0.5 * LOG2E)).reshape(MLA_HEAD_PAD, 1)
        kg = _head_layout(k_norm_g[l]).reshape(1, MLA_HEAD_PAD)

        if l == 0:
            h, zs = _rms_small(h_res, norm1_g[0], w_small, 0, RMS_SMALL_ROWS)
            hm, zsm = _rms_small(hm_res, norm1_g[0], w_small, 0, N_META)
        else:
            zs = _matmul_nt(h, w_small, l, IN_PROJ_SMALL_ROWS, N_SMALL, name="in_proj_small")
            zsm = _matmul_nt(hm, w_small, l, N_META, N_SMALL, name="in_proj_small_meta")
        z, zm = _in_proj(h, hm, w_in_t, l, *IN_PROJ_BLOCK, name="in_proj")

        gla_m, s_meta = _gla(zm, zsm, w2p, gla_gate_b[l], gla_onorm_g[l], s_zero,
                             batch=1, tokens=N_META, block=N_META, chunk=N_META, emit_state=True)
        (gla_r,) = _gla(z, zs, w2p, gla_gate_b[l], gla_onorm_g[l], s_meta,
                        batch=BATCH, tokens=SEQ, block=GLA_TOKENS, chunk=GLA_CHUNK, emit_state=False)

        km, vmt = _kvproj(zm, zsm, kv_a_norm_g[l], wk, wvt, kg, ctab_m, stab_m, N_META)
        kr, vrt = _kvproj(z, zs, kv_a_norm_g[l], wk, wvt, kg, ctab_r, stab_r, PROJ_ROWS)
        kmp = jnp.pad(km, ((0, 128 - N_META), (0, 0)))
        vmtp = jnp.pad(vmt, ((0, 0), (0, 128 - N_META)))
        qr = _qproj(z, q_a_norm_g[l], wqt, qg, cos_tr, sin_tr, PROJ_ROWS)
        att_r = _attention(qr, kr, vrt, km, vmt, ATTN_BLOCK, ATTN_HEADS)

        merged = _merge(gla_r, att_r, wa, wb, l, z, *MERGE_BLOCK)
        h_res, h2 = _outproj(merged, wo, l, h_res, norm2_g[l], OUT_PROJ_ROWS)
        if last:
            (act,) = _gateup(h2, None, w_gate_up, l, *GATE_UP_BLOCK)
            (h_res,) = _down(act, wd, l, h_res, None, *DOWN_BLOCK)
        else:
            qm = _qproj(zm, q_a_norm_g[l], wqt, qg, cos_tm, sin_tm, N_META)
            att_m = _meta_attention(qm.T, kmp, vmtp)
            merged_m = _merge(gla_m, att_m, wa, wb, l, zm, N_META, MERGE_BLOCK[1])
            hm_res, hm2 = _outproj(merged_m, wo, l, hm_res, norm2_g[l], N_META)
            act, act_m = _gateup(h2, hm2, w_gate_up, l, *GATE_UP_BLOCK)
            h_res, h = _down(act, wd, l, h_res, norm1_g[l + 1], *DOWN_BLOCK)
            hm_res, hm = _down(act_m, wd, l, hm_res, norm1_g[l + 1], N_META, DOWN_META_K)

    return h_res.reshape(BATCH, SEQ, D_MODEL)
```

```python
import functools

import jax
import jax.numpy as jnp
from jax import lax
from jax.experimental import pallas as pl
from jax.experimental.pallas import tpu as pltpu

D_MODEL = 2048
BATCH = 4
SEQ = 4096
DEPTH = 2
N_META = 16
ROWS = BATCH * SEQ

GLA_HEADS = 4
GLA_DK = 256
GLA_DV = 512
GLA_GATE_RANK = 16
GLA_GATE_TAU = 16.0
GLA_CHUNK = 64

MLA_HEADS = 16
MLA_LORA = 512
MLA_NOPE = 128
MLA_ROPE = 64
MLA_QK = 192
MLA_V = 128
VT_ROWS = 144
MLA_HEAD_PAD = 256
ROPE_THETA = 10000.0
FF_HIDDEN = 5632
EPS = 1e-6
LOG2E = 1.4426950408889634

W_IN_ZA = 6144
W_IN_C, W_IN_C_END = 6160, 7184
W_IN_G = 7248
Z_Q, Z_K, Z_V, Z_R, Z_CQ, Z_CKV, Z_A, Z_B = 0, 1024, 2048, 4096, 6144, 6656, 7168, 9216
N_Z = 11264
N_SMALL = 256

RMS_SMALL_ROWS = 1024
IN_PROJ_BLOCK = (1024, 1024)
IN_PROJ_SMALL_ROWS = 2048
GLA_TOKENS = 512
PROJ_ROWS = 1024
ATTN_BLOCK = 512
ATTN_HEADS = 4
MERGE_BLOCK = (1024, 512)
OUT_PROJ_ROWS = 512
GATE_UP_BLOCK = (1024, 512)
DOWN_BLOCK = (512, 2816)
DOWN_META_K = 1408

VMEM_LIMIT = 56 * 1024 * 1024
BF16 = jnp.bfloat16
F32 = jnp.float32

_NT = (((1,), (1,)), ((), ()))
_TN = (((0,), (0,)), ((), ()))


def _params(*sem):
    return pltpu.CompilerParams(dimension_semantics=sem, vmem_limit_bytes=VMEM_LIMIT)


def _sigmoid(x):
    return 1.0 / (1.0 + jnp.exp(-x))


def _rms_small_kernel(x_ref, g_ref, wt_ref, h_ref, zs_ref):
    x = x_ref[...]
    ms = jnp.mean(x * x, axis=-1, keepdims=True)
    h = (x * lax.rsqrt(ms + EPS) * g_ref[...]).astype(BF16)
    h_ref[...] = h
    zs_ref[...] = lax.dot_general(h, wt_ref[...].astype(BF16), _NT, preferred_element_type=F32)


def _rms_small(x, g, wt3, layer, bm):
    m, d = x.shape
    n = wt3.shape[1]
    return pl.pallas_call(
        _rms_small_kernel,
        grid=(m // bm,),
        in_specs=[pl.BlockSpec((bm, d), lambda i: (i, 0)),
                  pl.BlockSpec((1, d), lambda i: (0, 0)),
                  pl.BlockSpec((None, n, d), lambda i: (layer, 0, 0))],
        out_specs=[pl.BlockSpec((bm, d), lambda i: (i, 0)),
                   pl.BlockSpec((bm, n), lambda i: (i, 0))],
        out_shape=[jax.ShapeDtypeStruct((m, d), BF16),
                   jax.ShapeDtypeStruct((m, n), F32)],
        compiler_params=_params("parallel"),
        name="rmsnorm_small_proj",
    )(x, g.reshape(1, d), wt3)


def _mm_nt_kernel(a_ref, wt_ref, o_ref):
    o_ref[...] = lax.dot_general(a_ref[...], wt_ref[...].astype(BF16), _NT,
                                 preferred_element_type=F32)


def _matmul_nt(a, wt3, layer, bm, bn, name):
    m, k = a.shape
    n = wt3.shape[1]
    return pl.pallas_call(
        _mm_nt_kernel,
        grid=(m // bm, n // bn),
        in_specs=[pl.BlockSpec((bm, k), lambda i, j: (i, 0)),
                  pl.BlockSpec((None, bn, k), lambda i, j: (layer, j, 0))],
        out_specs=pl.BlockSpec((bm, bn), lambda i, j: (i, j)),
        out_shape=jax.ShapeDtypeStruct((m, n), F32),
        compiler_params=_params("parallel", "parallel"),
        name=name,
    )(a, wt3)


def _mm_nt_wstat_kernel(a_ref, am_ref, wt_ref, o_ref, om_ref, wb_ref):
    @pl.when(pl.program_id(1) == 0)
    def _():
        wb_ref[...] = wt_ref[0].astype(BF16)
        om_ref[...] = lax.dot_general(am_ref[...], wb_ref[...], _NT, preferred_element_type=F32)

    o_ref[...] = lax.dot_general(a_ref[...], wb_ref[...], _NT, preferred_element_type=F32)


def _in_proj_row0(j, bn):
    n_a = W_IN_ZA // bn
    n_c = (W_IN_C_END - W_IN_C) // bn
    return jnp.where(j < n_a, j * bn,
                     jnp.where(j < n_a + n_c, W_IN_C + (j - n_a) * bn,
                               W_IN_G + (j - n_a - n_c) * bn))


def _in_proj(a, a_meta, wt3, layer, bm, bn, name):
    m, k = a.shape
    return pl.pallas_call(
        _mm_nt_wstat_kernel,
        grid=(N_Z // bn, m // bm),
        in_specs=[pl.BlockSpec((bm, k), lambda j, i: (i, 0)),
                  pl.BlockSpec((N_META, k), lambda j, i: (0, 0)),
                  pl.BlockSpec((pl.Element(1), pl.Element(bn), pl.Element(k)),
                               lambda j, i: (layer, pl.multiple_of(_in_proj_row0(j, bn), 16), 0))],
        out_specs=[pl.BlockSpec((bm, bn), lambda j, i: (i, j)),
                   pl.BlockSpec((N_META, bn), lambda j, i: (0, j))],
        out_shape=[jax.ShapeDtypeStruct((m, N_Z), F32),
                   jax.ShapeDtypeStruct((N_META, N_Z), F32)],
        scratch_shapes=[pltpu.VMEM((bn, k), BF16)],
        compiler_params=_params("arbitrary", "arbitrary"),
        name=name,
    )(a, a_meta, wt3)


def _gla_kernel(q_ref, k_ref, v_ref, r_ref, zlr_ref, w2_ref, gb_ref, og_ref, s0_ref,
                o_ref, *rest, chunk, n_chunks, n_total, emit_state):
    if emit_state:
        sfin_ref, st_ref, b_sc = rest
    else:
        st_ref, b_sc = rest
    t = pl.program_id(1)
    heads = range(GLA_HEADS)

    row = lax.broadcasted_iota(jnp.int32, (chunk, chunk), 0)
    col = lax.broadcasted_iota(jnp.int32, (chunk, chunk), 1)
    causal = col <= row
    tri = jnp.where(causal, 1.0, 0.0).astype(BF16)
    w2 = w2_ref[...]
    gb = gb_ref[...]
    og = og_ref[...]
    kdim = GLA_HEADS * GLA_DK

    def log_gate_split(gc):
        sl = pl.ds(pl.multiple_of(gc * chunk, chunk), chunk)
        logit = jnp.dot(zlr_ref[sl, :].astype(BF16), w2, preferred_element_type=F32) + gb
        g = (jnp.minimum(logit, 0.0) - jnp.log1p(jnp.exp(-jnp.abs(logit)))) * (1.0 / GLA_GATE_TAU)
        g_hi = g.astype(BF16)
        g_lo = (g - g_hi.astype(F32)).astype(BF16)
        return jnp.concatenate([g_hi, g_lo], axis=1)

    def log_decay(g_split):
        cs = jnp.dot(tri, g_split, preferred_element_type=F32)
        return cs[:, :kdim] + cs[:, kdim:]

    @pl.when(t == 0)
    def _():
        st_ref[...] = s0_ref[...]
        b_sc[...] = log_decay(log_gate_split(0))

    def body(c, carry):
        sl = pl.ds(pl.multiple_of(c * chunk, chunk), chunk)
        g_split_next = log_gate_split(jnp.minimum(t * n_chunks + c + 1, n_total - 1))
        r = r_ref[sl, :]
        out_gate = r * _sigmoid(r)
        st = [st_ref[h] for h in heads]
        st_bf = [st[h].astype(BF16) for h in heads]
        b = b_sc[...]
        b_last = b[chunk - 1:chunk, :]
        eb = jnp.exp(b)
        enb = jnp.exp(-b)
        erel = jnp.exp(b_last - b)
        elast = jnp.exp(b_last)
        k = k_ref[sl, :]
        qd = (q_ref[sl, :] * (GLA_DK ** -0.5) * eb).astype(BF16)
        kd = (k * enb).astype(BF16)
        k2 = (k * erel).astype(BF16)
        v = v_ref[sl, :].astype(BF16)

        def hk(x, h):
            return x[:, h * GLA_DK:(h + 1) * GLA_DK]

        def hv(x, h):
            return x[:, h * GLA_DV:(h + 1) * GLA_DV]

        o_state = [jnp.dot(hk(qd, h), st_bf[h], preferred_element_type=F32) for h in heads]
        a = [lax.dot_general(hk(qd, h), hk(kd, h), _NT, preferred_element_type=F32) for h in heads]
        upd = [lax.dot_general(hk(k2, h), hv(v, h), _TN, preferred_element_type=F32)
               for h in heads]
        b_sc[...] = log_decay(g_split_next)
        o_local = [jnp.dot(jnp.where(causal, a[h], 0.0).astype(BF16), hv(v, h),
                           preferred_element_type=F32) for h in heads]
        decay_col = jnp.broadcast_to(elast, (128, kdim)).T
        for h in heads:
            col = decay_col[h * GLA_DK:(h + 1) * GLA_DK]
            st_ref[h] = st[h] * jnp.concatenate([col] * (GLA_DV // 128), axis=1) + upd[h]
        for h in heads:
            o = o_local[h] + o_state[h]
            ms = jnp.mean(o * o, axis=-1, keepdims=True)
            on = o * lax.rsqrt(ms + EPS) * og
            o_ref[sl, h * GLA_DV:(h + 1) * GLA_DV] = (on * hv(out_gate, h)).astype(o_ref.dtype)
        return carry

    lax.fori_loop(0, n_chunks, body, 0)

    if emit_state:
        @pl.when(t == pl.num_programs(1) - 1)
        def _():
            sfin_ref[...] = st_ref[...]


def _gla(z, zs, w2p, gate_b, onorm_g, s0, *, batch, tokens, block, chunk, emit_state):
    nt = tokens // block
    rows = batch * tokens
    kern = functools.partial(_gla_kernel, chunk=chunk, n_chunks=block // chunk,
                             n_total=tokens // chunk, emit_state=emit_state)
    kdim = GLA_HEADS * GLA_DK
    vdim = GLA_HEADS * GLA_DV
    in_specs = [
        pl.BlockSpec((block, kdim), lambda b, t: (b * nt + t, Z_Q // kdim)),
        pl.BlockSpec((block, kdim), lambda b, t: (b * nt + t, Z_K // kdim)),
        pl.BlockSpec((block, vdim), lambda b, t: (b * nt + t, Z_V // vdim)),
        pl.BlockSpec((block, vdim), lambda b, t: (b * nt + t, Z_R // vdim)),
        pl.BlockSpec((tokens, 128), lambda b, t: (b, 0)),
        pl.BlockSpec((128, kdim), lambda b, t: (0, 0)),
        pl.BlockSpec((1, kdim), lambda b, t: (0, 0)),
        pl.BlockSpec((1, GLA_DV), lambda b, t: (0, 0)),
        pl.BlockSpec((GLA_HEADS, GLA_DK, GLA_DV), lambda b, t: (0, 0, 0)),
    ]
    out_specs = [pl.BlockSpec((block, vdim), lambda b, t: (b * nt + t, 0))]
    out_shape = [jax.ShapeDtypeStruct((rows, vdim), BF16)]
    if emit_state:
        out_specs.append(pl.BlockSpec((GLA_HEADS, GLA_DK, GLA_DV), lambda b, t: (b, 0, 0)))
        out_shape.append(jax.ShapeDtypeStruct((batch * GLA_HEADS, GLA_DK, GLA_DV), F32))
    res = pl.pallas_call(
        kern,
        grid=(batch, nt),
        in_specs=in_specs,
        out_specs=out_specs,
        out_shape=out_shape,
        scratch_shapes=[pltpu.VMEM((GLA_HEADS, GLA_DK, GLA_DV), F32),
                        pltpu.VMEM((chunk, kdim), F32)],
        compiler_params=_params("parallel", "arbitrary"),
        name="gla",
    )(z, z, z, z, zs, w2p, gate_b.reshape(1, -1), onorm_g.reshape(1, -1), s0)
    return res


def _rope(x, c, s):
    return x * c + pltpu.roll(x, 64, 1) * s


def _qproj_kernel(zc_ref, ng_ref, wt_ref, hg_ref, c_ref, s_ref, o_ref, *, heads_per_dot):
    x = zc_ref[...]
    ms = jnp.mean(x * x, axis=-1, keepdims=True)
    xn = (x * lax.rsqrt(ms + EPS) * ng_ref[...]).astype(BF16)
    bm = x.shape[0]
    c = c_ref[...]
    s = s_ref[...]
    hg = jnp.broadcast_to(hg_ref[...], (MLA_HEAD_PAD, bm))
    for hc in range(MLA_HEADS // heads_per_dot):
        base = hc * heads_per_dot * MLA_HEAD_PAD
        yt = lax.dot_general(wt_ref[base:base + heads_per_dot * MLA_HEAD_PAD, :], xn, _NT,
                             preferred_element_type=F32)
        for j in range(heads_per_dot):
            lo = base + j * MLA_HEAD_PAD
            for g0 in range(0, bm, 128):
                cols = slice(g0, min(g0 + 128, bm))
                y = yt[j * MLA_HEAD_PAD:(j + 1) * MLA_HEAD_PAD, cols]
                ms = jnp.sum(y * y, axis=0, keepdims=True) * (1.0 / MLA_QK)
                yn = y * lax.rsqrt(ms + EPS) * hg[:, cols]
                x1 = yn[128:160]
                x2 = yn[192:224]
                cg = c[:, cols]
                sg = s[:, cols]
                o_ref[lo:lo + 128, cols] = yn[:128].astype(o_ref.dtype)
                o_ref[lo + 128:lo + 160, cols] = (x1 * cg - x2 * sg).astype(o_ref.dtype)
                o_ref[lo + 160:lo + 192, cols] = yn[160:192].astype(o_ref.dtype)
                o_ref[lo + 192:lo + 224, cols] = (x2 * cg + x1 * sg).astype(o_ref.dtype)
                o_ref[lo + 224:lo + 256, cols] = yn[224:256].astype(o_ref.dtype)


def _qproj(z, ng, wt, hg_col, cos_t, sin_t, bm):
    m = z.shape[0]
    nt = cos_t.shape[1] // bm
    kern = functools.partial(_qproj_kernel, heads_per_dot=1)
    return pl.pallas_call(
        kern,
        grid=(m // bm,),
        in_specs=[pl.BlockSpec((bm, MLA_LORA), lambda i: (i, Z_CQ // MLA_LORA)),
                  pl.BlockSpec((1, MLA_LORA), lambda i: (0, 0)),
                  pl.BlockSpec((MLA_HEADS * MLA_HEAD_PAD, MLA_LORA), lambda i: (0, 0)),
                  pl.BlockSpec((MLA_HEAD_PAD, 1), lambda i: (0, 0)),
                  pl.BlockSpec((MLA_ROPE // 2, bm), lambda i: (0, i % nt)),
                  pl.BlockSpec((MLA_ROPE // 2, bm), lambda i: (0, i % nt))],
        out_specs=pl.BlockSpec((MLA_HEADS * MLA_HEAD_PAD, bm), lambda i: (0, i)),
        out_shape=jax.ShapeDtypeStruct((MLA_HEADS * MLA_HEAD_PAD, m), BF16),
        compiler_params=_params("parallel"),
        name="q_proj",
    )(z, ng.reshape(1, -1), wt, hg_col, cos_t, sin_t)


def _kvproj_kernel(zc_ref, kr_ref, ng_ref, wk_ref, wvt_ref, hg_ref, c_ref, s_ref, k_ref, vt_ref):
    x = zc_ref[...]
    ms = jnp.mean(x * x, axis=-1, keepdims=True)
    xn = (x * lax.rsqrt(ms + EPS) * ng_ref[...]).astype(BF16)
    c = c_ref[...]
    s = s_ref[...]
    hg = hg_ref[...]
    kr = kr_ref[...]
    kr_ss = jnp.sum(kr * kr, axis=-1, keepdims=True)
    kr_rot = _rope(kr * hg[:, 128:], c, s)
    for hp in range(MLA_HEADS // 2):
        y = jnp.dot(xn, wk_ref[:, hp * 256:(hp + 1) * 256], preferred_element_type=F32)
        for sub in range(2):
            lo = (2 * hp + sub) * MLA_HEAD_PAD
            kn = y[:, sub * MLA_NOPE:(sub + 1) * MLA_NOPE]
            ms = (jnp.sum(kn * kn, axis=-1, keepdims=True) + kr_ss) * (1.0 / MLA_QK)
            rs = lax.rsqrt(ms + EPS)
            k_ref[:, lo:lo + 128] = (kn * rs * hg[:, :128]).astype(k_ref.dtype)
            k_ref[:, lo + 128:lo + 256] = (kr_rot * rs).astype(k_ref.dtype)
    rows_per_dot = 4 * VT_ROWS
    for j in range(MLA_HEADS * VT_ROWS // rows_per_dot):
        sl = slice(j * rows_per_dot, (j + 1) * rows_per_dot)
        vt_ref[sl, :] = lax.dot_general(wvt_ref[sl, :], xn, _NT,
                                        preferred_element_type=F32).astype(vt_ref.dtype)
    ones = jnp.ones((VT_ROWS - MLA_V, x.shape[0]), vt_ref.dtype)
    for h in range(MLA_HEADS):
        vt_ref[h * VT_ROWS + MLA_V:(h + 1) * VT_ROWS, :] = ones


def _kvproj(z, zs, ng, wk, wvt, hg, ctab, stab, bm):
    m = z.shape[0]
    nt = ctab.shape[0] // bm
    return pl.pallas_call(
        _kvproj_kernel,
        grid=(m // bm,),
        in_specs=[pl.BlockSpec((bm, MLA_LORA), lambda i: (i, Z_CKV // MLA_LORA)),
                  pl.BlockSpec((bm, 128), lambda i: (i, 1)),
                  pl.BlockSpec((1, MLA_LORA), lambda i: (0, 0)),
                  pl.BlockSpec((MLA_LORA, MLA_HEADS * MLA_NOPE), lambda i: (0, 0)),
                  pl.BlockSpec((MLA_HEADS * VT_ROWS, MLA_LORA), lambda i: (0, 0)),
                  pl.BlockSpec((1, MLA_HEAD_PAD), lambda i: (0, 0)),
                  pl.BlockSpec((bm, 128), lambda i: (i % nt, 0)),
                  pl.BlockSpec((bm, 128), lambda i: (i % nt, 0))],
        out_specs=[pl.BlockSpec((bm, MLA_HEADS * MLA_HEAD_PAD), lambda i: (i, 0)),
                   pl.BlockSpec((MLA_HEADS * VT_ROWS, bm), lambda i: (0, i))],
        out_shape=[jax.ShapeDtypeStruct((m, MLA_HEADS * MLA_HEAD_PAD), BF16),
                   jax.ShapeDtypeStruct((MLA_HEADS * VT_ROWS, m), BF16)],
        compiler_params=_params("parallel"),
        name="kv_proj",
    )(z, zs, ng.reshape(1, -1), wk, wvt, hg, ctab, stab)


def _attn_kernel(qt_ref, k_ref, vt_ref, km_ref, vmt_ref, o_ref, m_sc, acc_sc, s_sc, *, blk, heads,
                 nq):
    i = pl.program_id(2)

    def queries(qi, h):
        return qt_ref[h * MLA_HEAD_PAD:(h + 1) * MLA_HEAD_PAD,
                      pl.ds(pl.multiple_of(qi * blk, blk), blk)]

    s_meta = [jnp.dot(km_ref[:, h * MLA_HEAD_PAD:(h + 1) * MLA_HEAD_PAD], queries(i, h),
                      preferred_element_type=F32) for h in range(heads)]
    p_meta = []
    for h in range(heads):
        m = jnp.max(s_meta[h], axis=0, keepdims=True)
        p_meta.append(jnp.exp2(s_meta[h] - m).astype(BF16))
        m_sc[h] = m
    for h in range(heads):
        acc_sc[h] = jnp.dot(vmt_ref[h * VT_ROWS:(h + 1) * VT_ROWS, :], p_meta[h],
                            preferred_element_type=F32)

    def scores(qi, kb, h):
        ksl = pl.ds(pl.multiple_of(kb * blk, blk), blk)
        return jnp.dot(k_ref[ksl, h * MLA_HEAD_PAD:(h + 1) * MLA_HEAD_PAD], queries(qi, h),
                       preferred_element_type=F32)

    @pl.when(i == 0)
    def _():
        for h in range(heads):
            s_sc[h] = scores(0, 0, h)

    def step(kb, masked):
        ksl = pl.ds(pl.multiple_of(kb * blk, blk), blk)
        tiles = [s_sc[h] for h in range(heads)]
        for h in range(heads):
            s = tiles[h]
            if masked:
                s_sc[h] = scores(jnp.minimum(i + 1, nq - 1), 0, h)
                key = lax.broadcasted_iota(jnp.int32, s.shape, 0)
                qry = lax.broadcasted_iota(jnp.int32, s.shape, 1)
                s = jnp.where(key <= qry, s, -1e30)
            else:
                s_sc[h] = scores(i, kb + 1, h)
            for g0 in range(0, blk, 256):
                cols = slice(g0, g0 + 256)
                sg = s[:, cols]
                m_prev = m_sc[h, :, cols]
                m_new = jnp.maximum(m_prev, jnp.max(sg, axis=0, keepdims=True))
                alpha = jnp.exp2(m_prev - m_new)
                p = jnp.exp2(sg - m_new)
                acc_sc[h, :, cols] = alpha * acc_sc[h, :, cols] + jnp.dot(
                    vt_ref[h * VT_ROWS:(h + 1) * VT_ROWS, ksl], p.astype(BF16),
                    preferred_element_type=F32)
                m_sc[h, :, cols] = m_new

    def body(kb, carry):
        step(kb, False)
        return carry

    lax.fori_loop(0, i, body, 0)
    step(i, True)
    for h in range(heads):
        acc = acc_sc[h]
        o_ref[:, h * MLA_V:(h + 1) * MLA_V] = (
            acc[:MLA_V] * (1.0 / acc[MLA_V:MLA_V + 1])).T.astype(o_ref.dtype)


def _attention(q, k, vt, km, vmt, blk, heads):
    nq = SEQ // blk
    kern = functools.partial(_attn_kernel, blk=blk, heads=heads, nq=nq)
    return pl.pallas_call(
        kern,
        grid=(BATCH, MLA_HEADS // heads, nq),
        in_specs=[pl.BlockSpec((heads * MLA_HEAD_PAD, SEQ), lambda b, h, i: (h, b)),
                  pl.BlockSpec((SEQ, heads * MLA_HEAD_PAD), lambda b, h, i: (b, h)),
                  pl.BlockSpec((heads * VT_ROWS, SEQ), lambda b, h, i: (h, b)),
                  pl.BlockSpec((N_META, heads * MLA_HEAD_PAD), lambda b, h, i: (0, h)),
                  pl.BlockSpec((heads * VT_ROWS, N_META), lambda b, h, i: (h, 0))],
        out_specs=pl.BlockSpec((blk, heads * MLA_V), lambda b, h, i: (b * nq + i, h)),
        out_shape=jax.ShapeDtypeStruct((ROWS, MLA_HEADS * MLA_V), BF16),
        scratch_shapes=[pltpu.VMEM((heads, 1, blk), F32),
                        pltpu.VMEM((heads, VT_ROWS, blk), F32), pltpu.VMEM((heads, blk, blk), F32)],
        compiler_params=_params("parallel", "parallel", "arbitrary"),
        name="mla_attention",
    )(q, k, vt, km, vmt)


def _meta_attn_kernel(q_ref, km_ref, vmt_ref, o_ref):
    s = lax.dot_general(q_ref[...], km_ref[...], _NT, preferred_element_type=F32)
    row = lax.broadcasted_iota(jnp.int32, s.shape, 0)
    col = lax.broadcasted_iota(jnp.int32, s.shape, 1)
    s = jnp.where(col <= row, s, -1e30)
    m = jnp.max(s, axis=-1, keepdims=True)
    p = jnp.exp2(s - m)
    l = jnp.sum(p, axis=-1, keepdims=True)
    o = lax.dot_general(p.astype(BF16), vmt_ref[:MLA_V, :], _NT, preferred_element_type=F32)
    o_ref[...] = (o / l).astype(o_ref.dtype)


def _meta_attention(qm, kmp, vmtp):
    return pl.pallas_call(
        _meta_attn_kernel,
        grid=(MLA_HEADS,),
        in_specs=[pl.BlockSpec((N_META, MLA_HEAD_PAD), lambda h: (0, h)),
                  pl.BlockSpec((128, MLA_HEAD_PAD), lambda h: (0, h)),
                  pl.BlockSpec((VT_ROWS, 128), lambda h: (h, 0))],
        out_specs=pl.BlockSpec((N_META, MLA_V), lambda h: (0, h)),
        out_shape=jax.ShapeDtypeStruct((N_META, MLA_HEADS * MLA_V), BF16),
        compiler_params=_params("parallel"),
        name="meta_attention",
    )(qm, kmp, vmtp)


def _merge_kernel(a_ref, b_ref, wa_ref, wb_ref, za_ref, zb_ref, o_ref):
    ga = _sigmoid(za_ref[...])
    gb = _sigmoid(zb_ref[...])
    ya = jnp.dot(a_ref[...], wa_ref[...], preferred_element_type=F32)
    yb = jnp.dot(b_ref[...], wb_ref[...], preferred_element_type=F32)
    o_ref[...] = (ga * ya + gb * yb).astype(o_ref.dtype)


def _merge(a, b, wa3, wb3, layer, z, bm, bn):
    m = a.shape[0]
    return pl.pallas_call(
        _merge_kernel,
        grid=(m // bm, D_MODEL // bn),
        in_specs=[pl.BlockSpec((bm, D_MODEL), lambda i, j: (i, 0)),
                  pl.BlockSpec((bm, D_MODEL), lambda i, j: (i, 0)),
                  pl.BlockSpec((None, D_MODEL, bn), lambda i, j: (layer, 0, j)),
                  pl.BlockSpec((None, D_MODEL, bn), lambda i, j: (layer, 0, j)),
                  pl.BlockSpec((bm, bn), lambda i, j: (i, Z_A // bn + j)),
                  pl.BlockSpec((bm, bn), lambda i, j: (i, Z_B // bn + j))],
        out_specs=pl.BlockSpec((bm, bn), lambda i, j: (i, j)),
        out_shape=jax.ShapeDtypeStruct((m, D_MODEL), BF16),
        compiler_params=_params("parallel", "parallel"),
        name="branch_merge",
    )(a, b, wa3, wb3, z, z)


def _outproj_kernel(m_ref, w_ref, h_ref, g_ref, ho_ref, hn_ref):
    hn = h_ref[...] + jnp.dot(m_ref[...], w_ref[...], preferred_element_type=F32)
    ho_ref[...] = hn
    ms = jnp.mean(hn * hn, axis=-1, keepdims=True)
    hn_ref[...] = (hn * lax.rsqrt(ms + EPS) * g_ref[...]).astype(hn_ref.dtype)


def _outproj(mg, w3, layer, h, g, bm):
    m = mg.shape[0]
    return pl.pallas_call(
        _outproj_kernel,
        grid=(m // bm,),
        in_specs=[pl.BlockSpec((bm, D_MODEL), lambda i: (i, 0)),
                  pl.BlockSpec((None, D_MODEL, D_MODEL), lambda i: (layer, 0, 0)),
                  pl.BlockSpec((bm, D_MODEL), lambda i: (i, 0)),
                  pl.BlockSpec((1, D_MODEL), lambda i: (0, 0))],
        out_specs=[pl.BlockSpec((bm, D_MODEL), lambda i: (i, 0)),
                   pl.BlockSpec((bm, D_MODEL), lambda i: (i, 0))],
        out_shape=[jax.ShapeDtypeStruct((m, D_MODEL), F32),
                   jax.ShapeDtypeStruct((m, D_MODEL), BF16)],
        compiler_params=_params("parallel"),
        name="out_proj",
    )(mg, w3, h, g.reshape(1, -1))


def _gateup_kernel(*refs, with_meta):
    if with_meta:
        h_ref, hm_ref, wg_ref, wu_ref, o_ref, om_ref, wgb_ref, wub_ref = refs
    else:
        h_ref, wg_ref, wu_ref, o_ref, wgb_ref, wub_ref = refs

    def swiglu(h):
        g = jnp.dot(h, wgb_ref[...], preferred_element_type=F32)
        u = jnp.dot(h, wub_ref[...], preferred_element_type=F32)
        return (g * _sigmoid(g) * u).astype(o_ref.dtype)

    @pl.when(pl.program_id(1) == 0)
    def _():
        wgb_ref[...] = wg_ref[...].astype(BF16)
        wub_ref[...] = wu_ref[...].astype(BF16)
        if with_meta:
            om_ref[...] = swiglu(hm_ref[...])

    o_ref[...] = swiglu(h_ref[...])


def _gateup(h, h_meta, w3, layer, bm, bn):
    m = h.shape[0]
    nb = FF_HIDDEN // bn
    with_meta = h_meta is not None
    weight_specs = [pl.BlockSpec((None, D_MODEL, bn), lambda j, i: (layer, 0, j)),
                    pl.BlockSpec((None, D_MODEL, bn), lambda j, i: (layer, 0, nb + j))]
    in_specs = [pl.BlockSpec((bm, D_MODEL), lambda j, i: (i, 0))]
    operands = [h]
    out_specs = [pl.BlockSpec((bm, bn), lambda j, i: (i, j))]
    out_shape = [jax.ShapeDtypeStruct((m, FF_HIDDEN), BF16)]
    if with_meta:
        in_specs.append(pl.BlockSpec((N_META, D_MODEL), lambda j, i: (0, 0)))
        operands.append(h_meta)
        out_specs.append(pl.BlockSpec((N_META, bn), lambda j, i: (0, j)))
        out_shape.append(jax.ShapeDtypeStruct((N_META, FF_HIDDEN), BF16))
    return pl.pallas_call(
        functools.partial(_gateup_kernel, with_meta=with_meta),
        grid=(nb, m // bm),
        in_specs=in_specs + weight_specs,
        out_specs=out_specs,
        out_shape=out_shape,
        scratch_shapes=[pltpu.VMEM((D_MODEL, bn), BF16), pltpu.VMEM((D_MODEL, bn), BF16)],
        compiler_params=_params("arbitrary", "arbitrary"),
        name="gate_up",
    )(*operands, w3, w3)


def _down_kernel(a_ref, w_ref, h_ref, *rest, emit_norm):
    if emit_norm:
        g_ref, ho_ref, hn_ref, acc_ref = rest
    else:
        ho_ref, acc_ref = rest
    k = pl.program_id(1)
    last = pl.num_programs(1) - 1

    def product():
        return jnp.dot(a_ref[...], w_ref[...], preferred_element_type=F32)

    @pl.when(k == 0)
    def _():
        acc_ref[...] = product()

    @pl.when((k > 0) & (k < last))
    def _():
        acc_ref[...] += product()

    @pl.when(k == last)
    def _():
        hn = h_ref[...] + (acc_ref[...] + product())
        ho_ref[...] = hn
        if emit_norm:
            ms = jnp.mean(hn * hn, axis=-1, keepdims=True)
            hn_ref[...] = (hn * lax.rsqrt(ms + EPS) * g_ref[...]).astype(hn_ref.dtype)


def _down(a, w3, layer, h, g, bm, bk):
    m = a.shape[0]
    assert FF_HIDDEN // bk >= 2
    emit_norm = g is not None
    row_block = pl.BlockSpec((bm, D_MODEL), lambda i, k: (i, 0))
    in_specs = [pl.BlockSpec((bm, bk), lambda i, k: (i, k)),
                pl.BlockSpec((None, bk, D_MODEL), lambda i, k: (layer, k, 0)),
                row_block]
    operands = [a, w3, h]
    out_specs = [row_block]
    out_shape = [jax.ShapeDtypeStruct((m, D_MODEL), F32)]
    if emit_norm:
        in_specs.append(pl.BlockSpec((1, D_MODEL), lambda i, k: (0, 0)))
        operands.append(g.reshape(1, -1))
        out_specs.append(row_block)
        out_shape.append(jax.ShapeDtypeStruct((m, D_MODEL), BF16))
    return pl.pallas_call(
        functools.partial(_down_kernel, emit_norm=emit_norm),
        grid=(m // bm, FF_HIDDEN // bk),
        in_specs=in_specs,
        out_specs=out_specs,
        out_shape=out_shape,
        scratch_shapes=[pltpu.VMEM((bm, D_MODEL), F32)],
        compiler_params=_params("parallel", "arbitrary"),
        name="down_proj",
    )(*operands)


def _rope_layout(t):
    zeros = jnp.zeros(t.shape[:-1] + (32,), t.dtype)
    return jnp.concatenate([t[..., :32], zeros, t[..., 32:], zeros], axis=-1)


def _head_layout(t):
    return jnp.concatenate([t[..., :MLA_NOPE], _rope_layout(t[..., MLA_NOPE:])], axis=-1)


def _rope_tables():
    length = N_META + SEQ
    inv = 1.0 / (ROPE_THETA ** (jnp.arange(0, MLA_ROPE, 2, dtype=F32) / MLA_ROPE))
    ang = jnp.arange(length, dtype=F32)[:, None] * inv[None, :]
    cos, sin = jnp.cos(ang), jnp.sin(ang)
    zeros = jnp.zeros_like(cos)
    ctab = jnp.concatenate([cos, zeros, cos, zeros], axis=-1)
    stab = jnp.concatenate([-sin, zeros, sin, zeros], axis=-1)
    return ctab, stab, cos.T, sin.T


def kernel(x, meta_tokens, norm1_g, w_in, gla_gate_w2, gla_gate_b, gla_onorm_g, w_branch_a,
           q_a_norm_g, w_uq, kv_a_norm_g, w_ukv, q_norm_g, k_norm_g, w_branch_b, w_out,
           norm2_g, w_gate_up, w_down):
    ctab, stab, cos_t, sin_t = _rope_tables()
    ctab_m, stab_m = ctab[:N_META], stab[:N_META]
    ctab_r, stab_r = ctab[N_META:], stab[N_META:]
    cos_tm, sin_tm = cos_t[:, :N_META], sin_t[:, :N_META]
    cos_tr, sin_tr = cos_t[:, N_META:], sin_t[:, N_META:]

    h_res = x.reshape(ROWS, D_MODEL)
    hm_res = meta_tokens.astype(F32)
    s_zero = jnp.zeros((GLA_HEADS, GLA_DK, GLA_DV), F32)

    w_in_t = jnp.swapaxes(w_in, 1, 2)
    w_kr = w_in_t[:, 7184:7248]
    zrow = jnp.zeros((DEPTH, 32, D_MODEL), F32)
    w_small = jnp.concatenate(
        [w_in_t[:, 6144:6160], jnp.zeros((DEPTH, 128 - GLA_GATE_RANK, D_MODEL), F32),
         w_kr[:, :32], zrow, w_kr[:, 32:], zrow], axis=1)
    wa = w_branch_a.astype(BF16)
    wb = w_branch_b.astype(BF16)
    wo = w_out.astype(BF16)
    wd = w_down.astype(BF16)

    for l in range(DEPTH):
        last = l == DEPTH - 1
        w2p = jnp.concatenate(
            [gla_gate_w2[l], jnp.zeros((128 - GLA_GATE_RANK, GLA_HEADS * GLA_DK), F32)],
            axis=0).astype(BF16)
        wqt = _head_layout(w_uq[l].reshape(MLA_LORA, MLA_HEADS, MLA_QK)).reshape(
            MLA_LORA, MLA_HEADS * MLA_HEAD_PAD).T.astype(BF16)
        wkv3 = w_ukv[l].reshape(MLA_LORA, MLA_HEADS, MLA_NOPE + MLA_V)
        wk = wkv3[:, :, :MLA_NOPE].reshape(MLA_LORA, MLA_HEADS * MLA_NOPE).astype(BF16)
        wvt = jnp.pad(jnp.transpose(wkv3[:, :, MLA_NOPE:], (1, 2, 0)),
                      ((0, 0), (0, VT_ROWS - MLA_V), (0, 0))).reshape(
            MLA_HEADS * VT_ROWS, MLA_LORA).astype(BF16)
        qg = (_head_layout(q_norm_g[l]) * (MLA_QK ** -0.5 * LOG2E)).reshape(MLA_HEAD_PAD, 1)
        kg = _head_layout(k_norm_g[l]).reshape(1, MLA_HEAD_PAD)

        if l == 0:
            h, zs = _rms_small(h_res, norm1_g[0], w_small, 0, RMS_SMALL_ROWS)
            hm, zsm = _rms_small(hm_res, norm1_g[0], w_small, 0, N_META)
        else:
            zs = _matmul_nt(h, w_small, l, IN_PROJ_SMALL_ROWS, N_SMALL, name="in_proj_small")
            zsm = _matmul_nt(hm, w_small, l, N_META, N_SMALL, name="in_proj_small_meta")
        z, zm = _in_proj(h, hm, w_in_t, l, *IN_PROJ_BLOCK, name="in_proj")

        gla_m, s_meta = _gla(zm, zsm, w2p, gla_gate_b[l], gla_onorm_g[l], s_zero,
                             batch=1, tokens=N_META, block=N_META, chunk=N_META, emit_state=True)
        (gla_r,) = _gla(z, zs, w2p, gla_gate_b[l], gla_onorm_g[l], s_meta,
                        batch=BATCH, tokens=SEQ, block=GLA_TOKENS, chunk=GLA_CHUNK, emit_state=False)

        km, vmt = _kvproj(zm, zsm, kv_a_norm_g[l], wk, wvt, kg, ctab_m, stab_m, N_META)
        kr, vrt = _kvproj(z, zs, kv_a_norm_g[l], wk, wvt, kg, ctab_r, stab_r, PROJ_ROWS)
        kmp = jnp.pad(km, ((0, 128 - N_META), (0, 0)))
        vmtp = jnp.pad(vmt, ((0, 0), (0, 128 - N_META)))
        qr = _qproj(z, q_a_norm_g[l], wqt, qg, cos_tr, sin_tr, PROJ_ROWS)
        att_r = _attention(qr, kr, vrt, km, vmt, ATTN_BLOCK, ATTN_HEADS)

        merged = _merge(gla_r, att_r, wa, wb, l, z, *MERGE_BLOCK)
        h_res, h2 = _outproj(merged, wo, l, h_res, norm2_g[l], OUT_PROJ_ROWS)
        if last:
            (act,) = _gateup(h2, None, w_gate_up, l, *GATE_UP_BLOCK)
            (h_res,) = _down(act, wd, l, h_res, None, *DOWN_BLOCK)
        else:
            qm = _qproj(zm, q_a_norm_g[l], wqt, qg, cos_tm, sin_tm, N_META)
            att_m = _meta_attention(qm.T, kmp, vmtp)
            merged_m = _merge(gla_m, att_m, wa, wb, l, zm, N_META, MERGE_BLOCK[1])
            hm_res, hm2 = _outproj(merged_m, wo, l, hm_res, norm2_g[l], N_META)
            act, act_m = _gateup(h2, hm2, w_gate_up, l, *GATE_UP_BLOCK)
            h_res, h = _down(act, wd, l, h_res, norm1_g[l + 1], *DOWN_BLOCK)
            hm_res, hm = _down(act_m, wd, l, hm_res, norm1_g[l + 1], N_META, DOWN_META_K)

    return h_res.reshape(BATCH, SEQ, D_MODEL)
```

```python
import functools

import jax
import jax.numpy as jnp
from jax import lax
from jax.experimental import pallas as pl
from jax.experimental.pallas import tpu as pltpu

D_MODEL = 2048
BATCH = 4
SEQ = 4096
DEPTH = 2
N_META = 16
ROWS = BATCH * SEQ

GLA_HEADS = 4
GLA_DK = 256
GLA_DV = 512
GLA_GATE_RANK = 16
GLA_GATE_TAU = 16.0
GLA_CHUNK = 64

MLA_HEADS = 16
MLA_LORA = 512
MLA_NOPE = 128
MLA_ROPE = 64
MLA_QK = 192
MLA_V = 128
VT_ROWS = 144
MLA_HEAD_PAD = 256
ROPE_THETA = 10000.0
FF_HIDDEN = 5632
EPS = 1e-6
LOG2E = 1.4426950408889634

W_IN_ZA = 6144
W_IN_C, W_IN_C_END = 6160, 7184
W_IN_G = 7248
Z_Q, Z_K, Z_V, Z_R, Z_CQ, Z_CKV, Z_A, Z_B = 0, 1024, 2048, 4096, 6144, 6656, 7168, 9216
N_Z = 11264
N_SMALL = 256

RMS_SMALL_ROWS = 1024
IN_PROJ_BLOCK = (1024, 1024)
IN_PROJ_SMALL_ROWS = 2048
GLA_TOKENS = 512
PROJ_ROWS = 1024
ATTN_BLOCK = 512
ATTN_HEADS = 4
MERGE_BLOCK = (1024, 512)
OUT_PROJ_ROWS = 512
GATE_UP_BLOCK = (1024, 512)
DOWN_BLOCK = (512, 2816)
DOWN_META_K = 1408

VMEM_LIMIT = 56 * 1024 * 1024
BF16 = jnp.bfloat16
F32 = jnp.float32

_NT = (((1,), (1,)), ((), ()))
_TN = (((0,), (0,)), ((), ()))


def _params(*sem):
    return pltpu.CompilerParams(dimension_semantics=sem, vmem_limit_bytes=VMEM_LIMIT)


def _sigmoid(x):
    return 1.0 / (1.0 + jnp.exp(-x))


def _rms_small_kernel(x_ref, g_ref, wt_ref, h_ref, zs_ref):
    x = x_ref[...]
    ms = jnp.mean(x * x, axis=-1, keepdims=True)
    h = (x * lax.rsqrt(ms + EPS) * g_ref[...]).astype(BF16)
    h_ref[...] = h
    zs_ref[...] = lax.dot_general(h, wt_ref[...].astype(BF16), _NT, preferred_element_type=F32)


def _rms_small(x, g, wt3, layer, bm):
    m, d = x.shape
    n = wt3.shape[1]
    return pl.pallas_call(
        _rms_small_kernel,
        grid=(m // bm,),
        in_specs=[pl.BlockSpec((bm, d), lambda i: (i, 0)),
                  pl.BlockSpec((1, d), lambda i: (0, 0)),
                  pl.BlockSpec((None, n, d), lambda i: (layer, 0, 0))],
        out_specs=[pl.BlockSpec((bm, d), lambda i: (i, 0)),
                   pl.BlockSpec((bm, n), lambda i: (i, 0))],
        out_shape=[jax.ShapeDtypeStruct((m, d), BF16),
                   jax.ShapeDtypeStruct((m, n), F32)],
        compiler_params=_params("parallel"),
        name="rmsnorm_small_proj",
    )(x, g.reshape(1, d), wt3)


def _mm_nt_kernel(a_ref, wt_ref, o_ref):
    o_ref[...] = lax.dot_general(a_ref[...], wt_ref[...].astype(BF16), _NT,
                                 preferred_element_type=F32)


def _matmul_nt(a, wt3, layer, bm, bn, name):
    m, k = a.shape
    n = wt3.shape[1]
    return pl.pallas_call(
        _mm_nt_kernel,
        grid=(m // bm, n // bn),
        in_specs=[pl.BlockSpec((bm, k), lambda i, j: (i, 0)),
                  pl.BlockSpec((None, bn, k), lambda i, j: (layer, j, 0))],
        out_specs=pl.BlockSpec((bm, bn), lambda i, j: (i, j)),
        out_shape=jax.ShapeDtypeStruct((m, n), F32),
        compiler_params=_params("parallel", "parallel"),
        name=name,
    )(a, wt3)


def _mm_nt_wstat_kernel(a_ref, am_ref, wt_ref, o_ref, om_ref, wb_ref):
    @pl.when(pl.program_id(1) == 0)
    def _():
        wb_ref[...] = wt_ref[0].astype(BF16)
        om_ref[...] = lax.dot_general(am_ref[...], wb_ref[...], _NT, preferred_element_type=F32)

    o_ref[...] = lax.dot_general(a_ref[...], wb_ref[...], _NT, preferred_element_type=F32)


def _in_proj_row0(j, bn):
    n_a = W_IN_ZA // bn
    n_c = (W_IN_C_END - W_IN_C) // bn
    return jnp.where(j < n_a, j * bn,
                     jnp.where(j < n_a + n_c, W_IN_C + (j - n_a) * bn,
                               W_IN_G + (j - n_a - n_c) * bn))


def _in_proj(a, a_meta, wt3, layer, bm, bn, name):
    m, k = a.shape
    return pl.pallas_call(
        _mm_nt_wstat_kernel,
        grid=(N_Z // bn, m // bm),
        in_specs=[pl.BlockSpec((bm, k), lambda j, i: (i, 0)),
                  pl.BlockSpec((N_META, k), lambda j, i: (0, 0)),
                  pl.BlockSpec((pl.Element(1), pl.Element(bn), pl.Element(k)),
                               lambda j, i: (layer, pl.multiple_of(_in_proj_row0(j, bn), 16), 0))],
        out_specs=[pl.BlockSpec((bm, bn), lambda j, i: (i, j)),
                   pl.BlockSpec((N_META, bn), lambda j, i: (0, j))],
        out_shape=[jax.ShapeDtypeStruct((m, N_Z), F32),
                   jax.ShapeDtypeStruct((N_META, N_Z), F32)],
        scratch_shapes=[pltpu.VMEM((bn, k), BF16)],
        compiler_params=_params("arbitrary", "arbitrary"),
        name=name,
    )(a, a_meta, wt3)


def _gla_kernel(q_ref, k_ref, v_ref, r_ref, zlr_ref, w2_ref, gb_ref, og_ref, s0_ref,
                o_ref, *rest, chunk, n_chunks, n_total, emit_state):
    if emit_state:
        sfin_ref, st_ref, b_sc = rest
    else:
        st_ref, b_sc = rest
    t = pl.program_id(1)
    heads = range(GLA_HEADS)

    row = lax.broadcasted_iota(jnp.int32, (chunk, chunk), 0)
    col = lax.broadcasted_iota(jnp.int32, (chunk, chunk), 1)
    causal = col <= row
    tri = jnp.where(causal, 1.0, 0.0).astype(BF16)
    w2 = w2_ref[...]
    gb = gb_ref[...]
    og = og_ref[...]
    kdim = GLA_HEADS * GLA_DK

    def log_gate_split(gc):
        sl = pl.ds(pl.multiple_of(gc * chunk, chunk), chunk)
        logit = jnp.dot(zlr_ref[sl, :].astype(BF16), w2, preferred_element_type=F32) + gb
        g = (jnp.minimum(logit, 0.0) - jnp.log1p(jnp.exp(-jnp.abs(logit)))) * (1.0 / GLA_GATE_TAU)
        g_hi = g.astype(BF16)
        g_lo = (g - g_hi.astype(F32)).astype(BF16)
        return jnp.concatenate([g_hi, g_lo], axis=1)

    def log_decay(g_split):
        cs = jnp.dot(tri, g_split, preferred_element_type=F32)
        return cs[:, :kdim] + cs[:, kdim:]

    @pl.when(t == 0)
    def _():
        st_ref[...] = s0_ref[...]
        b_sc[...] = log_decay(log_gate_split(0))

    def body(c, carry):
        sl = pl.ds(pl.multiple_of(c * chunk, chunk), chunk)
        g_split_next = log_gate_split(jnp.minimum(t * n_chunks + c + 1, n_total - 1))
        r = r_ref[sl, :]
        out_gate = r * _sigmoid(r)
        st = [st_ref[h] for h in heads]
        st_bf = [st[h].astype(BF16) for h in heads]
        b = b_sc[...]
        b_last = b[chunk - 1:chunk, :]
        eb = jnp.exp(b)
        enb = jnp.exp(-b)
        erel = jnp.exp(b_last - b)
        elast = jnp.exp(b_last)
        k = k_ref[sl, :]
        qd = (q_ref[sl, :] * (GLA_DK ** -0.5) * eb).astype(BF16)
        kd = (k * enb).astype(BF16)
        k2 = (k * erel).astype(BF16)
        v = v_ref[sl, :].astype(BF16)

        def hk(x, h):
            return x[:, h * GLA_DK:(h + 1) * GLA_DK]

        def hv(x, h):
            return x[:, h * GLA_DV:(h + 1) * GLA_DV]

        o_state = [jnp.dot(hk(qd, h), st_bf[h], preferred_element_type=F32) for h in heads]
        a = [lax.dot_general(hk(qd, h), hk(kd, h), _NT, preferred_element_type=F32) for h in heads]
        upd = [lax.dot_general(hk(k2, h), hv(v, h), _TN, preferred_element_type=F32)
               for h in heads]
        b_sc[...] = log_decay(g_split_next)
        o_local = [jnp.dot(jnp.where(causal, a[h], 0.0).astype(BF16), hv(v, h),
                           preferred_element_type=F32) for h in heads]
        decay_col = jnp.broadcast_to(elast, (128, kdim)).T
        for h in heads:
            col = decay_col[h * GLA_DK:(h + 1) * GLA_DK]
            st_ref[h] = st[h] * jnp.concatenate([col] * (GLA_DV // 128), axis=1) + upd[h]
        for h in heads:
            o = o_local[h] + o_state[h]
            ms = jnp.mean(o * o, axis=-1, keepdims=True)
            on = o * lax.rsqrt(ms + EPS) * og
            o_ref[sl, h * GLA_DV:(h + 1) * GLA_DV] = (on * hv(out_gate, h)).astype(o_ref.dtype)
        return carry

    lax.fori_loop(0, n_chunks, body, 0)

    if emit_state:
        @pl.when(t == pl.num_programs(1) - 1)
        def _():
            sfin_ref[...] = st_ref[...]


def _gla(z, zs, w2p, gate_b, onorm_g, s0, *, batch, tokens, block, chunk, emit_state):
    nt = tokens // block
    rows = batch * tokens
    kern = functools.partial(_gla_kernel, chunk=chunk, n_chunks=block // chunk,
                             n_total=tokens // chunk, emit_state=emit_state)
    kdim = GLA_HEADS * GLA_DK
    vdim = GLA_HEADS * GLA_DV
    in_specs = [
        pl.BlockSpec((block, kdim), lambda b, t: (b * nt + t, Z_Q // kdim)),
        pl.BlockSpec((block, kdim), lambda b, t: (b * nt + t, Z_K // kdim)),
        pl.BlockSpec((block, vdim), lambda b, t: (b * nt + t, Z_V // vdim)),
        pl.BlockSpec((block, vdim), lambda b, t: (b * nt + t, Z_R // vdim)),
        pl.BlockSpec((tokens, 128), lambda b, t: (b, 0)),
        pl.BlockSpec((128, kdim), lambda b, t: (0, 0)),
        pl.BlockSpec((1, kdim), lambda b, t: (0, 0)),
        pl.BlockSpec((1, GLA_DV), lambda b, t: (0, 0)),
        pl.BlockSpec((GLA_HEADS, GLA_DK, GLA_DV), lambda b, t: (0, 0, 0)),
    ]
    out_specs = [pl.BlockSpec((block, vdim), lambda b, t: (b * nt + t, 0))]
    out_shape = [jax.ShapeDtypeStruct((rows, vdim), BF16)]
    if emit_state:
        out_specs.append(pl.BlockSpec((GLA_HEADS, GLA_DK, GLA_DV), lambda b, t: (b, 0, 0)))
        out_shape.append(jax.ShapeDtypeStruct((batch * GLA_HEADS, GLA_DK, GLA_DV), F32))
    res = pl.pallas_call(
        kern,
        grid=(batch, nt),
        in_specs=in_specs,
        out_specs=out_specs,
        out_shape=out_shape,
        scratch_shapes=[pltpu.VMEM((GLA_HEADS, GLA_DK, GLA_DV), F32),
                        pltpu.VMEM((chunk, kdim), F32)],
        compiler_params=_params("parallel", "arbitrary"),
        name="gla",
    )(z, z, z, z, zs, w2p, gate_b.reshape(1, -1), onorm_g.reshape(1, -1), s0)
    return res


def _rope(x, c, s):
    return x * c + pltpu.roll(x, 64, 1) * s


def _qproj_kernel(zc_ref, ng_ref, wt_ref, hg_ref, c_ref, s_ref, o_ref, *, heads_per_dot):
    x = zc_ref[...]
    ms = jnp.mean(x * x, axis=-1, keepdims=True)
    xn = (x * lax.rsqrt(ms + EPS) * ng_ref[...]).astype(BF16)
    bm = x.shape[0]
    c = c_ref[...]
    s = s_ref[...]
    hg = jnp.broadcast_to(hg_ref[...], (MLA_HEAD_PAD, bm))
    for hc in range(MLA_HEADS // heads_per_dot):
        base = hc * heads_per_dot * MLA_HEAD_PAD
        yt = lax.dot_general(wt_ref[base:base + heads_per_dot * MLA_HEAD_PAD, :], xn, _NT,
                             preferred_element_type=F32)
        for j in range(heads_per_dot):
            lo = base + j * MLA_HEAD_PAD
            y = yt[j * MLA_HEAD_PAD:(j + 1) * MLA_HEAD_PAD]
            ms = jnp.sum(y * y, axis=0, keepdims=True) * (1.0 / MLA_QK)
            yn = y * lax.rsqrt(ms + EPS) * hg
            x1 = yn[128:160]
            x2 = yn[192:224]
            o_ref[lo:lo + 128, :] = yn[:128].astype(o_ref.dtype)
            o_ref[lo + 128:lo + 160, :] = (x1 * c - x2 * s).astype(o_ref.dtype)
            o_ref[lo + 160:lo + 192, :] = yn[160:192].astype(o_ref.dtype)
            o_ref[lo + 192:lo + 224, :] = (x2 * c + x1 * s).astype(o_ref.dtype)
            o_ref[lo + 224:lo + 256, :] = yn[224:256].astype(o_ref.dtype)


def _qproj(z, ng, wt, hg_col, cos_t, sin_t, bm):
    m = z.shape[0]
    nt = cos_t.shape[1] // bm
    kern = functools.partial(_qproj_kernel, heads_per_dot=1)
    return pl.pallas_call(
        kern,
        grid=(m // bm,),
        in_specs=[pl.BlockSpec((bm, MLA_LORA), lambda i: (i, Z_CQ // MLA_LORA)),
                  pl.BlockSpec((1, MLA_LORA), lambda i: (0, 0)),
                  pl.BlockSpec((MLA_HEADS * MLA_HEAD_PAD, MLA_LORA), lambda i: (0, 0)),
                  pl.BlockSpec((MLA_HEAD_PAD, 1), lambda i: (0, 0)),
                  pl.BlockSpec((MLA_ROPE // 2, bm), lambda i: (0, i % nt)),
                  pl.BlockSpec((MLA_ROPE // 2, bm), lambda i: (0, i % nt))],
        out_specs=pl.BlockSpec((MLA_HEADS * MLA_HEAD_PAD, bm), lambda i: (0, i)),
        out_shape=jax.ShapeDtypeStruct((MLA_HEADS * MLA_HEAD_PAD, m), BF16),
        compiler_params=_params("parallel"),
        name="q_proj",
    )(z, ng.reshape(1, -1), wt, hg_col, cos_t, sin_t)


def _kvproj_kernel(zc_ref, kr_ref, ng_ref, wk_ref, wvt_ref, hg_ref, c_ref, s_ref, k_ref, vt_ref):
    x = zc_ref[...]
    ms = jnp.mean(x * x, axis=-1, keepdims=True)
    xn = (x * lax.rsqrt(ms + EPS) * ng_ref[...]).astype(BF16)
    c = c_ref[...]
    s = s_ref[...]
    hg = hg_ref[...]
    kr = kr_ref[...]
    kr_ss = jnp.sum(kr * kr, axis=-1, keepdims=True)
    kr_rot = _rope(kr * hg[:, 128:], c, s)
    for hp in range(MLA_HEADS // 2):
        y = jnp.dot(xn, wk_ref[:, hp * 256:(hp + 1) * 256], preferred_element_type=F32)
        for sub in range(2):
            lo = (2 * hp + sub) * MLA_HEAD_PAD
            kn = y[:, sub * MLA_NOPE:(sub + 1) * MLA_NOPE]
            ms = (jnp.sum(kn * kn, axis=-1, keepdims=True) + kr_ss) * (1.0 / MLA_QK)
            rs = lax.rsqrt(ms + EPS)
            k_ref[:, lo:lo + 128] = (kn * rs * hg[:, :128]).astype(k_ref.dtype)
            k_ref[:, lo + 128:lo + 256] = (kr_rot * rs).astype(k_ref.dtype)
    rows_per_dot = 4 * VT_ROWS
    for j in range(MLA_HEADS * VT_ROWS // rows_per_dot):
        sl = slice(j * rows_per_dot, (j + 1) * rows_per_dot)
        vt_ref[sl, :] = lax.dot_general(wvt_ref[sl, :], xn, _NT,
                                        preferred_element_type=F32).astype(vt_ref.dtype)
    ones = jnp.ones((VT_ROWS - MLA_V, x.shape[0]), vt_ref.dtype)
    for h in range(MLA_HEADS):
        vt_ref[h * VT_ROWS + MLA_V:(h + 1) * VT_ROWS, :] = ones


def _kvproj(z, zs, ng, wk, wvt, hg, ctab, stab, bm):
    m = z.shape[0]
    nt = ctab.shape[0] // bm
    return pl.pallas_call(
        _kvproj_kernel,
        grid=(m // bm,),
        in_specs=[pl.BlockSpec((bm, MLA_LORA), lambda i: (i, Z_CKV // MLA_LORA)),
                  pl.BlockSpec((bm, 128), lambda i: (i, 1)),
                  pl.BlockSpec((1, MLA_LORA), lambda i: (0, 0)),
                  pl.BlockSpec((MLA_LORA, MLA_HEADS * MLA_NOPE), lambda i: (0, 0)),
                  pl.BlockSpec((MLA_HEADS * VT_ROWS, MLA_LORA), lambda i: (0, 0)),
                  pl.BlockSpec((1, MLA_HEAD_PAD), lambda i: (0, 0)),
                  pl.BlockSpec((bm, 128), lambda i: (i % nt, 0)),
                  pl.BlockSpec((bm, 128), lambda i: (i % nt, 0))],
        out_specs=[pl.BlockSpec((bm, MLA_HEADS * MLA_HEAD_PAD), lambda i: (i, 0)),
                   pl.BlockSpec((MLA_HEADS * VT_ROWS, bm), lambda i: (0, i))],
        out_shape=[jax.ShapeDtypeStruct((m, MLA_HEADS * MLA_HEAD_PAD), BF16),
                   jax.ShapeDtypeStruct((MLA_HEADS * VT_ROWS, m), BF16)],
        compiler_params=_params("parallel"),
        name="kv_proj",
    )(z, zs, ng.reshape(1, -1), wk, wvt, hg, ctab, stab)


def _attn_kernel(qt_ref, k_ref, vt_ref, km_ref, vmt_ref, o_ref, m_sc, acc_sc, s_sc, *, blk, heads,
                 nq):
    i = pl.program_id(2)

    def queries(qi, h):
        return qt_ref[h * MLA_HEAD_PAD:(h + 1) * MLA_HEAD_PAD,
                      pl.ds(pl.multiple_of(qi * blk, blk), blk)]

    s_meta = [jnp.dot(km_ref[:, h * MLA_HEAD_PAD:(h + 1) * MLA_HEAD_PAD], queries(i, h),
                      preferred_element_type=F32) for h in range(heads)]
    p_meta = []
    for h in range(heads):
        m = jnp.max(s_meta[h], axis=0, keepdims=True)
        p_meta.append(jnp.exp2(s_meta[h] - m).astype(BF16))
        m_sc[h] = m
    for h in range(heads):
        acc_sc[h] = jnp.dot(vmt_ref[h * VT_ROWS:(h + 1) * VT_ROWS, :], p_meta[h],
                            preferred_element_type=F32)

    def scores(qi, kb, h):
        ksl = pl.ds(pl.multiple_of(kb * blk, blk), blk)
        return jnp.dot(k_ref[ksl, h * MLA_HEAD_PAD:(h + 1) * MLA_HEAD_PAD], queries(qi, h),
                       preferred_element_type=F32)

    @pl.when(i == 0)
    def _():
        for h in range(heads):
            s_sc[h] = scores(0, 0, h)

    def step(kb, masked):
        ksl = pl.ds(pl.multiple_of(kb * blk, blk), blk)
        tiles = [s_sc[h] for h in range(heads)]
        for h in range(heads):
            s = tiles[h]
            if masked:
                s_sc[h] = scores(jnp.minimum(i + 1, nq - 1), 0, h)
                key = lax.broadcasted_iota(jnp.int32, s.shape, 0)
                qry = lax.broadcasted_iota(jnp.int32, s.shape, 1)
                s = jnp.where(key <= qry, s, -1e30)
            else:
                s_sc[h] = scores(i, kb + 1, h)
            for g0 in range(0, blk, 256):
                cols = slice(g0, g0 + 256)
                sg = s[:, cols]
                m_prev = m_sc[h, :, cols]
                m_new = jnp.maximum(m_prev, jnp.max(sg, axis=0, keepdims=True))
                alpha = jnp.exp2(m_prev - m_new)
                p = jnp.exp2(sg - m_new)
                acc_sc[h, :, cols] = alpha * acc_sc[h, :, cols] + jnp.dot(
                    vt_ref[h * VT_ROWS:(h + 1) * VT_ROWS, ksl], p.astype(BF16),
                    preferred_element_type=F32)
                m_sc[h, :, cols] = m_new

    def body(kb, carry):
        step(kb, False)
        return carry

    lax.fori_loop(0, i, body, 0)
    step(i, True)
    for h in range(heads):
        acc = acc_sc[h]
        o_ref[:, h * MLA_V:(h + 1) * MLA_V] = (
            acc[:MLA_V] * (1.0 / acc[MLA_V:MLA_V + 1])).T.astype(o_ref.dtype)


def _attention(q, k, vt, km, vmt, blk, heads):
    nq = SEQ // blk
    kern = functools.partial(_attn_kernel, blk=blk, heads=heads, nq=nq)
    return pl.pallas_call(
        kern,
        grid=(BATCH, MLA_HEADS // heads, nq),
        in_specs=[pl.BlockSpec((heads * MLA_HEAD_PAD, SEQ), lambda b, h, i: (h, b)),
                  pl.BlockSpec((SEQ, heads * MLA_HEAD_PAD), lambda b, h, i: (b, h)),
                  pl.BlockSpec((heads * VT_ROWS, SEQ), lambda b, h, i: (h, b)),
                  pl.BlockSpec((N_META, heads * MLA_HEAD_PAD), lambda b, h, i: (0, h)),
                  pl.BlockSpec((heads * VT_ROWS, N_META), lambda b, h, i: (h, 0))],
        out_specs=pl.BlockSpec((blk, heads * MLA_V), lambda b, h, i: (b * nq + i, h)),
        out_shape=jax.ShapeDtypeStruct((ROWS, MLA_HEADS * MLA_V), BF16),
        scratch_shapes=[pltpu.VMEM((heads, 1, blk), F32),
                        pltpu.VMEM((heads, VT_ROWS, blk), F32), pltpu.VMEM((heads, blk, blk), F32)],
        compiler_params=_params("parallel", "parallel", "arbitrary"),
        name="mla_attention",
    )(q, k, vt, km, vmt)


def _meta_attn_kernel(q_ref, km_ref, vmt_ref, o_ref):
    s = lax.dot_general(q_ref[...], km_ref[...], _NT, preferred_element_type=F32)
    row = lax.broadcasted_iota(jnp.int32, s.shape, 0)
    col = lax.broadcasted_iota(jnp.int32, s.shape, 1)
    s = jnp.where(col <= row, s, -1e30)
    m = jnp.max(s, axis=-1, keepdims=True)
    p = jnp.exp2(s - m)
    l = jnp.sum(p, axis=-1, keepdims=True)
    o = lax.dot_general(p.astype(BF16), vmt_ref[:MLA_V, :], _NT, preferred_element_type=F32)
    o_ref[...] = (o / l).astype(o_ref.dtype)


def _meta_attention(qm, kmp, vmtp):
    return pl.pallas_call(
        _meta_attn_kernel,
        grid=(MLA_HEADS,),
        in_specs=[pl.BlockSpec((N_META, MLA_HEAD_PAD), lambda h: (0, h)),
                  pl.BlockSpec((128, MLA_HEAD_PAD), lambda h: (0, h)),
                  pl.BlockSpec((VT_ROWS, 128), lambda h: (h, 0))],
        out_specs=pl.BlockSpec((N_META, MLA_V), lambda h: (0, h)),
        out_shape=jax.ShapeDtypeStruct((N_META, MLA_HEADS * MLA_V), BF16),
        compiler_params=_params("parallel"),
        name="meta_attention",
    )(qm, kmp, vmtp)


def _merge_kernel(a_ref, b_ref, wa_ref, wb_ref, za_ref, zb_ref, o_ref):
    ga = _sigmoid(za_ref[...])
    gb = _sigmoid(zb_ref[...])
    ya = jnp.dot(a_ref[...], wa_ref[...], preferred_element_type=F32)
    yb = jnp.dot(b_ref[...], wb_ref[...], preferred_element_type=F32)
    o_ref[...] = (ga * ya + gb * yb).astype(o_ref.dtype)


def _merge(a, b, wa3, wb3, layer, z, bm, bn):
    m = a.shape[0]
    return pl.pallas_call(
        _merge_kernel,
        grid=(m // bm, D_MODEL // bn),
        in_specs=[pl.BlockSpec((bm, D_MODEL), lambda i, j: (i, 0)),
                  pl.BlockSpec((bm, D_MODEL), lambda i, j: (i, 0)),
                  pl.BlockSpec((None, D_MODEL, bn), lambda i, j: (layer, 0, j)),
                  pl.BlockSpec((None, D_MODEL, bn), lambda i, j: (layer, 0, j)),
                  pl.BlockSpec((bm, bn), lambda i, j: (i, Z_A // bn + j)),
                  pl.BlockSpec((bm, bn), lambda i, j: (i, Z_B // bn + j))],
        out_specs=pl.BlockSpec((bm, bn), lambda i, j: (i, j)),
        out_shape=jax.ShapeDtypeStruct((m, D_MODEL), BF16),
        compiler_params=_params("parallel", "parallel"),
        name="branch_merge",
    )(a, b, wa3, wb3, z, z)


def _outproj_kernel(m_ref, w_ref, h_ref, g_ref, ho_ref, hn_ref):
    hn = h_ref[...] + jnp.dot(m_ref[...], w_ref[...], preferred_element_type=F32)
    ho_ref[...] = hn
    ms = jnp.mean(hn * hn, axis=-1, keepdims=True)
    hn_ref[...] = (hn * lax.rsqrt(ms + EPS) * g_ref[...]).astype(hn_ref.dtype)


def _outproj(mg, w3, layer, h, g, bm):
    m = mg.shape[0]
    return pl.pallas_call(
        _outproj_kernel,
        grid=(m // bm,),
        in_specs=[pl.BlockSpec((bm, D_MODEL), lambda i: (i, 0)),
                  pl.BlockSpec((None, D_MODEL, D_MODEL), lambda i: (layer, 0, 0)),
                  pl.BlockSpec((bm, D_MODEL), lambda i: (i, 0)),
                  pl.BlockSpec((1, D_MODEL), lambda i: (0, 0))],
        out_specs=[pl.BlockSpec((bm, D_MODEL), lambda i: (i, 0)),
                   pl.BlockSpec((bm, D_MODEL), lambda i: (i, 0))],
        out_shape=[jax.ShapeDtypeStruct((m, D_MODEL), F32),
                   jax.ShapeDtypeStruct((m, D_MODEL), BF16)],
        compiler_params=_params("parallel"),
        name="out_proj",
    )(mg, w3, h, g.reshape(1, -1))


def _gateup_kernel(*refs, with_meta):
    if with_meta:
        h_ref, hm_ref, wg_ref, wu_ref, o_ref, om_ref, wgb_ref, wub_ref = refs
    else:
        h_ref, wg_ref, wu_ref, o_ref, wgb_ref, wub_ref = refs

    def swiglu(h):
        g = jnp.dot(h, wgb_ref[...], preferred_element_type=F32)
        u = jnp.dot(h, wub_ref[...], preferred_element_type=F32)
        return (g * _sigmoid(g) * u).astype(o_ref.dtype)

    @pl.when(pl.program_id(1) == 0)
    def _():
        wgb_ref[...] = wg_ref[...].astype(BF16)
        wub_ref[...] = wu_ref[...].astype(BF16)
        if with_meta:
            om_ref[...] = swiglu(hm_ref[...])

    o_ref[...] = swiglu(h_ref[...])


def _gateup(h, h_meta, w3, layer, bm, bn):
    m = h.shape[0]
    nb = FF_HIDDEN // bn
    with_meta = h_meta is not None
    weight_specs = [pl.BlockSpec((None, D_MODEL, bn), lambda j, i: (layer, 0, j)),
                    pl.BlockSpec((None, D_MODEL, bn), lambda j, i: (layer, 0, nb + j))]
    in_specs = [pl.BlockSpec((bm, D_MODEL), lambda j, i: (i, 0))]
    operands = [h]
    out_specs = [pl.BlockSpec((bm, bn), lambda j, i: (i, j))]
    out_shape = [jax.ShapeDtypeStruct((m, FF_HIDDEN), BF16)]
    if with_meta:
        in_specs.append(pl.BlockSpec((N_META, D_MODEL), lambda j, i: (0, 0)))
        operands.append(h_meta)
        out_specs.append(pl.BlockSpec((N_META, bn), lambda j, i: (0, j)))
        out_shape.append(jax.ShapeDtypeStruct((N_META, FF_HIDDEN), BF16))
    return pl.pallas_call(
        functools.partial(_gateup_kernel, with_meta=with_meta),
        grid=(nb, m // bm),
        in_specs=in_specs + weight_specs,
        out_specs=out_specs,
        out_shape=out_shape,
        scratch_shapes=[pltpu.VMEM((D_MODEL, bn), BF16), pltpu.VMEM((D_MODEL, bn), BF16)],
        compiler_params=_params("arbitrary", "arbitrary"),
        name="gate_up",
    )(*operands, w3, w3)


def _down_kernel(a_ref, w_ref, h_ref, *rest, emit_norm):
    if emit_norm:
        g_ref, ho_ref, hn_ref, acc_ref = rest
    else:
        ho_ref, acc_ref = rest
    k = pl.program_id(1)
    last = pl.num_programs(1) - 1

    def product():
        return jnp.dot(a_ref[...], w_ref[...], preferred_element_type=F32)

    @pl.when(k == 0)
    def _():
        acc_ref[...] = product()

    @pl.when((k > 0) & (k < last))
    def _():
        acc_ref[...] += product()

    @pl.when(k == last)
    def _():
        hn = h_ref[...] + (acc_ref[...] + product())
        ho_ref[...] = hn
        if emit_norm:
            ms = jnp.mean(hn * hn, axis=-1, keepdims=True)
            hn_ref[...] = (hn * lax.rsqrt(ms + EPS) * g_ref[...]).astype(hn_ref.dtype)


def _down(a, w3, layer, h, g, bm, bk):
    m = a.shape[0]
    assert FF_HIDDEN // bk >= 2
    emit_norm = g is not None
    row_block = pl.BlockSpec((bm, D_MODEL), lambda i, k: (i, 0))
    in_specs = [pl.BlockSpec((bm, bk), lambda i, k: (i, k)),
                pl.BlockSpec((None, bk, D_MODEL), lambda i, k: (layer, k, 0)),
                row_block]
    operands = [a, w3, h]
    out_specs = [row_block]
    out_shape = [jax.ShapeDtypeStruct((m, D_MODEL), F32)]
    if emit_norm:
        in_specs.append(pl.BlockSpec((1, D_MODEL), lambda i, k: (0, 0)))
        operands.append(g.reshape(1, -1))
        out_specs.append(row_block)
        out_shape.append(jax.ShapeDtypeStruct((m, D_MODEL), BF16))
    return pl.pallas_call(
        functools.partial(_down_kernel, emit_norm=emit_norm),
        grid=(m // bm, FF_HIDDEN // bk),
        in_specs=in_specs,
        out_specs=out_specs,
        out_shape=out_shape,
        scratch_shapes=[pltpu.VMEM((bm, D_MODEL), F32)],
        compiler_params=_params("parallel", "arbitrary"),
        name="down_proj",
    )(*operands)


def _rope_layout(t):
    zeros = jnp.zeros(t.shape[:-1] + (32,), t.dtype)
    return jnp.concatenate([t[..., :32], zeros, t[..., 32:], zeros], axis=-1)


def _head_layout(t):
    return jnp.concatenate([t[..., :MLA_NOPE], _rope_layout(t[..., MLA_NOPE:])], axis=-1)


def _rope_tables():
    length = N_META + SEQ
    inv = 1.0 / (ROPE_THETA ** (jnp.arange(0, MLA_ROPE, 2, dtype=F32) / MLA_ROPE))
    ang = jnp.arange(length, dtype=F32)[:, None] * inv[None, :]
    cos, sin = jnp.cos(ang), jnp.sin(ang)
    zeros = jnp.zeros_like(cos)
    ctab = jnp.concatenate([cos, zeros, cos, zeros], axis=-1)
    stab = jnp.concatenate([-sin, zeros, sin, zeros], axis=-1)
    return ctab, stab, cos.T, sin.T


def kernel(x, meta_tokens, norm1_g, w_in, gla_gate_w2, gla_gate_b, gla_onorm_g, w_branch_a,
           q_a_norm_g, w_uq, kv_a_norm_g, w_ukv, q_norm_g, k_norm_g, w_branch_b, w_out,
           norm2_g, w_gate_up, w_down):
    ctab, stab, cos_t, sin_t = _rope_tables()
    ctab_m, stab_m = ctab[:N_META], stab[:N_META]
    ctab_r, stab_r = ctab[N_META:], stab[N_META:]
    cos_tm, sin_tm = cos_t[:, :N_META], sin_t[:, :N_META]
    cos_tr, sin_tr = cos_t[:, N_META:], sin_t[:, N_META:]

    h_res = x.reshape(ROWS, D_MODEL)
    hm_res = meta_tokens.astype(F32)
    s_zero = jnp.zeros((GLA_HEADS, GLA_DK, GLA_DV), F32)

    w_in_t = jnp.swapaxes(w_in, 1, 2)
    w_kr = w_in_t[:, 7184:7248]
    zrow = jnp.zeros((DEPTH, 32, D_MODEL), F32)
    w_small = jnp.concatenate(
        [w_in_t[:, 6144:6160], jnp.zeros((DEPTH, 128 - GLA_GATE_RANK, D_MODEL), F32),
         w_kr[:, :32], zrow, w_kr[:, 32:], zrow], axis=1)
    wa = w_branch_a.astype(BF16)
    wb = w_branch_b.astype(BF16)
    wo = w_out.astype(BF16)
    wd = w_down.astype(BF16)

    for l in range(DEPTH):
        last = l == DEPTH - 1
        w2p = jnp.concatenate(
            [gla_gate_w2[l], jnp.zeros((128 - GLA_GATE_RANK, GLA_HEADS * GLA_DK), F32)],
            axis=0).astype(BF16)
        wqt = _head_layout(w_uq[l].reshape(MLA_LORA, MLA_HEADS, MLA_QK)).reshape(
            MLA_LORA, MLA_HEADS * MLA_HEAD_PAD).T.astype(BF16)
        wkv3 = w_ukv[l].reshape(MLA_LORA, MLA_HEADS, MLA_NOPE + MLA_V)
        wk = wkv3[:, :, :MLA_NOPE].reshape(MLA_LORA, MLA_HEADS * MLA_NOPE).astype(BF16)
        wvt = jnp.pad(jnp.transpose(wkv3[:, :, MLA_NOPE:], (1, 2, 0)),
                      ((0, 0), (0, VT_ROWS - MLA_V), (0, 0))).reshape(
            MLA_HEADS * VT_ROWS, MLA_LORA).astype(BF16)
        qg = (_head_layout(q_norm_g[l]) * (MLA_QK ** -0.5 * LOG2E)).reshape(MLA_HEAD_PAD, 1)
        kg = _head_layout(k_norm_g[l]).reshape(1, MLA_HEAD_PAD)

        if l == 0:
            h, zs = _rms_small(h_res, norm1_g[0], w_small, 0, RMS_SMALL_ROWS)
            hm, zsm = _rms_small(hm_res, norm1_g[0], w_small, 0, N_META)
        else:
            zs = _matmul_nt(h, w_small, l, IN_PROJ_SMALL_ROWS, N_SMALL, name="in_proj_small")
            zsm = _matmul_nt(hm, w_small, l, N_META, N_SMALL, name="in_proj_small_meta")
        z, zm = _in_proj(h, hm, w_in_t, l, *IN_PROJ_BLOCK, name="in_proj")

        gla_m, s_meta = _gla(zm, zsm, w2p, gla_gate_b[l], gla_onorm_g[l], s_zero,
                             batch=1, tokens=N_META, block=N_META, chunk=N_META, emit_state=True)
        (gla_r,) = _gla(z, zs, w2p, gla_gate_b[l], gla_onorm_g[l], s_meta,
                        batch=BATCH, tokens=SEQ, block=GLA_TOKENS, chunk=GLA_CHUNK, emit_state=False)

        km, vmt = _kvproj(zm, zsm, kv_a_norm_g[l], wk, wvt, kg, ctab_m, stab_m, N_META)
        kr, vrt = _kvproj(z, zs, kv_a_norm_g[l], wk, wvt, kg, ctab_r, stab_r, PROJ_ROWS)
        kmp = jnp.pad(km, ((0, 128 - N_META), (0, 0)))
        vmtp = jnp.pad(vmt, ((0, 0), (0, 128 - N_META)))
        qr = _qproj(z, q_a_norm_g[l], wqt, qg, cos_tr, sin_tr, PROJ_ROWS)
        att_r = _attention(qr, kr, vrt, km, vmt, ATTN_BLOCK, ATTN_HEADS)

        merged = _merge(gla_r, att_r, wa, wb, l, z, *MERGE_BLOCK)
        h_res, h2 = _outproj(merged, wo, l, h_res, norm2_g[l], OUT_PROJ_ROWS)
        if last:
            (act,) = _gateup(h2, None, w_gate_up, l, *GATE_UP_BLOCK)
            (h_res,) = _down(act, wd, l, h_res, None, *DOWN_BLOCK)
        else:
            qm = _qproj(zm, q_a_norm_g[l], wqt, qg, cos_tm, sin_tm, N_META)
            att_m = _meta_attention(qm.T, kmp, vmtp)
            merged_m = _merge(gla_m, att_m, wa, wb, l, zm, N_META, MERGE_BLOCK[1])
            hm_res, hm2 = _outproj(merged_m, wo, l, hm_res, norm2_g[l], N_META)
            act, act_m = _gateup(h2, hm2, w_gate_up, l, *GATE_UP_BLOCK)
            h_res, h = _down(act, wd, l, h_res, norm1_g[l + 1], *DOWN_BLOCK)
            hm_res, hm = _down(act_m, wd, l, hm_res, norm1_g[l + 1], N_META, DOWN_META_K)

    return h_res.reshape(BATCH, SEQ, D_MODEL)
```

```python
import functools

import jax
import jax.numpy as jnp
from jax import lax
from jax.experimental import pallas as pl
from jax.experimental.pallas import tpu as pltpu

D_MODEL = 2048
BATCH = 4
SEQ = 4096
DEPTH = 2
N_META = 16
ROWS = BATCH * SEQ

GLA_HEADS = 4
GLA_DK = 256
GLA_DV = 512
GLA_GATE_RANK = 16
GLA_GATE_TAU = 16.0
GLA_CHUNK = 64

MLA_HEADS = 16
MLA_LORA = 512
MLA_NOPE = 128
MLA_ROPE = 64
MLA_QK = 192
MLA_V = 128
VT_ROWS = 144
MLA_HEAD_PAD = 256
ROPE_THETA = 10000.0
FF_HIDDEN = 5632
EPS = 1e-6
LOG2E = 1.4426950408889634

W_IN_ZA = 6144
W_IN_C, W_IN_C_END = 6160, 7184
W_IN_G = 7248
Z_Q, Z_K, Z_V, Z_R, Z_CQ, Z_CKV, Z_A, Z_B = 0, 1024, 2048, 4096, 6144, 6656, 7168, 9216
N_Z = 11264
N_SMALL = 256

RMS_SMALL_ROWS = 1024
IN_PROJ_BLOCK = (1024, 1024)
IN_PROJ_SMALL_ROWS = 2048
GLA_TOKENS = 512
PROJ_ROWS = 1024
ATTN_BLOCK = 512
ATTN_HEADS = 4
MERGE_BLOCK = (1024, 512)
OUT_PROJ_ROWS = 512
MERGE_OUT_ROWS = 256
GATE_UP_BLOCK = (1024, 512)
DOWN_BLOCK = (512, 2816)
DOWN_META_K = 1408

VMEM_LIMIT = 56 * 1024 * 1024
BF16 = jnp.bfloat16
F32 = jnp.float32

_NT = (((1,), (1,)), ((), ()))
_TN = (((0,), (0,)), ((), ()))


def _params(*sem):
    return pltpu.CompilerParams(dimension_semantics=sem, vmem_limit_bytes=VMEM_LIMIT)


def _sigmoid(x):
    return 1.0 / (1.0 + jnp.exp(-x))


def _rms_small_kernel(x_ref, g_ref, wt_ref, h_ref, zs_ref):
    x = x_ref[...]
    ms = jnp.mean(x * x, axis=-1, keepdims=True)
    h = (x * lax.rsqrt(ms + EPS) * g_ref[...]).astype(BF16)
    h_ref[...] = h
    zs_ref[...] = lax.dot_general(h, wt_ref[...].astype(BF16), _NT, preferred_element_type=F32)


def _rms_small(x, g, wt3, layer, bm):
    m, d = x.shape
    n = wt3.shape[1]
    return pl.pallas_call(
        _rms_small_kernel,
        grid=(m // bm,),
        in_specs=[pl.BlockSpec((bm, d), lambda i: (i, 0)),
                  pl.BlockSpec((1, d), lambda i: (0, 0)),
                  pl.BlockSpec((None, n, d), lambda i: (layer, 0, 0))],
        out_specs=[pl.BlockSpec((bm, d), lambda i: (i, 0)),
                   pl.BlockSpec((bm, n), lambda i: (i, 0))],
        out_shape=[jax.ShapeDtypeStruct((m, d), BF16),
                   jax.ShapeDtypeStruct((m, n), F32)],
        compiler_params=_params("parallel"),
        name="rmsnorm_small_proj",
    )(x, g.reshape(1, d), wt3)


def _mm_nt_kernel(a_ref, wt_ref, o_ref):
    o_ref[...] = lax.dot_general(a_ref[...], wt_ref[...].astype(BF16), _NT,
                                 preferred_element_type=F32)


def _matmul_nt(a, wt3, layer, bm, bn, name):
    m, k = a.shape
    n = wt3.shape[1]
    return pl.pallas_call(
        _mm_nt_kernel,
        grid=(m // bm, n // bn),
        in_specs=[pl.BlockSpec((bm, k), lambda i, j: (i, 0)),
                  pl.BlockSpec((None, bn, k), lambda i, j: (layer, j, 0))],
        out_specs=pl.BlockSpec((bm, bn), lambda i, j: (i, j)),
        out_shape=jax.ShapeDtypeStruct((m, n), F32),
        compiler_params=_params("parallel", "parallel"),
        name=name,
    )(a, wt3)


def _mm_nt_wstat_kernel(a_ref, am_ref, wt_ref, o_ref, om_ref, wb_ref):
    @pl.when(pl.program_id(1) == 0)
    def _():
        wb_ref[...] = wt_ref[0].astype(BF16)
        om_ref[...] = lax.dot_general(am_ref[...], wb_ref[...], _NT, preferred_element_type=F32)

    o_ref[...] = lax.dot_general(a_ref[...], wb_ref[...], _NT, preferred_element_type=F32)


def _in_proj_row0(j, bn):
    n_a = W_IN_ZA // bn
    n_c = (W_IN_C_END - W_IN_C) // bn
    return jnp.where(j < n_a, j * bn,
                     jnp.where(j < n_a + n_c, W_IN_C + (j - n_a) * bn,
                               W_IN_G + (j - n_a - n_c) * bn))


def _in_proj(a, a_meta, wt3, layer, bm, bn, name):
    m, k = a.shape
    return pl.pallas_call(
        _mm_nt_wstat_kernel,
        grid=(N_Z // bn, m // bm),
        in_specs=[pl.BlockSpec((bm, k), lambda j, i: (i, 0)),
                  pl.BlockSpec((N_META, k), lambda j, i: (0, 0)),
                  pl.BlockSpec((pl.Element(1), pl.Element(bn), pl.Element(k)),
                               lambda j, i: (layer, pl.multiple_of(_in_proj_row0(j, bn), 16), 0))],
        out_specs=[pl.BlockSpec((bm, bn), lambda j, i: (i, j)),
                   pl.BlockSpec((N_META, bn), lambda j, i: (0, j))],
        out_shape=[jax.ShapeDtypeStruct((m, N_Z), F32),
                   jax.ShapeDtypeStruct((N_META, N_Z), F32)],
        scratch_shapes=[pltpu.VMEM((bn, k), BF16)],
        compiler_params=_params("arbitrary", "arbitrary"),
        name=name,
    )(a, a_meta, wt3)


def _gla_kernel(q_ref, k_ref, v_ref, r_ref, zlr_ref, w2_ref, gb_ref, og_ref, s0_ref,
                o_ref, *rest, chunk, n_chunks, n_total, emit_state):
    if emit_state:
        sfin_ref, st_ref, b_sc = rest
    else:
        st_ref, b_sc = rest
    t = pl.program_id(1)
    heads = range(GLA_HEADS)

    row = lax.broadcasted_iota(jnp.int32, (chunk, chunk), 0)
    col = lax.broadcasted_iota(jnp.int32, (chunk, chunk), 1)
    causal = col <= row
    tri = jnp.where(causal, 1.0, 0.0).astype(BF16)
    w2 = w2_ref[...]
    gb = gb_ref[...]
    og = og_ref[...]
    kdim = GLA_HEADS * GLA_DK

    def log_gate_split(gc):
        sl = pl.ds(pl.multiple_of(gc * chunk, chunk), chunk)
        logit = jnp.dot(zlr_ref[sl, :].astype(BF16), w2, preferred_element_type=F32) + gb
        g = (jnp.minimum(logit, 0.0) - jnp.log1p(jnp.exp(-jnp.abs(logit)))) * (1.0 / GLA_GATE_TAU)
        g_hi = g.astype(BF16)
        g_lo = (g - g_hi.astype(F32)).astype(BF16)
        return jnp.concatenate([g_hi, g_lo], axis=1)

    def log_decay(g_split):
        cs = jnp.dot(tri, g_split, preferred_element_type=F32)
        return cs[:, :kdim] + cs[:, kdim:]

    @pl.when(t == 0)
    def _():
        st_ref[...] = s0_ref[...]
        b_sc[...] = log_decay(log_gate_split(0))

    def body(c, carry):
        sl = pl.ds(pl.multiple_of(c * chunk, chunk), chunk)
        g_split_next = log_gate_split(jnp.minimum(t * n_chunks + c + 1, n_total - 1))
        r = r_ref[sl, :]
        out_gate = r * _sigmoid(r)
        st = [st_ref[h] for h in heads]
        st_bf = [st[h].astype(BF16) for h in heads]
        b = b_sc[...]
        b_last = b[chunk - 1:chunk, :]
        eb = jnp.exp(b)
        enb = jnp.exp(-b)
        erel = jnp.exp(b_last - b)
        elast = jnp.exp(b_last)
        k = k_ref[sl, :]
        qd = (q_ref[sl, :] * (GLA_DK ** -0.5) * eb).astype(BF16)
        kd = (k * enb).astype(BF16)
        k2 = (k * erel).astype(BF16)
        v = v_ref[sl, :].astype(BF16)

        def hk(x, h):
            return x[:, h * GLA_DK:(h + 1) * GLA_DK]

        def hv(x, h):
            return x[:, h * GLA_DV:(h + 1) * GLA_DV]

        o_state = [jnp.dot(hk(qd, h), st_bf[h], preferred_element_type=F32) for h in heads]
        a = [lax.dot_general(hk(qd, h), hk(kd, h), _NT, preferred_element_type=F32) for h in heads]
        upd = [lax.dot_general(hk(k2, h), hv(v, h), _TN, preferred_element_type=F32)
               for h in heads]
        b_sc[...] = log_decay(g_split_next)
        o_local = [jnp.dot(jnp.where(causal, a[h], 0.0).astype(BF16), hv(v, h),
                           preferred_element_type=F32) for h in heads]
        decay_col = jnp.broadcast_to(elast, (128, kdim)).T
        for h in heads:
            col = decay_col[h * GLA_DK:(h + 1) * GLA_DK]
            st_ref[h] = st[h] * jnp.concatenate([col] * (GLA_DV // 128), axis=1) + upd[h]
        for h in heads:
            o = o_local[h] + o_state[h]
            ms = jnp.mean(o * o, axis=-1, keepdims=True)
            on = o * lax.rsqrt(ms + EPS) * og
            o_ref[sl, h * GLA_DV:(h + 1) * GLA_DV] = (on * hv(out_gate, h)).astype(o_ref.dtype)
        return carry

    lax.fori_loop(0, n_chunks, body, 0)

    if emit_state:
        @pl.when(t == pl.num_programs(1) - 1)
        def _():
            sfin_ref[...] = st_ref[...]


def _gla(z, zs, w2p, gate_b, onorm_g, s0, *, batch, tokens, block, chunk, emit_state):
    nt = tokens // block
    rows = batch * tokens
    kern = functools.partial(_gla_kernel, chunk=chunk, n_chunks=block // chunk,
                             n_total=tokens // chunk, emit_state=emit_state)
    kdim = GLA_HEADS * GLA_DK
    vdim = GLA_HEADS * GLA_DV
    in_specs = [
        pl.BlockSpec((block, kdim), lambda b, t: (b * nt + t, Z_Q // kdim)),
        pl.BlockSpec((block, kdim), lambda b, t: (b * nt + t, Z_K // kdim)),
        pl.BlockSpec((block, vdim), lambda b, t: (b * nt + t, Z_V // vdim)),
        pl.BlockSpec((block, vdim), lambda b, t: (b * nt + t, Z_R // vdim)),
        pl.BlockSpec((tokens, 128), lambda b, t: (b, 0)),
        pl.BlockSpec((128, kdim), lambda b, t: (0, 0)),
        pl.BlockSpec((1, kdim), lambda b, t: (0, 0)),
        pl.BlockSpec((1, GLA_DV), lambda b, t: (0, 0)),
        pl.BlockSpec((GLA_HEADS, GLA_DK, GLA_DV), lambda b, t: (0, 0, 0)),
    ]
    out_specs = [pl.BlockSpec((block, vdim), lambda b, t: (b * nt + t, 0))]
    out_shape = [jax.ShapeDtypeStruct((rows, vdim), BF16)]
    if emit_state:
        out_specs.append(pl.BlockSpec((GLA_HEADS, GLA_DK, GLA_DV), lambda b, t: (b, 0, 0)))
        out_shape.append(jax.ShapeDtypeStruct((batch * GLA_HEADS, GLA_DK, GLA_DV), F32))
    res = pl.pallas_call(
        kern,
        grid=(batch, nt),
        in_specs=in_specs,
        out_specs=out_specs,
        out_shape=out_shape,
        scratch_shapes=[pltpu.VMEM((GLA_HEADS, GLA_DK, GLA_DV), F32),
                        pltpu.VMEM((chunk, kdim), F32)],
        compiler_params=_params("parallel", "arbitrary"),
        name="gla",
    )(z, z, z, z, zs, w2p, gate_b.reshape(1, -1), onorm_g.reshape(1, -1), s0)
    return res


def _rope(x, c, s):
    return x * c + pltpu.roll(x, 64, 1) * s


def _qproj_kernel(zc_ref, ng_ref, wt_ref, hg_ref, c_ref, s_ref, o_ref, *, heads_per_dot):
    x = zc_ref[...]
    ms = jnp.mean(x * x, axis=-1, keepdims=True)
    xn = (x * lax.rsqrt(ms + EPS) * ng_ref[...]).astype(BF16)
    bm = x.shape[0]
    c = c_ref[...]
    s = s_ref[...]
    hg = jnp.broadcast_to(hg_ref[...], (MLA_HEAD_PAD, bm))
    for hc in range(MLA_HEADS // heads_per_dot):
        base = hc * heads_per_dot * MLA_HEAD_PAD
        yt = lax.dot_general(wt_ref[base:base + heads_per_dot * MLA_HEAD_PAD, :], xn, _NT,
                             preferred_element_type=F32)
        for j in range(heads_per_dot):
            lo = base + j * MLA_HEAD_PAD
            y = yt[j * MLA_HEAD_PAD:(j + 1) * MLA_HEAD_PAD]
            ms = jnp.sum(y * y, axis=0, keepdims=True) * (1.0 / MLA_QK)
            yn = y * lax.rsqrt(ms + EPS) * hg
            x1 = yn[128:160]
            x2 = yn[192:224]
            o_ref[lo:lo + 128, :] = yn[:128].astype(o_ref.dtype)
            o_ref[lo + 128:lo + 160, :] = (x1 * c - x2 * s).astype(o_ref.dtype)
            o_ref[lo + 160:lo + 192, :] = yn[160:192].astype(o_ref.dtype)
            o_ref[lo + 192:lo + 224, :] = (x2 * c + x1 * s).astype(o_ref.dtype)
            o_ref[lo + 224:lo + 256, :] = yn[224:256].astype(o_ref.dtype)


def _qproj(z, ng, wt, hg_col, cos_t, sin_t, bm):
    m = z.shape[0]
    nt = cos_t.shape[1] // bm
    kern = functools.partial(_qproj_kernel, heads_per_dot=1)
    return pl.pallas_call(
        kern,
        grid=(m // bm,),
        in_specs=[pl.BlockSpec((bm, MLA_LORA), lambda i: (i, Z_CQ // MLA_LORA)),
                  pl.BlockSpec((1, MLA_LORA), lambda i: (0, 0)),
                  pl.BlockSpec((MLA_HEADS * MLA_HEAD_PAD, MLA_LORA), lambda i: (0, 0)),
                  pl.BlockSpec((MLA_HEAD_PAD, 1), lambda i: (0, 0)),
                  pl.BlockSpec((MLA_ROPE // 2, bm), lambda i: (0, i % nt)),
                  pl.BlockSpec((MLA_ROPE // 2, bm), lambda i: (0, i % nt))],
        out_specs=pl.BlockSpec((MLA_HEADS * MLA_HEAD_PAD, bm), lambda i: (0, i)),
        out_shape=jax.ShapeDtypeStruct((MLA_HEADS * MLA_HEAD_PAD, m), BF16),
        compiler_params=_params("parallel"),
        name="q_proj",
    )(z, ng.reshape(1, -1), wt, hg_col, cos_t, sin_t)


def _kvproj_kernel(zc_ref, kr_ref, ng_ref, wk_ref, wvt_ref, hg_ref, c_ref, s_ref, k_ref, vt_ref):
    x = zc_ref[...]
    ms = jnp.mean(x * x, axis=-1, keepdims=True)
    xn = (x * lax.rsqrt(ms + EPS) * ng_ref[...]).astype(BF16)
    c = c_ref[...]
    s = s_ref[...]
    hg = hg_ref[...]
    kr = kr_ref[...]
    kr_ss = jnp.sum(kr * kr, axis=-1, keepdims=True)
    kr_rot = _rope(kr * hg[:, 128:], c, s)
    for hp in range(MLA_HEADS // 2):
        y = jnp.dot(xn, wk_ref[:, hp * 256:(hp + 1) * 256], preferred_element_type=F32)
        for sub in range(2):
            lo = (2 * hp + sub) * MLA_HEAD_PAD
            kn = y[:, sub * MLA_NOPE:(sub + 1) * MLA_NOPE]
            ms = (jnp.sum(kn * kn, axis=-1, keepdims=True) + kr_ss) * (1.0 / MLA_QK)
            rs = lax.rsqrt(ms + EPS)
            k_ref[:, lo:lo + 128] = (kn * rs * hg[:, :128]).astype(k_ref.dtype)
            k_ref[:, lo + 128:lo + 256] = (kr_rot * rs).astype(k_ref.dtype)
    rows_per_dot = 4 * VT_ROWS
    for j in range(MLA_HEADS * VT_ROWS // rows_per_dot):
        sl = slice(j * rows_per_dot, (j + 1) * rows_per_dot)
        vt_ref[sl, :] = lax.dot_general(wvt_ref[sl, :], xn, _NT,
                                        preferred_element_type=F32).astype(vt_ref.dtype)
    ones = jnp.ones((VT_ROWS - MLA_V, x.shape[0]), vt_ref.dtype)
    for h in range(MLA_HEADS):
        vt_ref[h * VT_ROWS + MLA_V:(h + 1) * VT_ROWS, :] = ones


def _kvproj(z, zs, ng, wk, wvt, hg, ctab, stab, bm):
    m = z.shape[0]
    nt = ctab.shape[0] // bm
    return pl.pallas_call(
        _kvproj_kernel,
        grid=(m // bm,),
        in_specs=[pl.BlockSpec((bm, MLA_LORA), lambda i: (i, Z_CKV // MLA_LORA)),
                  pl.BlockSpec((bm, 128), lambda i: (i, 1)),
                  pl.BlockSpec((1, MLA_LORA), lambda i: (0, 0)),
                  pl.BlockSpec((MLA_LORA, MLA_HEADS * MLA_NOPE), lambda i: (0, 0)),
                  pl.BlockSpec((MLA_HEADS * VT_ROWS, MLA_LORA), lambda i: (0, 0)),
                  pl.BlockSpec((1, MLA_HEAD_PAD), lambda i: (0, 0)),
                  pl.BlockSpec((bm, 128), lambda i: (i % nt, 0)),
                  pl.BlockSpec((bm, 128), lambda i: (i % nt, 0))],
        out_specs=[pl.BlockSpec((bm, MLA_HEADS * MLA_HEAD_PAD), lambda i: (i, 0)),
                   pl.BlockSpec((MLA_HEADS * VT_ROWS, bm), lambda i: (0, i))],
        out_shape=[jax.ShapeDtypeStruct((m, MLA_HEADS * MLA_HEAD_PAD), BF16),
                   jax.ShapeDtypeStruct((MLA_HEADS * VT_ROWS, m), BF16)],
        compiler_params=_params("parallel"),
        name="kv_proj",
    )(z, zs, ng.reshape(1, -1), wk, wvt, hg, ctab, stab)


def _attn_kernel(qt_ref, k_ref, vt_ref, km_ref, vmt_ref, o_ref, m_sc, acc_sc, s_sc, *, blk, heads,
                 nq):
    i = pl.program_id(2)

    def queries(qi, h):
        return qt_ref[h * MLA_HEAD_PAD:(h + 1) * MLA_HEAD_PAD,
                      pl.ds(pl.multiple_of(qi * blk, blk), blk)]

    s_meta = [jnp.dot(km_ref[:, h * MLA_HEAD_PAD:(h + 1) * MLA_HEAD_PAD], queries(i, h),
                      preferred_element_type=F32) for h in range(heads)]
    p_meta = []
    for h in range(heads):
        m = jnp.max(s_meta[h], axis=0, keepdims=True)
        p_meta.append(jnp.exp2(s_meta[h] - m).astype(BF16))
        m_sc[h] = m
    for h in range(heads):
        acc_sc[h] = jnp.dot(vmt_ref[h * VT_ROWS:(h + 1) * VT_ROWS, :], p_meta[h],
                            preferred_element_type=F32)

    def scores(qi, kb, h):
        ksl = pl.ds(pl.multiple_of(kb * blk, blk), blk)
        return jnp.dot(k_ref[ksl, h * MLA_HEAD_PAD:(h + 1) * MLA_HEAD_PAD], queries(qi, h),
                       preferred_element_type=F32)

    @pl.when(i == 0)
    def _():
        for h in range(heads):
            s_sc[h] = scores(0, 0, h)

    def step(kb, masked):
        ksl = pl.ds(pl.multiple_of(kb * blk, blk), blk)
        tiles = [s_sc[h] for h in range(heads)]
        for h in range(heads):
            s = tiles[h]
            if masked:
                s_sc[h] = scores(jnp.minimum(i + 1, nq - 1), 0, h)
                key = lax.broadcasted_iota(jnp.int32, s.shape, 0)
                qry = lax.broadcasted_iota(jnp.int32, s.shape, 1)
                s = jnp.where(key <= qry, s, -1e30)
            else:
                s_sc[h] = scores(i, kb + 1, h)
            for g0 in range(0, blk, 256):
                cols = slice(g0, g0 + 256)
                sg = s[:, cols]
                m_prev = m_sc[h, :, cols]
                m_new = jnp.maximum(m_prev, jnp.max(sg, axis=0, keepdims=True))
                alpha = jnp.exp2(m_prev - m_new)
                p = jnp.exp2(sg - m_new)
                acc_sc[h, :, cols] = alpha * acc_sc[h, :, cols] + jnp.dot(
                    vt_ref[h * VT_ROWS:(h + 1) * VT_ROWS, ksl], p.astype(BF16),
                    preferred_element_type=F32)
                m_sc[h, :, cols] = m_new

    def body(kb, carry):
        step(kb, False)
        return carry

    lax.fori_loop(0, i, body, 0)
    step(i, True)
    for h in range(heads):
        acc = acc_sc[h]
        o_ref[:, h * MLA_V:(h + 1) * MLA_V] = (
            acc[:MLA_V] * (1.0 / acc[MLA_V:MLA_V + 1])).T.astype(o_ref.dtype)


def _attention(q, k, vt, km, vmt, blk, heads):
    nq = SEQ // blk
    kern = functools.partial(_attn_kernel, blk=blk, heads=heads, nq=nq)
    return pl.pallas_call(
        kern,
        grid=(BATCH, MLA_HEADS // heads, nq),
        in_specs=[pl.BlockSpec((heads * MLA_HEAD_PAD, SEQ), lambda b, h, i: (h, b)),
                  pl.BlockSpec((SEQ, heads * MLA_HEAD_PAD), lambda b, h, i: (b, h)),
                  pl.BlockSpec((heads * VT_ROWS, SEQ), lambda b, h, i: (h, b)),
                  pl.BlockSpec((N_META, heads * MLA_HEAD_PAD), lambda b, h, i: (0, h)),
                  pl.BlockSpec((heads * VT_ROWS, N_META), lambda b, h, i: (h, 0))],
        out_specs=pl.BlockSpec((blk, heads * MLA_V), lambda b, h, i: (b * nq + i, h)),
        out_shape=jax.ShapeDtypeStruct((ROWS, MLA_HEADS * MLA_V), BF16),
        scratch_shapes=[pltpu.VMEM((heads, 1, blk), F32),
                        pltpu.VMEM((heads, VT_ROWS, blk), F32), pltpu.VMEM((heads, blk, blk), F32)],
        compiler_params=_params("parallel", "parallel", "arbitrary"),
        name="mla_attention",
    )(q, k, vt, km, vmt)


def _meta_attn_kernel(q_ref, km_ref, vmt_ref, o_ref):
    s = lax.dot_general(q_ref[...], km_ref[...], _NT, preferred_element_type=F32)
    row = lax.broadcasted_iota(jnp.int32, s.shape, 0)
    col = lax.broadcasted_iota(jnp.int32, s.shape, 1)
    s = jnp.where(col <= row, s, -1e30)
    m = jnp.max(s, axis=-1, keepdims=True)
    p = jnp.exp2(s - m)
    l = jnp.sum(p, axis=-1, keepdims=True)
    o = lax.dot_general(p.astype(BF16), vmt_ref[:MLA_V, :], _NT, preferred_element_type=F32)
    o_ref[...] = (o / l).astype(o_ref.dtype)


def _meta_attention(qm, kmp, vmtp):
    return pl.pallas_call(
        _meta_attn_kernel,
        grid=(MLA_HEADS,),
        in_specs=[pl.BlockSpec((N_META, MLA_HEAD_PAD), lambda h: (0, h)),
                  pl.BlockSpec((128, MLA_HEAD_PAD), lambda h: (0, h)),
                  pl.BlockSpec((VT_ROWS, 128), lambda h: (h, 0))],
        out_specs=pl.BlockSpec((N_META, MLA_V), lambda h: (0, h)),
        out_shape=jax.ShapeDtypeStruct((N_META, MLA_HEADS * MLA_V), BF16),
        compiler_params=_params("parallel"),
        name="meta_attention",
    )(qm, kmp, vmtp)


def _merge_kernel(a_ref, b_ref, wa_ref, wb_ref, za_ref, zb_ref, o_ref):
    ga = _sigmoid(za_ref[...])
    gb = _sigmoid(zb_ref[...])
    ya = jnp.dot(a_ref[...], wa_ref[...], preferred_element_type=F32)
    yb = jnp.dot(b_ref[...], wb_ref[...], preferred_element_type=F32)
    o_ref[...] = (ga * ya + gb * yb).astype(o_ref.dtype)


def _merge(a, b, wa3, wb3, layer, z, bm, bn):
    m = a.shape[0]
    return pl.pallas_call(
        _merge_kernel,
        grid=(m // bm, D_MODEL // bn),
        in_specs=[pl.BlockSpec((bm, D_MODEL), lambda i, j: (i, 0)),
                  pl.BlockSpec((bm, D_MODEL), lambda i, j: (i, 0)),
                  pl.BlockSpec((None, D_MODEL, bn), lambda i, j: (layer, 0, j)),
                  pl.BlockSpec((None, D_MODEL, bn), lambda i, j: (layer, 0, j)),
                  pl.BlockSpec((bm, bn), lambda i, j: (i, Z_A // bn + j)),
                  pl.BlockSpec((bm, bn), lambda i, j: (i, Z_B // bn + j))],
        out_specs=pl.BlockSpec((bm, bn), lambda i, j: (i, j)),
        out_shape=jax.ShapeDtypeStruct((m, D_MODEL), BF16),
        compiler_params=_params("parallel", "parallel"),
        name="branch_merge",
    )(a, b, wa3, wb3, z, z)


def _outproj_kernel(m_ref, w_ref, h_ref, g_ref, ho_ref, hn_ref):
    hn = h_ref[...] + jnp.dot(m_ref[...], w_ref[...], preferred_element_type=F32)
    ho_ref[...] = hn
    ms = jnp.mean(hn * hn, axis=-1, keepdims=True)
    hn_ref[...] = (hn * lax.rsqrt(ms + EPS) * g_ref[...]).astype(hn_ref.dtype)


def _outproj(mg, w3, layer, h, g, bm):
    m = mg.shape[0]
    return pl.pallas_call(
        _outproj_kernel,
        grid=(m // bm,),
        in_specs=[pl.BlockSpec((bm, D_MODEL), lambda i: (i, 0)),
                  pl.BlockSpec((None, D_MODEL, D_MODEL), lambda i: (layer, 0, 0)),
                  pl.BlockSpec((bm, D_MODEL), lambda i: (i, 0)),
                  pl.BlockSpec((1, D_MODEL), lambda i: (0, 0))],
        out_specs=[pl.BlockSpec((bm, D_MODEL), lambda i: (i, 0)),
                   pl.BlockSpec((bm, D_MODEL), lambda i: (i, 0))],
        out_shape=[jax.ShapeDtypeStruct((m, D_MODEL), F32),
                   jax.ShapeDtypeStruct((m, D_MODEL), BF16)],
        compiler_params=_params("parallel"),
        name="out_proj",
    )(mg, w3, h, g.reshape(1, -1))


def _merge_out_kernel(a_ref, b_ref, wa_ref, wb_ref, wo_ref, za_ref, zb_ref, h_ref, g_ref,
                      ho_ref, hn_ref):
    ga = _sigmoid(za_ref[...])
    gb = _sigmoid(zb_ref[...])
    ya = jnp.dot(a_ref[...], wa_ref[...], preferred_element_type=F32)
    yb = jnp.dot(b_ref[...], wb_ref[...], preferred_element_type=F32)
    merged = (ga * ya + gb * yb).astype(BF16)
    hn = h_ref[...] + jnp.dot(merged, wo_ref[...], preferred_element_type=F32)
    ho_ref[...] = hn
    ms = jnp.mean(hn * hn, axis=-1, keepdims=True)
    hn_ref[...] = (hn * lax.rsqrt(ms + EPS) * g_ref[...]).astype(hn_ref.dtype)


def _merge_out(a, b, wa3, wb3, wo3, layer, z, h, g, bm):
    m = a.shape[0]
    rows = pl.BlockSpec((bm, D_MODEL), lambda i: (i, 0))

    def weight():
        return pl.BlockSpec((None, D_MODEL, D_MODEL), lambda i: (layer, 0, 0),
                            pipeline_mode=pl.Buffered(1))

    def gate(col0):
        return pl.BlockSpec((pl.Element(bm), pl.Element(D_MODEL)),
                            lambda i: (pl.multiple_of(i * bm, bm), col0))

    return pl.pallas_call(
        _merge_out_kernel,
        grid=(m // bm,),
        in_specs=[rows, rows, weight(), weight(), weight(), gate(Z_A), gate(Z_B), rows,
                  pl.BlockSpec((1, D_MODEL), lambda i: (0, 0))],
        out_specs=[rows, rows],
        out_shape=[jax.ShapeDtypeStruct((m, D_MODEL), F32),
                   jax.ShapeDtypeStruct((m, D_MODEL), BF16)],
        compiler_params=_params("parallel"),
        name="merge_out_proj",
    )(a, b, wa3, wb3, wo3, z, z, h, g.reshape(1, -1))


def _gateup_kernel(*refs, with_meta):
    if with_meta:
        h_ref, hm_ref, wg_ref, wu_ref, o_ref, om_ref, wgb_ref, wub_ref = refs
    else:
        h_ref, wg_ref, wu_ref, o_ref, wgb_ref, wub_ref = refs

    def swiglu(h):
        g = jnp.dot(h, wgb_ref[...], preferred_element_type=F32)
        u = jnp.dot(h, wub_ref[...], preferred_element_type=F32)
        return (g * _sigmoid(g) * u).astype(o_ref.dtype)

    @pl.when(pl.program_id(1) == 0)
    def _():
        wgb_ref[...] = wg_ref[...].astype(BF16)
        wub_ref[...] = wu_ref[...].astype(BF16)
        if with_meta:
            om_ref[...] = swiglu(hm_ref[...])

    o_ref[...] = swiglu(h_ref[...])


def _gateup(h, h_meta, w3, layer, bm, bn):
    m = h.shape[0]
    nb = FF_HIDDEN // bn
    with_meta = h_meta is not None
    weight_specs = [pl.BlockSpec((None, D_MODEL, bn), lambda j, i: (layer, 0, j)),
                    pl.BlockSpec((None, D_MODEL, bn), lambda j, i: (layer, 0, nb + j))]
    in_specs = [pl.BlockSpec((bm, D_MODEL), lambda j, i: (i, 0))]
    operands = [h]
    out_specs = [pl.BlockSpec((bm, bn), lambda j, i: (i, j))]
    out_shape = [jax.ShapeDtypeStruct((m, FF_HIDDEN), BF16)]
    if with_meta:
        in_specs.append(pl.BlockSpec((N_META, D_MODEL), lambda j, i: (0, 0)))
        operands.append(h_meta)
        out_specs.append(pl.BlockSpec((N_META, bn), lambda j, i: (0, j)))
        out_shape.append(jax.ShapeDtypeStruct((N_META, FF_HIDDEN), BF16))
    return pl.pallas_call(
        functools.partial(_gateup_kernel, with_meta=with_meta),
        grid=(nb, m // bm),
        in_specs=in_specs + weight_specs,
        out_specs=out_specs,
        out_shape=out_shape,
        scratch_shapes=[pltpu.VMEM((D_MODEL, bn), BF16), pltpu.VMEM((D_MODEL, bn), BF16)],
        compiler_params=_params("arbitrary", "arbitrary"),
        name="gate_up",
    )(*operands, w3, w3)


def _down_kernel(a_ref, w_ref, h_ref, *rest, emit_norm):
    if emit_norm:
        g_ref, ho_ref, hn_ref, acc_ref = rest
    else:
        ho_ref, acc_ref = rest
    k = pl.program_id(1)
    last = pl.num_programs(1) - 1

    def product():
        return jnp.dot(a_ref[...], w_ref[...], preferred_element_type=F32)

    @pl.when(k == 0)
    def _():
        acc_ref[...] = product()

    @pl.when((k > 0) & (k < last))
    def _():
        acc_ref[...] += product()

    @pl.when(k == last)
    def _():
        hn = h_ref[...] + (acc_ref[...] + product())
        ho_ref[...] = hn
        if emit_norm:
            ms = jnp.mean(hn * hn, axis=-1, keepdims=True)
            hn_ref[...] = (hn * lax.rsqrt(ms + EPS) * g_ref[...]).astype(hn_ref.dtype)


def _down(a, w3, layer, h, g, bm, bk):
    m = a.shape[0]
    assert FF_HIDDEN // bk >= 2
    emit_norm = g is not None
    row_block = pl.BlockSpec((bm, D_MODEL), lambda i, k: (i, 0))
    in_specs = [pl.BlockSpec((bm, bk), lambda i, k: (i, k)),
                pl.BlockSpec((None, bk, D_MODEL), lambda i, k: (layer, k, 0)),
                row_block]
    operands = [a, w3, h]
    out_specs = [row_block]
    out_shape = [jax.ShapeDtypeStruct((m, D_MODEL), F32)]
    if emit_norm:
        in_specs.append(pl.BlockSpec((1, D_MODEL), lambda i, k: (0, 0)))
        operands.append(g.reshape(1, -1))
        out_specs.append(row_block)
        out_shape.append(jax.ShapeDtypeStruct((m, D_MODEL), BF16))
    return pl.pallas_call(
        functools.partial(_down_kernel, emit_norm=emit_norm),
        grid=(m // bm, FF_HIDDEN // bk),
        in_specs=in_specs,
        out_specs=out_specs,
        out_shape=out_shape,
        scratch_shapes=[pltpu.VMEM((bm, D_MODEL), F32)],
        compiler_params=_params("parallel", "arbitrary"),
        name="down_proj",
    )(*operands)


def _rope_layout(t):
    zeros = jnp.zeros(t.shape[:-1] + (32,), t.dtype)
    return jnp.concatenate([t[..., :32], zeros, t[..., 32:], zeros], axis=-1)


def _head_layout(t):
    return jnp.concatenate([t[..., :MLA_NOPE], _rope_layout(t[..., MLA_NOPE:])], axis=-1)


def _rope_tables():
    length = N_META + SEQ
    inv = 1.0 / (ROPE_THETA ** (jnp.arange(0, MLA_ROPE, 2, dtype=F32) / MLA_ROPE))
    ang = jnp.arange(length, dtype=F32)[:, None] * inv[None, :]
    cos, sin = jnp.cos(ang), jnp.sin(ang)
    zeros = jnp.zeros_like(cos)
    ctab = jnp.concatenate([cos, zeros, cos, zeros], axis=-1)
    stab = jnp.concatenate([-sin, zeros, sin, zeros], axis=-1)
    return ctab, stab, cos.T, sin.T


def kernel(x, meta_tokens, norm1_g, w_in, gla_gate_w2, gla_gate_b, gla_onorm_g, w_branch_a,
           q_a_norm_g, w_uq, kv_a_norm_g, w_ukv, q_norm_g, k_norm_g, w_branch_b, w_out,
           norm2_g, w_gate_up, w_down):
    ctab, stab, cos_t, sin_t = _rope_tables()
    ctab_m, stab_m = ctab[:N_META], stab[:N_META]
    ctab_r, stab_r = ctab[N_META:], stab[N_META:]
    cos_tm, sin_tm = cos_t[:, :N_META], sin_t[:, :N_META]
    cos_tr, sin_tr = cos_t[:, N_META:], sin_t[:, N_META:]

    h_res = x.reshape(ROWS, D_MODEL)
    hm_res = meta_tokens.astype(F32)
    s_zero = jnp.zeros((GLA_HEADS, GLA_DK, GLA_DV), F32)

    w_in_t = jnp.swapaxes(w_in, 1, 2)
    w_kr = w_in_t[:, 7184:7248]
    zrow = jnp.zeros((DEPTH, 32, D_MODEL), F32)
    w_small = jnp.concatenate(
        [w_in_t[:, 6144:6160], jnp.zeros((DEPTH, 128 - GLA_GATE_RANK, D_MODEL), F32),
         w_kr[:, :32], zrow, w_kr[:, 32:], zrow], axis=1)
    wa = w_branch_a.astype(BF16)
    wb = w_branch_b.astype(BF16)
    wo = w_out.astype(BF16)
    wd = w_down.astype(BF16)

    for l in range(DEPTH):
        last = l == DEPTH - 1
        w2p = jnp.concatenate(
            [gla_gate_w2[l], jnp.zeros((128 - GLA_GATE_RANK, GLA_HEADS * GLA_DK), F32)],
            axis=0).astype(BF16)
        wqt = _head_layout(w_uq[l].reshape(MLA_LORA, MLA_HEADS, MLA_QK)).reshape(
            MLA_LORA, MLA_HEADS * MLA_HEAD_PAD).T.astype(BF16)
        wkv3 = w_ukv[l].reshape(MLA_LORA, MLA_HEADS, MLA_NOPE + MLA_V)
        wk = wkv3[:, :, :MLA_NOPE].reshape(MLA_LORA, MLA_HEADS * MLA_NOPE).astype(BF16)
        wvt = jnp.pad(jnp.transpose(wkv3[:, :, MLA_NOPE:], (1, 2, 0)),
                      ((0, 0), (0, VT_ROWS - MLA_V), (0, 0))).reshape(
            MLA_HEADS * VT_ROWS, MLA_LORA).astype(BF16)
        qg = (_head_layout(q_norm_g[l]) * (MLA_QK ** -0.5 * LOG2E)).reshape(MLA_HEAD_PAD, 1)
        kg = _head_layout(k_norm_g[l]).reshape(1, MLA_HEAD_PAD)

        if l == 0:
            h, zs = _rms_small(h_res, norm1_g[0], w_small, 0, RMS_SMALL_ROWS)
            hm, zsm = _rms_small(hm_res, norm1_g[0], w_small, 0, N_META)
        else:
            zs = _matmul_nt(h, w_small, l, IN_PROJ_SMALL_ROWS, N_SMALL, name="in_proj_small")
            zsm = _matmul_nt(hm, w_small, l, N_META, N_SMALL, name="in_proj_small_meta")
        z, zm = _in_proj(h, hm, w_in_t, l, *IN_PROJ_BLOCK, name="in_proj")

        gla_m, s_meta = _gla(zm, zsm, w2p, gla_gate_b[l], gla_onorm_g[l], s_zero,
                             batch=1, tokens=N_META, block=N_META, chunk=N_META, emit_state=True)
        (gla_r,) = _gla(z, zs, w2p, gla_gate_b[l], gla_onorm_g[l], s_meta,
                        batch=BATCH, tokens=SEQ, block=GLA_TOKENS, chunk=GLA_CHUNK, emit_state=False)

        km, vmt = _kvproj(zm, zsm, kv_a_norm_g[l], wk, wvt, kg, ctab_m, stab_m, N_META)
        kr, vrt = _kvproj(z, zs, kv_a_norm_g[l], wk, wvt, kg, ctab_r, stab_r, PROJ_ROWS)
        kmp = jnp.pad(km, ((0, 128 - N_META), (0, 0)))
        vmtp = jnp.pad(vmt, ((0, 0), (0, 128 - N_META)))
        qr = _qproj(z, q_a_norm_g[l], wqt, qg, cos_tr, sin_tr, PROJ_ROWS)
        att_r = _attention(qr, kr, vrt, km, vmt, ATTN_BLOCK, ATTN_HEADS)

        h_res, h2 = _merge_out(gla_r, att_r, wa, wb, wo, l, z, h_res, norm2_g[l], MERGE_OUT_ROWS)
        if last:
            (act,) = _gateup(h2, None, w_gate_up, l, *GATE_UP_BLOCK)
            (h_res,) = _down(act, wd, l, h_res, None, *DOWN_BLOCK)
        else:
            qm = _qproj(zm, q_a_norm_g[l], wqt, qg, cos_tm, sin_tm, N_META)
            att_m = _meta_attention(qm.T, kmp, vmtp)
            merged_m = _merge(gla_m, att_m, wa, wb, l, zm, N_META, MERGE_BLOCK[1])
            hm_res, hm2 = _outproj(merged_m, wo, l, hm_res, norm2_g[l], N_META)
            act, act_m = _gateup(h2, hm2, w_gate_up, l, *GATE_UP_BLOCK)
            h_res, h = _down(act, wd, l, h_res, norm1_g[l + 1], *DOWN_BLOCK)
            hm_res, hm = _down(act_m, wd, l, hm_res, norm1_g[l + 1], N_META, DOWN_META_K)

    return h_res.reshape(BATCH, SEQ, D_MODEL)
```

```python
import functools

import jax
import jax.numpy as jnp
from jax import lax
from jax.experimental import pallas as pl
from jax.experimental.pallas import tpu as pltpu

D_MODEL = 2048
BATCH = 4
SEQ = 4096
DEPTH = 2
N_META = 16
ROWS = BATCH * SEQ

GLA_HEADS = 4
GLA_DK = 256
GLA_DV = 512
GLA_GATE_RANK = 16
GLA_GATE_TAU = 16.0
GLA_CHUNK = 64

MLA_HEADS = 16
MLA_LORA = 512
MLA_NOPE = 128
MLA_ROPE = 64
MLA_QK = 192
MLA_V = 128
VT_ROWS = 144
MLA_HEAD_PAD = 256
ROPE_THETA = 10000.0
FF_HIDDEN = 5632
EPS = 1e-6
LOG2E = 1.4426950408889634

W_IN_ZA = 6144
W_IN_C, W_IN_C_END = 6160, 7184
W_IN_G = 7248
Z_Q, Z_K, Z_V, Z_R, Z_CQ, Z_CKV, Z_A, Z_B = 0, 1024, 2048, 4096, 6144, 6656, 7168, 9216
N_Z = 11264
N_SMALL = 256

RMS_SMALL_ROWS = 1024
IN_PROJ_BLOCK = (1024, 1024)
IN_PROJ_SMALL_ROWS = 2048
GLA_TOKENS = 512
PROJ_ROWS = 1024
ATTN_BLOCK = 512
ATTN_HEADS = 4
MERGE_OUT_ROWS = 256
GATE_UP_BLOCK = (1024, 512)
DOWN_BLOCK = (512, 2816)
DOWN_META_K = 1408

VMEM_LIMIT = 56 * 1024 * 1024
BF16 = jnp.bfloat16
F32 = jnp.float32

_NT = (((1,), (1,)), ((), ()))
_TN = (((0,), (0,)), ((), ()))


def _params(*sem):
    return pltpu.CompilerParams(dimension_semantics=sem, vmem_limit_bytes=VMEM_LIMIT)


def _sigmoid(x):
    return 1.0 / (1.0 + jnp.exp(-x))


def _rms_small_kernel(x_ref, g_ref, wt_ref, h_ref, zs_ref):
    x = x_ref[...]
    ms = jnp.mean(x * x, axis=-1, keepdims=True)
    h = (x * lax.rsqrt(ms + EPS) * g_ref[...]).astype(BF16)
    h_ref[...] = h
    zs_ref[...] = lax.dot_general(h, wt_ref[...].astype(BF16), _NT, preferred_element_type=F32)


def _rms_small(x, g, wt3, layer, bm):
    m, d = x.shape
    n = wt3.shape[1]
    return pl.pallas_call(
        _rms_small_kernel,
        grid=(m // bm,),
        in_specs=[pl.BlockSpec((bm, d), lambda i: (i, 0)),
                  pl.BlockSpec((1, d), lambda i: (0, 0)),
                  pl.BlockSpec((None, n, d), lambda i: (layer, 0, 0))],
        out_specs=[pl.BlockSpec((bm, d), lambda i: (i, 0)),
                   pl.BlockSpec((bm, n), lambda i: (i, 0))],
        out_shape=[jax.ShapeDtypeStruct((m, d), BF16),
                   jax.ShapeDtypeStruct((m, n), F32)],
        compiler_params=_params("parallel"),
        name="rmsnorm_small_proj",
    )(x, g.reshape(1, d), wt3)


def _mm_nt_kernel(a_ref, wt_ref, o_ref):
    o_ref[...] = lax.dot_general(a_ref[...], wt_ref[...].astype(BF16), _NT,
                                 preferred_element_type=F32)


def _matmul_nt(a, wt3, layer, bm, bn, name):
    m, k = a.shape
    n = wt3.shape[1]
    return pl.pallas_call(
        _mm_nt_kernel,
        grid=(m // bm, n // bn),
        in_specs=[pl.BlockSpec((bm, k), lambda i, j: (i, 0)),
                  pl.BlockSpec((None, bn, k), lambda i, j: (layer, j, 0))],
        out_specs=pl.BlockSpec((bm, bn), lambda i, j: (i, j)),
        out_shape=jax.ShapeDtypeStruct((m, n), F32),
        compiler_params=_params("parallel", "parallel"),
        name=name,
    )(a, wt3)


def _mm_nt_wstat_kernel(a_ref, am_ref, wt_ref, o_ref, om_ref, wb_ref):
    @pl.when(pl.program_id(1) == 0)
    def _():
        wb_ref[...] = wt_ref[0].astype(BF16)
        om_ref[...] = lax.dot_general(am_ref[...], wb_ref[...], _NT, preferred_element_type=F32)

    o_ref[...] = lax.dot_general(a_ref[...], wb_ref[...], _NT, preferred_element_type=F32)


def _in_proj_row0(j, bn):
    n_a = W_IN_ZA // bn
    n_c = (W_IN_C_END - W_IN_C) // bn
    return jnp.where(j < n_a, j * bn,
                     jnp.where(j < n_a + n_c, W_IN_C + (j - n_a) * bn,
                               W_IN_G + (j - n_a - n_c) * bn))


def _in_proj(a, a_meta, wt3, layer, bm, bn, name):
    m, k = a.shape
    return pl.pallas_call(
        _mm_nt_wstat_kernel,
        grid=(N_Z // bn, m // bm),
        in_specs=[pl.BlockSpec((bm, k), lambda j, i: (i, 0)),
                  pl.BlockSpec((N_META, k), lambda j, i: (0, 0)),
                  pl.BlockSpec((pl.Element(1), pl.Element(bn), pl.Element(k)),
                               lambda j, i: (layer, pl.multiple_of(_in_proj_row0(j, bn), 16), 0))],
        out_specs=[pl.BlockSpec((bm, bn), lambda j, i: (i, j)),
                   pl.BlockSpec((N_META, bn), lambda j, i: (0, j))],
        out_shape=[jax.ShapeDtypeStruct((m, N_Z), F32),
                   jax.ShapeDtypeStruct((N_META, N_Z), F32)],
        scratch_shapes=[pltpu.VMEM((bn, k), BF16)],
        compiler_params=_params("arbitrary", "arbitrary"),
        name=name,
    )(a, a_meta, wt3)


def _gla_kernel(q_ref, k_ref, v_ref, r_ref, zlr_ref, w2_ref, gb_ref, og_ref, s0_ref,
                o_ref, *rest, chunk, n_chunks, n_total, emit_state):
    if emit_state:
        sfin_ref, st_ref, b_sc = rest
    else:
        st_ref, b_sc = rest
    t = pl.program_id(1)
    heads = range(GLA_HEADS)

    row = lax.broadcasted_iota(jnp.int32, (chunk, chunk), 0)
    col = lax.broadcasted_iota(jnp.int32, (chunk, chunk), 1)
    causal = col <= row
    tri = jnp.where(causal, 1.0, 0.0).astype(BF16)
    w2 = w2_ref[...]
    gb = gb_ref[...]
    og = og_ref[...]
    kdim = GLA_HEADS * GLA_DK

    def log_gate_split(gc):
        sl = pl.ds(pl.multiple_of(gc * chunk, chunk), chunk)
        logit = jnp.dot(zlr_ref[sl, :].astype(BF16), w2, preferred_element_type=F32) + gb
        g = (jnp.minimum(logit, 0.0) - jnp.log1p(jnp.exp(-jnp.abs(logit)))) * (1.0 / GLA_GATE_TAU)
        g_hi = g.astype(BF16)
        g_lo = (g - g_hi.astype(F32)).astype(BF16)
        return jnp.concatenate([g_hi, g_lo], axis=1)

    def log_decay(g_split):
        cs = jnp.dot(tri, g_split, preferred_element_type=F32)
        return cs[:, :kdim] + cs[:, kdim:]

    @pl.when(t == 0)
    def _():
        st_ref[...] = s0_ref[...]
        b_sc[...] = log_decay(log_gate_split(0))

    def body(c, carry):
        sl = pl.ds(pl.multiple_of(c * chunk, chunk), chunk)
        g_split_next = log_gate_split(jnp.minimum(t * n_chunks + c + 1, n_total - 1))
        r = r_ref[sl, :]
        out_gate = r * _sigmoid(r)
        st = [st_ref[h] for h in heads]
        st_bf = [st[h].astype(BF16) for h in heads]
        b = b_sc[...]
        b_last = b[chunk - 1:chunk, :]
        eb = jnp.exp(b)
        enb = jnp.exp(-b)
        erel = jnp.exp(b_last - b)
        elast = jnp.exp(b_last)
        k = k_ref[sl, :]
        qd = (q_ref[sl, :] * (GLA_DK ** -0.5) * eb).astype(BF16)
        kd = (k * enb).astype(BF16)
        k2 = (k * erel).astype(BF16)
        v = v_ref[sl, :].astype(BF16)

        def hk(x, h):
            return x[:, h * GLA_DK:(h + 1) * GLA_DK]

        def hv(x, h):
            return x[:, h * GLA_DV:(h + 1) * GLA_DV]

        o_state = [jnp.dot(hk(qd, h), st_bf[h], preferred_element_type=F32) for h in heads]
        a = [lax.dot_general(hk(qd, h), hk(kd, h), _NT, preferred_element_type=F32) for h in heads]
        upd = [lax.dot_general(hk(k2, h), hv(v, h), _TN, preferred_element_type=F32)
               for h in heads]
        b_sc[...] = log_decay(g_split_next)
        o_local = [jnp.dot(jnp.where(causal, a[h], 0.0).astype(BF16), hv(v, h),
                           preferred_element_type=F32) for h in heads]
        decay_col = jnp.broadcast_to(elast, (128, kdim)).T
        for h in heads:
            col = decay_col[h * GLA_DK:(h + 1) * GLA_DK]
            st_ref[h] = st[h] * jnp.concatenate([col] * (GLA_DV // 128), axis=1) + upd[h]
        for h in heads:
            o = o_local[h] + o_state[h]
            ms = jnp.mean(o * o, axis=-1, keepdims=True)
            on = o * lax.rsqrt(ms + EPS) * og
            o_ref[sl, h * GLA_DV:(h + 1) * GLA_DV] = (on * hv(out_gate, h)).astype(o_ref.dtype)
        return carry

    lax.fori_loop(0, n_chunks, body, 0)

    if emit_state:
        @pl.when(t == pl.num_programs(1) - 1)
        def _():
            sfin_ref[...] = st_ref[...]


def _gla(z, zs, w2p, gate_b, onorm_g, s0, *, batch, tokens, block, chunk, emit_state):
    nt = tokens // block
    rows = batch * tokens
    kern = functools.partial(_gla_kernel, chunk=chunk, n_chunks=block // chunk,
                             n_total=tokens // chunk, emit_state=emit_state)
    kdim = GLA_HEADS * GLA_DK
    vdim = GLA_HEADS * GLA_DV
    in_specs = [
        pl.BlockSpec((block, kdim), lambda b, t: (b * nt + t, Z_Q // kdim)),
        pl.BlockSpec((block, kdim), lambda b, t: (b * nt + t, Z_K // kdim)),
        pl.BlockSpec((block, vdim), lambda b, t: (b * nt + t, Z_V // vdim)),
        pl.BlockSpec((block, vdim), lambda b, t: (b * nt + t, Z_R // vdim)),
        pl.BlockSpec((tokens, 128), lambda b, t: (b, 0)),
        pl.BlockSpec((128, kdim), lambda b, t: (0, 0)),
        pl.BlockSpec((1, kdim), lambda b, t: (0, 0)),
        pl.BlockSpec((1, GLA_DV), lambda b, t: (0, 0)),
        pl.BlockSpec((GLA_HEADS, GLA_DK, GLA_DV), lambda b, t: (0, 0, 0)),
    ]
    out_specs = [pl.BlockSpec((block, vdim), lambda b, t: (b * nt + t, 0))]
    out_shape = [jax.ShapeDtypeStruct((rows, vdim), BF16)]
    if emit_state:
        out_specs.append(pl.BlockSpec((GLA_HEADS, GLA_DK, GLA_DV), lambda b, t: (b, 0, 0)))
        out_shape.append(jax.ShapeDtypeStruct((batch * GLA_HEADS, GLA_DK, GLA_DV), F32))
    res = pl.pallas_call(
        kern,
        grid=(batch, nt),
        in_specs=in_specs,
        out_specs=out_specs,
        out_shape=out_shape,
        scratch_shapes=[pltpu.VMEM((GLA_HEADS, GLA_DK, GLA_DV), F32),
                        pltpu.VMEM((chunk, kdim), F32)],
        compiler_params=_params("parallel", "arbitrary"),
        name="gla",
    )(z, z, z, z, zs, w2p, gate_b.reshape(1, -1), onorm_g.reshape(1, -1), s0)
    return res


def _rope(x, c, s):
    return x * c + pltpu.roll(x, 64, 1) * s


def _qproj_kernel(zc_ref, ng_ref, wt_ref, hg_ref, c_ref, s_ref, o_ref, *, heads_per_dot):
    x = zc_ref[...]
    ms = jnp.mean(x * x, axis=-1, keepdims=True)
    xn = (x * lax.rsqrt(ms + EPS) * ng_ref[...]).astype(BF16)
    bm = x.shape[0]
    c = c_ref[...]
    s = s_ref[...]
    hg = jnp.broadcast_to(hg_ref[...], (MLA_HEAD_PAD, bm))
    for hc in range(MLA_HEADS // heads_per_dot):
        base = hc * heads_per_dot * MLA_HEAD_PAD
        yt = lax.dot_general(wt_ref[base:base + heads_per_dot * MLA_HEAD_PAD, :], xn, _NT,
                             preferred_element_type=F32)
        for j in range(heads_per_dot):
            lo = base + j * MLA_HEAD_PAD
            y = yt[j * MLA_HEAD_PAD:(j + 1) * MLA_HEAD_PAD]
            ms = jnp.sum(y * y, axis=0, keepdims=True) * (1.0 / MLA_QK)
            yn = y * lax.rsqrt(ms + EPS) * hg
            x1 = yn[128:160]
            x2 = yn[192:224]
            o_ref[lo:lo + 128, :] = yn[:128].astype(o_ref.dtype)
            o_ref[lo + 128:lo + 160, :] = (x1 * c - x2 * s).astype(o_ref.dtype)
            o_ref[lo + 160:lo + 192, :] = yn[160:192].astype(o_ref.dtype)
            o_ref[lo + 192:lo + 224, :] = (x2 * c + x1 * s).astype(o_ref.dtype)
            o_ref[lo + 224:lo + 256, :] = yn[224:256].astype(o_ref.dtype)


def _qproj(z, ng, wt, hg_col, cos_t, sin_t, bm):
    m = z.shape[0]
    nt = cos_t.shape[1] // bm
    kern = functools.partial(_qproj_kernel, heads_per_dot=1)
    return pl.pallas_call(
        kern,
        grid=(m // bm,),
        in_specs=[pl.BlockSpec((bm, MLA_LORA), lambda i: (i, Z_CQ // MLA_LORA)),
                  pl.BlockSpec((1, MLA_LORA), lambda i: (0, 0)),
                  pl.BlockSpec((MLA_HEADS * MLA_HEAD_PAD, MLA_LORA), lambda i: (0, 0)),
                  pl.BlockSpec((MLA_HEAD_PAD, 1), lambda i: (0, 0)),
                  pl.BlockSpec((MLA_ROPE // 2, bm), lambda i: (0, i % nt)),
                  pl.BlockSpec((MLA_ROPE // 2, bm), lambda i: (0, i % nt))],
        out_specs=pl.BlockSpec((MLA_HEADS * MLA_HEAD_PAD, bm), lambda i: (0, i)),
        out_shape=jax.ShapeDtypeStruct((MLA_HEADS * MLA_HEAD_PAD, m), BF16),
        compiler_params=_params("parallel"),
        name="q_proj",
    )(z, ng.reshape(1, -1), wt, hg_col, cos_t, sin_t)


def _kvproj_kernel(zc_ref, kr_ref, ng_ref, wk_ref, wvt_ref, hg_ref, c_ref, s_ref, k_ref, vt_ref):
    x = zc_ref[...]
    ms = jnp.mean(x * x, axis=-1, keepdims=True)
    xn = (x * lax.rsqrt(ms + EPS) * ng_ref[...]).astype(BF16)
    c = c_ref[...]
    s = s_ref[...]
    hg = hg_ref[...]
    kr = kr_ref[...]
    kr_ss = jnp.sum(kr * kr, axis=-1, keepdims=True)
    kr_rot = _rope(kr * hg[:, 128:], c, s)
    for hp in range(MLA_HEADS // 2):
        y = jnp.dot(xn, wk_ref[:, hp * 256:(hp + 1) * 256], preferred_element_type=F32)
        for sub in range(2):
            lo = (2 * hp + sub) * MLA_HEAD_PAD
            kn = y[:, sub * MLA_NOPE:(sub + 1) * MLA_NOPE]
            ms = (jnp.sum(kn * kn, axis=-1, keepdims=True) + kr_ss) * (1.0 / MLA_QK)
            rs = lax.rsqrt(ms + EPS)
            k_ref[:, lo:lo + 128] = (kn * rs * hg[:, :128]).astype(k_ref.dtype)
            k_ref[:, lo + 128:lo + 256] = (kr_rot * rs).astype(k_ref.dtype)
    rows_per_dot = 4 * VT_ROWS
    for j in range(MLA_HEADS * VT_ROWS // rows_per_dot):
        sl = slice(j * rows_per_dot, (j + 1) * rows_per_dot)
        vt_ref[sl, :] = lax.dot_general(wvt_ref[sl, :], xn, _NT,
                                        preferred_element_type=F32).astype(vt_ref.dtype)
    ones = jnp.ones((VT_ROWS - MLA_V, x.shape[0]), vt_ref.dtype)
    for h in range(MLA_HEADS):
        vt_ref[h * VT_ROWS + MLA_V:(h + 1) * VT_ROWS, :] = ones


def _kvproj(z, zs, ng, wk, wvt, hg, ctab, stab, bm):
    m = z.shape[0]
    nt = ctab.shape[0] // bm
    return pl.pallas_call(
        _kvproj_kernel,
        grid=(m // bm,),
        in_specs=[pl.BlockSpec((bm, MLA_LORA), lambda i: (i, Z_CKV // MLA_LORA)),
                  pl.BlockSpec((bm, 128), lambda i: (i, 1)),
                  pl.BlockSpec((1, MLA_LORA), lambda i: (0, 0)),
                  pl.BlockSpec((MLA_LORA, MLA_HEADS * MLA_NOPE), lambda i: (0, 0)),
                  pl.BlockSpec((MLA_HEADS * VT_ROWS, MLA_LORA), lambda i: (0, 0)),
                  pl.BlockSpec((1, MLA_HEAD_PAD), lambda i: (0, 0)),
                  pl.BlockSpec((bm, 128), lambda i: (i % nt, 0)),
                  pl.BlockSpec((bm, 128), lambda i: (i % nt, 0))],
        out_specs=[pl.BlockSpec((bm, MLA_HEADS * MLA_HEAD_PAD), lambda i: (i, 0)),
                   pl.BlockSpec((MLA_HEADS * VT_ROWS, bm), lambda i: (0, i))],
        out_shape=[jax.ShapeDtypeStruct((m, MLA_HEADS * MLA_HEAD_PAD), BF16),
                   jax.ShapeDtypeStruct((MLA_HEADS * VT_ROWS, m), BF16)],
        compiler_params=_params("parallel"),
        name="kv_proj",
    )(z, zs, ng.reshape(1, -1), wk, wvt, hg, ctab, stab)


def _attn_kernel(qt_ref, k_ref, vt_ref, km_ref, vmt_ref, o_ref, m_sc, acc_sc, s_sc, *, blk, heads,
                 nq):
    i = pl.program_id(2)

    def queries(qi, h):
        return qt_ref[h * MLA_HEAD_PAD:(h + 1) * MLA_HEAD_PAD,
                      pl.ds(pl.multiple_of(qi * blk, blk), blk)]

    s_meta = [jnp.dot(km_ref[:, h * MLA_HEAD_PAD:(h + 1) * MLA_HEAD_PAD], queries(i, h),
                      preferred_element_type=F32) for h in range(heads)]
    p_meta = []
    for h in range(heads):
        m = jnp.max(s_meta[h], axis=0, keepdims=True)
        p_meta.append(jnp.exp2(s_meta[h] - m).astype(BF16))
        m_sc[h] = m
    for h in range(heads):
        acc_sc[h] = jnp.dot(vmt_ref[h * VT_ROWS:(h + 1) * VT_ROWS, :], p_meta[h],
                            preferred_element_type=F32)

    def scores(qi, kb, h):
        ksl = pl.ds(pl.multiple_of(kb * blk, blk), blk)
        return jnp.dot(k_ref[ksl, h * MLA_HEAD_PAD:(h + 1) * MLA_HEAD_PAD], queries(qi, h),
                       preferred_element_type=F32)

    @pl.when(i == 0)
    def _():
        for h in range(heads):
            s_sc[h] = scores(0, 0, h)

    def step(kb, masked):
        ksl = pl.ds(pl.multiple_of(kb * blk, blk), blk)
        tiles = [s_sc[h] for h in range(heads)]
        for h in range(heads):
            s = tiles[h]
            if masked:
                s_sc[h] = scores(jnp.minimum(i + 1, nq - 1), 0, h)
                key = lax.broadcasted_iota(jnp.int32, s.shape, 0)
                qry = lax.broadcasted_iota(jnp.int32, s.shape, 1)
                s = jnp.where(key <= qry, s, -1e30)
            else:
                s_sc[h] = scores(i, kb + 1, h)
            for g0 in range(0, blk, 256):
                cols = slice(g0, g0 + 256)
                sg = s[:, cols]
                m_prev = m_sc[h, :, cols]
                m_new = jnp.maximum(m_prev, jnp.max(sg, axis=0, keepdims=True))
                alpha = jnp.exp2(m_prev - m_new)
                p = jnp.exp2(sg - m_new)
                acc_sc[h, :, cols] = alpha * acc_sc[h, :, cols] + jnp.dot(
                    vt_ref[h * VT_ROWS:(h + 1) * VT_ROWS, ksl], p.astype(BF16),
                    preferred_element_type=F32)
                m_sc[h, :, cols] = m_new

    def body(kb, carry):
        step(kb, False)
        return carry

    lax.fori_loop(0, i, body, 0)
    step(i, True)
    for h in range(heads):
        acc = acc_sc[h]
        o_ref[:, h * MLA_V:(h + 1) * MLA_V] = (
            acc[:MLA_V] * (1.0 / acc[MLA_V:MLA_V + 1])).T.astype(o_ref.dtype)


def _attention(q, k, vt, km, vmt, blk, heads):
    nq = SEQ // blk
    kern = functools.partial(_attn_kernel, blk=blk, heads=heads, nq=nq)
    return pl.pallas_call(
        kern,
        grid=(BATCH, MLA_HEADS // heads, nq),
        in_specs=[pl.BlockSpec((heads * MLA_HEAD_PAD, SEQ), lambda b, h, i: (h, b)),
                  pl.BlockSpec((SEQ, heads * MLA_HEAD_PAD), lambda b, h, i: (b, h)),
                  pl.BlockSpec((heads * VT_ROWS, SEQ), lambda b, h, i: (h, b)),
                  pl.BlockSpec((N_META, heads * MLA_HEAD_PAD), lambda b, h, i: (0, h)),
                  pl.BlockSpec((heads * VT_ROWS, N_META), lambda b, h, i: (h, 0))],
        out_specs=pl.BlockSpec((blk, heads * MLA_V), lambda b, h, i: (b * nq + i, h)),
        out_shape=jax.ShapeDtypeStruct((ROWS, MLA_HEADS * MLA_V), BF16),
        scratch_shapes=[pltpu.VMEM((heads, 1, blk), F32),
                        pltpu.VMEM((heads, VT_ROWS, blk), F32), pltpu.VMEM((heads, blk, blk), F32)],
        compiler_params=_params("parallel", "parallel", "arbitrary"),
        name="mla_attention",
    )(q, k, vt, km, vmt)


def _meta_attn_kernel(q_ref, km_ref, vmt_ref, o_ref):
    s = lax.dot_general(q_ref[...], km_ref[...], _NT, preferred_element_type=F32)
    row = lax.broadcasted_iota(jnp.int32, s.shape, 0)
    col = lax.broadcasted_iota(jnp.int32, s.shape, 1)
    s = jnp.where(col <= row, s, -1e30)
    m = jnp.max(s, axis=-1, keepdims=True)
    p = jnp.exp2(s - m)
    l = jnp.sum(p, axis=-1, keepdims=True)
    o = lax.dot_general(p.astype(BF16), vmt_ref[:MLA_V, :], _NT, preferred_element_type=F32)
    o_ref[...] = (o / l).astype(o_ref.dtype)


def _meta_attention(qm, kmp, vmtp):
    return pl.pallas_call(
        _meta_attn_kernel,
        grid=(MLA_HEADS,),
        in_specs=[pl.BlockSpec((N_META, MLA_HEAD_PAD), lambda h: (0, h)),
                  pl.BlockSpec((128, MLA_HEAD_PAD), lambda h: (0, h)),
                  pl.BlockSpec((VT_ROWS, 128), lambda h: (h, 0))],
        out_specs=pl.BlockSpec((N_META, MLA_V), lambda h: (0, h)),
        out_shape=jax.ShapeDtypeStruct((N_META, MLA_HEADS * MLA_V), BF16),
        compiler_params=_params("parallel"),
        name="meta_attention",
    )(qm, kmp, vmtp)


def _merge_out_kernel(a_ref, b_ref, wa_ref, wb_ref, wo_ref, za_ref, zb_ref, h_ref, g_ref,
                      ho_ref, hn_ref):
    ga = _sigmoid(za_ref[...])
    gb = _sigmoid(zb_ref[...])
    ya = jnp.dot(a_ref[...], wa_ref[...], preferred_element_type=F32)
    yb = jnp.dot(b_ref[...], wb_ref[...], preferred_element_type=F32)
    merged = (ga * ya + gb * yb).astype(BF16)
    hn = h_ref[...] + jnp.dot(merged, wo_ref[...], preferred_element_type=F32)
    ho_ref[...] = hn
    ms = jnp.mean(hn * hn, axis=-1, keepdims=True)
    hn_ref[...] = (hn * lax.rsqrt(ms + EPS) * g_ref[...]).astype(hn_ref.dtype)


def _merge_out(a, b, wa3, wb3, wo3, layer, z, h, g, bm):
    m = a.shape[0]
    rows = pl.BlockSpec((bm, D_MODEL), lambda i: (i, 0))

    def weight():
        return pl.BlockSpec((None, D_MODEL, D_MODEL), lambda i: (layer, 0, 0),
                            pipeline_mode=pl.Buffered(1))

    def gate(col0):
        return pl.BlockSpec((pl.Element(bm), pl.Element(D_MODEL)),
                            lambda i: (pl.multiple_of(i * bm, bm), col0))

    return pl.pallas_call(
        _merge_out_kernel,
        grid=(m // bm,),
        in_specs=[rows, rows, weight(), weight(), weight(), gate(Z_A), gate(Z_B), rows,
                  pl.BlockSpec((1, D_MODEL), lambda i: (0, 0))],
        out_specs=[rows, rows],
        out_shape=[jax.ShapeDtypeStruct((m, D_MODEL), F32),
                   jax.ShapeDtypeStruct((m, D_MODEL), BF16)],
        compiler_params=_params("parallel"),
        name="merge_out_proj",
    )(a, b, wa3, wb3, wo3, z, z, h, g.reshape(1, -1))


def _gateup_kernel(*refs, with_meta):
    if with_meta:
        h_ref, hm_ref, wg_ref, wu_ref, o_ref, om_ref, wgb_ref, wub_ref = refs
    else:
        h_ref, wg_ref, wu_ref, o_ref, wgb_ref, wub_ref = refs

    def swiglu(h):
        g = jnp.dot(h, wgb_ref[...], preferred_element_type=F32)
        u = jnp.dot(h, wub_ref[...], preferred_element_type=F32)
        return (g * _sigmoid(g) * u).astype(o_ref.dtype)

    @pl.when(pl.program_id(1) == 0)
    def _():
        wgb_ref[...] = wg_ref[...].astype(BF16)
        wub_ref[...] = wu_ref[...].astype(BF16)
        if with_meta:
            om_ref[...] = swiglu(hm_ref[...])

    o_ref[...] = swiglu(h_ref[...])


def _gateup(h, h_meta, w3, layer, bm, bn):
    m = h.shape[0]
    nb = FF_HIDDEN // bn
    with_meta = h_meta is not None
    weight_specs = [pl.BlockSpec((None, D_MODEL, bn), lambda j, i: (layer, 0, j)),
                    pl.BlockSpec((None, D_MODEL, bn), lambda j, i: (layer, 0, nb + j))]
    in_specs = [pl.BlockSpec((bm, D_MODEL), lambda j, i: (i, 0))]
    operands = [h]
    out_specs = [pl.BlockSpec((bm, bn), lambda j, i: (i, j))]
    out_shape = [jax.ShapeDtypeStruct((m, FF_HIDDEN), BF16)]
    if with_meta:
        in_specs.append(pl.BlockSpec((N_META, D_MODEL), lambda j, i: (0, 0)))
        operands.append(h_meta)
        out_specs.append(pl.BlockSpec((N_META, bn), lambda j, i: (0, j)))
        out_shape.append(jax.ShapeDtypeStruct((N_META, FF_HIDDEN), BF16))
    return pl.pallas_call(
        functools.partial(_gateup_kernel, with_meta=with_meta),
        grid=(nb, m // bm),
        in_specs=in_specs + weight_specs,
        out_specs=out_specs,
        out_shape=out_shape,
        scratch_shapes=[pltpu.VMEM((D_MODEL, bn), BF16), pltpu.VMEM((D_MODEL, bn), BF16)],
        compiler_params=_params("arbitrary", "arbitrary"),
        name="gate_up",
    )(*operands, w3, w3)


def _down_kernel(a_ref, w_ref, h_ref, *rest, emit_norm):
    if emit_norm:
        g_ref, ho_ref, hn_ref, acc_ref = rest
    else:
        ho_ref, acc_ref = rest
    k = pl.program_id(1)
    last = pl.num_programs(1) - 1

    def product():
        return jnp.dot(a_ref[...], w_ref[...], preferred_element_type=F32)

    @pl.when(k == 0)
    def _():
        acc_ref[...] = product()

    @pl.when((k > 0) & (k < last))
    def _():
        acc_ref[...] += product()

    @pl.when(k == last)
    def _():
        hn = h_ref[...] + (acc_ref[...] + product())
        ho_ref[...] = hn
        if emit_norm:
            ms = jnp.mean(hn * hn, axis=-1, keepdims=True)
            hn_ref[...] = (hn * lax.rsqrt(ms + EPS) * g_ref[...]).astype(hn_ref.dtype)


def _down(a, w3, layer, h, g, bm, bk):
    m = a.shape[0]
    assert FF_HIDDEN // bk >= 2
    emit_norm = g is not None
    row_block = pl.BlockSpec((bm, D_MODEL), lambda i, k: (i, 0))
    in_specs = [pl.BlockSpec((bm, bk), lambda i, k: (i, k)),
                pl.BlockSpec((None, bk, D_MODEL), lambda i, k: (layer, k, 0)),
                row_block]
    operands = [a, w3, h]
    out_specs = [row_block]
    out_shape = [jax.ShapeDtypeStruct((m, D_MODEL), F32)]
    if emit_norm:
        in_specs.append(pl.BlockSpec((1, D_MODEL), lambda i, k: (0, 0)))
        operands.append(g.reshape(1, -1))
        out_specs.append(row_block)
        out_shape.append(jax.ShapeDtypeStruct((m, D_MODEL), BF16))
    return pl.pallas_call(
        functools.partial(_down_kernel, emit_norm=emit_norm),
        grid=(m // bm, FF_HIDDEN // bk),
        in_specs=in_specs,
        out_specs=out_specs,
        out_shape=out_shape,
        scratch_shapes=[pltpu.VMEM((bm, D_MODEL), F32)],
        compiler_params=_params("parallel", "arbitrary"),
        name="down_proj",
    )(*operands)


def _rope_layout(t):
    zeros = jnp.zeros(t.shape[:-1] + (32,), t.dtype)
    return jnp.concatenate([t[..., :32], zeros, t[..., 32:], zeros], axis=-1)


def _head_layout(t):
    return jnp.concatenate([t[..., :MLA_NOPE], _rope_layout(t[..., MLA_NOPE:])], axis=-1)


def _rope_tables():
    length = N_META + SEQ
    inv = 1.0 / (ROPE_THETA ** (jnp.arange(0, MLA_ROPE, 2, dtype=F32) / MLA_ROPE))
    ang = jnp.arange(length, dtype=F32)[:, None] * inv[None, :]
    cos, sin = jnp.cos(ang), jnp.sin(ang)
    zeros = jnp.zeros_like(cos)
    ctab = jnp.concatenate([cos, zeros, cos, zeros], axis=-1)
    stab = jnp.concatenate([-sin, zeros, sin, zeros], axis=-1)
    return ctab, stab, cos.T, sin.T


def kernel(x, meta_tokens, norm1_g, w_in, gla_gate_w2, gla_gate_b, gla_onorm_g, w_branch_a,
           q_a_norm_g, w_uq, kv_a_norm_g, w_ukv, q_norm_g, k_norm_g, w_branch_b, w_out,
           norm2_g, w_gate_up, w_down):
    ctab, stab, cos_t, sin_t = _rope_tables()
    ctab_m, stab_m = ctab[:N_META], stab[:N_META]
    ctab_r, stab_r = ctab[N_META:], stab[N_META:]
    cos_tm, sin_tm = cos_t[:, :N_META], sin_t[:, :N_META]
    cos_tr, sin_tr = cos_t[:, N_META:], sin_t[:, N_META:]

    h_res = x.reshape(ROWS, D_MODEL)
    hm_res = meta_tokens.astype(F32)
    s_zero = jnp.zeros((GLA_HEADS, GLA_DK, GLA_DV), F32)

    w_in_t = jnp.swapaxes(w_in, 1, 2)
    w_kr = w_in_t[:, 7184:7248]
    zrow = jnp.zeros((DEPTH, 32, D_MODEL), F32)
    w_small = jnp.concatenate(
        [w_in_t[:, 6144:6160], jnp.zeros((DEPTH, 128 - GLA_GATE_RANK, D_MODEL), F32),
         w_kr[:, :32], zrow, w_kr[:, 32:], zrow], axis=1)
    wa = w_branch_a.astype(BF16)
    wb = w_branch_b.astype(BF16)
    wo = w_out.astype(BF16)
    wd = w_down.astype(BF16)

    for l in range(DEPTH):
        last = l == DEPTH - 1
        w2p = jnp.concatenate(
            [gla_gate_w2[l], jnp.zeros((128 - GLA_GATE_RANK, GLA_HEADS * GLA_DK), F32)],
            axis=0).astype(BF16)
        wqt = _head_layout(w_uq[l].reshape(MLA_LORA, MLA_HEADS, MLA_QK)).reshape(
            MLA_LORA, MLA_HEADS * MLA_HEAD_PAD).T.astype(BF16)
        wkv3 = w_ukv[l].reshape(MLA_LORA, MLA_HEADS, MLA_NOPE + MLA_V)
        wk = wkv3[:, :, :MLA_NOPE].reshape(MLA_LORA, MLA_HEADS * MLA_NOPE).astype(BF16)
        wvt = jnp.pad(jnp.transpose(wkv3[:, :, MLA_NOPE:], (1, 2, 0)),
                      ((0, 0), (0, VT_ROWS - MLA_V), (0, 0))).reshape(
            MLA_HEADS * VT_ROWS, MLA_LORA).astype(BF16)
        qg = (_head_layout(q_norm_g[l]) * (MLA_QK ** -0.5 * LOG2E)).reshape(MLA_HEAD_PAD, 1)
        kg = _head_layout(k_norm_g[l]).reshape(1, MLA_HEAD_PAD)

        if l == 0:
            h, zs = _rms_small(h_res, norm1_g[0], w_small, 0, RMS_SMALL_ROWS)
            hm, zsm = _rms_small(hm_res, norm1_g[0], w_small, 0, N_META)
        else:
            zs = _matmul_nt(h, w_small, l, IN_PROJ_SMALL_ROWS, N_SMALL, name="in_proj_small")
            zsm = _matmul_nt(hm, w_small, l, N_META, N_SMALL, name="in_proj_small_meta")
        z, zm = _in_proj(h, hm, w_in_t, l, *IN_PROJ_BLOCK, name="in_proj")

        gla_m, s_meta = _gla(zm, zsm, w2p, gla_gate_b[l], gla_onorm_g[l], s_zero,
                             batch=1, tokens=N_META, block=N_META, chunk=N_META, emit_state=True)
        (gla_r,) = _gla(z, zs, w2p, gla_gate_b[l], gla_onorm_g[l], s_meta,
                        batch=BATCH, tokens=SEQ, block=GLA_TOKENS, chunk=GLA_CHUNK, emit_state=False)

        km, vmt = _kvproj(zm, zsm, kv_a_norm_g[l], wk, wvt, kg, ctab_m, stab_m, N_META)
        kr, vrt = _kvproj(z, zs, kv_a_norm_g[l], wk, wvt, kg, ctab_r, stab_r, PROJ_ROWS)
        kmp = jnp.pad(km, ((0, 128 - N_META), (0, 0)))
        vmtp = jnp.pad(vmt, ((0, 0), (0, 128 - N_META)))
        qr = _qproj(z, q_a_norm_g[l], wqt, qg, cos_tr, sin_tr, PROJ_ROWS)
        att_r = _attention(qr, kr, vrt, km, vmt, ATTN_BLOCK, ATTN_HEADS)

        h_res, h2 = _merge_out(gla_r, att_r, wa, wb, wo, l, z, h_res, norm2_g[l], MERGE_OUT_ROWS)
        if last:
            (act,) = _gateup(h2, None, w_gate_up, l, *GATE_UP_BLOCK)
            (h_res,) = _down(act, wd, l, h_res, None, *DOWN_BLOCK)
        else:
            qm = _qproj(zm, q_a_norm_g[l], wqt, qg, cos_tm, sin_tm, N_META)
            att_m = _meta_attention(qm.T, kmp, vmtp)
            hm_res, hm2 = _merge_out(gla_m, att_m, wa, wb, wo, l, zm, hm_res, norm2_g[l], N_META)
            act, act_m = _gateup(h2, hm2, w_gate_up, l, *GATE_UP_BLOCK)
            h_res, h = _down(act, wd, l, h_res, norm1_g[l + 1], *DOWN_BLOCK)
            hm_res, hm = _down(act_m, wd, l, hm_res, norm1_g[l + 1], N_META, DOWN_META_K)

    return h_res.reshape(BATCH, SEQ, D_MODEL)
```
